```python
import math
import jax, jax.numpy as jnp
from jax import lax
import numpy as np

D_MODEL = 1024
BATCH = 32
SEQ = 256
DEPTH = 2
DEC_BATCH = 4
DEC_SEQ = 2048
PAST_LEN = 512

GRID_W = 64
MIX_W = D_MODEL
FOURIER_W = D_MODEL // 4
FOURIER_HEADS = 4
FOURIER_DH = FOURIER_W // FOURIER_HEADS
MLSTM_W = 3 * D_MODEL // 8
MLSTM_HEADS = 4
MLSTM_DH = MLSTM_W // MLSTM_HEADS
S5_W = MIX_W - FOURIER_W - MLSTM_W
S5_GROUP_CH = 16
S5_GROUPS = S5_W // S5_GROUP_CH
S5_STATE = 64
N_GATES = 2 * 2 * MLSTM_HEADS
P_IN = FOURIER_W + 3 * MLSTM_W + N_GATES + MLSTM_W + S5_W
D_FF = -(-8 * D_MODEL // (3 * 256)) * 256
CHUNK = 128
EPS = 1e-6

kernel_name = "hybrid_fnet_mlstm_s5_prefix_dit_step"

F32 = jnp.float32


def rmsnorm(x, w):
    x32 = x.astype(F32)
    y = x32 * lax.rsqrt(jnp.mean(x32 * x32, axis=-1, keepdims=True) + EPS)
    return (y * w.astype(F32)).astype(x.dtype)


def ada_mod(cvec, w_ada, b_ada):
    m = jax.nn.silu(cvec) @ w_ada + b_ada
    return jnp.split(m, 6, axis=-1)


def fourier_mix(zf, grid):
    B, S, _ = zf.shape
    z = zf.astype(F32).reshape(B, S, FOURIER_HEADS, FOURIER_DH)
    if grid:
        rows = S // GRID_W
        z = z.reshape(B, rows, GRID_W, FOURIER_HEADS, FOURIER_DH)
        f = jnp.fft.fftn(z, axes=(1, 2, 4), norm="ortho")
    else:
        f = jnp.fft.fftn(z, axes=(1, 3), norm="ortho")
    return f.real.reshape(B, S, FOURIER_W)


def mlstm_chunked(q, k, v, ig, lf, c0, n0, m0):
    B, H, S, DH = q.shape
    nc = S // CHUNK

    def chunks(t):
        return jnp.moveaxis(t.reshape((B, H, nc, CHUNK) + t.shape[3:]), 2, 0)

    causal = jnp.tril(jnp.ones((CHUNK, CHUNK), dtype=bool))

    def step(carry, xs):
        c, n, m = carry
        qc, kc, vc, igc, lfc = xs
        b = jnp.cumsum(lfc, axis=-1)
        log_d = jnp.where(causal, b[..., :, None] - b[..., None, :] + igc[..., None, :], -jnp.inf)
        m_inter = b + m[..., None]
        m_row = jnp.maximum(m_inter, jnp.max(log_d, axis=-1))
        d = jnp.exp(log_d - m_row[..., None])
        inter = jnp.exp(m_inter - m_row)
        s = jnp.einsum("bhjd,bhsd->bhjs", qc, kc) * d
        num = jnp.einsum("bhjs,bhse->bhje", s, vc) + inter[..., None] * jnp.einsum("bhjd,bhde->bhje", qc, c)
        den = jnp.sum(s, axis=-1) + inter * jnp.einsum("bhjd,bhd->bhj", qc, n)
        h = num / jnp.maximum(jnp.abs(den), jnp.exp(-m_row))[..., None]
        m_new = m_row[..., -1]
        w = jnp.exp(b[..., -1:] - b + igc - m_new[..., None])
        decay = jnp.exp(b[..., -1] + m - m_new)
        c_new = decay[..., None, None] * c + jnp.einsum("bhs,bhsd,bhse->bhde", w, kc, vc)
        n_new = decay[..., None] * n + jnp.einsum("bhs,bhsd->bhd", w, kc)
        return (c_new, n_new, m_new), h

    (c, n, m), hs = lax.scan(step, (c0, n0, m0),
                             (chunks(q), chunks(k), chunks(v), chunks(ig), chunks(lf)))
    h = jnp.moveaxis(hs, 0, 2).reshape(B, H, S, DH)
    return h, (c, n, m)


def s5_zoh(lam_re, lam_im, log_step, b_re, b_im):
    lam = lax.complex(lam_re.astype(F32), lam_im.astype(F32))
    step = jnp.exp(log_step.astype(F32))[:, None]
    lam_bar = jnp.exp(lam * step)
    bmat = lax.complex(b_re.astype(F32), b_im.astype(F32))
    b_bar = ((lam_bar - 1.0) / lam)[..., None] * bmat
    return lam_bar, b_bar


def s5_scan(u, lam_bar, b_bar, x0):
    bu = jnp.einsum("bsgc,gpc->bsgp", u, b_bar)
    bu = bu.at[:, 0].add(lam_bar * x0)
    a = jnp.broadcast_to(lam_bar, bu.shape)
    _, xs = lax.associative_scan(lambda e, l: (e[0] * l[0], l[0] * e[1] + l[1]), (a, bu), axis=1)
    return xs


def mixer(xn, p, init, grid):
    B, S, _ = xn.shape
    dt = xn.dtype
    z = xn @ p["w_in"]
    o = FOURIER_W
    idx = [o, o + MLSTM_W, o + 2 * MLSTM_W, o + 3 * MLSTM_W, o + 3 * MLSTM_W + N_GATES,
           o + 4 * MLSTM_W + N_GATES]
    zf, zq, zk, zv, zg, zo, zu = jnp.split(z, idx, axis=-1)

    f_out = fourier_mix(zf, grid).astype(dt) @ p["w_fourier"]

    def heads(t):
        return t.astype(F32).reshape(B, S, MLSTM_HEADS, MLSTM_DH).transpose(0, 2, 1, 3)

    q = heads(zq)
    k = heads(zk) * (MLSTM_DH ** -0.5)
    v = heads(zv)
    g = (zg.astype(F32) + p["b_gates"].astype(F32)).reshape(B, S, 2, 2, MLSTM_HEADS).transpose(2, 3, 0, 4, 1)
    cf, nf, mf, cb, nb, mb, sf, sb = init
    h_f, (cf2, nf2, mf2) = mlstm_chunked(q, k, v, g[0, 0], jax.nn.log_sigmoid(g[0, 1]), cf, nf, mf)

    def rev(t):
        return jnp.flip(t, axis=2)

    h_b, (cb2, nb2, mb2) = mlstm_chunked(rev(q), rev(k), rev(v), rev(g[1, 0]),
                                         rev(jax.nn.log_sigmoid(g[1, 1])), cb, nb, mb)
    h = (h_f + rev(h_b)).transpose(0, 2, 1, 3)
    h = h * lax.rsqrt(jnp.mean(h * h, axis=-1, keepdims=True) + EPS)
    h = h.reshape(B, S, MLSTM_W) * p["mlstm_norm_w"].astype(F32)
    m_out = (h * jax.nn.sigmoid(zo.astype(F32))).astype(dt)

    u = zu.astype(F32).reshape(B, S, S5_GROUPS, S5_GROUP_CH)
    uc = u.astype(jnp.complex64)
    lam_f, bb_f = s5_zoh(p["s5_lambda_re"][0], p["s5_lambda_im"][0], p["s5_log_step"][0], p["s5_b_re"], p["s5_b_im"])
    lam_b, bb_b = s5_zoh(p["s5_lambda_re"][1], p["s5_lambda_im"][1], p["s5_log_step"][1], p["s5_b_re"], p["s5_b_im"])
    xs_f = s5_scan(uc, lam_f, bb_f, sf)
    xs_b = s5_scan(jnp.flip(uc, axis=1), lam_b, bb_b, sb)
    cmat = lax.complex(p["s5_c_re"].astype(F32), p["s5_c_im"].astype(F32))
    y = jnp.einsum("bsgp,gcp->bsgc", xs_f + jnp.flip(xs_b, axis=1), cmat).real + p["s5_d"].astype(F32) * u
    y = jax.nn.gelu(y.reshape(B, S, S5_W)).astype(dt)
    ga, gb = jnp.split(y @ p["w_glu"], 2, axis=-1)
    s_out = ga * jax.nn.sigmoid(gb)

    out = jnp.concatenate([f_out, m_out, s_out], axis=-1) @ p["w_out"]
    final = (cf2, nf2, mf2, cb2, nb2, mb2, xs_f[:, -1], xs_b[:, -1])
    return out, final


def block(x, mods, p, init, grid):
    sh1, sc1, g1, sh2, sc2, g2 = mods
    xn = rmsnorm(x, p["norm1_w"]) * (1 + sc1) + sh1
    mix, final = mixer(xn, p, init, grid)
    x = x + g1 * mix
    xn = rmsnorm(x, p["norm2_w"]) * (1 + sc2) + sh2
    ff = (jax.nn.silu(xn @ p["w_gate"]) * (xn @ p["w_up"])) @ p["w_down"]
    x = x + g2 * ff
    return x, final


def setup_inputs(seed: int = 0) -> dict:
    key = jax.random.key(seed)
    ks = iter(jax.random.split(key, 48))

    def nrm(shape, scale):
        return scale * jax.random.normal(next(ks), shape, F32)

    H, DH, G, P, GC = MLSTM_HEADS, MLSTM_DH, S5_GROUPS, S5_STATE, S5_GROUP_CH
    x_prompt = nrm((BATCH, SEQ, D_MODEL), 1.0)
    x_sample = nrm((DEC_BATCH, DEC_SEQ, D_MODEL), 1.0)
    state_mlstm_C = nrm((DEC_BATCH, DEPTH, 2, H, DH, DH), 0.3)
    state_mlstm_n = nrm((DEC_BATCH, DEPTH, 2, H, DH), 0.3)
    state_mlstm_m = jax.random.uniform(next(ks), (DEC_BATCH, DEPTH, 2, H), F32, 0.0, 2.0)
    state_s5_re = nrm((DEC_BATCH, DEPTH, 2, G, P), 0.3)
    state_s5_im = nrm((DEC_BATCH, DEPTH, 2, G, P), 0.3)
    c = nrm((DEC_BATCH, D_MODEL), 1.0)
    c_ctx = nrm((D_MODEL,), 1.0)
    w_ada = nrm((DEPTH, D_MODEL, 6 * D_MODEL), 0.5 * D_MODEL ** -0.5)
    b_ada = nrm((DEPTH, 6 * D_MODEL), 0.02)
    norm1_w = 1.0 + nrm((DEPTH, D_MODEL), 0.02)
    norm2_w = 1.0 + nrm((DEPTH, D_MODEL), 0.02)
    w_in = nrm((DEPTH, D_MODEL, P_IN), D_MODEL ** -0.5)
    ig_b = nrm((DEPTH, 2, 1, H), 0.1)
    fg_b = jnp.linspace(3.0, 6.0, H, dtype=F32) + nrm((DEPTH, 2, 1, H), 0.1)
    b_gates = jnp.concatenate([ig_b, fg_b], axis=2).reshape(DEPTH, N_GATES)
    w_fourier = nrm((DEPTH, FOURIER_W, FOURIER_W), FOURIER_W ** -0.5)
    mlstm_norm_w = 1.0 + nrm((DEPTH, MLSTM_W), 0.02)
    s5_lambda_re = -0.5 + nrm((DEPTH, 2, G, P), 0.01)
    s5_lambda_im = math.pi * jnp.arange(P, dtype=F32) + nrm((DEPTH, 2, G, P), 0.01)
    s5_log_step = jax.random.uniform(next(ks), (DEPTH, 2, G), F32, math.log(1e-3), math.log(1e-1))
    s5_b_re = nrm((DEPTH, G, P, GC), (2 * GC) ** -0.5)
    s5_b_im = nrm((DEPTH, G, P, GC), (2 * GC) ** -0.5)
    s5_c_re = nrm((DEPTH, G, GC, P), P ** -0.5)
    s5_c_im = nrm((DEPTH, G, GC, P), P ** -0.5)
    s5_d = nrm((DEPTH, G, GC), 1.0)
    w_glu = nrm((DEPTH, S5_W, 2 * S5_W), S5_W ** -0.5)
    w_out = nrm((DEPTH, MIX_W, D_MODEL), MIX_W ** -0.5)
    w_gate = nrm((DEPTH, D_MODEL, D_FF), D_MODEL ** -0.5)
    w_up = nrm((DEPTH, D_MODEL, D_FF), D_MODEL ** -0.5)
    w_down = nrm((DEPTH, D_FF, D_MODEL), D_FF ** -0.5)
    norm_f = 1.0 + nrm((D_MODEL,), 0.02)
    return {"x_prompt": x_prompt, "x_sample": x_sample,
            "state_mlstm_C": state_mlstm_C, "state_mlstm_n": state_mlstm_n, "state_mlstm_m": state_mlstm_m,
            "state_s5_re": state_s5_re, "state_s5_im": state_s5_im, "c": c, "c_ctx": c_ctx,
            "w_ada": w_ada, "b_ada": b_ada, "norm1_w": norm1_w, "norm2_w": norm2_w, "w_in": w_in,
            "b_gates": b_gates, "w_fourier": w_fourier, "mlstm_norm_w": mlstm_norm_w,
            "s5_lambda_re": s5_lambda_re, "s5_lambda_im": s5_lambda_im, "s5_log_step": s5_log_step,
            "s5_b_re": s5_b_re, "s5_b_im": s5_b_im, "s5_c_re": s5_c_re, "s5_c_im": s5_c_im, "s5_d": s5_d,
            "w_glu": w_glu, "w_out": w_out, "w_gate": w_gate, "w_up": w_up, "w_down": w_down,
            "norm_f": norm_f}


def reference(x_prompt, x_sample, state_mlstm_C, state_mlstm_n, state_mlstm_m, state_s5_re, state_s5_im,
              c, c_ctx, w_ada, b_ada, norm1_w, norm2_w, w_in, b_gates, w_fourier, mlstm_norm_w,
              s5_lambda_re, s5_lambda_im, s5_log_step, s5_b_re, s5_b_im, s5_c_re, s5_c_im, s5_d,
              w_glu, w_out, w_gate, w_up, w_down, norm_f):
    params = [{"w_ada": w_ada[l], "b_ada": b_ada[l], "norm1_w": norm1_w[l], "norm2_w": norm2_w[l],
               "w_in": w_in[l], "b_gates": b_gates[l], "w_fourier": w_fourier[l],
               "mlstm_norm_w": mlstm_norm_w[l], "s5_lambda_re": s5_lambda_re[l],
               "s5_lambda_im": s5_lambda_im[l], "s5_log_step": s5_log_step[l], "s5_b_re": s5_b_re[l],
               "s5_b_im": s5_b_im[l], "s5_c_re": s5_c_re[l], "s5_c_im": s5_c_im[l], "s5_d": s5_d[l],
               "w_glu": w_glu[l], "w_out": w_out[l], "w_gate": w_gate[l], "w_up": w_up[l],
               "w_down": w_down[l]} for l in range(DEPTH)]

    bp = x_prompt.shape[0]
    zc = jnp.zeros((bp, MLSTM_HEADS, MLSTM_DH, MLSTM_DH), F32)
    zn = jnp.zeros((bp, MLSTM_HEADS, MLSTM_DH), F32)
    zm = jnp.zeros((bp, MLSTM_HEADS), F32)
    zs = jnp.zeros((bp, S5_GROUPS, S5_STATE), jnp.complex64)
    xp = x_prompt
    finals = []
    for l in range(DEPTH):
        p = params[l]
        mods = ada_mod(c_ctx, p["w_ada"], p["b_ada"])
        xp, fin = block(xp, mods, p, (zc, zn, zm, zc, zn, zm, zs, zs), False)
        finals.append(fin)
    y_prompt = rmsnorm(xp, norm_f)

    xs = x_sample
    for l in range(DEPTH):
        p = params[l]
        mods = [t[:, None, :] for t in ada_mod(c, p["w_ada"], p["b_ada"])]
        init = (state_mlstm_C[:, l, 0].astype(F32), state_mlstm_n[:, l, 0].astype(F32),
                state_mlstm_m[:, l, 0].astype(F32),
                state_mlstm_C[:, l, 1].astype(F32), state_mlstm_n[:, l, 1].astype(F32),
                state_mlstm_m[:, l, 1].astype(F32),
                lax.complex(state_s5_re[:, l, 0].astype(F32), state_s5_im[:, l, 0].astype(F32)),
                lax.complex(state_s5_re[:, l, 1].astype(F32), state_s5_im[:, l, 1].astype(F32)))
        xs, _ = block(xs, mods, p, init, True)
    y_sample = rmsnorm(xs, norm_f)

    new_mlstm_C = jnp.stack([jnp.stack([f[0], f[3]], axis=1) for f in finals], axis=1)
    new_mlstm_n = jnp.stack([jnp.stack([f[1], f[4]], axis=1) for f in finals], axis=1)
    new_mlstm_m = jnp.stack([jnp.stack([f[2], f[5]], axis=1) for f in finals], axis=1)
    s_states = jnp.stack([jnp.stack([f[6], f[7]], axis=1) for f in finals], axis=1)
    new_s5_re = jnp.real(s_states)
    new_s5_im = jnp.imag(s_states)
    return (y_prompt, y_sample, new_mlstm_C, new_mlstm_n, new_mlstm_m, new_s5_re, new_s5_im)
```

```python
import functools
import math

import numpy as np
import jax
import jax.numpy as jnp
from jax import lax
from jax.experimental import pallas as pl
from jax.experimental.pallas import tpu as pltpu

F32 = jnp.float32
BF16 = jnp.bfloat16

D_MODEL = 1024
BATCH = 32
SEQ = 256
DEPTH = 2
DEC_BATCH = 4
DEC_SEQ = 2048
GRID_W = 64
FOURIER_W = 256
FOURIER_DH = 64
MLSTM_W = 384
MLSTM_HEADS = 4
MLSTM_DH = 96
S5_W = 384
S5_GROUP_CH = 16
S5_GROUPS = 24
S5_STATE = 64
N_GATES = 16
D_FF = 2816
EPS = 1e-6

T_PROMPT = BATCH * SEQ
T_SAMPLE = DEC_BATCH * DEC_SEQ
T_ALL = T_PROMPT + T_SAMPLE

LANES = 128
SUBLANES = 8
VMEM_LIMIT = 56 * 1024 * 1024

HEAD_PAD = LANES
HEADS_W = MLSTM_HEADS * HEAD_PAD
N_AUG = MLSTM_DH
Z_W = 4 * HEADS_W + FOURIER_W + S5_W
MLSTM_CHUNK = 256
S5_CHUNK = 16
S5_IN = S5_CHUNK * S5_GROUP_CH
S5_ST = 2 * S5_STATE
S5_GB = 4
SAMPLE_BPAD = SUBLANES
R_PROMPT = (SEQ // S5_CHUNK) * BATCH
R_SAMPLE = (DEC_SEQ // S5_CHUNK) * SAMPLE_BPAD
R_ALL = R_PROMPT + R_SAMPLE
TM = 512
TF = D_FF // 2
N_MODS = 8
NEG = -1e30

_NT = (((1,), (1,)), ((), ()))
_TN = (((0,), (0,)), ((), ()))


def _cparams(sem):
    return pltpu.CompilerParams(dimension_semantics=sem, vmem_limit_bytes=VMEM_LIMIT)


def _mod_index(i):
    n_prompt = T_PROMPT // TM
    per_seq = DEC_SEQ // TM
    return jnp.where(i < n_prompt, 0, 1 + (i - n_prompt) // per_seq)


def _rms(x, w):
    return x * lax.rsqrt(jnp.mean(x * x, axis=-1, keepdims=True) + EPS) * w


def _log_sigmoid(x):
    return jnp.minimum(x, 0.0) - jnp.log1p(jnp.exp(-jnp.abs(x)))


def _ada_kernel(c_ref, w_ref, b_ref, o_ref):
    a = c_ref[...]
    a = (a * jax.nn.sigmoid(a)).astype(BF16)
    o_ref[0] = jnp.dot(a, w_ref[0].astype(BF16), preferred_element_type=F32) + b_ref[0]


def _ada(cc, w_ada, b_ada):
    tn = 512
    return pl.pallas_call(
        _ada_kernel,
        grid=(DEPTH, 6 * D_MODEL // tn),
        in_specs=[pl.BlockSpec((N_MODS, D_MODEL), lambda l, j: (0, 0)),
                  pl.BlockSpec((1, D_MODEL, tn), lambda l, j: (l, 0, j)),
                  pl.BlockSpec((1, 1, tn), lambda l, j: (l, 0, j))],
        out_specs=pl.BlockSpec((1, N_MODS, tn), lambda l, j: (l, 0, j)),
        out_shape=jax.ShapeDtypeStruct((DEPTH, N_MODS, 6 * D_MODEL), F32),
        compiler_params=_cparams(("arbitrary", "arbitrary")),
        name="ada_mod",
    )(cc, w_ada, b_ada.reshape(DEPTH, 1, 6 * D_MODEL))


def _in_kernel(x_ref, mod_ref, nw_ref, w_ref, wg_ref, bg_ref,
               zq_ref, zk_ref, zv_ref, zo_ref, zf_ref, zu_ref, gt_ref):
    x = x_ref[...]
    mod = mod_ref[0]
    sh = mod[:, 0:D_MODEL]
    sc = mod[:, D_MODEL:2 * D_MODEL]
    xn = (_rms(x, nw_ref[...]) * (1.0 + sc) + sh).astype(BF16)
    z = jnp.dot(xn, w_ref[...], preferred_element_type=F32)
    o = 0
    for ref, w in ((zq_ref, HEADS_W), (zk_ref, HEADS_W), (zv_ref, HEADS_W), (zo_ref, HEADS_W),
                   (zf_ref, FOURIER_W), (zu_ref, S5_W)):
        ref[...] = z[:, o:o + w].astype(BF16)
        o += w
    gt_ref[...] = lax.dot_general(wg_ref[...], xn, _NT, preferred_element_type=F32) + bg_ref[...]


def _in_proj(x, mods, nw, w, wg_t, bg):
    tok = lambda w_: pl.BlockSpec((TM, w_), lambda i: (i, 0))
    full = lambda a: pl.BlockSpec(a.shape, lambda i: (0,) * a.ndim)
    outs = [(HEADS_W, BF16)] * 4 + [(FOURIER_W, BF16), (S5_W, BF16)]
    return pl.pallas_call(
        _in_kernel,
        grid=(T_ALL // TM,),
        in_specs=[tok(D_MODEL),
                  pl.BlockSpec((1, 1, 6 * D_MODEL), lambda i: (_mod_index(i), 0, 0)),
                  full(nw), full(w), full(wg_t), full(bg)],
        out_specs=[tok(w_) for w_, _ in outs] + [pl.BlockSpec((N_GATES, TM), lambda i: (0, i))],
        out_shape=[jax.ShapeDtypeStruct((T_ALL, w_), dt) for w_, dt in outs]
        + [jax.ShapeDtypeStruct((N_GATES, T_ALL), F32)],
        compiler_params=_cparams(("arbitrary",)),
        name="in_proj",
    )(x, mods, nw, w, wg_t, bg)


def _dft_consts():
    d = np.arange(FOURIER_DH)
    phi = 2.0 * np.pi * ((d[:, None] * d[None, :]) % FOURIER_DH) / FOURIER_DH
    eye = np.eye(FOURIER_W // FOURIER_DH)
    cd = np.kron(eye, np.cos(phi)) / math.sqrt(FOURIER_DH)
    sd = np.kron(eye, np.sin(phi)) / math.sqrt(FOURIER_DH)
    s = np.arange(SEQ)
    th = 2.0 * np.pi * ((s[:, None] * s[None, :]) % SEQ) / SEQ
    rows = DEC_SEQ // GRID_W
    pos = np.arange(DEC_SEQ)
    r, c = pos // GRID_W, pos % GRID_W
    ph = ((r[:, None] * r[None, :]) * (GRID_W // rows) + c[:, None] * c[None, :]) % GRID_W
    th2 = 2.0 * np.pi * ph / GRID_W
    return (np.concatenate([cd, sd], axis=1),
            np.concatenate([np.cos(th), -np.sin(th)], axis=1) / math.sqrt(SEQ),
            np.concatenate([np.cos(th2), -np.sin(th2)], axis=1) / math.sqrt(DEC_SEQ))


def _fourier_prompt_kernel(nb, zf_ref, cdsd_ref, cs_ref, wf_ref, o_ref):
    t = jnp.dot(zf_ref[...], cdsd_ref[...], preferred_element_type=F32).astype(BF16)
    for b in range(nb):
        tb = t[b * SEQ:(b + 1) * SEQ]
        st = jnp.concatenate([tb[:, :FOURIER_W], tb[:, FOURIER_W:]], axis=0)
        f = jnp.dot(cs_ref[...], st, preferred_element_type=F32)
        o_ref[b * SEQ:(b + 1) * SEQ, :] = jnp.dot(
            f.astype(BF16), wf_ref[...], preferred_element_type=F32).astype(BF16)


def _fourier_prompt(zf, cdsd, cs, wf):
    nb = 4
    full = lambda a: pl.BlockSpec(a.shape, lambda i: (0,) * a.ndim)
    return pl.pallas_call(
        functools.partial(_fourier_prompt_kernel, nb),
        grid=(BATCH // nb,),
        in_specs=[pl.BlockSpec((nb * SEQ, FOURIER_W), lambda i: (i, 0)), full(cdsd), full(cs), full(wf)],
        out_specs=pl.BlockSpec((nb * SEQ, FOURIER_W), lambda i: (i, 0)),
        out_shape=jax.ShapeDtypeStruct((T_ALL, FOURIER_W), BF16),
        compiler_params=_cparams(("arbitrary",)),
        name="fourier_prompt",
    )(zf, cdsd, cs, wf)


def _fourier_sample_kernel(zf_ref, cdsd_ref, ab_ref, wf_ref, prev_ref, o_ref, tt_ref):
    del prev_ref

    @pl.when(pl.program_id(0) == 0)
    def _():
        for b in range(DEC_BATCH):
            t = jnp.dot(zf_ref[b * DEC_SEQ:(b + 1) * DEC_SEQ, :], cdsd_ref[...],
                        preferred_element_type=F32).astype(BF16)
            tt_ref[b, 0:DEC_SEQ, :] = t[:, :FOURIER_W]
            tt_ref[b, DEC_SEQ:2 * DEC_SEQ, :] = t[:, FOURIER_W:]

    for b in range(DEC_BATCH):
        f = jnp.dot(ab_ref[...], tt_ref[b], preferred_element_type=F32)
        o_ref[b] = jnp.dot(f.astype(BF16), wf_ref[...], preferred_element_type=F32).astype(BF16)


def _fourier_sample(zf, cdsd, ab, wf, prev):
    tk = 512
    full = lambda a: pl.BlockSpec(a.shape, lambda i: (0,) * a.ndim)
    prev3 = prev.reshape(T_ALL // DEC_SEQ, DEC_SEQ, FOURIER_W)
    out = pl.pallas_call(
        _fourier_sample_kernel,
        grid=(DEC_SEQ // tk,),
        in_specs=[pl.BlockSpec((T_SAMPLE, FOURIER_W), lambda i: (1, 0)), full(cdsd),
                  pl.BlockSpec((tk, 2 * DEC_SEQ), lambda i: (i, 0)), full(wf),
                  pl.BlockSpec(memory_space=pl.ANY)],
        out_specs=pl.BlockSpec((DEC_BATCH, tk, FOURIER_W), lambda i: (1, i, 0)),
        out_shape=jax.ShapeDtypeStruct(prev3.shape, BF16),
        scratch_shapes=[pltpu.VMEM((DEC_BATCH, 2 * DEC_SEQ, FOURIER_W), BF16)],
        input_output_aliases={4: 0},
        compiler_params=_cparams(("arbitrary",)),
        name="fourier_sample",
    )(zf, cdsd, ab, wf, prev3)
    return out.reshape(T_ALL, FOURIER_W)


def _mlstm_chunk(q, k, v, ig_row, fg_row, c, m, fwd):
    L = q.shape[0]
    row = lax.broadcasted_iota(jnp.int32, (L, L), 0)
    col = lax.broadcasted_iota(jnp.int32, (L, L), 1)
    mask = (col <= row) if fwd else (col >= row)
    tri = mask.astype(F32)
    eye = (row == col).astype(F32)
    lf_row = _log_sigmoid(fg_row)
    rows8 = jnp.concatenate([lf_row, ig_row, jnp.zeros((SUBLANES - 2, L), F32)], axis=0)
    cols = lax.dot_general(jnp.concatenate([tri, eye], axis=0), rows8, _NT,
                           precision=lax.Precision.HIGHEST, preferred_element_type=F32)
    b_col = cols[:L, 0:1]
    ig_col = cols[L:, 1:2]
    b_row = lax.dot_general(rows8, tri, _NT, precision=lax.Precision.HIGHEST,
                            preferred_element_type=F32)[0:1]

    log_d = jnp.where(mask, b_col - b_row + ig_row, NEG)
    m_inter = b_col + m
    m_row = jnp.maximum(m_inter, jnp.max(log_d, axis=1, keepdims=True))
    d = jnp.where(mask, jnp.exp(log_d - m_row), 0.0)
    inter = jnp.exp(m_inter - m_row)
    scale = MLSTM_DH ** -0.5
    s = lax.dot_general(q, k, _NT, preferred_element_type=F32) * scale
    num = (jnp.dot((s * d).astype(BF16), v, preferred_element_type=F32)
           + inter * jnp.dot(q, c.astype(BF16), preferred_element_type=F32))
    den = num[:, N_AUG:N_AUG + 1]
    h = num / jnp.maximum(jnp.abs(den), jnp.exp(-m_row))
    last = L - 1 if fwd else 0
    m_new = m_row[last:last + 1, :]
    b_last = b_col[last:last + 1, :]
    w_col = jnp.exp(b_last - b_col + ig_col - m_new)
    decay = jnp.exp(b_last + m - m_new)
    kw = (k.astype(F32) * w_col).astype(BF16)
    c_new = decay * c + scale * lax.dot_general(kw, v, _TN, preferred_element_type=F32)
    return h, c_new, m_new


def _mlstm_kernel(nc, m0_ref, q_ref, k_ref, v_ref, zo_ref, g_ref, c0_ref, nw_ref,
                  o_ref, cfin_ref, mfin_ref, hbuf_ref):
    L = MLSTM_CHUNK
    b = pl.program_id(0)
    hd = pl.program_id(1)
    lane = lax.broadcasted_iota(jnp.int32, (L, HEAD_PAD), 1)

    def load(ci):
        sl = pl.ds(pl.multiple_of(ci * L, L), L)
        v = v_ref[sl, :]
        v = jnp.where(lane == N_AUG, jnp.ones_like(v), v)
        return sl, q_ref[sl, :], k_ref[sl, :], v, g_ref[0, ci]

    def fwd_step(ci, carry):
        c, m = carry
        sl, q, k, v, g = load(ci)
        h, c, m = _mlstm_chunk(q, k, v, g[0:1], g[1:2], c, m, True)
        hbuf_ref[sl, :] = h
        return c, m

    def bwd_step(i, carry):
        c, m = carry
        ci = nc - 1 - i
        sl, q, k, v, g = load(ci)
        h, c, m = _mlstm_chunk(q, k, v, g[2:3], g[3:4], c, m, False)
        h = jnp.where(lane < MLSTM_DH, h + hbuf_ref[sl, :], 0.0)
        ms = jnp.sum(h * h, axis=-1, keepdims=True) * (1.0 / MLSTM_DH)
        hn = h * lax.rsqrt(ms + EPS) * nw_ref[0]
        o_ref[sl, :] = (hn * jax.nn.sigmoid(zo_ref[sl, :].astype(F32))).astype(BF16)
        return c, m

    for dr, step in ((0, fwd_step), (1, bwd_step)):
        carry = (c0_ref[0, dr, 0], jnp.full((1, 1), m0_ref[b, dr, hd], F32))
        if nc == 1:
            c, m = step(0, carry)
        else:
            c, m = lax.fori_loop(0, nc, step, carry)
        cfin_ref[0, dr, 0] = c
        mfin_ref[0, dr, 0] = jnp.broadcast_to(m, (SUBLANES, LANES))


def _mlstm(seq, nb, blk_off, q, k, v, zo, g4, c0, m0, nw, prev):
    nc = seq // MLSTM_CHUNK
    tok = pl.BlockSpec((seq, HEAD_PAD), lambda b, h: (b + blk_off, h))
    st = pl.BlockSpec((1, 2, 1, HEAD_PAD, HEAD_PAD), lambda b, h: (b, 0, h, 0, 0))
    in_specs = [pl.BlockSpec(memory_space=pltpu.SMEM), tok, tok, tok, tok,
                pl.BlockSpec((1, nc, 4, MLSTM_CHUNK), lambda b, h: (h, b + blk_off, 0, 0)),
                st, pl.BlockSpec((1, 1, HEAD_PAD), lambda b, h: (h, 0, 0))]
    args = [m0, q, k, v, zo, g4, c0, nw]
    aliases = {}
    if prev is not None:
        in_specs.append(pl.BlockSpec(memory_space=pl.ANY))
        args.append(prev)
        aliases = {len(args) - 1: 0}

    def body(*refs):
        if prev is not None:
            refs = refs[:8] + refs[9:]
        _mlstm_kernel(nc, *refs)

    return pl.pallas_call(
        body,
        grid=(nb, MLSTM_HEADS),
        in_specs=in_specs,
        out_specs=[tok, st,
                   pl.BlockSpec((1, 2, 1, SUBLANES, LANES), lambda b, h: (b, 0, h, 0, 0))],
        out_shape=[jax.ShapeDtypeStruct((T_ALL, HEADS_W), BF16),
                   jax.ShapeDtypeStruct((nb, 2, MLSTM_HEADS, HEAD_PAD, HEAD_PAD), F32),
                   jax.ShapeDtypeStruct((nb, 2, MLSTM_HEADS, SUBLANES, LANES), F32)],
        scratch_shapes=[pltpu.VMEM((seq, HEAD_PAD), F32)],
        input_output_aliases=aliases,
        compiler_params=_cparams(("arbitrary", "arbitrary")),
        name="mlstm_%d" % seq,
    )(*args)


def _s5_prep_kernel(lamc_re_ref, lamc_im_ref, lamr_re_ref, lamr_im_ref, lstep_ref,
                    bt_re_ref, bt_im_ref, ct_re_ref, ct_im_ref,
                    t_ref, m_ref, n_ref, a_ref):
    C = S5_CHUNK
    kk = lax.broadcasted_iota(jnp.int32, (S5_ST, S5_IN), 1) >> 4
    top = lax.broadcasted_iota(jnp.int32, (S5_ST, S5_IN), 0) < S5_STATE
    srow = lax.broadcasted_iota(jnp.int32, (S5_IN, S5_ST), 0) >> 4
    left = lax.broadcasted_iota(jnp.int32, (S5_IN, S5_ST), 1) < S5_STATE
    left16 = lax.broadcasted_iota(jnp.int32, (S5_GROUP_CH, S5_ST), 1) < S5_STATE
    left1 = lax.broadcasted_iota(jnp.int32, (1, S5_ST), 1) < S5_STATE
    lane = lax.broadcasted_iota(jnp.int32, (S5_GROUP_CH, S5_IN), 1)
    ct_re = ct_re_ref[0]
    ct_im = ct_im_ref[0]
    resp = []
    for d in range(2):
        step = jnp.exp(lstep_ref[d, 0])
        lr_c, li_c = lamc_re_ref[d, 0] * step, lamc_im_ref[d, 0] * step
        lr_r, li_r = lamr_re_ref[d, 0], lamr_im_ref[d, 0]

        def cp(lag):
            lagf = lag.astype(F32)
            mag = jnp.exp(lagf * lr_c)
            pr, pi = mag * jnp.cos(lagf * li_c), mag * jnp.sin(lagf * li_c)
            return ct_re * pr - ct_im * pi, ct_re * pi + ct_im * pr

        def rpow(lagf):
            mag = jnp.exp(lagf * lr_r * step)
            return mag * jnp.cos(lagf * li_r * step), mag * jnp.sin(lagf * li_r * step)

        lb_re, lb_im = rpow(1.0)
        nr, ni = lb_re - 1.0, lb_im
        den = lr_r * lr_r + li_r * li_r
        kap_re = (nr * lr_r + ni * li_r) / den
        kap_im = (ni * lr_r - nr * li_r) / den
        bb_re = kap_re * bt_re_ref[0] - kap_im * bt_im_ref[0]
        bb_im = kap_re * bt_im_ref[0] + kap_im * bt_re_ref[0]

        cpr, cpi = cp(kk if d == 0 else (C - 1) - kk)
        resp.append(jnp.dot(jnp.where(left16, bb_re, -bb_im), jnp.where(top, cpr, cpi),
                            precision=lax.Precision.HIGHEST, preferred_element_type=F32))

        cpr, cpi = cp(kk + 1 if d == 0 else C - kk)
        m_ref[d, 0] = jnp.where(top, cpr, -cpi).astype(BF16)

        pr, pi = rpow(((C - 1) - srow if d == 0 else srow).astype(F32))
        bt_r = jnp.concatenate([bb_re] * C, axis=0)
        bt_i = jnp.concatenate([bb_im] * C, axis=0)
        n_ref[d, 0] = jnp.where(left, pr * bt_r - pi * bt_i, pr * bt_i + pi * bt_r).astype(BF16)

        ar, ai = rpow(float(C))
        a_ref[d, 0] = jnp.concatenate([ar, jnp.where(left1, -ai, ai)], axis=0)

    rf, rb = resp
    for s in range(C):
        nf = S5_GROUP_CH * s
        blk = jnp.where(lane >= nf, pltpu.roll(rf, nf, 1) if nf else rf, 0.0)
        nb = S5_GROUP_CH * (C - 1 - s)
        blk = blk + jnp.where(lane < S5_IN - nb, pltpu.roll(rb, S5_IN - nb, 1) if nb else rb, 0.0)
        t_ref[0, S5_GROUP_CH * s:S5_GROUP_CH * (s + 1), :] = blk.astype(BF16)


def _s5_prep(lam_re, lam_im, log_step, b_re, b_im, c_re, c_im):
    G, P = S5_GROUPS, S5_STATE
    dup = lambda a: jnp.concatenate([a, a], axis=-1)
    lamc = [dup(a).reshape(2, G, S5_ST, 1) for a in (lam_re, lam_im)]
    lamr = [dup(a).reshape(2, G, 1, S5_ST) for a in (lam_re, lam_im)]
    lstep = log_step.reshape(2, G, 1, 1)
    bt = [dup(jnp.swapaxes(a, 1, 2)) for a in (b_re, b_im)]
    ct = [jnp.tile(jnp.swapaxes(a, 1, 2), (1, 2, S5_CHUNK)) for a in (c_re, c_im)]
    dspec = lambda r, c: pl.BlockSpec((2, 1, r, c), lambda g: (0, g, 0, 0))
    gspec = lambda r, c: pl.BlockSpec((1, r, c), lambda g: (g, 0, 0))
    return pl.pallas_call(
        _s5_prep_kernel,
        grid=(G,),
        in_specs=[dspec(S5_ST, 1), dspec(S5_ST, 1), dspec(1, S5_ST), dspec(1, S5_ST), dspec(1, 1),
                  gspec(S5_GROUP_CH, S5_ST), gspec(S5_GROUP_CH, S5_ST),
                  gspec(S5_ST, S5_IN), gspec(S5_ST, S5_IN)],
        out_specs=[gspec(S5_IN, S5_IN), dspec(S5_ST, S5_IN), dspec(S5_IN, S5_ST), dspec(2, S5_ST)],
        out_shape=[jax.ShapeDtypeStruct((G, S5_IN, S5_IN), BF16),
                   jax.ShapeDtypeStruct((2, G, S5_ST, S5_IN), BF16),
                   jax.ShapeDtypeStruct((2, G, S5_IN, S5_ST), BF16),
                   jax.ShapeDtypeStruct((2, G, 2, S5_ST), F32)],
        compiler_params=_cparams(("arbitrary",)),
        name="s5_prep",
    )(*lamc, *lamr, lstep, *bt, *ct)


def _s5_kernel(u_ref, t_ref, m_ref, n_ref, a_ref, x0_ref, y_ref, xfin_ref, v_ref, xp_ref):
    W = S5_GB * S5_ST
    for gi in range(S5_GB):
        for d in range(2):
            v_ref[d, :, gi * S5_ST:(gi + 1) * S5_ST] = jnp.dot(
                u_ref[gi], n_ref[d, gi], preferred_element_type=F32)

    im_half = (lax.broadcasted_iota(jnp.int32, (1, W), 1) & (S5_ST - 1)) >= S5_STATE

    def advance(x, rows, d):
        a = a_ref[d, 0:1, :]
        a2 = a_ref[d, 1:2, :]
        xp_ref[d, rows, :] = x
        swapped = jnp.where(im_half, pltpu.roll(x, S5_STATE, 1), pltpu.roll(x, W - S5_STATE, 1))
        return a * x + a2 * swapped + v_ref[d, rows, :]

    def scan(x0, base, nrows, nchunks, d):
        def step(i, x):
            ci = i if d == 0 else nchunks - 1 - i
            rows = pl.ds(pl.multiple_of(base + ci * nrows, nrows), nrows)
            return advance(x, rows, d)
        return lax.fori_loop(0, nchunks, step, x0)

    for d in range(2):
        xfin_ref[d] = scan(jnp.zeros((BATCH, W), F32), 0, BATCH, SEQ // S5_CHUNK, d)
        scan(x0_ref[d], R_PROMPT, SAMPLE_BPAD, DEC_SEQ // S5_CHUNK, d)

    for gi in range(S5_GB):
        cols = slice(gi * S5_ST, (gi + 1) * S5_ST)
        y = jnp.dot(u_ref[gi], t_ref[gi], preferred_element_type=F32)
        for d in range(2):
            y = y + jnp.dot(xp_ref[d, :, cols].astype(BF16), m_ref[d, gi], preferred_element_type=F32)
        y_ref[gi] = y


def _s5(u, t, m, n, a, x0):
    G, W = S5_GROUPS, S5_GB * S5_ST
    a = jnp.transpose(a, (0, 2, 1, 3)).reshape(2, 2, G * S5_ST)
    gsp = lambda r, c: pl.BlockSpec((S5_GB, r, c), lambda j: (j, 0, 0))
    dsp = lambda r, c: pl.BlockSpec((2, S5_GB, r, c), lambda j: (0, j, 0, 0))
    lsp = lambda r: pl.BlockSpec((2, r, W), lambda j: (0, 0, j))
    return pl.pallas_call(
        _s5_kernel,
        grid=(G // S5_GB,),
        in_specs=[gsp(R_ALL, S5_IN), gsp(S5_IN, S5_IN), dsp(S5_ST, S5_IN), dsp(S5_IN, S5_ST),
                  lsp(2), lsp(SAMPLE_BPAD)],
        out_specs=[gsp(R_ALL, S5_IN), lsp(BATCH)],
        out_shape=[jax.ShapeDtypeStruct((G, R_ALL, S5_IN), F32),
                   jax.ShapeDtypeStruct((2, BATCH, G * S5_ST), F32)],
        scratch_shapes=[pltpu.VMEM((2, R_ALL, W), F32), pltpu.VMEM((2, R_ALL, W), F32)],
        compiler_params=_cparams(("arbitrary",)),
        name="s5_scan",
    )(u, t, m, n, a, x0)


def _s5_to_chunks(zu):
    def one(z, nb, seq, bpad):
        z = z.reshape(nb, seq // S5_CHUNK, S5_CHUNK, S5_GROUPS, S5_GROUP_CH)
        z = jnp.transpose(z, (3, 1, 0, 2, 4))
        z = jnp.pad(z, ((0, 0), (0, 0), (0, bpad - nb), (0, 0), (0, 0)))
        return z.reshape(S5_GROUPS, -1, S5_IN)
    return jnp.concatenate([one(zu[:T_PROMPT], BATCH, SEQ, BATCH),
                            one(zu[T_PROMPT:], DEC_BATCH, DEC_SEQ, SAMPLE_BPAD)], axis=1)


def _s5_from_chunks(y):
    def one(y, nb, seq, bpad):
        y = y.reshape(S5_GROUPS, seq // S5_CHUNK, bpad, S5_CHUNK, S5_GROUP_CH)[:, :, :nb]
        return jnp.transpose(y, (2, 1, 3, 0, 4)).reshape(nb * seq, S5_W)
    return jnp.concatenate([one(y[:, :R_PROMPT], BATCH, SEQ, BATCH),
                            one(y[:, R_PROMPT:], DEC_BATCH, DEC_SEQ, SAMPLE_BPAD)], axis=0)


def _mix_kernel(x_ref, mod_ref, fo_ref, mo_ref, ys_ref, zu_ref, d_ref, wglu_ref,
                wof_ref, wom_ref, wos_ref, nw_ref, x1_ref, xn_ref):
    mod = mod_ref[0]
    g1 = mod[:, 2 * D_MODEL:3 * D_MODEL]
    sh2 = mod[:, 3 * D_MODEL:4 * D_MODEL]
    sc2 = mod[:, 4 * D_MODEL:5 * D_MODEL]
    y = ys_ref[...] + d_ref[...] * zu_ref[...].astype(F32)
    y = jax.nn.gelu(y).astype(BF16)
    gg = jnp.dot(y, wglu_ref[...], preferred_element_type=F32)
    s_out = (gg[:, :S5_W] * jax.nn.sigmoid(gg[:, S5_W:])).astype(BF16)
    mix = (jnp.dot(fo_ref[...], wof_ref[...], preferred_element_type=F32)
           + jnp.dot(mo_ref[...], wom_ref[...], preferred_element_type=F32)
           + jnp.dot(s_out, wos_ref[...], preferred_element_type=F32))
    x1 = x_ref[...] + g1 * mix
    x1_ref[...] = x1
    xn_ref[...] = (_rms(x1, nw_ref[...]) * (1.0 + sc2) + sh2).astype(BF16)


def _mix(x, mods, fo, mo, ys, zu, d, wglu, wof, wom, wos, nw):
    tok = lambda w_: pl.BlockSpec((TM, w_), lambda i: (i, 0))
    full = lambda a: pl.BlockSpec(a.shape, lambda i: (0,) * a.ndim)
    return pl.pallas_call(
        _mix_kernel,
        grid=(T_ALL // TM,),
        in_specs=[tok(D_MODEL),
                  pl.BlockSpec((1, 1, 6 * D_MODEL), lambda i: (_mod_index(i), 0, 0)),
                  tok(FOURIER_W), tok(HEADS_W), tok(S5_W), tok(S5_W),
                  full(d), full(wglu), full(wof), full(wom), full(wos), full(nw)],
        out_specs=[tok(D_MODEL), tok(D_MODEL)],
        out_shape=[jax.ShapeDtypeStruct((T_ALL, D_MODEL), F32),
                   jax.ShapeDtypeStruct((T_ALL, D_MODEL), BF16)],
        compiler_params=_cparams(("arbitrary",)),
        name="mix_out",
    )(x, mods, fo, mo, ys, zu, d, wglu, wof, wom, wos, nw)


def _ffn_kernel(final, xn_ref, x1_ref, mod_ref, wg_ref, wu_ref, wd_ref, nf_ref, o_ref, acc_ref):
    j = pl.program_id(1)
    xn = xn_ref[...]
    a = jnp.dot(xn, wg_ref[...], preferred_element_type=F32)
    u = jnp.dot(xn, wu_ref[...], preferred_element_type=F32)
    h = (a * jax.nn.sigmoid(a) * u).astype(BF16)
    part = jnp.dot(h, wd_ref[...], preferred_element_type=F32)

    @pl.when(j == 0)
    def _():
        acc_ref[...] = part

    @pl.when(j > 0)
    def _():
        acc_ref[...] += part

    @pl.when(j == pl.num_programs(1) - 1)
    def _():
        g2 = mod_ref[0][:, 5 * D_MODEL:6 * D_MODEL]
        x2 = x1_ref[...] + g2 * acc_ref[...]
        o_ref[...] = _rms(x2, nf_ref[...]) if final else x2


def _ffn(final, xn, x1, mods, wg, wu, wd, nf):
    tok = pl.BlockSpec((TM, D_MODEL), lambda i, j: (i, 0))
    return pl.pallas_call(
        functools.partial(_ffn_kernel, final),
        grid=(T_ALL // TM, D_FF // TF),
        in_specs=[tok, tok,
                  pl.BlockSpec((1, 1, 6 * D_MODEL), lambda i, j: (_mod_index(i), 0, 0)),
                  pl.BlockSpec((D_MODEL, TF), lambda i, j: (0, j)),
                  pl.BlockSpec((D_MODEL, TF), lambda i, j: (0, j)),
                  pl.BlockSpec((TF, D_MODEL), lambda i, j: (j, 0)),
                  pl.BlockSpec((1, D_MODEL), lambda i, j: (0, 0))],
        out_specs=tok,
        out_shape=jax.ShapeDtypeStruct((T_ALL, D_MODEL), F32),
        scratch_shapes=[pltpu.VMEM((TM, D_MODEL), F32)],
        compiler_params=_cparams(("arbitrary", "arbitrary")),
        name="ffn",
    )(xn, x1, mods, wg, wu, wd, nf)


def _pad_heads(a, axis):
    shape = a.shape[:axis] + (MLSTM_HEADS, MLSTM_DH) + a.shape[axis + 1:]
    pad = [(0, 0)] * (a.ndim + 1)
    pad[axis + 1] = (0, HEAD_PAD - MLSTM_DH)
    return jnp.pad(a.reshape(shape), pad).reshape(a.shape[:axis] + (HEADS_W,) + a.shape[axis + 1:])


def _mlstm_state_in(c, n):
    cn = jnp.concatenate([c, n[..., None]], axis=-1)
    return jnp.pad(cn, ((0, 0),) * 3 + ((0, HEAD_PAD - MLSTM_DH), (0, HEAD_PAD - MLSTM_DH - 1)))


def kernel(x_prompt, x_sample, state_mlstm_C, state_mlstm_n, state_mlstm_m, state_s5_re, state_s5_im,
           c, c_ctx, w_ada, b_ada, norm1_w, norm2_w, w_in, b_gates, w_fourier, mlstm_norm_w,
           s5_lambda_re, s5_lambda_im, s5_log_step, s5_b_re, s5_b_im, s5_c_re, s5_c_im, s5_d,
           w_glu, w_out, w_gate, w_up, w_down, norm_f):
    x = jnp.concatenate([x_prompt.reshape(T_PROMPT, D_MODEL), x_sample.reshape(T_SAMPLE, D_MODEL)], axis=0)
    cc = jnp.concatenate([c_ctx[None], c, jnp.zeros((N_MODS - 1 - DEC_BATCH, D_MODEL), F32)], axis=0)
    mods_all = _ada(cc, w_ada, b_ada).reshape(DEPTH, N_MODS, 1, 6 * D_MODEL)
    cdsd, cs, ab = (jnp.asarray(a.astype(np.float32)).astype(BF16) for a in _dft_consts())

    o_q = FOURIER_W
    o_g = o_q + 3 * MLSTM_W
    o_o = o_g + N_GATES
    o_u = o_o + MLSTM_W

    zeros_c = jnp.zeros((BATCH, 2, MLSTM_HEADS, HEAD_PAD, HEAD_PAD), F32)
    zeros_m = jnp.zeros((BATCH, 2, MLSTM_HEADS), F32)
    new_c, new_n, new_m, new_re, new_im = [], [], [], [], []
    for l in range(DEPTH):
        mods = mods_all[l]
        wl = w_in[l]
        w_cat = jnp.concatenate(
            [_pad_heads(wl[:, o_q + i * MLSTM_W:o_q + (i + 1) * MLSTM_W], 1) for i in range(3)]
            + [_pad_heads(wl[:, o_o:o_u], 1), wl[:, :FOURIER_W], wl[:, o_u:]], axis=1).astype(BF16)
        wg_t = wl[:, o_g:o_o].T.astype(BF16)
        zq, zk, zv, zo, zf, zu, gt = _in_proj(x, mods, norm1_w[l][None], w_cat, wg_t,
                                              b_gates[l][:, None])

        wf = w_fourier[l].astype(BF16)
        fo = _fourier_prompt(zf, cdsd, cs, wf)
        fo = _fourier_sample(zf, cdsd, ab, wf, fo)

        g4 = jnp.transpose(gt.reshape(4, MLSTM_HEADS, T_ALL // MLSTM_CHUNK, MLSTM_CHUNK), (1, 2, 0, 3))
        nw = _pad_heads(mlstm_norm_w[l], 0).reshape(MLSTM_HEADS, 1, HEAD_PAD)
        mo, cfin, mfin = _mlstm(SEQ, BATCH, 0, zq, zk, zv, zo, g4, zeros_c, zeros_m, nw, None)
        c0 = _mlstm_state_in(state_mlstm_C[:, l], state_mlstm_n[:, l])
        mo, _, _ = _mlstm(DEC_SEQ, DEC_BATCH, T_PROMPT // DEC_SEQ, zq, zk, zv, zo, g4, c0,
                          state_mlstm_m[:, l], nw, mo)
        new_c.append(cfin[..., :MLSTM_DH, :MLSTM_DH])
        new_n.append(cfin[..., :MLSTM_DH, N_AUG])
        new_m.append(mfin[..., 0, 0])

        ops = _s5_prep(s5_lambda_re[l], s5_lambda_im[l], s5_log_step[l],
                       s5_b_re[l], s5_b_im[l], s5_c_re[l], s5_c_im[l])
        x0 = jnp.concatenate([state_s5_re[:, l], state_s5_im[:, l]], axis=-1)
        x0 = jnp.transpose(x0, (1, 0, 2, 3)).reshape(2, DEC_BATCH, S5_GROUPS * S5_ST)
        x0 = jnp.pad(x0, ((0, 0), (0, SAMPLE_BPAD - DEC_BATCH), (0, 0)))
        ys, xfin = _s5(_s5_to_chunks(zu), *ops, x0)
        ys = _s5_from_chunks(ys)
        xfin = xfin.reshape(2, BATCH, S5_GROUPS, 2, S5_STATE)
        new_re.append(jnp.transpose(xfin[:, :, :, 0], (1, 0, 2, 3)))
        new_im.append(jnp.transpose(xfin[:, :, :, 1], (1, 0, 2, 3)))

        wo = w_out[l]
        wom = jnp.pad(wo[FOURIER_W:FOURIER_W + MLSTM_W].reshape(MLSTM_HEADS, MLSTM_DH, D_MODEL),
                      ((0, 0), (0, HEAD_PAD - MLSTM_DH), (0, 0))).reshape(HEADS_W, D_MODEL)
        x1, xn2 = _mix(x, mods, fo, mo, ys, zu, s5_d[l].reshape(1, S5_W), w_glu[l].astype(BF16),
                       wo[:FOURIER_W].astype(BF16), wom.astype(BF16),
                       wo[FOURIER_W + MLSTM_W:].astype(BF16), norm2_w[l][None])
        x = _ffn(l == DEPTH - 1, xn2, x1, mods, w_gate[l].astype(BF16), w_up[l].astype(BF16),
                 w_down[l].astype(BF16), norm_f[None])

    y_prompt = x[:T_PROMPT].reshape(BATCH, SEQ, D_MODEL)
    y_sample = x[T_PROMPT:].reshape(DEC_BATCH, DEC_SEQ, D_MODEL)
    stack = lambda parts: jnp.stack(parts, axis=1)
    return (y_prompt, y_sample, stack(new_c), stack(new_n), stack(new_m), stack(new_re), stack(new_im))
```

```python
import collections
import functools
import math

import numpy as np
import jax
import jax.numpy as jnp
from jax import lax
from jax.experimental import pallas as pl
from jax.experimental.pallas import tpu as pltpu

F32 = jnp.float32
BF16 = jnp.bfloat16

D_MODEL = 1024
BATCH = 32
SEQ = 256
DEPTH = 2
DEC_BATCH = 4
DEC_SEQ = 2048
GRID_W = 64
FOURIER_W = 256
FOURIER_DH = 64
MLSTM_W = 384
MLSTM_HEADS = 4
MLSTM_DH = 96
S5_W = 384
S5_GROUP_CH = 16
S5_GROUPS = 24
S5_STATE = 64
N_GATES = 16
D_FF = 2816
EPS = 1e-6

LANES = 128
SUBLANES = 8
VMEM_LIMIT = 56 * 1024 * 1024

HEAD_PAD = LANES
HEADS_W = MLSTM_HEADS * HEAD_PAD
N_AUG = MLSTM_DH
Z_W = 4 * HEADS_W + FOURIER_W + S5_W
MLSTM_CHUNK = 256
S5_CHUNK = 16
S5_IN = S5_CHUNK * S5_GROUP_CH
S5_ST = 2 * S5_STATE
S5_GB = LANES // S5_GROUP_CH
S5_NB = S5_W // LANES
S5_BW = S5_GB * S5_ST
S5_ROW = S5_CHUNK * LANES
TM = 512
TF = D_FF // 2
N_MODS = 8
NEG = -1e30

Pass = collections.namedtuple("Pass", "nb seq mod_row")
PROMPT = Pass(BATCH, SEQ, lambda i: 0)
SAMPLE = Pass(DEC_BATCH, DEC_SEQ, lambda i: 1 + i // (DEC_SEQ // TM))
T_PASS = BATCH * SEQ
assert T_PASS == DEC_BATCH * DEC_SEQ

_NT = (((1,), (1,)), ((), ()))
_TN = (((0,), (0,)), ((), ()))


def _cparams(sem):
    return pltpu.CompilerParams(dimension_semantics=sem, vmem_limit_bytes=VMEM_LIMIT)


def _full(a):
    return pl.BlockSpec(a.shape, lambda *_: (0,) * a.ndim)


def _rms(x, w):
    return x * lax.rsqrt(jnp.mean(x * x, axis=-1, keepdims=True) + EPS) * w


def _log_sigmoid(x):
    return jnp.minimum(x, 0.0) - jnp.log1p(jnp.exp(-jnp.abs(x)))


def _chunk_spec(p):
    if p.seq <= TM:
        return pl.BlockSpec((S5_NB, p.seq // S5_CHUNK, TM // p.seq, S5_CHUNK, LANES),
                            lambda i: (0, 0, i, 0, 0))
    per_seq = p.seq // TM
    return pl.BlockSpec((S5_NB, TM // S5_CHUNK, None, S5_CHUNK, LANES),
                        lambda i: (0, i % per_seq, i // per_seq, 0, 0))


def _chunk_store(p, ref, bl, val):
    if p.seq <= TM:
        cs = p.seq // S5_CHUNK
        for b in range(TM // p.seq):
            ref[bl, :, b] = val[b * p.seq:(b + 1) * p.seq].reshape(cs, S5_CHUNK, LANES)
    else:
        ref[bl] = val.reshape(TM // S5_CHUNK, S5_CHUNK, LANES)


def _chunk_load(p, ref, bl):
    if p.seq <= TM:
        return jnp.concatenate([ref[bl, :, b].reshape(p.seq, LANES) for b in range(TM // p.seq)], axis=0)
    return ref[bl].reshape(TM, LANES)


def _ada_kernel(c_ref, w_ref, b_ref, o_ref):
    a = c_ref[...]
    a = (a * jax.nn.sigmoid(a)).astype(BF16)
    o_ref[0] = jnp.dot(a, w_ref[0].astype(BF16), preferred_element_type=F32) + b_ref[0]


def _ada(cc, w_ada, b_ada):
    tn = 512
    return pl.pallas_call(
        _ada_kernel,
        grid=(DEPTH, 6 * D_MODEL // tn),
        in_specs=[pl.BlockSpec((N_MODS, D_MODEL), lambda l, j: (0, 0)),
                  pl.BlockSpec((1, D_MODEL, tn), lambda l, j: (l, 0, j)),
                  pl.BlockSpec((1, 1, tn), lambda l, j: (l, 0, j))],
        out_specs=pl.BlockSpec((1, N_MODS, tn), lambda l, j: (l, 0, j)),
        out_shape=jax.ShapeDtypeStruct((DEPTH, N_MODS, 6 * D_MODEL), F32),
        compiler_params=_cparams(("arbitrary", "arbitrary")),
        name="ada_mod",
    )(cc, w_ada, b_ada.reshape(DEPTH, 1, 6 * D_MODEL))


def _in_kernel(p, x_ref, mod_ref, nw_ref, w_ref, wg_ref, bg_ref,
               zq_ref, zk_ref, zv_ref, zo_ref, zf_ref, zu_ref, gt_ref):
    x = x_ref[...]
    mod = mod_ref[0]
    sh = mod[:, 0:D_MODEL]
    sc = mod[:, D_MODEL:2 * D_MODEL]
    xn = (_rms(x, nw_ref[...]) * (1.0 + sc) + sh).astype(BF16)
    z = jnp.dot(xn, w_ref[...], preferred_element_type=F32)
    o = 0
    for ref, w in ((zq_ref, HEADS_W), (zk_ref, HEADS_W), (zv_ref, HEADS_W), (zo_ref, HEADS_W),
                   (zf_ref, FOURIER_W)):
        ref[...] = z[:, o:o + w].astype(BF16)
        o += w
    for bl in range(S5_NB):
        _chunk_store(p, zu_ref, bl, z[:, o + bl * LANES:o + (bl + 1) * LANES])
    gt_ref[...] = lax.dot_general(wg_ref[...], xn, _NT, preferred_element_type=F32) + bg_ref[...]


def _in_proj(p, x, mods, nw, w, wg_t, bg):
    tok = lambda w_: pl.BlockSpec((TM, w_), lambda i: (i, 0))
    outs = [HEADS_W] * 4 + [FOURIER_W]
    return pl.pallas_call(
        functools.partial(_in_kernel, p),
        grid=(T_PASS // TM,),
        in_specs=[tok(D_MODEL),
                  pl.BlockSpec((1, 1, 6 * D_MODEL), lambda i: (p.mod_row(i), 0, 0)),
                  _full(nw), _full(w), _full(wg_t), _full(bg)],
        out_specs=[tok(w_) for w_ in outs] + [_chunk_spec(p), pl.BlockSpec((N_GATES, TM), lambda i: (0, i))],
        out_shape=[jax.ShapeDtypeStruct((T_PASS, w_), BF16) for w_ in outs]
        + [jax.ShapeDtypeStruct((S5_NB, p.seq // S5_CHUNK, p.nb, S5_CHUNK, LANES), F32),
           jax.ShapeDtypeStruct((N_GATES, T_PASS), F32)],
        compiler_params=_cparams(("arbitrary",)),
        name="in_proj_%d" % p.seq,
    )(x, mods, nw, w, wg_t, bg)


def _dft_consts():
    d = np.arange(FOURIER_DH)
    phi = 2.0 * np.pi * ((d[:, None] * d[None, :]) % FOURIER_DH) / FOURIER_DH
    eye = np.eye(FOURIER_W // FOURIER_DH)
    cd = np.kron(eye, np.cos(phi)) / math.sqrt(FOURIER_DH)
    sd = np.kron(eye, np.sin(phi)) / math.sqrt(FOURIER_DH)
    s = np.arange(SEQ)
    th = 2.0 * np.pi * ((s[:, None] * s[None, :]) % SEQ) / SEQ
    rows = DEC_SEQ // GRID_W
    pos = np.arange(DEC_SEQ)
    r, c = pos // GRID_W, pos % GRID_W
    ph = ((r[:, None] * r[None, :]) * (GRID_W // rows) + c[:, None] * c[None, :]) % GRID_W
    th2 = 2.0 * np.pi * ph / GRID_W
    return (np.concatenate([cd, sd], axis=1),
            np.concatenate([np.cos(th), -np.sin(th)], axis=1) / math.sqrt(SEQ),
            np.concatenate([np.cos(th2), -np.sin(th2)], axis=1) / math.sqrt(DEC_SEQ))


def _fourier_prompt_kernel(nb, zf_ref, cdsd_ref, cs_ref, wf_ref, o_ref):
    t = jnp.dot(zf_ref[...], cdsd_ref[...], preferred_element_type=F32).astype(BF16)
    for b in range(nb):
        tb = t[b * SEQ:(b + 1) * SEQ]
        st = jnp.concatenate([tb[:, :FOURIER_W], tb[:, FOURIER_W:]], axis=0)
        f = jnp.dot(cs_ref[...], st, preferred_element_type=F32)
        o_ref[b * SEQ:(b + 1) * SEQ, :] = jnp.dot(
            f.astype(BF16), wf_ref[...], preferred_element_type=F32).astype(BF16)


def _fourier_prompt(zf, cdsd, cs, wf):
    nb = 4
    return pl.pallas_call(
        functools.partial(_fourier_prompt_kernel, nb),
        grid=(BATCH // nb,),
        in_specs=[pl.BlockSpec((nb * SEQ, FOURIER_W), lambda i: (i, 0)), _full(cdsd), _full(cs), _full(wf)],
        out_specs=pl.BlockSpec((nb * SEQ, FOURIER_W), lambda i: (i, 0)),
        out_shape=jax.ShapeDtypeStruct((T_PASS, FOURIER_W), BF16),
        compiler_params=_cparams(("arbitrary",)),
        name="fourier_prompt",
    )(zf, cdsd, cs, wf)


def _fourier_sample_kernel(zf_ref, cdsd_ref, ab_ref, wf_ref, o_ref, tt_ref):
    @pl.when(pl.program_id(0) == 0)
    def _():
        for b in range(DEC_BATCH):
            t = jnp.dot(zf_ref[b * DEC_SEQ:(b + 1) * DEC_SEQ, :], cdsd_ref[...],
                        preferred_element_type=F32).astype(BF16)
            tt_ref[b, 0:DEC_SEQ, :] = t[:, :FOURIER_W]
            tt_ref[b, DEC_SEQ:2 * DEC_SEQ, :] = t[:, FOURIER_W:]

    for b in range(DEC_BATCH):
        f = jnp.dot(ab_ref[...], tt_ref[b], preferred_element_type=F32)
        o_ref[b] = jnp.dot(f.astype(BF16), wf_ref[...], preferred_element_type=F32).astype(BF16)


def _fourier_sample(zf, cdsd, ab, wf):
    tk = 512
    out = pl.pallas_call(
        _fourier_sample_kernel,
        grid=(DEC_SEQ // tk,),
        in_specs=[_full(zf), _full(cdsd), pl.BlockSpec((tk, 2 * DEC_SEQ), lambda i: (i, 0)), _full(wf)],
        out_specs=pl.BlockSpec((DEC_BATCH, tk, FOURIER_W), lambda i: (0, i, 0)),
        out_shape=jax.ShapeDtypeStruct((DEC_BATCH, DEC_SEQ, FOURIER_W), BF16),
        scratch_shapes=[pltpu.VMEM((DEC_BATCH, 2 * DEC_SEQ, FOURIER_W), BF16)],
        compiler_params=_cparams(("arbitrary",)),
        name="fourier_sample",
    )(zf, cdsd, ab, wf)
    return out.reshape(T_PASS, FOURIER_W)


def _mlstm_chunk(q, k, v, ig_row, fg_row, c, m, fwd):
    L = q.shape[0]
    row = lax.broadcasted_iota(jnp.int32, (L, L), 0)
    col = lax.broadcasted_iota(jnp.int32, (L, L), 1)
    mask = (col <= row) if fwd else (col >= row)
    tri = mask.astype(F32)
    eye = (row == col).astype(F32)
    lf_row = _log_sigmoid(fg_row)
    rows8 = jnp.concatenate([lf_row, ig_row, jnp.zeros((SUBLANES - 2, L), F32)], axis=0)
    cols = lax.dot_general(jnp.concatenate([tri, eye], axis=0), rows8, _NT,
                           precision=lax.Precision.HIGHEST, preferred_element_type=F32)
    b_col = cols[:L, 0:1]
    ig_col = cols[L:, 1:2]
    b_row = lax.dot_general(rows8, tri, _NT, precision=lax.Precision.HIGHEST,
                            preferred_element_type=F32)[0:1]

    log_d = jnp.where(mask, b_col - b_row + ig_row, NEG)
    m_inter = b_col + m
    m_row = jnp.maximum(m_inter, jnp.max(log_d, axis=1, keepdims=True))
    d = jnp.where(mask, jnp.exp(log_d - m_row), 0.0)
    inter = jnp.exp(m_inter - m_row)
    scale = MLSTM_DH ** -0.5
    s = lax.dot_general(q, k, _NT, preferred_element_type=F32) * scale
    num = (jnp.dot((s * d).astype(BF16), v, preferred_element_type=F32)
           + inter * jnp.dot(q, c.astype(BF16), preferred_element_type=F32))
    den = num[:, N_AUG:N_AUG + 1]
    h = num / jnp.maximum(jnp.abs(den), jnp.exp(-m_row))
    last = L - 1 if fwd else 0
    m_new = m_row[last:last + 1, :]
    b_last = b_col[last:last + 1, :]
    w_col = jnp.exp(b_last - b_col + ig_col - m_new)
    decay = jnp.exp(b_last + m - m_new)
    kw = (k.astype(F32) * w_col).astype(BF16)
    c_new = decay * c + scale * lax.dot_general(kw, v, _TN, preferred_element_type=F32)
    return h, c_new, m_new


def _mlstm_kernel(nc, m0_ref, q_ref, k_ref, v_ref, zo_ref, g_ref, c0_ref, nw_ref,
                  o_ref, cfin_ref, mfin_ref, hbuf_ref):
    L = MLSTM_CHUNK
    b = pl.program_id(0)
    hd = pl.program_id(1)
    lane = lax.broadcasted_iota(jnp.int32, (L, HEAD_PAD), 1)

    def load(ci):
        sl = pl.ds(pl.multiple_of(ci * L, L), L)
        v = v_ref[sl, :]
        v = jnp.where(lane == N_AUG, jnp.ones_like(v), v)
        return sl, q_ref[sl, :], k_ref[sl, :], v, g_ref[0, ci]

    def fwd_step(ci, carry):
        c, m = carry
        sl, q, k, v, g = load(ci)
        h, c, m = _mlstm_chunk(q, k, v, g[0:1], g[1:2], c, m, True)
        hbuf_ref[sl, :] = h
        return c, m

    def bwd_step(i, carry):
        c, m = carry
        ci = nc - 1 - i
        sl, q, k, v, g = load(ci)
        h, c, m = _mlstm_chunk(q, k, v, g[2:3], g[3:4], c, m, False)
        h = jnp.where(lane < MLSTM_DH, h + hbuf_ref[sl, :], 0.0)
        ms = jnp.sum(h * h, axis=-1, keepdims=True) * (1.0 / MLSTM_DH)
        hn = h * lax.rsqrt(ms + EPS) * nw_ref[0]
        o_ref[sl, :] = (hn * jax.nn.sigmoid(zo_ref[sl, :].astype(F32))).astype(BF16)
        return c, m

    for dr, step in ((0, fwd_step), (1, bwd_step)):
        carry = (c0_ref[0, dr, 0], jnp.full((1, 1), m0_ref[b, dr, hd], F32))
        if nc == 1:
            c, m = step(0, carry)
        else:
            c, m = lax.fori_loop(0, nc, step, carry)
        cfin_ref[0, dr, 0] = c
        mfin_ref[0, dr, 0] = jnp.broadcast_to(m, (SUBLANES, LANES))


def _mlstm(p, q, k, v, zo, g4, c0, m0, nw):
    nc = p.seq // MLSTM_CHUNK
    tok = pl.BlockSpec((p.seq, HEAD_PAD), lambda b, h: (b, h))
    st = pl.BlockSpec((1, 2, 1, HEAD_PAD, HEAD_PAD), lambda b, h: (b, 0, h, 0, 0))
    return pl.pallas_call(
        functools.partial(_mlstm_kernel, nc),
        grid=(p.nb, MLSTM_HEADS),
        in_specs=[pl.BlockSpec(memory_space=pltpu.SMEM), tok, tok, tok, tok,
                  pl.BlockSpec((1, nc, 4, MLSTM_CHUNK), lambda b, h: (h, b, 0, 0)),
                  st, pl.BlockSpec((1, 1, HEAD_PAD), lambda b, h: (h, 0, 0))],
        out_specs=[tok, st,
                   pl.BlockSpec((1, 2, 1, SUBLANES, LANES), lambda b, h: (b, 0, h, 0, 0))],
        out_shape=[jax.ShapeDtypeStruct((T_PASS, HEADS_W), BF16),
                   jax.ShapeDtypeStruct((p.nb, 2, MLSTM_HEADS, HEAD_PAD, HEAD_PAD), F32),
                   jax.ShapeDtypeStruct((p.nb, 2, MLSTM_HEADS, SUBLANES, LANES), F32)],
        scratch_shapes=[pltpu.VMEM((p.seq, HEAD_PAD), F32)],
        compiler_params=_cparams(("arbitrary", "arbitrary")),
        name="mlstm_%d" % p.seq,
    )(m0, q, k, v, zo, g4, c0, nw)


def _cpow(br, bi, e, nbits):
    pr = pi = None
    for bit in range(nbits):
        sel = ((e >> bit) & 1) == 1
        if pr is None:
            pr, pi = jnp.where(sel, br, 1.0), jnp.where(sel, bi, 0.0)
        else:
            pr, pi = jnp.where(sel, pr * br - pi * bi, pr), jnp.where(sel, pr * bi + pi * br, pi)
        if bit + 1 < nbits:
            br, bi = br * br - bi * bi, 2.0 * br * bi
    return pr, pi


def _s5_prep_kernel(lamc_re_ref, lamc_im_ref, lamr_re_ref, lamr_im_ref, lstep_ref,
                    bt_re_ref, bt_im_ref, ct_re_ref, ct_im_ref,
                    t_ref, m_ref, n_ref, a_ref):
    C = S5_CHUNK
    nbits = C.bit_length()
    kk = lax.broadcasted_iota(jnp.int32, (S5_ST, S5_IN), 1) >> 4
    top = lax.broadcasted_iota(jnp.int32, (S5_ST, S5_IN), 0) < S5_STATE
    srow = lax.broadcasted_iota(jnp.int32, (S5_IN, S5_ST), 0) >> 4
    left = lax.broadcasted_iota(jnp.int32, (S5_IN, S5_ST), 1) < S5_STATE
    left16 = lax.broadcasted_iota(jnp.int32, (S5_GROUP_CH, S5_ST), 1) < S5_STATE
    left1 = lax.broadcasted_iota(jnp.int32, (1, S5_ST), 1) < S5_STATE
    lane = lax.broadcasted_iota(jnp.int32, (S5_GROUP_CH, S5_IN), 1)
    ct_re = ct_re_ref[0, 0]
    ct_im = ct_im_ref[0, 0]
    resp = []
    for d in range(2):
        step = jnp.exp(lstep_ref[0, d, 0])
        lr_c, li_c = lamc_re_ref[0, d, 0] * step, lamc_im_ref[0, d, 0] * step
        lr_r, li_r = lamr_re_ref[0, d, 0], lamr_im_ref[0, d, 0]
        lbc_re, lbc_im = jnp.exp(lr_c) * jnp.cos(li_c), jnp.exp(lr_c) * jnp.sin(li_c)
        mag = jnp.exp(lr_r * step)
        lb_re, lb_im = mag * jnp.cos(li_r * step), mag * jnp.sin(li_r * step)

        def cp(lag):
            pr, pi = _cpow(lbc_re, lbc_im, lag, nbits)
            return ct_re * pr - ct_im * pi, ct_re * pi + ct_im * pr

        nr, ni = lb_re - 1.0, lb_im
        den = lr_r * lr_r + li_r * li_r
        kap_re = (nr * lr_r + ni * li_r) / den
        kap_im = (ni * lr_r - nr * li_r) / den
        bb_re = kap_re * bt_re_ref[0, 0] - kap_im * bt_im_ref[0, 0]
        bb_im = kap_re * bt_im_ref[0, 0] + kap_im * bt_re_ref[0, 0]

        cpr, cpi = cp(kk if d == 0 else (C - 1) - kk)
        resp.append(jnp.dot(jnp.where(left16, bb_re, -bb_im), jnp.where(top, cpr, cpi),
                            precision=lax.Precision.HIGHEST, preferred_element_type=F32))

        cpr, cpi = cp(kk + 1 if d == 0 else C - kk)
        m_ref[0, d, 0] = jnp.where(top, cpr, -cpi).astype(BF16)

        pr, pi = _cpow(lb_re, lb_im, (C - 1) - srow if d == 0 else srow, nbits)
        bt_r = jnp.concatenate([bb_re] * C, axis=0)
        bt_i = jnp.concatenate([bb_im] * C, axis=0)
        n_ref[0, d, 0] = jnp.where(left, pr * bt_r - pi * bt_i, pr * bt_i + pi * bt_r).astype(BF16)

        ar, ai = _cpow(lb_re, lb_im, jnp.full((1, S5_ST), C, jnp.int32), nbits)
        a_ref[0, d, 0] = jnp.concatenate([ar, jnp.where(left1, -ai, ai)], axis=0)

    rf, rb = resp
    for s in range(C):
        nf = S5_GROUP_CH * s
        blk = jnp.where(lane >= nf, pltpu.roll(rf, nf, 1) if nf else rf, 0.0)
        nb = S5_GROUP_CH * (C - 1 - s)
        blk = blk + jnp.where(lane < S5_IN - nb, pltpu.roll(rb, S5_IN - nb, 1) if nb else rb, 0.0)
        t_ref[0, 0, S5_GROUP_CH * s:S5_GROUP_CH * (s + 1), :] = blk.astype(BF16)


def _s5_prep(lam_re, lam_im, log_step, b_re, b_im, c_re, c_im):
    G = S5_GROUPS
    dup = lambda a: jnp.concatenate([a, a], axis=-1)
    lamc = [dup(a).reshape(DEPTH, 2, G, S5_ST, 1) for a in (lam_re, lam_im)]
    lamr = [dup(a).reshape(DEPTH, 2, G, 1, S5_ST) for a in (lam_re, lam_im)]
    lstep = log_step.reshape(DEPTH, 2, G, 1, 1)
    bt = [dup(jnp.swapaxes(a, 2, 3)) for a in (b_re, b_im)]
    ct = [jnp.tile(jnp.swapaxes(a, 2, 3), (1, 1, 2, S5_CHUNK)) for a in (c_re, c_im)]
    dspec = lambda r, c: pl.BlockSpec((1, 2, 1, r, c), lambda l, g: (l, 0, g, 0, 0))
    gspec = lambda r, c: pl.BlockSpec((1, 1, r, c), lambda l, g: (l, g, 0, 0))
    return pl.pallas_call(
        _s5_prep_kernel,
        grid=(DEPTH, G),
        in_specs=[dspec(S5_ST, 1), dspec(S5_ST, 1), dspec(1, S5_ST), dspec(1, S5_ST), dspec(1, 1),
                  gspec(S5_GROUP_CH, S5_ST), gspec(S5_GROUP_CH, S5_ST),
                  gspec(S5_ST, S5_IN), gspec(S5_ST, S5_IN)],
        out_specs=[gspec(S5_IN, S5_IN), dspec(S5_ST, S5_IN), dspec(S5_IN, S5_ST), dspec(2, S5_ST)],
        out_shape=[jax.ShapeDtypeStruct((DEPTH, G, S5_IN, S5_IN), BF16),
                   jax.ShapeDtypeStruct((DEPTH, 2, G, S5_ST, S5_IN), BF16),
                   jax.ShapeDtypeStruct((DEPTH, 2, G, S5_IN, S5_ST), BF16),
                   jax.ShapeDtypeStruct((DEPTH, 2, G, 2, S5_ST), F32)],
        compiler_params=_cparams(("arbitrary", "arbitrary")),
        name="s5_prep",
    )(*lamc, *lamr, lstep, *bt, *ct)


def _s5_kernel(nb, nchunks, zr_ref, t_ref, m_ref, n_ref, a_ref, x0_ref,
               y_ref, xfin_ref, u_ref, v_ref, xp_ref, yb_ref):
    R = nb * nchunks
    RT = 128
    W = S5_BW

    def gather(i, _):
        rows = pl.ds(pl.multiple_of(i * RT, RT), RT)
        xs = [zr_ref[0, rows, s * LANES:(s + 1) * LANES] for s in range(S5_CHUNK)]
        for gl in range(S5_GB):
            lo = gl * S5_GROUP_CH
            u_ref[gl, rows, :] = jnp.concatenate(
                [x[:, lo:lo + S5_GROUP_CH] for x in xs], axis=1).astype(BF16)
        return 0

    lax.fori_loop(0, R // RT, gather, 0)

    for gl in range(S5_GB):
        for d in range(2):
            v_ref[d, :, gl * S5_ST:(gl + 1) * S5_ST] = jnp.dot(
                u_ref[gl], n_ref[d, gl], preferred_element_type=F32)

    im_half = (lax.broadcasted_iota(jnp.int32, (1, W), 1) & (S5_ST - 1)) >= S5_STATE
    per = max(1, SUBLANES // nb)
    srows = nb * per

    def scan(d):
        a = a_ref[d, 0:1, :]
        a2 = a_ref[d, 1:2, :]

        def advance(x, v):
            swapped = jnp.where(im_half, pltpu.roll(x, S5_STATE, 1), pltpu.roll(x, W - S5_STATE, 1))
            return a * x + a2 * swapped + v

        def step(i, x):
            si = i if d == 0 else nchunks // per - 1 - i
            rows = pl.ds(pl.multiple_of(si * srows, srows), srows)
            v = v_ref[d, rows, :]
            order = range(per) if d == 0 else range(per - 1, -1, -1)
            entering = [None] * per
            for j in order:
                entering[j] = x
                x = advance(x, v[j * nb:(j + 1) * nb])
            xp_ref[d, rows, :] = entering[0] if per == 1 else jnp.concatenate(entering, axis=0)
            return x

        return lax.fori_loop(0, nchunks // per, step, x0_ref[d])

    for d in range(2):
        xfin_ref[d] = scan(d)

    for gl in range(S5_GB):
        cols = slice(gl * S5_ST, (gl + 1) * S5_ST)
        y = jnp.dot(u_ref[gl], t_ref[gl], preferred_element_type=F32)
        for d in range(2):
            y = y + jnp.dot(xp_ref[d, :, cols].astype(BF16), m_ref[d, gl], preferred_element_type=F32)
        yb_ref[gl] = y

    def scatter(i, _):
        rows = pl.ds(pl.multiple_of(i * RT, RT), RT)
        ys = [yb_ref[gl, rows, :] for gl in range(S5_GB)]
        for t in range(S5_CHUNK):
            lo = t * S5_GROUP_CH
            y_ref[0, rows, t * LANES:(t + 1) * LANES] = jnp.concatenate(
                [y[:, lo:lo + S5_GROUP_CH] for y in ys], axis=1)
        return 0

    lax.fori_loop(0, R // RT, scatter, 0)


def _s5(p, zu, t, m, n, a, x0):
    G = S5_GROUPS
    nchunks = p.seq // S5_CHUNK
    R = nchunks * p.nb
    zr = zu.reshape(S5_NB, R, S5_ROW)
    a = jnp.transpose(a, (0, 2, 1, 3)).reshape(2, 2, G * S5_ST)
    row = pl.BlockSpec((1, R, S5_ROW), lambda j: (j, 0, 0))
    dsp = lambda r, c: pl.BlockSpec((2, S5_GB, r, c), lambda j: (0, j, 0, 0))
    lsp = lambda r: pl.BlockSpec((2, r, S5_BW), lambda j: (0, 0, j))
    y, xfin = pl.pallas_call(
        functools.partial(_s5_kernel, p.nb, nchunks),
        grid=(S5_NB,),
        in_specs=[row, pl.BlockSpec((S5_GB, S5_IN, S5_IN), lambda j: (j, 0, 0)),
                  dsp(S5_ST, S5_IN), dsp(S5_IN, S5_ST), lsp(2), lsp(p.nb)],
        out_specs=[row, lsp(p.nb)],
        out_shape=[jax.ShapeDtypeStruct((S5_NB, R, S5_ROW), F32),
                   jax.ShapeDtypeStruct((2, p.nb, G * S5_ST), F32)],
        scratch_shapes=[pltpu.VMEM((S5_GB, R, S5_IN), BF16), pltpu.VMEM((2, R, S5_BW), F32),
                        pltpu.VMEM((2, R, S5_BW), F32), pltpu.VMEM((S5_GB, R, S5_IN), F32)],
        compiler_params=_cparams(("arbitrary",)),
        name="s5_scan_%d" % p.seq,
    )(zr, t, m, n, a, x0)
    return y.reshape(zu.shape), xfin


def _mix_kernel(p, x_ref, mod_ref, fo_ref, mo_ref, ys_ref, zu_ref, d_ref, wglu_ref,
                wof_ref, wom_ref, wos_ref, nw_ref, x1_ref, xn_ref):
    mod = mod_ref[0]
    g1 = mod[:, 2 * D_MODEL:3 * D_MODEL]
    sh2 = mod[:, 3 * D_MODEL:4 * D_MODEL]
    sc2 = mod[:, 4 * D_MODEL:5 * D_MODEL]
    ys = jnp.concatenate([_chunk_load(p, ys_ref, bl) for bl in range(S5_NB)], axis=1)
    zu = jnp.concatenate([_chunk_load(p, zu_ref, bl) for bl in range(S5_NB)], axis=1)
    y = jax.nn.gelu(ys + d_ref[...] * zu).astype(BF16)
    gg = jnp.dot(y, wglu_ref[...], preferred_element_type=F32)
    s_out = (gg[:, :S5_W] * jax.nn.sigmoid(gg[:, S5_W:])).astype(BF16)
    mix = (jnp.dot(fo_ref[...], wof_ref[...], preferred_element_type=F32)
           + jnp.dot(mo_ref[...], wom_ref[...], preferred_element_type=F32)
           + jnp.dot(s_out, wos_ref[...], preferred_element_type=F32))
    x1 = x_ref[...] + g1 * mix
    x1_ref[...] = x1
    xn_ref[...] = (_rms(x1, nw_ref[...]) * (1.0 + sc2) + sh2).astype(BF16)


def _mix(p, x, mods, fo, mo, ys, zu, d, wglu, wof, wom, wos, nw):
    tok = lambda w_: pl.BlockSpec((TM, w_), lambda i: (i, 0))
    return pl.pallas_call(
        functools.partial(_mix_kernel, p),
        grid=(T_PASS // TM,),
        in_specs=[tok(D_MODEL),
                  pl.BlockSpec((1, 1, 6 * D_MODEL), lambda i: (p.mod_row(i), 0, 0)),
                  tok(FOURIER_W), tok(HEADS_W), _chunk_spec(p), _chunk_spec(p),
                  _full(d), _full(wglu), _full(wof), _full(wom), _full(wos), _full(nw)],
        out_specs=[tok(D_MODEL), tok(D_MODEL)],
        out_shape=[jax.ShapeDtypeStruct((T_PASS, D_MODEL), F32),
                   jax.ShapeDtypeStruct((T_PASS, D_MODEL), BF16)],
        compiler_params=_cparams(("arbitrary",)),
        name="mix_out_%d" % p.seq,
    )(x, mods, fo, mo, ys, zu, d, wglu, wof, wom, wos, nw)


def _ffn_kernel(final, xn_ref, x1_ref, mod_ref, wg_ref, wu_ref, wd_ref, nf_ref, o_ref, acc_ref):
    j = pl.program_id(1)
    xn = xn_ref[...]
    a = jnp.dot(xn, wg_ref[...], preferred_element_type=F32)
    u = jnp.dot(xn, wu_ref[...], preferred_element_type=F32)
    h = (a * jax.nn.sigmoid(a) * u).astype(BF16)
    part = jnp.dot(h, wd_ref[...], preferred_element_type=F32)

    @pl.when(j == 0)
    def _():
        acc_ref[...] = part

    @pl.when(j > 0)
    def _():
        acc_ref[...] += part

    @pl.when(j == pl.num_programs(1) - 1)
    def _():
        g2 = mod_ref[0][:, 5 * D_MODEL:6 * D_MODEL]
        x2 = x1_ref[...] + g2 * acc_ref[...]
        o_ref[...] = _rms(x2, nf_ref[...]) if final else x2


def _ffn(p, final, xn, x1, mods, wg, wu, wd, nf):
    tok = pl.BlockSpec((TM, D_MODEL), lambda i, j: (i, 0))
    return pl.pallas_call(
        functools.partial(_ffn_kernel, final),
        grid=(T_PASS // TM, D_FF // TF),
        in_specs=[tok, tok,
                  pl.BlockSpec((1, 1, 6 * D_MODEL), lambda i, j: (p.mod_row(i), 0, 0)),
                  pl.BlockSpec((D_MODEL, TF), lambda i, j: (0, j)),
                  pl.BlockSpec((D_MODEL, TF), lambda i, j: (0, j)),
                  pl.BlockSpec((TF, D_MODEL), lambda i, j: (j, 0)),
                  pl.BlockSpec((1, D_MODEL), lambda i, j: (0, 0))],
        out_specs=tok,
        out_shape=jax.ShapeDtypeStruct((T_PASS, D_MODEL), F32),
        scratch_shapes=[pltpu.VMEM((TM, D_MODEL), F32)],
        compiler_params=_cparams(("arbitrary", "arbitrary")),
        name="ffn_%d" % p.seq,
    )(xn, x1, mods, wg, wu, wd, nf)


def _pad_heads(a, axis):
    shape = a.shape[:axis] + (MLSTM_HEADS, MLSTM_DH) + a.shape[axis + 1:]
    pad = [(0, 0)] * (a.ndim + 1)
    pad[axis + 1] = (0, HEAD_PAD - MLSTM_DH)
    return jnp.pad(a.reshape(shape), pad).reshape(a.shape[:axis] + (HEADS_W,) + a.shape[axis + 1:])


def _mlstm_state_in(c, n):
    cn = jnp.concatenate([c, n[..., None]], axis=-1)
    return jnp.pad(cn, ((0, 0),) * 3 + ((0, HEAD_PAD - MLSTM_DH), (0, HEAD_PAD - MLSTM_DH - 1)))


def kernel(x_prompt, x_sample, state_mlstm_C, state_mlstm_n, state_mlstm_m, state_s5_re, state_s5_im,
           c, c_ctx, w_ada, b_ada, norm1_w, norm2_w, w_in, b_gates, w_fourier, mlstm_norm_w,
           s5_lambda_re, s5_lambda_im, s5_log_step, s5_b_re, s5_b_im, s5_c_re, s5_c_im, s5_d,
           w_glu, w_out, w_gate, w_up, w_down, norm_f):
    xs = {PROMPT: x_prompt.reshape(T_PASS, D_MODEL), SAMPLE: x_sample.reshape(T_PASS, D_MODEL)}
    cc = jnp.concatenate([c_ctx[None], c, jnp.zeros((N_MODS - 1 - DEC_BATCH, D_MODEL), F32)], axis=0)
    mods_all = _ada(cc, w_ada, b_ada).reshape(DEPTH, N_MODS, 1, 6 * D_MODEL)
    cdsd, cs, ab = (jnp.asarray(a.astype(np.float32)).astype(BF16) for a in _dft_consts())
    s5_ops = _s5_prep(s5_lambda_re, s5_lambda_im, s5_log_step, s5_b_re, s5_b_im, s5_c_re, s5_c_im)

    o_q = FOURIER_W
    o_g = o_q + 3 * MLSTM_W
    o_o = o_g + N_GATES
    o_u = o_o + MLSTM_W

    new_c, new_n, new_m, new_re, new_im = [], [], [], [], []
    for l in range(DEPTH):
        mods = mods_all[l]
        wl = w_in[l]
        w_cat = jnp.concatenate(
            [_pad_heads(wl[:, o_q + i * MLSTM_W:o_q + (i + 1) * MLSTM_W], 1) for i in range(3)]
            + [_pad_heads(wl[:, o_o:o_u], 1), wl[:, :FOURIER_W], wl[:, o_u:]], axis=1).astype(BF16)
        wg_t = wl[:, o_g:o_o].T.astype(BF16)
        wf = w_fourier[l].astype(BF16)
        nw = _pad_heads(mlstm_norm_w[l], 0).reshape(MLSTM_HEADS, 1, HEAD_PAD)
        ops = [o[l] for o in s5_ops]
        wo = w_out[l]
        wom = jnp.pad(wo[FOURIER_W:FOURIER_W + MLSTM_W].reshape(MLSTM_HEADS, MLSTM_DH, D_MODEL),
                      ((0, 0), (0, HEAD_PAD - MLSTM_DH), (0, 0))).reshape(HEADS_W, D_MODEL).astype(BF16)
        wof = wo[:FOURIER_W].astype(BF16)
        wos = wo[FOURIER_W + MLSTM_W:].astype(BF16)
        wglu = w_glu[l].astype(BF16)
        wg, wu, wd = w_gate[l].astype(BF16), w_up[l].astype(BF16), w_down[l].astype(BF16)

        for p in (PROMPT, SAMPLE):
            x = xs[p]
            zq, zk, zv, zo, zf, zu, gt = _in_proj(p, x, mods, norm1_w[l][None], w_cat, wg_t,
                                                  b_gates[l][:, None])
            fo = _fourier_prompt(zf, cdsd, cs, wf) if p is PROMPT else _fourier_sample(zf, cdsd, ab, wf)

            g4 = jnp.transpose(gt.reshape(4, MLSTM_HEADS, T_PASS // MLSTM_CHUNK, MLSTM_CHUNK), (1, 2, 0, 3))
            if p is PROMPT:
                c0 = jnp.zeros((BATCH, 2, MLSTM_HEADS, HEAD_PAD, HEAD_PAD), F32)
                m0 = jnp.zeros((BATCH, 2, MLSTM_HEADS), F32)
                x0 = jnp.zeros((2, BATCH, S5_GROUPS * S5_ST), F32)
            else:
                c0 = _mlstm_state_in(state_mlstm_C[:, l], state_mlstm_n[:, l])
                m0 = state_mlstm_m[:, l]
                x0 = jnp.concatenate([state_s5_re[:, l], state_s5_im[:, l]], axis=-1)
                x0 = jnp.transpose(x0, (1, 0, 2, 3)).reshape(2, DEC_BATCH, S5_GROUPS * S5_ST)
            mo, cfin, mfin = _mlstm(p, zq, zk, zv, zo, g4, c0, m0, nw)

            ys, xfin = _s5(p, zu, *ops, x0)

            if p is PROMPT:
                new_c.append(cfin[..., :MLSTM_DH, :MLSTM_DH])
                new_n.append(cfin[..., :MLSTM_DH, N_AUG])
                new_m.append(mfin[..., 0, 0])
                xfin = xfin.reshape(2, BATCH, S5_GROUPS, 2, S5_STATE)
                new_re.append(jnp.transpose(xfin[:, :, :, 0], (1, 0, 2, 3)))
                new_im.append(jnp.transpose(xfin[:, :, :, 1], (1, 0, 2, 3)))

            x1, xn2 = _mix(p, x, mods, fo, mo, ys, zu, s5_d[l].reshape(1, S5_W), wglu,
                           wof, wom, wos, norm2_w[l][None])
            xs[p] = _ffn(p, l == DEPTH - 1, xn2, x1, mods, wg, wu, wd, norm_f[None])

    y_prompt = xs[PROMPT].reshape(BATCH, SEQ, D_MODEL)
    y_sample = xs[SAMPLE].reshape(DEC_BATCH, DEC_SEQ, D_MODEL)
    stack = lambda parts: jnp.stack(parts, axis=1)
    return (y_prompt, y_sample, stack(new_c), stack(new_n), stack(new_m), stack(new_re), stack(new_im))
```

```python
import collections
import functools
import math

import numpy as np
import jax
import jax.numpy as jnp
from jax import lax
from jax.experimental import pallas as pl
from jax.experimental.pallas import tpu as pltpu

F32 = jnp.float32
BF16 = jnp.bfloat16

D_MODEL = 1024
BATCH = 32
SEQ = 256
DEPTH = 2
DEC_BATCH = 4
DEC_SEQ = 2048
GRID_W = 64
FOURIER_W = 256
FOURIER_DH = 64
MLSTM_W = 384
MLSTM_HEADS = 4
MLSTM_DH = 96
S5_W = 384
S5_GROUP_CH = 16
S5_GROUPS = 24
S5_STATE = 64
N_GATES = 16
D_FF = 2816
EPS = 1e-6

LANES = 128
SUBLANES = 8
VMEM_LIMIT = 56 * 1024 * 1024

HEAD_PAD = LANES
HEADS_W = MLSTM_HEADS * HEAD_PAD
N_AUG = MLSTM_DH
Z_W = 3 * HEADS_W + FOURIER_W + S5_W
MLSTM_CHUNK = 256
S5_CHUNK = 16
S5_IN = S5_CHUNK * S5_GROUP_CH
S5_ST = 2 * S5_STATE
S5_GB = LANES // S5_GROUP_CH
S5_NB = S5_W // LANES
S5_BW = S5_GB * S5_ST
S5_ROW = S5_CHUNK * LANES
TM = 512
TF = D_FF // 2
N_MODS = 8
NEG = -1e30

Pass = collections.namedtuple("Pass", "nb seq mod_row")
PROMPT = Pass(BATCH, SEQ, lambda i: 0)
SAMPLE = Pass(DEC_BATCH, DEC_SEQ, lambda i: 1 + i // (DEC_SEQ // TM))
T_PASS = BATCH * SEQ
assert T_PASS == DEC_BATCH * DEC_SEQ

_NT = (((1,), (1,)), ((), ()))
_TN = (((0,), (0,)), ((), ()))


def _cparams(sem):
    return pltpu.CompilerParams(dimension_semantics=sem, vmem_limit_bytes=VMEM_LIMIT)


def _full(a):
    return pl.BlockSpec(a.shape, lambda *_: (0,) * a.ndim)


def _rms(x, w):
    return x * lax.rsqrt(jnp.mean(x * x, axis=-1, keepdims=True) + EPS) * w


def _log_sigmoid(x):
    return jnp.minimum(x, 0.0) - jnp.log1p(jnp.exp(-jnp.abs(x)))


def _chunk_spec(p):
    if p.seq <= TM:
        return pl.BlockSpec((S5_NB, p.seq // S5_CHUNK, TM // p.seq, S5_CHUNK, LANES),
                            lambda i: (0, 0, i, 0, 0))
    per_seq = p.seq // TM
    return pl.BlockSpec((S5_NB, TM // S5_CHUNK, None, S5_CHUNK, LANES),
                        lambda i: (0, i % per_seq, i // per_seq, 0, 0))


def _chunk_store(p, ref, bl, val):
    if p.seq <= TM:
        cs = p.seq // S5_CHUNK
        for b in range(TM // p.seq):
            ref[bl, :, b] = val[b * p.seq:(b + 1) * p.seq].reshape(cs, S5_CHUNK, LANES)
    else:
        ref[bl] = val.reshape(TM // S5_CHUNK, S5_CHUNK, LANES)


def _chunk_load(p, ref, bl):
    if p.seq <= TM:
        return jnp.concatenate([ref[bl, :, b].reshape(p.seq, LANES) for b in range(TM // p.seq)], axis=0)
    return ref[bl].reshape(TM, LANES)


def _ada_kernel(c_ref, w_ref, b_ref, o_ref):
    a = c_ref[...]
    a = (a * jax.nn.sigmoid(a)).astype(BF16)
    o_ref[0] = jnp.dot(a, w_ref[0].astype(BF16), preferred_element_type=F32) + b_ref[0]


def _ada(cc, w_ada, b_ada):
    tn = 512
    return pl.pallas_call(
        _ada_kernel,
        grid=(DEPTH, 6 * D_MODEL // tn),
        in_specs=[pl.BlockSpec((N_MODS, D_MODEL), lambda l, j: (0, 0)),
                  pl.BlockSpec((1, D_MODEL, tn), lambda l, j: (l, 0, j)),
                  pl.BlockSpec((1, 1, tn), lambda l, j: (l, 0, j))],
        out_specs=pl.BlockSpec((1, N_MODS, tn), lambda l, j: (l, 0, j)),
        out_shape=jax.ShapeDtypeStruct((DEPTH, N_MODS, 6 * D_MODEL), F32),
        compiler_params=_cparams(("arbitrary", "arbitrary")),
        name="ada_mod",
    )(cc, w_ada, b_ada.reshape(DEPTH, 1, 6 * D_MODEL))


def _in_kernel(p, x_ref, mod_ref, nw_ref, w_ref, wv_ref, wg_ref, bg_ref,
               zq_ref, zk_ref, zo_ref, zf_ref, zu_ref, vt_ref, gt_ref):
    x = x_ref[...]
    mod = mod_ref[0]
    sh = mod[:, 0:D_MODEL]
    sc = mod[:, D_MODEL:2 * D_MODEL]
    xn = (_rms(x, nw_ref[...]) * (1.0 + sc) + sh).astype(BF16)
    z = jnp.dot(xn, w_ref[...], preferred_element_type=F32)
    o = 0
    for ref, w in ((zq_ref, HEADS_W), (zk_ref, HEADS_W), (zo_ref, HEADS_W), (zf_ref, FOURIER_W)):
        ref[...] = z[:, o:o + w].astype(BF16)
        o += w
    for bl in range(S5_NB):
        _chunk_store(p, zu_ref, bl, z[:, o + bl * LANES:o + (bl + 1) * LANES])
    vt_ref[...] = lax.dot_general(wv_ref[...], xn, _NT, preferred_element_type=F32).astype(BF16)
    gt_ref[...] = lax.dot_general(wg_ref[...], xn, _NT, preferred_element_type=F32) + bg_ref[...]


def _in_proj(p, x, mods, nw, w, wv_t, wg_t, bg):
    tok = lambda w_: pl.BlockSpec((TM, w_), lambda i: (i, 0))
    chan = lambda c_: pl.BlockSpec((c_, TM), lambda i: (0, i))
    outs = [HEADS_W] * 3 + [FOURIER_W]
    return pl.pallas_call(
        functools.partial(_in_kernel, p),
        grid=(T_PASS // TM,),
        in_specs=[tok(D_MODEL),
                  pl.BlockSpec((1, 1, 6 * D_MODEL), lambda i: (p.mod_row(i), 0, 0)),
                  _full(nw), _full(w), _full(wv_t), _full(wg_t), _full(bg)],
        out_specs=[tok(w_) for w_ in outs] + [_chunk_spec(p), chan(HEADS_W), chan(N_GATES)],
        out_shape=[jax.ShapeDtypeStruct((T_PASS, w_), BF16) for w_ in outs]
        + [jax.ShapeDtypeStruct((S5_NB, p.seq // S5_CHUNK, p.nb, S5_CHUNK, LANES), F32),
           jax.ShapeDtypeStruct((HEADS_W, T_PASS), BF16),
           jax.ShapeDtypeStruct((N_GATES, T_PASS), F32)],
        compiler_params=_cparams(("arbitrary",)),
        name="in_proj_%d" % p.seq,
    )(x, mods, nw, w, wv_t, wg_t, bg)


def _dft_consts():
    d = np.arange(FOURIER_DH)
    phi = 2.0 * np.pi * ((d[:, None] * d[None, :]) % FOURIER_DH) / FOURIER_DH
    eye = np.eye(FOURIER_W // FOURIER_DH)
    cd = np.kron(eye, np.cos(phi)) / math.sqrt(FOURIER_DH)
    sd = np.kron(eye, np.sin(phi)) / math.sqrt(FOURIER_DH)
    s = np.arange(SEQ)
    th = 2.0 * np.pi * ((s[:, None] * s[None, :]) % SEQ) / SEQ
    rows = DEC_SEQ // GRID_W
    pos = np.arange(DEC_SEQ)
    r, c = pos // GRID_W, pos % GRID_W
    ph = ((r[:, None] * r[None, :]) * (GRID_W // rows) + c[:, None] * c[None, :]) % GRID_W
    th2 = 2.0 * np.pi * ph / GRID_W
    return (np.concatenate([cd, sd], axis=1),
            np.concatenate([np.cos(th), -np.sin(th)], axis=1) / math.sqrt(SEQ),
            np.concatenate([np.cos(th2), -np.sin(th2)], axis=1) / math.sqrt(DEC_SEQ))


def _fourier_prompt_kernel(nb, zf_ref, cdsd_ref, cs_ref, wf_ref, o_ref):
    t = jnp.dot(zf_ref[...], cdsd_ref[...], preferred_element_type=F32).astype(BF16)
    for b in range(nb):
        tb = t[b * SEQ:(b + 1) * SEQ]
        st = jnp.concatenate([tb[:, :FOURIER_W], tb[:, FOURIER_W:]], axis=0)
        f = jnp.dot(cs_ref[...], st, preferred_element_type=F32)
        o_ref[b * SEQ:(b + 1) * SEQ, :] = jnp.dot(
            f.astype(BF16), wf_ref[...], preferred_element_type=F32).astype(BF16)


def _fourier_prompt(zf, cdsd, cs, wf):
    nb = 4
    return pl.pallas_call(
        functools.partial(_fourier_prompt_kernel, nb),
        grid=(BATCH // nb,),
        in_specs=[pl.BlockSpec((nb * SEQ, FOURIER_W), lambda i: (i, 0)), _full(cdsd), _full(cs), _full(wf)],
        out_specs=pl.BlockSpec((nb * SEQ, FOURIER_W), lambda i: (i, 0)),
        out_shape=jax.ShapeDtypeStruct((T_PASS, FOURIER_W), BF16),
        compiler_params=_cparams(("arbitrary",)),
        name="fourier_prompt",
    )(zf, cdsd, cs, wf)


def _fourier_sample_kernel(zf_ref, cdsd_ref, ab_ref, wf_ref, o_ref, tt_ref):
    @pl.when(pl.program_id(0) == 0)
    def _():
        for b in range(DEC_BATCH):
            t = jnp.dot(zf_ref[b * DEC_SEQ:(b + 1) * DEC_SEQ, :], cdsd_ref[...],
                        preferred_element_type=F32).astype(BF16)
            tt_ref[b, 0:DEC_SEQ, :] = t[:, :FOURIER_W]
            tt_ref[b, DEC_SEQ:2 * DEC_SEQ, :] = t[:, FOURIER_W:]

    for b in range(DEC_BATCH):
        f = jnp.dot(ab_ref[...], tt_ref[b], preferred_element_type=F32)
        o_ref[b] = jnp.dot(f.astype(BF16), wf_ref[...], preferred_element_type=F32).astype(BF16)


def _fourier_sample(zf, cdsd, ab, wf):
    tk = 512
    out = pl.pallas_call(
        _fourier_sample_kernel,
        grid=(DEC_SEQ // tk,),
        in_specs=[_full(zf), _full(cdsd), pl.BlockSpec((tk, 2 * DEC_SEQ), lambda i: (i, 0)), _full(wf)],
        out_specs=pl.BlockSpec((DEC_BATCH, tk, FOURIER_W), lambda i: (0, i, 0)),
        out_shape=jax.ShapeDtypeStruct((DEC_BATCH, DEC_SEQ, FOURIER_W), BF16),
        scratch_shapes=[pltpu.VMEM((DEC_BATCH, 2 * DEC_SEQ, FOURIER_W), BF16)],
        compiler_params=_cparams(("arbitrary",)),
        name="fourier_sample",
    )(zf, cdsd, ab, wf)
    return out.reshape(T_PASS, FOURIER_W)


def _split3(x):
    hi = x.astype(BF16).astype(F32)
    mid = (x - hi).astype(BF16).astype(F32)
    lo = (x - hi - mid).astype(BF16).astype(F32)
    return hi, mid, lo


def _gate_kernel(g_ref, o_ref):
    L = MLSTM_CHUNK
    nrow = N_GATES // 2
    row = lax.broadcasted_iota(jnp.int32, (L, L), 0)
    col = lax.broadcasted_iota(jnp.int32, (L, L), 1)
    tri_pre = jnp.where(row <= col, 1.0, 0.0).astype(BF16)
    tri_suf = jnp.where(row >= col, 1.0, 0.0).astype(BF16)
    is_fwd = lax.broadcasted_iota(jnp.int32, (nrow, L), 0) < MLSTM_HEADS
    lane = lax.broadcasted_iota(jnp.int32, (nrow, L), 1)
    for c in range(g_ref.shape[1] // L):
        cols = slice(c * L, (c + 1) * L)
        ig = g_ref[0:nrow, cols]
        lf = _log_sigmoid(g_ref[nrow:, cols])
        parts = jnp.concatenate(_split3(lf), axis=0).astype(BF16)
        pre = jnp.dot(parts, tri_pre, preferred_element_type=F32)
        suf = jnp.dot(parts, tri_suf, preferred_element_type=F32)
        fold = lambda a: a[0:nrow] + a[nrow:2 * nrow] + a[2 * nrow:]
        b = jnp.where(is_fwd, fold(pre), fold(suf))
        r = ig - b
        pm = sm = r
        sh = 1
        while sh < L:
            pm = jnp.maximum(pm, jnp.where(lane >= sh, pltpu.roll(pm, sh, 1), NEG))
            sm = jnp.maximum(sm, jnp.where(lane < L - sh, pltpu.roll(sm, L - sh, 1), NEG))
            sh *= 2
        o_ref[0, :, cols] = b
        o_ref[1, :, cols] = r
        o_ref[2, :, cols] = jnp.where(is_fwd, pm, sm)


def _gate_prep(gt):
    tb = 2048
    nrow = N_GATES // 2
    out = pl.pallas_call(
        _gate_kernel,
        grid=(T_PASS // tb,),
        in_specs=[pl.BlockSpec((N_GATES, tb), lambda i: (0, i))],
        out_specs=pl.BlockSpec((3, nrow, tb), lambda i: (0, 0, i)),
        out_shape=jax.ShapeDtypeStruct((3, nrow, T_PASS), F32),
        compiler_params=_cparams(("arbitrary",)),
        name="gate_prep",
    )(gt)
    return jnp.transpose(out.reshape(3, nrow, T_PASS // MLSTM_CHUNK, MLSTM_CHUNK), (1, 2, 0, 3))


def _mlstm_chunk(q, k, vt, pr, ct, m, fwd):
    L = q.shape[0]
    scale = MLSTM_DH ** -0.5
    b, r, cm = pr[0:1], pr[1:2], pr[2:3]
    ones = jnp.ones((3, L), F32)
    zeros = jnp.zeros((SUBLANES - 6, L), F32)
    lhs = jnp.concatenate(_split3(r) + (ones, zeros), axis=0).astype(BF16)
    rhs = jnp.concatenate((ones,) + _split3(-cm) + (zeros,), axis=0).astype(BF16)
    arg = lax.dot_general(lhs, rhs, _TN, preferred_element_type=F32)
    row = lax.broadcasted_iota(jnp.int32, (L, L), 0)
    col = lax.broadcasted_iota(jnp.int32, (L, L), 1)
    e = jnp.where((row <= col) if fwd else (row >= col), jnp.exp(arg), 0.0)
    st = lax.dot_general(k, q, _NT, preferred_element_type=F32)
    num = jnp.dot(vt, (st * e).astype(BF16), preferred_element_type=F32)

    mx = jnp.maximum(m, cm)
    cq = lax.dot_general(ct.astype(BF16), q, _NT, preferred_element_type=F32)
    num = (jnp.exp(cm - mx) * scale) * num + jnp.exp(m - mx) * cq
    den = num[N_AUG:N_AUG + 1, :]
    h = num * (1.0 / jnp.maximum(jnp.abs(den), jnp.exp(-(b + mx))))

    last = L - 1 if fwd else 0
    cm_last = cm[:, last:last + 1]
    mx_last = jnp.maximum(m, cm_last)
    vw = (vt.astype(F32) * jnp.exp(r - cm_last)).astype(BF16)
    ct_new = (jnp.exp(m - mx_last) * ct
              + (jnp.exp(cm_last - mx_last) * scale) * jnp.dot(vw, k, preferred_element_type=F32))
    return h, ct_new, b[:, last:last + 1] + mx_last


def _mlstm_kernel(nc, m0_ref, q_ref, k_ref, vt_ref, zo_ref, prf_ref, prb_ref, c0_ref, nw_ref,
                  o_ref, cfin_ref, mfin_ref, hbuf_ref):
    L = MLSTM_CHUNK
    bi = pl.program_id(0)
    hd = pl.program_id(1)
    vrow = lax.broadcasted_iota(jnp.int32, (HEAD_PAD, L), 0)

    def chunk(ci, pr_ref, ct, m, dr):
        rows = pl.ds(pl.multiple_of(ci * L, L), L)
        vt = vt_ref[:, rows]
        vt = jnp.where(vrow == N_AUG, jnp.ones_like(vt), vt)
        h, ct, m = _mlstm_chunk(q_ref[rows, :], k_ref[rows, :], vt, pr_ref[0, ci], ct, m, dr == 0)
        hbuf_ref[dr, ci] = h
        return ct, m

    def step(i, carry):
        cf, mf, cb, mb = carry
        cf, mf = chunk(i, prf_ref, cf, mf, 0)
        cb, mb = chunk(nc - 1 - i, prb_ref, cb, mb, 1)
        return cf, mf, cb, mb

    carry = tuple(x for dr in range(2)
                  for x in (c0_ref[0, dr, 0].T, jnp.full((1, 1), m0_ref[bi, dr, hd], F32)))
    carry = step(0, carry) if nc == 1 else lax.fori_loop(0, nc, step, carry)
    for dr in range(2):
        cfin_ref[0, dr, 0] = carry[2 * dr].T
        mfin_ref[0, dr, 0] = jnp.broadcast_to(carry[2 * dr + 1], (SUBLANES, LANES))

    nw = nw_ref[0]

    def finish(ci, _):
        rows = pl.ds(pl.multiple_of(ci * L, L), L)
        h = jnp.where(vrow < MLSTM_DH, hbuf_ref[0, ci] + hbuf_ref[1, ci], 0.0)
        ms = jnp.sum(h * h, axis=0, keepdims=True) * (1.0 / MLSTM_DH)
        hn = (h * lax.rsqrt(ms + EPS) * nw).T
        o_ref[rows, :] = (hn * jax.nn.sigmoid(zo_ref[rows, :].astype(F32))).astype(BF16)
        return 0

    if nc == 1:
        finish(0, 0)
    else:
        lax.fori_loop(0, nc, finish, 0)


def _mlstm(p, q, k, vt, zo, pr, c0, m0, nw):
    nc = p.seq // MLSTM_CHUNK
    tok = pl.BlockSpec((p.seq, HEAD_PAD), lambda b, h: (b, h))
    st = pl.BlockSpec((1, 2, 1, HEAD_PAD, HEAD_PAD), lambda b, h: (b, 0, h, 0, 0))
    prs = lambda dr: pl.BlockSpec((1, nc, 3, MLSTM_CHUNK), lambda b, h: (dr * MLSTM_HEADS + h, b, 0, 0))
    return pl.pallas_call(
        functools.partial(_mlstm_kernel, nc),
        grid=(p.nb, MLSTM_HEADS),
        in_specs=[pl.BlockSpec(memory_space=pltpu.SMEM), tok, tok,
                  pl.BlockSpec((HEAD_PAD, p.seq), lambda b, h: (h, b)), tok, prs(0), prs(1),
                  st, pl.BlockSpec((1, HEAD_PAD, 1), lambda b, h: (h, 0, 0))],
        out_specs=[tok, st,
                   pl.BlockSpec((1, 2, 1, SUBLANES, LANES), lambda b, h: (b, 0, h, 0, 0))],
        out_shape=[jax.ShapeDtypeStruct((T_PASS, HEADS_W), BF16),
                   jax.ShapeDtypeStruct((p.nb, 2, MLSTM_HEADS, HEAD_PAD, HEAD_PAD), F32),
                   jax.ShapeDtypeStruct((p.nb, 2, MLSTM_HEADS, SUBLANES, LANES), F32)],
        scratch_shapes=[pltpu.VMEM((2, nc, HEAD_PAD, MLSTM_CHUNK), F32)],
        compiler_params=_cparams(("arbitrary", "arbitrary")),
        name="mlstm_%d" % p.seq,
    )(m0, q, k, vt, zo, pr, pr, c0, nw)


def _cpow(br, bi, e, nbits):
    pr = pi = None
    for bit in range(nbits):
        sel = ((e >> bit) & 1) == 1
        if pr is None:
            pr, pi = jnp.where(sel, br, 1.0), jnp.where(sel, bi, 0.0)
        else:
            pr, pi = jnp.where(sel, pr * br - pi * bi, pr), jnp.where(sel, pr * bi + pi * br, pi)
        if bit + 1 < nbits:
            br, bi = br * br - bi * bi, 2.0 * br * bi
    return pr, pi


def _s5_prep_kernel(lamc_re_ref, lamc_im_ref, lamr_re_ref, lamr_im_ref, lstep_ref,
                    bt_re_ref, bt_im_ref, ct_re_ref, ct_im_ref,
                    t_ref, m_ref, n_ref, a_ref):
    C = S5_CHUNK
    nbits = C.bit_length()
    kk = lax.broadcasted_iota(jnp.int32, (S5_ST, S5_IN), 1) >> 4
    top = lax.broadcasted_iota(jnp.int32, (S5_ST, S5_IN), 0) < S5_STATE
    srow = lax.broadcasted_iota(jnp.int32, (S5_IN, S5_ST), 0) >> 4
    left = lax.broadcasted_iota(jnp.int32, (S5_IN, S5_ST), 1) < S5_STATE
    left16 = lax.broadcasted_iota(jnp.int32, (S5_GROUP_CH, S5_ST), 1) < S5_STATE
    left1 = lax.broadcasted_iota(jnp.int32, (1, S5_ST), 1) < S5_STATE
    lane = lax.broadcasted_iota(jnp.int32, (S5_GROUP_CH, S5_IN), 1)
    ct_re = ct_re_ref[0, 0]
    ct_im = ct_im_ref[0, 0]
    resp = []
    for d in range(2):
        step = jnp.exp(lstep_ref[0, d, 0])
        lr_c, li_c = lamc_re_ref[0, d, 0] * step, lamc_im_ref[0, d, 0] * step
        lr_r, li_r = lamr_re_ref[0, d, 0], lamr_im_ref[0, d, 0]
        lbc_re, lbc_im = jnp.exp(lr_c) * jnp.cos(li_c), jnp.exp(lr_c) * jnp.sin(li_c)
        mag = jnp.exp(lr_r * step)
        lb_re, lb_im = mag * jnp.cos(li_r * step), mag * jnp.sin(li_r * step)

        def cp(lag):
            pr, pi = _cpow(lbc_re, lbc_im, lag, nbits)
            return ct_re * pr - ct_im * pi, ct_re * pi + ct_im * pr

        nr, ni = lb_re - 1.0, lb_im
        den = lr_r * lr_r + li_r * li_r
        kap_re = (nr * lr_r + ni * li_r) / den
        kap_im = (ni * lr_r - nr * li_r) / den
        bb_re = kap_re * bt_re_ref[0, 0] - kap_im * bt_im_ref[0, 0]
        bb_im = kap_re * bt_im_ref[0, 0] + kap_im * bt_re_ref[0, 0]

        cpr, cpi = cp(kk if d == 0 else (C - 1) - kk)
        resp.append(jnp.dot(jnp.where(left16, bb_re, -bb_im), jnp.where(top, cpr, cpi),
                            precision=lax.Precision.HIGHEST, preferred_element_type=F32))

        cpr, cpi = cp(kk + 1 if d == 0 else C - kk)
        m_ref[0, d, 0] = jnp.where(top, cpr, -cpi).astype(BF16)

        pr, pi = _cpow(lb_re, lb_im, (C - 1) - srow if d == 0 else srow, nbits)
        bt_r = jnp.concatenate([bb_re] * C, axis=0)
        bt_i = jnp.concatenate([bb_im] * C, axis=0)
        n_ref[0, d, 0] = jnp.where(left, pr * bt_r - pi * bt_i, pr * bt_i + pi * bt_r).astype(BF16)

        ar, ai = _cpow(lb_re, lb_im, jnp.full((1, S5_ST), C, jnp.int32), nbits)
        a_ref[0, d, 0] = jnp.concatenate([ar, jnp.where(left1, -ai, ai)], axis=0)

    rf, rb = resp
    for s in range(C):
        nf = S5_GROUP_CH * s
        blk = jnp.where(lane >= nf, pltpu.roll(rf, nf, 1) if nf else rf, 0.0)
        nb = S5_GROUP_CH * (C - 1 - s)
        blk = blk + jnp.where(lane < S5_IN - nb, pltpu.roll(rb, S5_IN - nb, 1) if nb else rb, 0.0)
        t_ref[0, 0, S5_GROUP_CH * s:S5_GROUP_CH * (s + 1), :] = blk.astype(BF16)


def _s5_prep(lam_re, lam_im, log_step, b_re, b_im, c_re, c_im):
    G = S5_GROUPS
    dup = lambda a: jnp.concatenate([a, a], axis=-1)
    lamc = [dup(a).reshape(DEPTH, 2, G, S5_ST, 1) for a in (lam_re, lam_im)]
    lamr = [dup(a).reshape(DEPTH, 2, G, 1, S5_ST) for a in (lam_re, lam_im)]
    lstep = log_step.reshape(DEPTH, 2, G, 1, 1)
    bt = [dup(jnp.swapaxes(a, 2, 3)) for a in (b_re, b_im)]
    ct = [jnp.tile(jnp.swapaxes(a, 2, 3), (1, 1, 2, S5_CHUNK)) for a in (c_re, c_im)]
    dspec = lambda r, c: pl.BlockSpec((1, 2, 1, r, c), lambda l, g: (l, 0, g, 0, 0))
    gspec = lambda r, c: pl.BlockSpec((1, 1, r, c), lambda l, g: (l, g, 0, 0))
    return pl.pallas_call(
        _s5_prep_kernel,
        grid=(DEPTH, G),
        in_specs=[dspec(S5_ST, 1), dspec(S5_ST, 1), dspec(1, S5_ST), dspec(1, S5_ST), dspec(1, 1),
                  gspec(S5_GROUP_CH, S5_ST), gspec(S5_GROUP_CH, S5_ST),
                  gspec(S5_ST, S5_IN), gspec(S5_ST, S5_IN)],
        out_specs=[gspec(S5_IN, S5_IN), dspec(S5_ST, S5_IN), dspec(S5_IN, S5_ST), dspec(2, S5_ST)],
        out_shape=[jax.ShapeDtypeStruct((DEPTH, G, S5_IN, S5_IN), BF16),
                   jax.ShapeDtypeStruct((DEPTH, 2, G, S5_ST, S5_IN), BF16),
                   jax.ShapeDtypeStruct((DEPTH, 2, G, S5_IN, S5_ST), BF16),
                   jax.ShapeDtypeStruct((DEPTH, 2, G, 2, S5_ST), F32)],
        compiler_params=_cparams(("arbitrary", "arbitrary")),
        name="s5_prep",
    )(*lamc, *lamr, lstep, *bt, *ct)


def _s5_kernel(nb, nchunks, zr_ref, t_ref, m_ref, n_ref, a_ref, x0_ref,
               y_ref, xfin_ref, u_ref, v_ref, xp_ref, yb_ref):
    R = nb * nchunks
    RT = 128
    W = S5_BW

    def gather(i, _):
        rows = pl.ds(pl.multiple_of(i * RT, RT), RT)
        xs = [zr_ref[0, rows, s * LANES:(s + 1) * LANES] for s in range(S5_CHUNK)]
        for gl in range(S5_GB):
            lo = gl * S5_GROUP_CH
            u_ref[gl, rows, :] = jnp.concatenate(
                [x[:, lo:lo + S5_GROUP_CH] for x in xs], axis=1).astype(BF16)
        return 0

    lax.fori_loop(0, R // RT, gather, 0)

    for gl in range(S5_GB):
        for d in range(2):
            v_ref[d, :, gl * S5_ST:(gl + 1) * S5_ST] = jnp.dot(
                u_ref[gl], n_ref[d, gl], preferred_element_type=F32)

    im_half = (lax.broadcasted_iota(jnp.int32, (1, W), 1) & (S5_ST - 1)) >= S5_STATE
    per = max(1, SUBLANES // nb)
    srows = nb * per

    def scan(d):
        a = a_ref[d, 0:1, :]
        a2 = a_ref[d, 1:2, :]

        def advance(x, v):
            swapped = jnp.where(im_half, pltpu.roll(x, S5_STATE, 1), pltpu.roll(x, W - S5_STATE, 1))
            return a * x + a2 * swapped + v

        def step(i, x):
            si = i if d == 0 else nchunks // per - 1 - i
            rows = pl.ds(pl.multiple_of(si * srows, srows), srows)
            v = v_ref[d, rows, :]
            order = range(per) if d == 0 else range(per - 1, -1, -1)
            entering = [None] * per
            for j in order:
                entering[j] = x
                x = advance(x, v[j * nb:(j + 1) * nb])
            xp_ref[d, rows, :] = entering[0] if per == 1 else jnp.concatenate(entering, axis=0)
            return x

        return lax.fori_loop(0, nchunks // per, step, x0_ref[d])

    for d in range(2):
        xfin_ref[d] = scan(d)

    for gl in range(S5_GB):
        cols = slice(gl * S5_ST, (gl + 1) * S5_ST)
        y = jnp.dot(u_ref[gl], t_ref[gl], preferred_element_type=F32)
        for d in range(2):
            y = y + jnp.dot(xp_ref[d, :, cols].astype(BF16), m_ref[d, gl], preferred_element_type=F32)
        yb_ref[gl] = y

    def scatter(i, _):
        rows = pl.ds(pl.multiple_of(i * RT, RT), RT)
        ys = [yb_ref[gl, rows, :] for gl in range(S5_GB)]
        for t in range(S5_CHUNK):
            lo = t * S5_GROUP_CH
            y_ref[0, rows, t * LANES:(t + 1) * LANES] = jnp.concatenate(
                [y[:, lo:lo + S5_GROUP_CH] for y in ys], axis=1)
        return 0

    lax.fori_loop(0, R // RT, scatter, 0)


def _s5(p, zu, t, m, n, a, x0):
    G = S5_GROUPS
    nchunks = p.seq // S5_CHUNK
    R = nchunks * p.nb
    zr = zu.reshape(S5_NB, R, S5_ROW)
    a = jnp.transpose(a, (0, 2, 1, 3)).reshape(2, 2, G * S5_ST)
    row = pl.BlockSpec((1, R, S5_ROW), lambda j: (j, 0, 0))
    dsp = lambda r, c: pl.BlockSpec((2, S5_GB, r, c), lambda j: (0, j, 0, 0))
    lsp = lambda r: pl.BlockSpec((2, r, S5_BW), lambda j: (0, 0, j))
    y, xfin = pl.pallas_call(
        functools.partial(_s5_kernel, p.nb, nchunks),
        grid=(S5_NB,),
        in_specs=[row, pl.BlockSpec((S5_GB, S5_IN, S5_IN), lambda j: (j, 0, 0)),
                  dsp(S5_ST, S5_IN), dsp(S5_IN, S5_ST), lsp(2), lsp(p.nb)],
        out_specs=[row, lsp(p.nb)],
        out_shape=[jax.ShapeDtypeStruct((S5_NB, R, S5_ROW), F32),
                   jax.ShapeDtypeStruct((2, p.nb, G * S5_ST), F32)],
        scratch_shapes=[pltpu.VMEM((S5_GB, R, S5_IN), BF16), pltpu.VMEM((2, R, S5_BW), F32),
                        pltpu.VMEM((2, R, S5_BW), F32), pltpu.VMEM((S5_GB, R, S5_IN), F32)],
        compiler_params=_cparams(("arbitrary",)),
        name="s5_scan_%d" % p.seq,
    )(zr, t, m, n, a, x0)
    return y.reshape(zu.shape), xfin


def _mix_kernel(p, x_ref, mod_ref, fo_ref, mo_ref, ys_ref, zu_ref, d_ref, wglu_ref,
                wof_ref, wom_ref, wos_ref, nw_ref, x1_ref, xn_ref):
    mod = mod_ref[0]
    g1 = mod[:, 2 * D_MODEL:3 * D_MODEL]
    sh2 = mod[:, 3 * D_MODEL:4 * D_MODEL]
    sc2 = mod[:, 4 * D_MODEL:5 * D_MODEL]
    ys = jnp.concatenate([_chunk_load(p, ys_ref, bl) for bl in range(S5_NB)], axis=1)
    zu = jnp.concatenate([_chunk_load(p, zu_ref, bl) for bl in range(S5_NB)], axis=1)
    y = jax.nn.gelu(ys + d_ref[...] * zu).astype(BF16)
    gg = jnp.dot(y, wglu_ref[...], preferred_element_type=F32)
    s_out = (gg[:, :S5_W] * jax.nn.sigmoid(gg[:, S5_W:])).astype(BF16)
    mix = (jnp.dot(fo_ref[...], wof_ref[...], preferred_element_type=F32)
           + jnp.dot(mo_ref[...], wom_ref[...], preferred_element_type=F32)
           + jnp.dot(s_out, wos_ref[...], preferred_element_type=F32))
    x1 = x_ref[...] + g1 * mix
    x1_ref[...] = x1
    xn_ref[...] = (_rms(x1, nw_ref[...]) * (1.0 + sc2) + sh2).astype(BF16)


def _mix(p, x, mods, fo, mo, ys, zu, d, wglu, wof, wom, wos, nw):
    tok = lambda w_: pl.BlockSpec((TM, w_), lambda i: (i, 0))
    return pl.pallas_call(
        functools.partial(_mix_kernel, p),
        grid=(T_PASS // TM,),
        in_specs=[tok(D_MODEL),
                  pl.BlockSpec((1, 1, 6 * D_MODEL), lambda i: (p.mod_row(i), 0, 0)),
                  tok(FOURIER_W), tok(HEADS_W), _chunk_spec(p), _chunk_spec(p),
                  _full(d), _full(wglu), _full(wof), _full(wom), _full(wos), _full(nw)],
        out_specs=[tok(D_MODEL), tok(D_MODEL)],
        out_shape=[jax.ShapeDtypeStruct((T_PASS, D_MODEL), F32),
                   jax.ShapeDtypeStruct((T_PASS, D_MODEL), BF16)],
        compiler_params=_cparams(("arbitrary",)),
        name="mix_out_%d" % p.seq,
    )(x, mods, fo, mo, ys, zu, d, wglu, wof, wom, wos, nw)


def _ffn_kernel(final, xn_ref, x1_ref, mod_ref, wg_ref, wu_ref, wd_ref, nf_ref, o_ref, acc_ref):
    j = pl.program_id(1)
    xn = xn_ref[...]
    a = jnp.dot(xn, wg_ref[...], preferred_element_type=F32)
    u = jnp.dot(xn, wu_ref[...], preferred_element_type=F32)
    h = (a * jax.nn.sigmoid(a) * u).astype(BF16)
    part = jnp.dot(h, wd_ref[...], preferred_element_type=F32)

    @pl.when(j == 0)
    def _():
        acc_ref[...] = part

    @pl.when(j > 0)
    def _():
        acc_ref[...] += part

    @pl.when(j == pl.num_programs(1) - 1)
    def _():
        g2 = mod_ref[0][:, 5 * D_MODEL:6 * D_MODEL]
        x2 = x1_ref[...] + g2 * acc_ref[...]
        o_ref[...] = _rms(x2, nf_ref[...]) if final else x2


def _ffn(p, final, xn, x1, mods, wg, wu, wd, nf):
    tok = pl.BlockSpec((TM, D_MODEL), lambda i, j: (i, 0))
    return pl.pallas_call(
        functools.partial(_ffn_kernel, final),
        grid=(T_PASS // TM, D_FF // TF),
        in_specs=[tok, tok,
                  pl.BlockSpec((1, 1, 6 * D_MODEL), lambda i, j: (p.mod_row(i), 0, 0)),
                  pl.BlockSpec((D_MODEL, TF), lambda i, j: (0, j)),
                  pl.BlockSpec((D_MODEL, TF), lambda i, j: (0, j)),
                  pl.BlockSpec((TF, D_MODEL), lambda i, j: (j, 0)),
                  pl.BlockSpec((1, D_MODEL), lambda i, j: (0, 0))],
        out_specs=tok,
        out_shape=jax.ShapeDtypeStruct((T_PASS, D_MODEL), F32),
        scratch_shapes=[pltpu.VMEM((TM, D_MODEL), F32)],
        compiler_params=_cparams(("arbitrary", "arbitrary")),
        name="ffn_%d" % p.seq,
    )(xn, x1, mods, wg, wu, wd, nf)


def _pad_heads(a, axis):
    shape = a.shape[:axis] + (MLSTM_HEADS, MLSTM_DH) + a.shape[axis + 1:]
    pad = [(0, 0)] * (a.ndim + 1)
    pad[axis + 1] = (0, HEAD_PAD - MLSTM_DH)
    return jnp.pad(a.reshape(shape), pad).reshape(a.shape[:axis] + (HEADS_W,) + a.shape[axis + 1:])


def _mlstm_state_in(c, n):
    cn = jnp.concatenate([c, n[..., None]], axis=-1)
    return jnp.pad(cn, ((0, 0),) * 3 + ((0, HEAD_PAD - MLSTM_DH), (0, HEAD_PAD - MLSTM_DH - 1)))


def kernel(x_prompt, x_sample, state_mlstm_C, state_mlstm_n, state_mlstm_m, state_s5_re, state_s5_im,
           c, c_ctx, w_ada, b_ada, norm1_w, norm2_w, w_in, b_gates, w_fourier, mlstm_norm_w,
           s5_lambda_re, s5_lambda_im, s5_log_step, s5_b_re, s5_b_im, s5_c_re, s5_c_im, s5_d,
           w_glu, w_out, w_gate, w_up, w_down, norm_f):
    xs = {PROMPT: x_prompt.reshape(T_PASS, D_MODEL), SAMPLE: x_sample.reshape(T_PASS, D_MODEL)}
    cc = jnp.concatenate([c_ctx[None], c, jnp.zeros((N_MODS - 1 - DEC_BATCH, D_MODEL), F32)], axis=0)
    mods_all = _ada(cc, w_ada, b_ada).reshape(DEPTH, N_MODS, 1, 6 * D_MODEL)
    cdsd, cs, ab = (jnp.asarray(a.astype(np.float32)).astype(BF16) for a in _dft_consts())
    s5_ops = _s5_prep(s5_lambda_re, s5_lambda_im, s5_log_step, s5_b_re, s5_b_im, s5_c_re, s5_c_im)

    o_q = FOURIER_W
    o_g = o_q + 3 * MLSTM_W
    o_o = o_g + N_GATES
    o_u = o_o + MLSTM_W

    new_c, new_n, new_m, new_re, new_im = [], [], [], [], []
    for l in range(DEPTH):
        mods = mods_all[l]
        wl = w_in[l]
        heads = lambda o: _pad_heads(wl[:, o:o + MLSTM_W], 1)
        w_cat = jnp.concatenate([heads(o_q), heads(o_q + MLSTM_W), heads(o_o), wl[:, :FOURIER_W], wl[:, o_u:]],
                                axis=1).astype(BF16)
        wv_t = heads(o_q + 2 * MLSTM_W).T.astype(BF16)
        gate_perm = np.arange(N_GATES).reshape(2, 2, MLSTM_HEADS).transpose(1, 0, 2).reshape(-1)
        wg_t = wl[:, o_g:o_o].T[gate_perm].astype(BF16)
        bg = b_gates[l][gate_perm][:, None]
        wf = w_fourier[l].astype(BF16)
        nw = _pad_heads(mlstm_norm_w[l], 0).reshape(MLSTM_HEADS, HEAD_PAD, 1)
        ops = [o[l] for o in s5_ops]
        wo = w_out[l]
        wom = jnp.pad(wo[FOURIER_W:FOURIER_W + MLSTM_W].reshape(MLSTM_HEADS, MLSTM_DH, D_MODEL),
                      ((0, 0), (0, HEAD_PAD - MLSTM_DH), (0, 0))).reshape(HEADS_W, D_MODEL).astype(BF16)
        wof = wo[:FOURIER_W].astype(BF16)
        wos = wo[FOURIER_W + MLSTM_W:].astype(BF16)
        wglu = w_glu[l].astype(BF16)
        wg, wu, wd = w_gate[l].astype(BF16), w_up[l].astype(BF16), w_down[l].astype(BF16)

        for p in (PROMPT, SAMPLE):
            x = xs[p]
            zq, zk, zo, zf, zu, vt, gt = _in_proj(p, x, mods, norm1_w[l][None], w_cat, wv_t, wg_t, bg)
            fo = _fourier_prompt(zf, cdsd, cs, wf) if p is PROMPT else _fourier_sample(zf, cdsd, ab, wf)

            if p is PROMPT:
                c0 = jnp.zeros((BATCH, 2, MLSTM_HEADS, HEAD_PAD, HEAD_PAD), F32)
                m0 = jnp.zeros((BATCH, 2, MLSTM_HEADS), F32)
                x0 = jnp.zeros((2, BATCH, S5_GROUPS * S5_ST), F32)
            else:
                c0 = _mlstm_state_in(state_mlstm_C[:, l], state_mlstm_n[:, l])
                m0 = state_mlstm_m[:, l]
                x0 = jnp.concatenate([state_s5_re[:, l], state_s5_im[:, l]], axis=-1)
                x0 = jnp.transpose(x0, (1, 0, 2, 3)).reshape(2, DEC_BATCH, S5_GROUPS * S5_ST)
            mo, cfin, mfin = _mlstm(p, zq, zk, vt, zo, _gate_prep(gt), c0, m0, nw)

            ys, xfin = _s5(p, zu, *ops, x0)

            if p is PROMPT:
                new_c.append(cfin[..., :MLSTM_DH, :MLSTM_DH])
                new_n.append(cfin[..., :MLSTM_DH, N_AUG])
                new_m.append(mfin[..., 0, 0])
                xfin = xfin.reshape(2, BATCH, S5_GROUPS, 2, S5_STATE)
                new_re.append(jnp.transpose(xfin[:, :, :, 0], (1, 0, 2, 3)))
                new_im.append(jnp.transpose(xfin[:, :, :, 1], (1, 0, 2, 3)))

            x1, xn2 = _mix(p, x, mods, fo, mo, ys, zu, s5_d[l].reshape(1, S5_W), wglu,
                           wof, wom, wos, norm2_w[l][None])
            xs[p] = _ffn(p, l == DEPTH - 1, xn2, x1, mods, wg, wu, wd, norm_f[None])

    y_prompt = xs[PROMPT].reshape(BATCH, SEQ, D_MODEL)
    y_sample = xs[SAMPLE].reshape(DEC_BATCH, DEC_SEQ, D_MODEL)
    stack = lambda parts: jnp.stack(parts, axis=1)
    return (y_prompt, y_sample, stack(new_c), stack(new_n), stack(new_m), stack(new_re), stack(new_im))
```

```python
import collections
import functools
import math

import numpy as np
import jax
import jax.numpy as jnp
from jax import lax
from jax.experimental import pallas as pl
from jax.experimental.pallas import tpu as pltpu

F32 = jnp.float32
BF16 = jnp.bfloat16

D_MODEL = 1024
BATCH = 32
SEQ = 256
DEPTH = 2
DEC_BATCH = 4
DEC_SEQ = 2048
GRID_W = 64
FOURIER_W = 256
FOURIER_DH = 64
MLSTM_W = 384
MLSTM_HEADS = 4
MLSTM_DH = 96
S5_W = 384
S5_GROUP_CH = 16
S5_GROUPS = 24
S5_STATE = 64
N_GATES = 16
D_FF = 2816
EPS = 1e-6

LANES = 128
SUBLANES = 8
VMEM_LIMIT = 56 * 1024 * 1024

HEAD_PAD = LANES
HEADS_W = MLSTM_HEADS * HEAD_PAD
N_AUG = MLSTM_DH
Z_W = 3 * HEADS_W + FOURIER_W + S5_W
MLSTM_CHUNK = 256
S5_CHUNK = 16
S5_IN = S5_CHUNK * S5_GROUP_CH
S5_ST = 2 * S5_STATE
S5_ST2 = 2 * S5_ST
S5_GB = LANES // S5_GROUP_CH
S5_NB = S5_W // LANES
S5_ROW = S5_GB * S5_IN
TB = 4
TT = 128
TROWS = TB * TT // S5_CHUNK
TM = 512
TF = D_FF // 2
N_MODS = 8
NEG = -1e30

Pass = collections.namedtuple("Pass", "nb seq mod_row")
PROMPT = Pass(BATCH, SEQ, lambda i: 0)
SAMPLE = Pass(DEC_BATCH, DEC_SEQ, lambda i: 1 + i // (DEC_SEQ // TM))
T_PASS = BATCH * SEQ
assert T_PASS == DEC_BATCH * DEC_SEQ
R_PASS = T_PASS // S5_CHUNK

_NT = (((1,), (1,)), ((), ()))
_TN = (((0,), (0,)), ((), ()))


def _cparams(sem):
    return pltpu.CompilerParams(dimension_semantics=sem, vmem_limit_bytes=VMEM_LIMIT)


def _full(a):
    return pl.BlockSpec(a.shape, lambda *_: (0,) * a.ndim)


def _log_sigmoid(x):
    return jnp.minimum(x, 0.0) - jnp.log1p(jnp.exp(-jnp.abs(x)))


def _tile_spec(w):
    return pl.BlockSpec((TB, TT, w), lambda j, k: (j, k, 0))


def _chunk_rows_spec(p):
    per_group = p.seq // TT
    return pl.BlockSpec((S5_NB, TROWS, S5_ROW), lambda j, k: (0, j * per_group + k, 0))


def _slab(b, c):
    tok = slice(b * TT + c * S5_CHUNK, b * TT + (c + 1) * S5_CHUNK)
    chk = slice((c * TB + b) * S5_CHUNK, (c * TB + b + 1) * S5_CHUNK)
    return tok, chk


def _ada_kernel(c_ref, w_ref, b_ref, o_ref):
    a = c_ref[...]
    a = (a * jax.nn.sigmoid(a)).astype(BF16)
    o_ref[0] = jnp.dot(a, w_ref[0].astype(BF16), preferred_element_type=F32) + b_ref[0]


def _ada(cc, w_ada, b_ada):
    tn = 512
    return pl.pallas_call(
        _ada_kernel,
        grid=(DEPTH, 6 * D_MODEL // tn),
        in_specs=[pl.BlockSpec((N_MODS, D_MODEL), lambda l, j: (0, 0)),
                  pl.BlockSpec((1, D_MODEL, tn), lambda l, j: (l, 0, j)),
                  pl.BlockSpec((1, 1, tn), lambda l, j: (l, 0, j))],
        out_specs=pl.BlockSpec((1, N_MODS, tn), lambda l, j: (l, 0, j)),
        out_shape=jax.ShapeDtypeStruct((DEPTH, N_MODS, 6 * D_MODEL), F32),
        compiler_params=_cparams(("arbitrary", "arbitrary")),
        name="ada_mod",
    )(cc, w_ada, b_ada.reshape(DEPTH, 1, 6 * D_MODEL))


def _mod_norm(x3, mod, nw, first):
    sh = mod[:, :, first * D_MODEL:(first + 1) * D_MODEL]
    sc = mod[:, :, (first + 1) * D_MODEL:(first + 2) * D_MODEL]
    y = x3 * lax.rsqrt(jnp.mean(x3 * x3, axis=-1, keepdims=True) + EPS) * nw
    return (y * (1.0 + sc) + sh).reshape(TB * TT, D_MODEL).astype(BF16)


def _in_kernel(x_ref, mod_ref, nw_ref, w_ref, wv_ref, wg_ref, bg_ref,
               zq_ref, zk_ref, zo_ref, zf_ref, zu_ref, vt_ref, gt_ref, scr_ref):
    xn = _mod_norm(x_ref[...], mod_ref[...], nw_ref[...], 0)
    z = jnp.dot(xn, w_ref[...], preferred_element_type=F32)
    o = 0
    for ref, w in ((zq_ref, HEADS_W), (zk_ref, HEADS_W), (zo_ref, HEADS_W), (zf_ref, FOURIER_W)):
        ref[...] = z[:, o:o + w].astype(BF16).reshape(TB, TT, w)
        o += w
    for bl in range(S5_NB):
        zb = z[:, o + bl * LANES:o + (bl + 1) * LANES]
        for b in range(TB):
            for c in range(TT // S5_CHUNK):
                tok, chk = _slab(b, c)
                scr_ref[chk, :] = zb[tok]
        by_token = [scr_ref[pl.ds(s, TROWS, stride=S5_CHUNK), :] for s in range(S5_CHUNK)]
        for gl in range(S5_GB):
            lo = gl * S5_GROUP_CH
            zu_ref[bl, :, gl * S5_IN:(gl + 1) * S5_IN] = jnp.concatenate(
                [x[:, lo:lo + S5_GROUP_CH] for x in by_token], axis=1)
    vt = lax.dot_general(wv_ref[...], xn, _NT, preferred_element_type=F32).astype(BF16)
    gt = lax.dot_general(wg_ref[...], xn, _NT, preferred_element_type=F32) + bg_ref[...]
    for b in range(TB):
        vt_ref[b] = vt[:, b * TT:(b + 1) * TT]
        gt_ref[b] = gt[:, b * TT:(b + 1) * TT]


def _in_proj(p, x, mods, nw, w, wv_t, wg_t, bg):
    chan = lambda c_: pl.BlockSpec((TB, c_, TT), lambda j, k: (j, 0, k))
    outs = [HEADS_W] * 3 + [FOURIER_W]
    return pl.pallas_call(
        _in_kernel,
        grid=(p.nb // TB, p.seq // TT),
        in_specs=[_tile_spec(D_MODEL),
                  pl.BlockSpec((TB, 1, 6 * D_MODEL), lambda j, k: (j, 0, 0)),
                  _full(nw), _full(w), _full(wv_t), _full(wg_t), _full(bg)],
        out_specs=[_tile_spec(w_) for w_ in outs] + [_chunk_rows_spec(p), chan(HEADS_W), chan(N_GATES)],
        out_shape=[jax.ShapeDtypeStruct((p.nb, p.seq, w_), BF16) for w_ in outs]
        + [jax.ShapeDtypeStruct((S5_NB, R_PASS, S5_ROW), F32),
           jax.ShapeDtypeStruct((p.nb, HEADS_W, p.seq), BF16),
           jax.ShapeDtypeStruct((p.nb, N_GATES, p.seq), F32)],
        scratch_shapes=[pltpu.VMEM((TB * TT, LANES), F32)],
        compiler_params=_cparams(("arbitrary", "arbitrary")),
        name="in_proj_%d" % p.seq,
    )(x, mods, nw, w, wv_t, wg_t, bg)


def _dft_consts():
    d = np.arange(FOURIER_DH)
    phi = 2.0 * np.pi * ((d[:, None] * d[None, :]) % FOURIER_DH) / FOURIER_DH
    eye = np.eye(FOURIER_W // FOURIER_DH)
    cd = np.kron(eye, np.cos(phi)) / math.sqrt(FOURIER_DH)
    sd = np.kron(eye, np.sin(phi)) / math.sqrt(FOURIER_DH)
    s = np.arange(SEQ)
    th = 2.0 * np.pi * ((s[:, None] * s[None, :]) % SEQ) / SEQ
    rows = DEC_SEQ // GRID_W
    pos = np.arange(DEC_SEQ)
    r, c = pos // GRID_W, pos % GRID_W
    ph = ((r[:, None] * r[None, :]) * (GRID_W // rows) + c[:, None] * c[None, :]) % GRID_W
    th2 = 2.0 * np.pi * ph / GRID_W
    return (np.concatenate([cd, sd], axis=1),
            np.concatenate([np.cos(th), -np.sin(th)], axis=1) / math.sqrt(SEQ),
            np.concatenate([np.cos(th2), -np.sin(th2)], axis=1) / math.sqrt(DEC_SEQ))


def _fourier_prompt_kernel(nb, zf_ref, cdsd_ref, cs_ref, wf_ref, o_ref):
    t = jnp.dot(zf_ref[...].reshape(nb * SEQ, FOURIER_W), cdsd_ref[...],
                preferred_element_type=F32).astype(BF16)
    for b in range(nb):
        tb = t[b * SEQ:(b + 1) * SEQ]
        st = jnp.concatenate([tb[:, :FOURIER_W], tb[:, FOURIER_W:]], axis=0)
        f = jnp.dot(cs_ref[...], st, preferred_element_type=F32)
        o_ref[b] = jnp.dot(f.astype(BF16), wf_ref[...], preferred_element_type=F32).astype(BF16)


def _fourier_prompt(zf, cdsd, cs, wf):
    nb = 4
    blk = pl.BlockSpec((nb, SEQ, FOURIER_W), lambda i: (i, 0, 0))
    return pl.pallas_call(
        functools.partial(_fourier_prompt_kernel, nb),
        grid=(BATCH // nb,),
        in_specs=[blk, _full(cdsd), _full(cs), _full(wf)],
        out_specs=blk,
        out_shape=jax.ShapeDtypeStruct((BATCH, SEQ, FOURIER_W), BF16),
        compiler_params=_cparams(("arbitrary",)),
        name="fourier_prompt",
    )(zf, cdsd, cs, wf)


def _fourier_sample_kernel(zf_ref, cdsd_ref, ab_ref, wf_ref, o_ref, tt_ref):
    @pl.when(pl.program_id(0) == 0)
    def _():
        for b in range(DEC_BATCH):
            t = jnp.dot(zf_ref[b], cdsd_ref[...], preferred_element_type=F32).astype(BF16)
            tt_ref[b, 0:DEC_SEQ, :] = t[:, :FOURIER_W]
            tt_ref[b, DEC_SEQ:2 * DEC_SEQ, :] = t[:, FOURIER_W:]

    for b in range(DEC_BATCH):
        f = jnp.dot(ab_ref[...], tt_ref[b], preferred_element_type=F32)
        o_ref[b] = jnp.dot(f.astype(BF16), wf_ref[...], preferred_element_type=F32).astype(BF16)


def _fourier_sample(zf, cdsd, ab, wf):
    tk = 512
    return pl.pallas_call(
        _fourier_sample_kernel,
        grid=(DEC_SEQ // tk,),
        in_specs=[_full(zf), _full(cdsd), pl.BlockSpec((tk, 2 * DEC_SEQ), lambda i: (i, 0)), _full(wf)],
        out_specs=pl.BlockSpec((DEC_BATCH, tk, FOURIER_W), lambda i: (0, i, 0)),
        out_shape=jax.ShapeDtypeStruct((DEC_BATCH, DEC_SEQ, FOURIER_W), BF16),
        scratch_shapes=[pltpu.VMEM((DEC_BATCH, 2 * DEC_SEQ, FOURIER_W), BF16)],
        compiler_params=_cparams(("arbitrary",)),
        name="fourier_sample",
    )(zf, cdsd, ab, wf)


def _split3(x):
    hi = x.astype(BF16).astype(F32)
    mid = (x - hi).astype(BF16).astype(F32)
    lo = (x - hi - mid).astype(BF16).astype(F32)
    return hi, mid, lo


def _gate_kernel(g_ref, o_ref):
    L = MLSTM_CHUNK
    nrow = N_GATES // 2
    row = lax.broadcasted_iota(jnp.int32, (L, L), 0)
    col = lax.broadcasted_iota(jnp.int32, (L, L), 1)
    tri_pre = jnp.where(row <= col, 1.0, 0.0).astype(BF16)
    tri_suf = jnp.where(row >= col, 1.0, 0.0).astype(BF16)
    is_fwd = lax.broadcasted_iota(jnp.int32, (nrow, L), 0) < MLSTM_HEADS
    lane = lax.broadcasted_iota(jnp.int32, (nrow, L), 1)
    for bi in range(g_ref.shape[0]):
        for c in range(g_ref.shape[2] // L):
            cols = slice(c * L, (c + 1) * L)
            ig = g_ref[bi, 0:nrow, cols]
            lf = _log_sigmoid(g_ref[bi, nrow:, cols])
            parts = jnp.concatenate(_split3(lf), axis=0).astype(BF16)
            pre = jnp.dot(parts, tri_pre, preferred_element_type=F32)
            suf = jnp.dot(parts, tri_suf, preferred_element_type=F32)
            fold = lambda a: a[0:nrow] + a[nrow:2 * nrow] + a[2 * nrow:]
            b = jnp.where(is_fwd, fold(pre), fold(suf))
            r = ig - b
            pm = sm = r
            sh = 1
            while sh < L:
                pm = jnp.maximum(pm, jnp.where(lane >= sh, pltpu.roll(pm, sh, 1), NEG))
                sm = jnp.maximum(sm, jnp.where(lane < L - sh, pltpu.roll(sm, L - sh, 1), NEG))
                sh *= 2
            cm = jnp.where(is_fwd, pm, sm)
            for q, val in enumerate((b, r, cm)):
                for dh in range(nrow):
                    o_ref[bi, dh, q:q + 1, cols] = val[dh:dh + 1]


def _gate_prep(p, gt):
    bb = max(1, DEC_SEQ // p.seq)
    nrow = N_GATES // 2
    return pl.pallas_call(
        _gate_kernel,
        grid=(p.nb // bb,),
        in_specs=[pl.BlockSpec((bb, N_GATES, p.seq), lambda i: (i, 0, 0))],
        out_specs=pl.BlockSpec((bb, nrow, 3, p.seq), lambda i: (i, 0, 0, 0)),
        out_shape=jax.ShapeDtypeStruct((p.nb, nrow, 3, p.seq), F32),
        compiler_params=_cparams(("arbitrary",)),
        name="gate_prep_%d" % p.seq,
    )(gt)


def _mlstm_chunk(q, k, vt, pr, ct, m, fwd):
    L = q.shape[0]
    scale = MLSTM_DH ** -0.5
    b, r, cm = pr[0:1], pr[1:2], pr[2:3]
    ones = jnp.ones((3, L), F32)
    zeros = jnp.zeros((SUBLANES - 6, L), F32)
    lhs = jnp.concatenate(_split3(r) + (ones, zeros), axis=0).astype(BF16)
    rhs = jnp.concatenate((ones,) + _split3(-cm) + (zeros,), axis=0).astype(BF16)
    arg = lax.dot_general(lhs, rhs, _TN, preferred_element_type=F32)
    row = lax.broadcasted_iota(jnp.int32, (L, L), 0)
    col = lax.broadcasted_iota(jnp.int32, (L, L), 1)
    e = jnp.where((row <= col) if fwd else (row >= col), jnp.exp(arg), 0.0)
    st = lax.dot_general(k, q, _NT, preferred_element_type=F32)
    num = jnp.dot(vt, (st * e).astype(BF16), preferred_element_type=F32)

    mx = jnp.maximum(m, cm)
    cq = lax.dot_general(ct.astype(BF16), q, _NT, preferred_element_type=F32)
    num = (jnp.exp(cm - mx) * scale) * num + jnp.exp(m - mx) * cq
    den = num[N_AUG:N_AUG + 1, :]
    h = num * (1.0 / jnp.maximum(jnp.abs(den), jnp.exp(-(b + mx))))

    last = L - 1 if fwd else 0
    cm_last = cm[:, last:last + 1]
    mx_last = jnp.maximum(m, cm_last)
    vw = (vt.astype(F32) * jnp.exp(r - cm_last)).astype(BF16)
    ct_new = (jnp.exp(m - mx_last) * ct
              + (jnp.exp(cm_last - mx_last) * scale) * jnp.dot(vw, k, preferred_element_type=F32))
    return h, ct_new, b[:, last:last + 1] + mx_last


def _mlstm_kernel(nc, m0_ref, q_ref, k_ref, vt_ref, zo_ref, prf_ref, prb_ref, c0_ref, nw_ref,
                  o_ref, cfin_ref, mfin_ref, hbuf_ref):
    L = MLSTM_CHUNK
    bi = pl.program_id(0)
    hd = pl.program_id(1)
    vrow = lax.broadcasted_iota(jnp.int32, (HEAD_PAD, L), 0)

    def chunk(ci, pr_ref, ct, m, dr):
        rows = pl.ds(pl.multiple_of(ci * L, L), L)
        vt = vt_ref[:, rows]
        vt = jnp.where(vrow == N_AUG, jnp.ones_like(vt), vt)
        h, ct, m = _mlstm_chunk(q_ref[rows, :], k_ref[rows, :], vt, pr_ref[:, rows], ct, m, dr == 0)
        hbuf_ref[dr, ci] = h
        return ct, m

    def step(i, carry):
        cf, mf, cb, mb = carry
        cf, mf = chunk(i, prf_ref, cf, mf, 0)
        cb, mb = chunk(nc - 1 - i, prb_ref, cb, mb, 1)
        return cf, mf, cb, mb

    carry = tuple(x for dr in range(2)
                  for x in (c0_ref[0, dr, 0].T, jnp.full((1, 1), m0_ref[bi, dr, hd], F32)))
    carry = step(0, carry) if nc == 1 else lax.fori_loop(0, nc, step, carry)
    for dr in range(2):
        cfin_ref[0, dr, 0] = carry[2 * dr].T
        mfin_ref[0, dr, 0] = jnp.broadcast_to(carry[2 * dr + 1], (SUBLANES, LANES))

    nw = nw_ref[0]

    def finish(ci, _):
        rows = pl.ds(pl.multiple_of(ci * L, L), L)
        h = jnp.where(vrow < MLSTM_DH, hbuf_ref[0, ci] + hbuf_ref[1, ci], 0.0)
        ms = jnp.sum(h * h, axis=0, keepdims=True) * (1.0 / MLSTM_DH)
        hn = (h * lax.rsqrt(ms + EPS) * nw).T
        o_ref[rows, :] = (hn * jax.nn.sigmoid(zo_ref[rows, :].astype(F32))).astype(BF16)
        return 0

    if nc == 1:
        finish(0, 0)
    else:
        lax.fori_loop(0, nc, finish, 0)


def _mlstm(p, q, k, vt, zo, pr, c0, m0, nw):
    nc = p.seq // MLSTM_CHUNK
    tok = pl.BlockSpec((None, p.seq, HEAD_PAD), lambda b, h: (b, 0, h))
    st = pl.BlockSpec((1, 2, 1, HEAD_PAD, HEAD_PAD), lambda b, h: (b, 0, h, 0, 0))
    prs = lambda dr: pl.BlockSpec((None, None, 3, p.seq), lambda b, h: (b, dr * MLSTM_HEADS + h, 0, 0))
    return pl.pallas_call(
        functools.partial(_mlstm_kernel, nc),
        grid=(p.nb, MLSTM_HEADS),
        in_specs=[pl.BlockSpec(memory_space=pltpu.SMEM), tok, tok,
                  pl.BlockSpec((None, HEAD_PAD, p.seq), lambda b, h: (b, h, 0)), tok, prs(0), prs(1),
                  st, pl.BlockSpec((1, HEAD_PAD, 1), lambda b, h: (h, 0, 0))],
        out_specs=[tok, st,
                   pl.BlockSpec((1, 2, 1, SUBLANES, LANES), lambda b, h: (b, 0, h, 0, 0))],
        out_shape=[jax.ShapeDtypeStruct((p.nb, p.seq, HEADS_W), BF16),
                   jax.ShapeDtypeStruct((p.nb, 2, MLSTM_HEADS, HEAD_PAD, HEAD_PAD), F32),
                   jax.ShapeDtypeStruct((p.nb, 2, MLSTM_HEADS, SUBLANES, LANES), F32)],
        scratch_shapes=[pltpu.VMEM((2, nc, HEAD_PAD, MLSTM_CHUNK), F32)],
        compiler_params=_cparams(("arbitrary", "arbitrary")),
        name="mlstm_%d" % p.seq,
    )(m0, q, k, vt, zo, pr, pr, c0, nw)


def _cpow(br, bi, e, nbits):
    pr = pi = None
    for bit in range(nbits):
        sel = ((e >> bit) & 1) == 1
        if pr is None:
            pr, pi = jnp.where(sel, br, 1.0), jnp.where(sel, bi, 0.0)
        else:
            pr, pi = jnp.where(sel, pr * br - pi * bi, pr), jnp.where(sel, pr * bi + pi * br, pi)
        if bit + 1 < nbits:
            br, bi = br * br - bi * bi, 2.0 * br * bi
    return pr, pi


def _s5_prep_kernel(lamc_re_ref, lamc_im_ref, lamr_re_ref, lamr_im_ref, lstep_ref,
                    bt_re_ref, bt_im_ref, ct_re_ref, ct_im_ref,
                    t_ref, m_ref, n_ref, a_ref):
    C = S5_CHUNK
    nbits = C.bit_length()
    kk = lax.broadcasted_iota(jnp.int32, (S5_ST, S5_IN), 1) >> 4
    top = lax.broadcasted_iota(jnp.int32, (S5_ST, S5_IN), 0) < S5_STATE
    srow = lax.broadcasted_iota(jnp.int32, (S5_IN, S5_ST), 0) >> 4
    left = lax.broadcasted_iota(jnp.int32, (S5_IN, S5_ST), 1) < S5_STATE
    left16 = lax.broadcasted_iota(jnp.int32, (S5_GROUP_CH, S5_ST), 1) < S5_STATE
    left1 = lax.broadcasted_iota(jnp.int32, (1, S5_ST), 1) < S5_STATE
    lane = lax.broadcasted_iota(jnp.int32, (S5_GROUP_CH, S5_IN), 1)
    ct_re = ct_re_ref[0, 0]
    ct_im = ct_im_ref[0, 0]
    resp = []
    for d in range(2):
        step = jnp.exp(lstep_ref[0, d, 0])
        lr_c, li_c = lamc_re_ref[0, d, 0] * step, lamc_im_ref[0, d, 0] * step
        lr_r, li_r = lamr_re_ref[0, d, 0], lamr_im_ref[0, d, 0]
        lbc_re, lbc_im = jnp.exp(lr_c) * jnp.cos(li_c), jnp.exp(lr_c) * jnp.sin(li_c)
        mag = jnp.exp(lr_r * step)
        lb_re, lb_im = mag * jnp.cos(li_r * step), mag * jnp.sin(li_r * step)

        def cp(lag):
            pr, pi = _cpow(lbc_re, lbc_im, lag, nbits)
            return ct_re * pr - ct_im * pi, ct_re * pi + ct_im * pr

        nr, ni = lb_re - 1.0, lb_im
        den = lr_r * lr_r + li_r * li_r
        kap_re = (nr * lr_r + ni * li_r) / den
        kap_im = (ni * lr_r - nr * li_r) / den
        bb_re = kap_re * bt_re_ref[0, 0] - kap_im * bt_im_ref[0, 0]
        bb_im = kap_re * bt_im_ref[0, 0] + kap_im * bt_re_ref[0, 0]

        cpr, cpi = cp(kk if d == 0 else (C - 1) - kk)
        resp.append(jnp.dot(jnp.where(left16, bb_re, -bb_im), jnp.where(top, cpr, cpi),
                            precision=lax.Precision.HIGHEST, preferred_element_type=F32))

        cpr, cpi = cp(kk + 1 if d == 0 else C - kk)
        m_ref[0, d, 0] = jnp.where(top, cpr, -cpi).astype(BF16)

        pr, pi = _cpow(lb_re, lb_im, (C - 1) - srow if d == 0 else srow, nbits)
        bt_r = jnp.concatenate([bb_re] * C, axis=0)
        bt_i = jnp.concatenate([bb_im] * C, axis=0)
        n_re, n_im = pr * bt_r - pi * bt_i, pr * bt_i + pi * bt_r
        n_ref[0, d, 0] = jnp.concatenate([jnp.where(left, n_re, n_im), jnp.where(left, n_im, n_re)],
                                         axis=1).astype(BF16)

        ar, ai = _cpow(lb_re, lb_im, jnp.full((1, S5_ST), C, jnp.int32), nbits)
        a2 = jnp.where(left1, -ai, ai)
        a_ref[0, d, 0] = jnp.concatenate([jnp.concatenate([ar, ar], axis=1),
                                          jnp.concatenate([a2, -a2], axis=1)], axis=0)

    rf, rb = resp
    for s in range(C):
        nf = S5_GROUP_CH * s
        blk = jnp.where(lane >= nf, pltpu.roll(rf, nf, 1) if nf else rf, 0.0)
        nb = S5_GROUP_CH * (C - 1 - s)
        blk = blk + jnp.where(lane < S5_IN - nb, pltpu.roll(rb, S5_IN - nb, 1) if nb else rb, 0.0)
        t_ref[0, 0, S5_GROUP_CH * s:S5_GROUP_CH * (s + 1), :] = blk.astype(BF16)


def _s5_prep(lam_re, lam_im, log_step, b_re, b_im, c_re, c_im):
    G = S5_GROUPS
    dup = lambda a: jnp.concatenate([a, a], axis=-1)
    lamc = [dup(a).reshape(DEPTH, 2, G, S5_ST, 1) for a in (lam_re, lam_im)]
    lamr = [dup(a).reshape(DEPTH, 2, G, 1, S5_ST) for a in (lam_re, lam_im)]
    lstep = log_step.reshape(DEPTH, 2, G, 1, 1)
    bt = [dup(jnp.swapaxes(a, 2, 3)) for a in (b_re, b_im)]
    ct = [jnp.tile(jnp.swapaxes(a, 2, 3), (1, 1, 2, S5_CHUNK)) for a in (c_re, c_im)]
    dspec = lambda r, c: pl.BlockSpec((1, 2, 1, r, c), lambda l, g: (l, 0, g, 0, 0))
    gspec = lambda r, c: pl.BlockSpec((1, 1, r, c), lambda l, g: (l, g, 0, 0))
    return pl.pallas_call(
        _s5_prep_kernel,
        grid=(DEPTH, G),
        in_specs=[dspec(S5_ST, 1), dspec(S5_ST, 1), dspec(1, S5_ST), dspec(1, S5_ST), dspec(1, 1),
                  gspec(S5_GROUP_CH, S5_ST), gspec(S5_GROUP_CH, S5_ST),
                  gspec(S5_ST, S5_IN), gspec(S5_ST, S5_IN)],
        out_specs=[gspec(S5_IN, S5_IN), dspec(S5_ST, S5_IN), dspec(S5_IN, S5_ST2), dspec(2, S5_ST2)],
        out_shape=[jax.ShapeDtypeStruct((DEPTH, G, S5_IN, S5_IN), BF16),
                   jax.ShapeDtypeStruct((DEPTH, 2, G, S5_ST, S5_IN), BF16),
                   jax.ShapeDtypeStruct((DEPTH, 2, G, S5_IN, S5_ST2), BF16),
                   jax.ShapeDtypeStruct((DEPTH, 2, G, 2, S5_ST2), F32)],
        compiler_params=_cparams(("arbitrary", "arbitrary")),
        name="s5_prep",
    )(*lamc, *lamr, lstep, *bt, *ct)


def _s5_kernel(nseg, nchunks, zr_ref, t_ref, m_ref, n_ref, a_ref, x0_ref, d_ref,
               y_ref, xfin_ref, v_ref, xp_ref):
    W = S5_GB * S5_ST2
    for gl in range(S5_GB):
        u = zr_ref[0, :, gl * S5_IN:(gl + 1) * S5_IN].astype(BF16)
        for d in range(2):
            v_ref[d, :, gl * S5_ST2:(gl + 1) * S5_ST2] = jnp.dot(
                u, n_ref[d, gl], preferred_element_type=F32)

    def halves(x, which):
        return [x[:, g * S5_ST2 + h * S5_ST:g * S5_ST2 + (h + 1) * S5_ST]
                for g in range(S5_GB) for h in which]

    per = SUBLANES // TB
    steps = nchunks // per

    for d in range(2):
        a = a_ref[d, 0:1, :]
        a2 = a_ref[d, 1:2, :]

        def advance(x, v):
            swapped = jnp.concatenate(halves(x, (1, 0)), axis=1)
            return a * x + a2 * swapped + v

        for seg in range(nseg):
            def step(i, x):
                si = i if d == 0 else steps - 1 - i
                rows = pl.ds(pl.multiple_of(seg * nchunks * TB + si * SUBLANES, SUBLANES), SUBLANES)
                v = v_ref[d, rows, :]
                order = range(per) if d == 0 else range(per - 1, -1, -1)
                entering = [None] * per
                for j in order:
                    entering[j] = jnp.concatenate(halves(x, (0,)), axis=1)
                    x = advance(x, v[j * TB:(j + 1) * TB])
                xp_ref[d, rows, :] = jnp.concatenate(entering, axis=0)
                return x

            x = lax.fori_loop(0, steps, step, x0_ref[d, seg * TB:(seg + 1) * TB, :])
            xfin_ref[d, seg * TB:(seg + 1) * TB, :] = jnp.concatenate(halves(x, (0,)), axis=1)

    for gl in range(S5_GB):
        cols = slice(gl * S5_IN, (gl + 1) * S5_IN)
        u = zr_ref[0, :, cols]
        y = jnp.dot(u.astype(BF16), t_ref[gl], preferred_element_type=F32)
        for d in range(2):
            y = y + jnp.dot(xp_ref[d, :, gl * S5_ST:(gl + 1) * S5_ST].astype(BF16), m_ref[d, gl],
                            preferred_element_type=F32)
        y_ref[0, :, cols] = jax.nn.gelu(y + d_ref[0, :, cols] * u)


def _s5(p, zr, t, m, n, a, x0, dt):
    G = S5_GROUPS
    nseg = p.nb // TB
    a = jnp.transpose(a, (0, 2, 1, 3)).reshape(2, 2, G * S5_ST2)
    row = pl.BlockSpec((1, R_PASS, S5_ROW), lambda j: (j, 0, 0))
    dsp = lambda r, c: pl.BlockSpec((2, S5_GB, r, c), lambda j: (0, j, 0, 0))
    lsp = lambda r, w: pl.BlockSpec((2, r, S5_GB * w), lambda j: (0, 0, j))
    return pl.pallas_call(
        functools.partial(_s5_kernel, nseg, p.seq // S5_CHUNK),
        grid=(S5_NB,),
        in_specs=[row, pl.BlockSpec((S5_GB, S5_IN, S5_IN), lambda j: (j, 0, 0)),
                  dsp(S5_ST, S5_IN), dsp(S5_IN, S5_ST2), lsp(2, S5_ST2), lsp(p.nb, S5_ST2),
                  pl.BlockSpec((1, 1, S5_ROW), lambda j: (j, 0, 0))],
        out_specs=[row, lsp(p.nb, S5_ST)],
        out_shape=[jax.ShapeDtypeStruct((S5_NB, R_PASS, S5_ROW), F32),
                   jax.ShapeDtypeStruct((2, p.nb, G * S5_ST), F32)],
        scratch_shapes=[pltpu.VMEM((2, R_PASS, S5_GB * S5_ST2), F32),
                        pltpu.VMEM((2, R_PASS, S5_GB * S5_ST), F32)],
        compiler_params=_cparams(("arbitrary",)),
        name="s5_scan_%d" % p.seq,
    )(zr, t, m, n, a, x0, dt)


def _mix_kernel(x_ref, mod_ref, fo_ref, mo_ref, ys_ref, wglu_ref,
                wof_ref, wom_ref, wos_ref, nw_ref, x1_ref, xn_ref, scr_ref):
    mod = mod_ref[...]
    g1 = mod[:, :, 2 * D_MODEL:3 * D_MODEL]
    blocks = []
    for bl in range(S5_NB):
        by_group = [ys_ref[bl, :, gl * S5_IN:(gl + 1) * S5_IN] for gl in range(S5_GB)]
        for t in range(S5_CHUNK):
            lo = t * S5_GROUP_CH
            scr_ref[pl.ds(t, TROWS, stride=S5_CHUNK), :] = jnp.concatenate(
                [y[:, lo:lo + S5_GROUP_CH] for y in by_group], axis=1)
        slabs = [scr_ref[_slab(b, c)[1], :] for b in range(TB) for c in range(TT // S5_CHUNK)]
        blocks.append(jnp.concatenate(slabs, axis=0))
    y = jnp.concatenate(blocks, axis=1).astype(BF16)
    gg = jnp.dot(y, wglu_ref[...], preferred_element_type=F32)
    s_out = (gg[:, :S5_W] * jax.nn.sigmoid(gg[:, S5_W:])).astype(BF16)
    flat = lambda ref: ref[...].reshape(TB * TT, ref.shape[-1])
    mix = (jnp.dot(flat(fo_ref), wof_ref[...], preferred_element_type=F32)
           + jnp.dot(flat(mo_ref), wom_ref[...], preferred_element_type=F32)
           + jnp.dot(s_out, wos_ref[...], preferred_element_type=F32))
    x1 = x_ref[...] + g1 * mix.reshape(TB, TT, D_MODEL)
    x1_ref[...] = x1
    xn_ref[...] = _mod_norm(x1, mod, nw_ref[...], 3).reshape(TB, TT, D_MODEL)


def _mix(p, x, mods, fo, mo, ys, wglu, wof, wom, wos, nw):
    return pl.pallas_call(
        _mix_kernel,
        grid=(p.nb // TB, p.seq // TT),
        in_specs=[_tile_spec(D_MODEL),
                  pl.BlockSpec((TB, 1, 6 * D_MODEL), lambda j, k: (j, 0, 0)),
                  _tile_spec(FOURIER_W), _tile_spec(HEADS_W), _chunk_rows_spec(p),
                  _full(wglu), _full(wof), _full(wom), _full(wos), _full(nw)],
        out_specs=[_tile_spec(D_MODEL), _tile_spec(D_MODEL)],
        out_shape=[jax.ShapeDtypeStruct((p.nb, p.seq, D_MODEL), F32),
                   jax.ShapeDtypeStruct((p.nb, p.seq, D_MODEL), BF16)],
        scratch_shapes=[pltpu.VMEM((TB * TT, LANES), F32)],
        compiler_params=_cparams(("arbitrary", "arbitrary")),
        name="mix_out_%d" % p.seq,
    )(x, mods, fo, mo, ys, wglu, wof, wom, wos, nw)


def _ffn_kernel(final, xn_ref, x1_ref, mod_ref, wg_ref, wu_ref, wd_ref, nf_ref, o_ref, acc_ref):
    j = pl.program_id(1)
    xn = xn_ref[...]
    a = jnp.dot(xn, wg_ref[...], preferred_element_type=F32)
    u = jnp.dot(xn, wu_ref[...], preferred_element_type=F32)
    h = (a * jax.nn.sigmoid(a) * u).astype(BF16)
    part = jnp.dot(h, wd_ref[...], preferred_element_type=F32)

    @pl.when(j == 0)
    def _():
        acc_ref[...] = part

    @pl.when(j > 0)
    def _():
        acc_ref[...] += part

    @pl.when(j == pl.num_programs(1) - 1)
    def _():
        g2 = mod_ref[0][:, 5 * D_MODEL:6 * D_MODEL]
        x2 = x1_ref[...] + g2 * acc_ref[...]
        if final:
            x2 = x2 * lax.rsqrt(jnp.mean(x2 * x2, axis=-1, keepdims=True) + EPS) * nf_ref[...]
        o_ref[...] = x2


def _ffn(p, final, xn, x1, mods, wg, wu, wd, nf):
    tok = pl.BlockSpec((TM, D_MODEL), lambda i, j: (i, 0))
    return pl.pallas_call(
        functools.partial(_ffn_kernel, final),
        grid=(T_PASS // TM, D_FF // TF),
        in_specs=[tok, tok,
                  pl.BlockSpec((1, 1, 6 * D_MODEL), lambda i, j: (p.mod_row(i), 0, 0)),
                  pl.BlockSpec((D_MODEL, TF), lambda i, j: (0, j)),
                  pl.BlockSpec((D_MODEL, TF), lambda i, j: (0, j)),
                  pl.BlockSpec((TF, D_MODEL), lambda i, j: (j, 0)),
                  pl.BlockSpec((1, D_MODEL), lambda i, j: (0, 0))],
        out_specs=tok,
        out_shape=jax.ShapeDtypeStruct((T_PASS, D_MODEL), F32),
        scratch_shapes=[pltpu.VMEM((TM, D_MODEL), F32)],
        compiler_params=_cparams(("arbitrary", "arbitrary")),
        name="ffn_%d" % p.seq,
    )(xn, x1, mods, wg, wu, wd, nf)


def _pad_heads(a, axis):
    shape = a.shape[:axis] + (MLSTM_HEADS, MLSTM_DH) + a.shape[axis + 1:]
    pad = [(0, 0)] * (a.ndim + 1)
    pad[axis + 1] = (0, HEAD_PAD - MLSTM_DH)
    return jnp.pad(a.reshape(shape), pad).reshape(a.shape[:axis] + (HEADS_W,) + a.shape[axis + 1:])


def _mlstm_state_in(c, n):
    cn = jnp.concatenate([c, n[..., None]], axis=-1)
    return jnp.pad(cn, ((0, 0),) * 3 + ((0, HEAD_PAD - MLSTM_DH), (0, HEAD_PAD - MLSTM_DH - 1)))


def kernel(x_prompt, x_sample, state_mlstm_C, state_mlstm_n, state_mlstm_m, state_s5_re, state_s5_im,
           c, c_ctx, w_ada, b_ada, norm1_w, norm2_w, w_in, b_gates, w_fourier, mlstm_norm_w,
           s5_lambda_re, s5_lambda_im, s5_log_step, s5_b_re, s5_b_im, s5_c_re, s5_c_im, s5_d,
           w_glu, w_out, w_gate, w_up, w_down, norm_f):
    xs = {PROMPT: x_prompt, SAMPLE: x_sample}
    cc = jnp.concatenate([c_ctx[None], c, jnp.zeros((N_MODS - 1 - DEC_BATCH, D_MODEL), F32)], axis=0)
    mods_all = _ada(cc, w_ada, b_ada).reshape(DEPTH, N_MODS, 1, 6 * D_MODEL)
    cdsd, cs, ab = (jnp.asarray(a.astype(np.float32)).astype(BF16) for a in _dft_consts())
    s5_ops = _s5_prep(s5_lambda_re, s5_lambda_im, s5_log_step, s5_b_re, s5_b_im, s5_c_re, s5_c_im)

    o_q = FOURIER_W
    o_g = o_q + 3 * MLSTM_W
    o_o = o_g + N_GATES
    o_u = o_o + MLSTM_W

    new_c, new_n, new_m, new_re, new_im = [], [], [], [], []
    for l in range(DEPTH):
        mods = mods_all[l]
        seq_mods = {PROMPT: jnp.broadcast_to(mods[0:1], (BATCH, 1, 6 * D_MODEL)),
                    SAMPLE: mods[1:1 + DEC_BATCH]}
        wl = w_in[l]
        heads = lambda o: _pad_heads(wl[:, o:o + MLSTM_W], 1)
        w_cat = jnp.concatenate([heads(o_q), heads(o_q + MLSTM_W), heads(o_o), wl[:, :FOURIER_W], wl[:, o_u:]],
                                axis=1).astype(BF16)
        wv_t = heads(o_q + 2 * MLSTM_W).T.astype(BF16)
        gate_perm = np.arange(N_GATES).reshape(2, 2, MLSTM_HEADS).transpose(1, 0, 2).reshape(-1)
        wg_t = wl[:, o_g:o_o].T[gate_perm].astype(BF16)
        bg = b_gates[l][gate_perm][:, None]
        wf = w_fourier[l].astype(BF16)
        nw = _pad_heads(mlstm_norm_w[l], 0).reshape(MLSTM_HEADS, HEAD_PAD, 1)
        ops = [o[l] for o in s5_ops]
        dt = jnp.tile(s5_d[l][:, None, :], (1, S5_CHUNK, 1)).reshape(S5_NB, 1, S5_ROW)
        wo = w_out[l]
        wom = jnp.pad(wo[FOURIER_W:FOURIER_W + MLSTM_W].reshape(MLSTM_HEADS, MLSTM_DH, D_MODEL),
                      ((0, 0), (0, HEAD_PAD - MLSTM_DH), (0, 0))).reshape(HEADS_W, D_MODEL).astype(BF16)
        wof = wo[:FOURIER_W].astype(BF16)
        wos = wo[FOURIER_W + MLSTM_W:].astype(BF16)
        wglu = w_glu[l].astype(BF16)
        wg, wu, wd = w_gate[l].astype(BF16), w_up[l].astype(BF16), w_down[l].astype(BF16)

        for p in (PROMPT, SAMPLE):
            x = xs[p]
            zq, zk, zo, zf, zu, vt, gt = _in_proj(p, x, seq_mods[p], norm1_w[l][None], w_cat, wv_t, wg_t, bg)
            fo = _fourier_prompt(zf, cdsd, cs, wf) if p is PROMPT else _fourier_sample(zf, cdsd, ab, wf)

            if p is PROMPT:
                c0 = jnp.zeros((BATCH, 2, MLSTM_HEADS, HEAD_PAD, HEAD_PAD), F32)
                m0 = jnp.zeros((BATCH, 2, MLSTM_HEADS), F32)
                x0 = jnp.zeros((2, BATCH, S5_GROUPS * S5_ST2), F32)
            else:
                c0 = _mlstm_state_in(state_mlstm_C[:, l], state_mlstm_n[:, l])
                m0 = state_mlstm_m[:, l]
                re, im = state_s5_re[:, l], state_s5_im[:, l]
                x0 = jnp.transpose(jnp.concatenate([re, im, im, re], axis=-1), (1, 0, 2, 3))
                x0 = x0.reshape(2, DEC_BATCH, S5_GROUPS * S5_ST2)
            mo, cfin, mfin = _mlstm(p, zq, zk, vt, zo, _gate_prep(p, gt), c0, m0, nw)

            ys, xfin = _s5(p, zu, *ops, x0, dt)

            if p is PROMPT:
                new_c.append(cfin[..., :MLSTM_DH, :MLSTM_DH])
                new_n.append(cfin[..., :MLSTM_DH, N_AUG])
                new_m.append(mfin[..., 0, 0])
                xfin = xfin.reshape(2, BATCH, S5_GROUPS, 2, S5_STATE)
                new_re.append(jnp.transpose(xfin[:, :, :, 0], (1, 0, 2, 3)))
                new_im.append(jnp.transpose(xfin[:, :, :, 1], (1, 0, 2, 3)))

            x1, xn2 = _mix(p, x, seq_mods[p], fo, mo, ys, wglu, wof, wom, wos, norm2_w[l][None])
            x2 = _ffn(p, l == DEPTH - 1, xn2.reshape(T_PASS, D_MODEL), x1.reshape(T_PASS, D_MODEL),
                      mods, wg, wu, wd, norm_f[None])
            xs[p] = x2.reshape(p.nb, p.seq, D_MODEL)

    stack = lambda parts: jnp.stack(parts, axis=1)
    return (xs[PROMPT], xs[SAMPLE], stack(new_c), stack(new_n), stack(new_m), stack(new_re), stack(new_im))
```

```python
import collections
import functools
import math

import numpy as np
import jax
import jax.numpy as jnp
from jax import lax
from jax.experimental import pallas as pl
from jax.experimental.pallas import tpu as pltpu

F32 = jnp.float32
BF16 = jnp.bfloat16

D_MODEL = 1024
BATCH = 32
SEQ = 256
DEPTH = 2
DEC_BATCH = 4
DEC_SEQ = 2048
GRID_W = 64
FOURIER_W = 256
FOURIER_DH = 64
MLSTM_W = 384
MLSTM_HEADS = 4
MLSTM_DH = 96
S5_W = 384
S5_GROUP_CH = 16
S5_GROUPS = 24
S5_STATE = 64
N_GATES = 16
D_FF = 2816
EPS = 1e-6

LANES = 128
SUBLANES = 8
VMEM_LIMIT = 56 * 1024 * 1024

HEAD_PAD = LANES
HEADS_W = MLSTM_HEADS * HEAD_PAD
N_AUG = MLSTM_DH
Z_W = 3 * HEADS_W + FOURIER_W + S5_W
MLSTM_CHUNK = 256
S5_CHUNK = 16
S5_IN = S5_CHUNK * S5_GROUP_CH
S5_ST = 2 * S5_STATE
S5_ST2 = 2 * S5_ST
S5_GB = LANES // S5_GROUP_CH
S5_NB = S5_W // LANES
S5_ROW = S5_GB * S5_IN
TB = 4
TT = 128
TROWS = TB * TT // S5_CHUNK
TM = 512
TF = D_FF // 2
HPS = 2
N_MODS = 8
NEG = -1e30

Pass = collections.namedtuple("Pass", "nb seq mod_first mod_each")
PROMPT = Pass(BATCH, SEQ, DEC_BATCH, False)
SAMPLE = Pass(DEC_BATCH, DEC_SEQ, 0, True)
T_PASS = BATCH * SEQ
assert T_PASS == DEC_BATCH * DEC_SEQ
R_PASS = T_PASS // S5_CHUNK

_NT = (((1,), (1,)), ((), ()))
_TN = (((0,), (0,)), ((), ()))


def _cparams(sem):
    return pltpu.CompilerParams(dimension_semantics=sem, vmem_limit_bytes=VMEM_LIMIT)


def _full(a):
    return pl.BlockSpec(a.shape, lambda *_: (0,) * a.ndim)


def _layer(a, l):
    return pl.BlockSpec((None,) + a.shape[1:], lambda *_: (l,) + (0,) * (a.ndim - 1))


def _tile_mod_spec(p, l):
    if p.mod_each:
        return pl.BlockSpec((None, TB, 1, 6 * D_MODEL), lambda j, k: (l, p.mod_first // TB + j, 0, 0))
    return pl.BlockSpec((None, 1, 1, 6 * D_MODEL), lambda j, k: (l, p.mod_first, 0, 0))


def _log_sigmoid(x):
    return jnp.minimum(x, 0.0) - jnp.log1p(jnp.exp(-jnp.abs(x)))


def _tile_spec(w):
    return pl.BlockSpec((TB, TT, w), lambda j, k: (j, k, 0))


def _chunk_rows_spec(p):
    per_group = p.seq // TT
    return pl.BlockSpec((S5_NB, TROWS, S5_ROW), lambda j, k: (0, j * per_group + k, 0))


def _slab(b, c):
    tok = slice(b * TT + c * S5_CHUNK, b * TT + (c + 1) * S5_CHUNK)
    chk = slice((c * TB + b) * S5_CHUNK, (c * TB + b + 1) * S5_CHUNK)
    return tok, chk


def _ada_kernel(c_ref, w_ref, b_ref, o_ref):
    a = c_ref[...]
    a = (a * jax.nn.sigmoid(a)).astype(BF16)
    o_ref[0] = jnp.dot(a, w_ref[0].astype(BF16), preferred_element_type=F32) + b_ref[0]


def _ada(cc, w_ada, b_ada):
    tn = 512
    return pl.pallas_call(
        _ada_kernel,
        grid=(DEPTH, 6 * D_MODEL // tn),
        in_specs=[pl.BlockSpec((N_MODS, D_MODEL), lambda l, j: (0, 0)),
                  pl.BlockSpec((1, D_MODEL, tn), lambda l, j: (l, 0, j)),
                  pl.BlockSpec((1, 1, tn), lambda l, j: (l, 0, j))],
        out_specs=pl.BlockSpec((1, N_MODS, tn), lambda l, j: (l, 0, j)),
        out_shape=jax.ShapeDtypeStruct((DEPTH, N_MODS, 6 * D_MODEL), F32),
        compiler_params=_cparams(("arbitrary", "arbitrary")),
        name="ada_mod",
    )(cc, w_ada, b_ada.reshape(DEPTH, 1, 6 * D_MODEL))


def _mod_norm(x3, mod, nw, first):
    sh = mod[:, :, first * D_MODEL:(first + 1) * D_MODEL]
    sc = mod[:, :, (first + 1) * D_MODEL:(first + 2) * D_MODEL]
    y = x3 * lax.rsqrt(jnp.mean(x3 * x3, axis=-1, keepdims=True) + EPS) * nw
    return (y * (1.0 + sc) + sh).reshape(TB * TT, D_MODEL).astype(BF16)


def _in_kernel(x_ref, mod_ref, nw_ref, w_ref, wv_ref, wg_ref, bg_ref,
               zq_ref, zk_ref, zo_ref, zf_ref, zu_ref, vt_ref, gt_ref, scr_ref):
    xn = _mod_norm(x_ref[...], mod_ref[...], nw_ref[...], 0)
    z = jnp.dot(xn, w_ref[...], preferred_element_type=F32)
    o = 0
    for ref, w in ((zq_ref, HEADS_W), (zk_ref, HEADS_W), (zo_ref, HEADS_W), (zf_ref, FOURIER_W)):
        ref[...] = z[:, o:o + w].astype(BF16).reshape(TB, TT, w)
        o += w
    for bl in range(S5_NB):
        zb = z[:, o + bl * LANES:o + (bl + 1) * LANES]
        for b in range(TB):
            for c in range(TT // S5_CHUNK):
                tok, chk = _slab(b, c)
                scr_ref[chk, :] = zb[tok]
        by_token = [scr_ref[pl.ds(s, TROWS, stride=S5_CHUNK), :] for s in range(S5_CHUNK)]
        for gl in range(S5_GB):
            lo = gl * S5_GROUP_CH
            zu_ref[bl, :, gl * S5_IN:(gl + 1) * S5_IN] = jnp.concatenate(
                [x[:, lo:lo + S5_GROUP_CH] for x in by_token], axis=1)
    vt = lax.dot_general(wv_ref[...], xn, _NT, preferred_element_type=F32).astype(BF16)
    gt = lax.dot_general(wg_ref[...], xn, _NT, preferred_element_type=F32) + bg_ref[...]
    for b in range(TB):
        vt_ref[b] = vt[:, b * TT:(b + 1) * TT]
        gt_ref[b] = gt[:, b * TT:(b + 1) * TT]


def _in_proj(p, l, x, mods, nw, w, wv_t, wg_t, bg):
    chan = lambda c_: pl.BlockSpec((TB, c_, TT), lambda j, k: (j, 0, k))
    outs = [HEADS_W] * 3 + [FOURIER_W]
    return pl.pallas_call(
        _in_kernel,
        grid=(p.nb // TB, p.seq // TT),
        in_specs=[_tile_spec(D_MODEL), _tile_mod_spec(p, l),
                  _layer(nw, l), _layer(w, l), _layer(wv_t, l), _layer(wg_t, l), _layer(bg, l)],
        out_specs=[_tile_spec(w_) for w_ in outs] + [_chunk_rows_spec(p), chan(HEADS_W), chan(N_GATES)],
        out_shape=[jax.ShapeDtypeStruct((p.nb, p.seq, w_), BF16) for w_ in outs]
        + [jax.ShapeDtypeStruct((S5_NB, R_PASS, S5_ROW), F32),
           jax.ShapeDtypeStruct((p.nb, HEADS_W, p.seq), BF16),
           jax.ShapeDtypeStruct((p.nb, N_GATES, p.seq), F32)],
        scratch_shapes=[pltpu.VMEM((TB * TT, LANES), F32)],
        compiler_params=_cparams(("arbitrary", "arbitrary")),
        name="in_proj_%d" % p.seq,
    )(x, mods, nw, w, wv_t, wg_t, bg)


def _dft_consts():
    d = np.arange(FOURIER_DH)
    phi = 2.0 * np.pi * ((d[:, None] * d[None, :]) % FOURIER_DH) / FOURIER_DH
    eye = np.eye(FOURIER_W // FOURIER_DH)
    cd = np.kron(eye, np.cos(phi)) / math.sqrt(FOURIER_DH)
    sd = np.kron(eye, np.sin(phi)) / math.sqrt(FOURIER_DH)
    s = np.arange(SEQ)
    th = 2.0 * np.pi * ((s[:, None] * s[None, :]) % SEQ) / SEQ
    rows = DEC_SEQ // GRID_W
    pos = np.arange(DEC_SEQ)
    r, c = pos // GRID_W, pos % GRID_W
    ph = ((r[:, None] * r[None, :]) * (GRID_W // rows) + c[:, None] * c[None, :]) % GRID_W
    th2 = 2.0 * np.pi * ph / GRID_W
    return (np.concatenate([cd, sd], axis=1),
            np.concatenate([np.cos(th), -np.sin(th)], axis=1) / math.sqrt(SEQ),
            np.concatenate([np.cos(th2), -np.sin(th2)], axis=1) / math.sqrt(DEC_SEQ))


def _fourier_prompt_kernel(nb, zf_ref, cdsd_ref, cs_ref, wf_ref, o_ref):
    t = jnp.dot(zf_ref[...].reshape(nb * SEQ, FOURIER_W), cdsd_ref[...],
                preferred_element_type=F32).astype(BF16)
    for b in range(nb):
        tb = t[b * SEQ:(b + 1) * SEQ]
        st = jnp.concatenate([tb[:, :FOURIER_W], tb[:, FOURIER_W:]], axis=0)
        f = jnp.dot(cs_ref[...], st, preferred_element_type=F32)
        o_ref[b] = jnp.dot(f.astype(BF16), wf_ref[...], preferred_element_type=F32).astype(BF16)


def _fourier_prompt(l, zf, cdsd, cs, wf):
    nb = 4
    blk = pl.BlockSpec((nb, SEQ, FOURIER_W), lambda i: (i, 0, 0))
    return pl.pallas_call(
        functools.partial(_fourier_prompt_kernel, nb),
        grid=(BATCH // nb,),
        in_specs=[blk, _full(cdsd), _full(cs), _layer(wf, l)],
        out_specs=blk,
        out_shape=jax.ShapeDtypeStruct((BATCH, SEQ, FOURIER_W), BF16),
        compiler_params=_cparams(("arbitrary",)),
        name="fourier_prompt",
    )(zf, cdsd, cs, wf)


def _fourier_sample_kernel(zf_ref, cdsd_ref, ab_ref, wf_ref, o_ref, tt_ref):
    @pl.when(pl.program_id(0) == 0)
    def _():
        for b in range(DEC_BATCH):
            t = jnp.dot(zf_ref[b], cdsd_ref[...], preferred_element_type=F32).astype(BF16)
            tt_ref[b, 0:DEC_SEQ, :] = t[:, :FOURIER_W]
            tt_ref[b, DEC_SEQ:2 * DEC_SEQ, :] = t[:, FOURIER_W:]

    for b in range(DEC_BATCH):
        f = jnp.dot(ab_ref[...], tt_ref[b], preferred_element_type=F32)
        o_ref[b] = jnp.dot(f.astype(BF16), wf_ref[...], preferred_element_type=F32).astype(BF16)


def _fourier_sample(l, zf, cdsd, ab, wf):
    tk = 512
    return pl.pallas_call(
        _fourier_sample_kernel,
        grid=(DEC_SEQ // tk,),
        in_specs=[_full(zf), _full(cdsd), pl.BlockSpec((tk, 2 * DEC_SEQ), lambda i: (i, 0)), _layer(wf, l)],
        out_specs=pl.BlockSpec((DEC_BATCH, tk, FOURIER_W), lambda i: (0, i, 0)),
        out_shape=jax.ShapeDtypeStruct((DEC_BATCH, DEC_SEQ, FOURIER_W), BF16),
        scratch_shapes=[pltpu.VMEM((DEC_BATCH, 2 * DEC_SEQ, FOURIER_W), BF16)],
        compiler_params=_cparams(("arbitrary",)),
        name="fourier_sample",
    )(zf, cdsd, ab, wf)


def _split3(x):
    hi = x.astype(BF16).astype(F32)
    mid = (x - hi).astype(BF16).astype(F32)
    lo = (x - hi - mid).astype(BF16).astype(F32)
    return hi, mid, lo


def _gate_kernel(g_ref, o_ref):
    L = MLSTM_CHUNK
    nrow = N_GATES // 2
    row = lax.broadcasted_iota(jnp.int32, (L, L), 0)
    col = lax.broadcasted_iota(jnp.int32, (L, L), 1)
    tri_pre = jnp.where(row <= col, 1.0, 0.0).astype(BF16)
    tri_suf = jnp.where(row >= col, 1.0, 0.0).astype(BF16)
    is_fwd = lax.broadcasted_iota(jnp.int32, (nrow, L), 0) < MLSTM_HEADS
    lane = lax.broadcasted_iota(jnp.int32, (nrow, L), 1)
    for bi in range(g_ref.shape[0]):
        for c in range(g_ref.shape[2] // L):
            cols = slice(c * L, (c + 1) * L)
            ig = g_ref[bi, 0:nrow, cols]
            lf = _log_sigmoid(g_ref[bi, nrow:, cols])
            parts = jnp.concatenate(_split3(lf), axis=0).astype(BF16)
            pre = jnp.dot(parts, tri_pre, preferred_element_type=F32)
            suf = jnp.dot(parts, tri_suf, preferred_element_type=F32)
            fold = lambda a: a[0:nrow] + a[nrow:2 * nrow] + a[2 * nrow:]
            b = jnp.where(is_fwd, fold(pre), fold(suf))
            r = ig - b
            pm = sm = r
            sh = 1
            while sh < L:
                pm = jnp.maximum(pm, jnp.where(lane >= sh, pltpu.roll(pm, sh, 1), NEG))
                sm = jnp.maximum(sm, jnp.where(lane < L - sh, pltpu.roll(sm, L - sh, 1), NEG))
                sh *= 2
            cm = jnp.where(is_fwd, pm, sm)
            for q, val in enumerate((b, r, cm)):
                for dh in range(nrow):
                    o_ref[bi, dh, q:q + 1, cols] = val[dh:dh + 1]


def _gate_prep(p, gt):
    bb = max(1, DEC_SEQ // p.seq)
    nrow = N_GATES // 2
    return pl.pallas_call(
        _gate_kernel,
        grid=(p.nb // bb,),
        in_specs=[pl.BlockSpec((bb, N_GATES, p.seq), lambda i: (i, 0, 0))],
        out_specs=pl.BlockSpec((bb, nrow, 3, p.seq), lambda i: (i, 0, 0, 0)),
        out_shape=jax.ShapeDtypeStruct((p.nb, nrow, 3, p.seq), F32),
        compiler_params=_cparams(("arbitrary",)),
        name="gate_prep_%d" % p.seq,
    )(gt)


def _mlstm_chunk(q, k, vt, pr, ct, m, fwd):
    L = q.shape[0]
    scale = MLSTM_DH ** -0.5
    b, r, cm = pr[0:1], pr[1:2], pr[2:3]
    ones = jnp.ones((3, L), F32)
    zeros = jnp.zeros((SUBLANES - 6, L), F32)
    lhs = jnp.concatenate(_split3(r) + (ones, zeros), axis=0).astype(BF16)
    rhs = jnp.concatenate((ones,) + _split3(-cm) + (zeros,), axis=0).astype(BF16)
    arg = lax.dot_general(lhs, rhs, _TN, preferred_element_type=F32)
    row = lax.broadcasted_iota(jnp.int32, (L, L), 0)
    col = lax.broadcasted_iota(jnp.int32, (L, L), 1)
    e = jnp.where((row <= col) if fwd else (row >= col), jnp.exp(arg), 0.0)
    st = lax.dot_general(k, q, _NT, preferred_element_type=F32)
    num = jnp.dot(vt, (st * e).astype(BF16), preferred_element_type=F32)

    mx = jnp.maximum(m, cm)
    cq = lax.dot_general(ct.astype(BF16), q, _NT, preferred_element_type=F32)
    num = (jnp.exp(cm - mx) * scale) * num + jnp.exp(m - mx) * cq
    den = num[N_AUG:N_AUG + 1, :]
    h = num * (1.0 / jnp.maximum(jnp.abs(den), jnp.exp(-(b + mx))))

    last = L - 1 if fwd else 0
    cm_last = cm[:, last:last + 1]
    mx_last = jnp.maximum(m, cm_last)
    vw = (vt.astype(F32) * jnp.exp(r - cm_last)).astype(BF16)
    ct_new = (jnp.exp(m - mx_last) * ct
              + (jnp.exp(cm_last - mx_last) * scale) * jnp.dot(vw, k, preferred_element_type=F32))
    return h, ct_new, b[:, last:last + 1] + mx_last


def _mlstm_kernel(nc, has_init, want_final, *refs):
    refs = list(refs)
    m0_ref, c0_ref = (refs.pop(0), refs.pop(0)) if has_init else (None, None)
    q_ref, k_ref, vt_ref, zo_ref, prf_ref, prb_ref, nw_ref, o_ref = refs[:8]
    hbuf_ref = refs[-1]
    L = MLSTM_CHUNK
    bi = pl.program_id(0)
    hg = pl.program_id(1)
    vrow = lax.broadcasted_iota(jnp.int32, (HEAD_PAD, L), 0)

    def chunk(hh, ci, pr_ref, ct, m, dr):
        rows = pl.ds(pl.multiple_of(ci * L, L), L)
        lanes = slice(hh * HEAD_PAD, (hh + 1) * HEAD_PAD)
        vt = vt_ref[lanes, rows]
        vt = jnp.where(vrow == N_AUG, jnp.ones_like(vt), vt)
        h, ct, m = _mlstm_chunk(q_ref[rows, lanes], k_ref[rows, lanes], vt, pr_ref[hh, :, rows], ct, m, dr == 0)
        hbuf_ref[hh, dr, ci] = h
        return ct, m

    def step(i, carry):
        out = []
        for hh in range(HPS):
            cf, mf, cb, mb = carry[4 * hh:4 * hh + 4]
            out += [*chunk(hh, i, prf_ref, cf, mf, 0), *chunk(hh, nc - 1 - i, prb_ref, cb, mb, 1)]
        return tuple(out)

    carry = []
    for hh in range(HPS):
        for dr in range(2):
            if has_init:
                carry += [c0_ref[0, dr, hh].T, jnp.full((1, 1), m0_ref[bi, dr, hg * HPS + hh], F32)]
            else:
                carry += [jnp.zeros((HEAD_PAD, HEAD_PAD), F32), jnp.zeros((1, 1), F32)]
    carry = step(0, tuple(carry)) if nc == 1 else lax.fori_loop(0, nc, step, tuple(carry))
    if want_final:
        cfin_ref, nfin_ref, mfin_ref = refs[8:11]
        for hh in range(HPS):
            for dr in range(2):
                ct, m = carry[4 * hh + 2 * dr], carry[4 * hh + 2 * dr + 1]
                cfin_ref[0, dr, hh] = ct.T[:MLSTM_DH, :MLSTM_DH]
                nfin_ref[0, dr, hh] = ct[N_AUG:N_AUG + 1, :MLSTM_DH]
                mfin_ref[0, dr, hh] = jnp.broadcast_to(m, (SUBLANES, LANES))

    def finish(ci, _):
        rows = pl.ds(pl.multiple_of(ci * L, L), L)
        for hh in range(HPS):
            lanes = slice(hh * HEAD_PAD, (hh + 1) * HEAD_PAD)
            h = jnp.where(vrow < MLSTM_DH, hbuf_ref[hh, 0, ci] + hbuf_ref[hh, 1, ci], 0.0)
            ms = jnp.sum(h * h, axis=0, keepdims=True) * (1.0 / MLSTM_DH)
            hn = (h * lax.rsqrt(ms + EPS) * nw_ref[hh]).T
            o_ref[rows, lanes] = (hn * jax.nn.sigmoid(zo_ref[rows, lanes].astype(F32))).astype(BF16)
        return 0

    if nc == 1:
        finish(0, 0)
    else:
        lax.fori_loop(0, nc, finish, 0)


def _mlstm(p, l, q, k, vt, zo, pr, nw, init=None, want_final=False):
    nc = p.seq // MLSTM_CHUNK
    tok = pl.BlockSpec((None, p.seq, HPS * HEAD_PAD), lambda b, h: (b, 0, h))
    st = lambda r, c: pl.BlockSpec((1, 2, HPS, r, c), lambda b, h: (b, 0, h, 0, 0))
    prs = lambda dr: pl.BlockSpec((None, HPS, 3, p.seq), lambda b, h: (b, dr * (MLSTM_HEADS // HPS) + h, 0, 0))
    in_specs, args = [], []
    if init is not None:
        in_specs += [pl.BlockSpec(memory_space=pltpu.SMEM), st(HEAD_PAD, HEAD_PAD)]
        args += list(init)
    in_specs += [tok, tok, pl.BlockSpec((None, HPS * HEAD_PAD, p.seq), lambda b, h: (b, h, 0)), tok, prs(0), prs(1),
                 pl.BlockSpec((None, HPS, HEAD_PAD, 1), lambda b, h: (l, h, 0, 0))]
    args += [q, k, vt, zo, pr, pr, nw]
    out_specs = [tok]
    out_shape = [jax.ShapeDtypeStruct((p.nb, p.seq, HEADS_W), BF16)]
    if want_final:
        out_specs += [st(MLSTM_DH, MLSTM_DH), st(1, MLSTM_DH), st(SUBLANES, LANES)]
        out_shape += [jax.ShapeDtypeStruct((p.nb, 2, MLSTM_HEADS, r, c), F32)
                      for r, c in ((MLSTM_DH, MLSTM_DH), (1, MLSTM_DH), (SUBLANES, LANES))]
    return pl.pallas_call(
        functools.partial(_mlstm_kernel, nc, init is not None, want_final),
        grid=(p.nb, MLSTM_HEADS // HPS),
        in_specs=in_specs,
        out_specs=out_specs,
        out_shape=out_shape,
        scratch_shapes=[pltpu.VMEM((HPS, 2, nc, HEAD_PAD, MLSTM_CHUNK), F32)],
        compiler_params=_cparams(("arbitrary", "arbitrary")),
        name="mlstm_%d" % p.seq,
    )(*args)


def _cpow(br, bi, e, nbits):
    pr = pi = None
    for bit in range(nbits):
        sel = ((e >> bit) & 1) == 1
        if pr is None:
            pr, pi = jnp.where(sel, br, 1.0), jnp.where(sel, bi, 0.0)
        else:
            pr, pi = jnp.where(sel, pr * br - pi * bi, pr), jnp.where(sel, pr * bi + pi * br, pi)
        if bit + 1 < nbits:
            br, bi = br * br - bi * bi, 2.0 * br * bi
    return pr, pi


def _s5_prep_kernel(lamc_re_ref, lamc_im_ref, lamr_re_ref, lamr_im_ref, lstep_ref,
                    bt_re_ref, bt_im_ref, ct_re_ref, ct_im_ref,
                    t_ref, m_ref, n_ref, a_ref):
    C = S5_CHUNK
    nbits = C.bit_length()
    kk = lax.broadcasted_iota(jnp.int32, (S5_ST, S5_IN), 1) >> 4
    top = lax.broadcasted_iota(jnp.int32, (S5_ST, S5_IN), 0) < S5_STATE
    srow = lax.broadcasted_iota(jnp.int32, (S5_IN, S5_ST), 0) >> 4
    left = lax.broadcasted_iota(jnp.int32, (S5_IN, S5_ST), 1) < S5_STATE
    left16 = lax.broadcasted_iota(jnp.int32, (S5_GROUP_CH, S5_ST), 1) < S5_STATE
    left1 = lax.broadcasted_iota(jnp.int32, (1, S5_ST), 1) < S5_STATE
    lane = lax.broadcasted_iota(jnp.int32, (S5_GROUP_CH, S5_IN), 1)
    sel = jnp.where((lane & (S5_GROUP_CH - 1)) == lax.broadcasted_iota(jnp.int32, (S5_GROUP_CH, S5_IN), 0),
                    1.0, 0.0)
    spread = lambda ref: jnp.dot(jnp.concatenate([ref[0, 0], ref[0, 0]], axis=0), sel,
                                 precision=lax.Precision.HIGHEST, preferred_element_type=F32)
    ct_re = spread(ct_re_ref)
    ct_im = spread(ct_im_ref)
    resp = []
    for d in range(2):
        step = jnp.exp(lstep_ref[0, d, 0])
        lr_c, li_c = lamc_re_ref[0, d, 0] * step, lamc_im_ref[0, d, 0] * step
        lr_r, li_r = lamr_re_ref[0, d, 0], lamr_im_ref[0, d, 0]
        lbc_re, lbc_im = jnp.exp(lr_c) * jnp.cos(li_c), jnp.exp(lr_c) * jnp.sin(li_c)
        mag = jnp.exp(lr_r * step)
        lb_re, lb_im = mag * jnp.cos(li_r * step), mag * jnp.sin(li_r * step)

        def cp(lag):
            pr, pi = _cpow(lbc_re, lbc_im, lag, nbits)
            return ct_re * pr - ct_im * pi, ct_re * pi + ct_im * pr

        nr, ni = lb_re - 1.0, lb_im
        den = lr_r * lr_r + li_r * li_r
        kap_re = (nr * lr_r + ni * li_r) / den
        kap_im = (ni * lr_r - nr * li_r) / den
        bb_re = kap_re * bt_re_ref[0, 0] - kap_im * bt_im_ref[0, 0]
        bb_im = kap_re * bt_im_ref[0, 0] + kap_im * bt_re_ref[0, 0]

        cpr, cpi = cp(kk if d == 0 else (C - 1) - kk)
        resp.append(jnp.dot(jnp.where(left16, bb_re, -bb_im), jnp.where(top, cpr, cpi),
                            precision=lax.Precision.HIGHEST, preferred_element_type=F32))

        cpr, cpi = cp(kk + 1 if d == 0 else C - kk)
        m_ref[0, d, 0] = jnp.where(top, cpr, -cpi).astype(BF16)

        pr, pi = _cpow(lb_re, lb_im, (C - 1) - srow if d == 0 else srow, nbits)
        bt_r = jnp.concatenate([bb_re] * C, axis=0)
        bt_i = jnp.concatenate([bb_im] * C, axis=0)
        n_re, n_im = pr * bt_r - pi * bt_i, pr * bt_i + pi * bt_r
        n_ref[0, d, 0] = jnp.concatenate([jnp.where(left, n_re, n_im), jnp.where(left, n_im, n_re)],
                                         axis=1).astype(BF16)

        ar, ai = _cpow(lb_re, lb_im, jnp.full((1, S5_ST), C, jnp.int32), nbits)
        a2 = jnp.where(left1, -ai, ai)
        a_ref[0, d, 0] = jnp.concatenate([jnp.concatenate([ar, ar], axis=1),
                                          jnp.concatenate([a2, -a2], axis=1)], axis=0)

    rf, rb = resp
    for s in range(C):
        nf = S5_GROUP_CH * s
        blk = jnp.where(lane >= nf, pltpu.roll(rf, nf, 1) if nf else rf, 0.0)
        nb = S5_GROUP_CH * (C - 1 - s)
        blk = blk + jnp.where(lane < S5_IN - nb, pltpu.roll(rb, S5_IN - nb, 1) if nb else rb, 0.0)
        t_ref[0, 0, S5_GROUP_CH * s:S5_GROUP_CH * (s + 1), :] = blk.astype(BF16)


def _s5_prep(lam_re, lam_im, log_step, b_re, b_im, c_re, c_im):
    G = S5_GROUPS
    dup = lambda a: jnp.concatenate([a, a], axis=-1)
    lamc = [dup(a).reshape(DEPTH, 2, G, S5_ST, 1) for a in (lam_re, lam_im)]
    lamr = [dup(a).reshape(DEPTH, 2, G, 1, S5_ST) for a in (lam_re, lam_im)]
    lstep = log_step.reshape(DEPTH, 2, G, 1, 1)
    bt = [dup(jnp.swapaxes(a, 2, 3)) for a in (b_re, b_im)]
    ct = [jnp.swapaxes(a, 2, 3) for a in (c_re, c_im)]
    dspec = lambda r, c: pl.BlockSpec((1, 2, 1, r, c), lambda l, g: (l, 0, g, 0, 0))
    gspec = lambda r, c: pl.BlockSpec((1, 1, r, c), lambda l, g: (l, g, 0, 0))
    t, m, n, a = pl.pallas_call(
        _s5_prep_kernel,
        grid=(DEPTH, G),
        in_specs=[dspec(S5_ST, 1), dspec(S5_ST, 1), dspec(1, S5_ST), dspec(1, S5_ST), dspec(1, 1),
                  gspec(S5_GROUP_CH, S5_ST), gspec(S5_GROUP_CH, S5_ST),
                  gspec(S5_STATE, S5_GROUP_CH), gspec(S5_STATE, S5_GROUP_CH)],
        out_specs=[gspec(S5_IN, S5_IN), dspec(S5_ST, S5_IN), dspec(S5_IN, S5_ST2), dspec(2, S5_ST2)],
        out_shape=[jax.ShapeDtypeStruct((DEPTH, G, S5_IN, S5_IN), BF16),
                   jax.ShapeDtypeStruct((DEPTH, 2, G, S5_ST, S5_IN), BF16),
                   jax.ShapeDtypeStruct((DEPTH, 2, G, S5_IN, S5_ST2), BF16),
                   jax.ShapeDtypeStruct((DEPTH, 2, G, 2, S5_ST2), F32)],
        compiler_params=_cparams(("arbitrary", "arbitrary")),
        name="s5_prep",
    )(*lamc, *lamr, lstep, *bt, *ct)
    return t, m, n, jnp.transpose(a, (0, 1, 3, 2, 4)).reshape(DEPTH, 2, 2, G * S5_ST2)


def _s5_kernel(nseg, nchunks, zr_ref, t_ref, m_ref, n_ref, a_ref, x0_ref, d_ref,
               y_ref, xfin_ref, v_ref, xp_ref):
    W = S5_GB * S5_ST2
    for gl in range(S5_GB):
        u = zr_ref[0, :, gl * S5_IN:(gl + 1) * S5_IN].astype(BF16)
        for d in range(2):
            v_ref[d, :, gl * S5_ST2:(gl + 1) * S5_ST2] = jnp.dot(
                u, n_ref[d, gl], preferred_element_type=F32)

    def halves(x, which):
        return [x[:, g * S5_ST2 + h * S5_ST:g * S5_ST2 + (h + 1) * S5_ST]
                for g in range(S5_GB) for h in which]

    per = SUBLANES // TB
    steps = nchunks // per

    for d in range(2):
        a = a_ref[d, 0:1, :]
        a2 = a_ref[d, 1:2, :]

        def advance(x, v):
            swapped = jnp.concatenate(halves(x, (1, 0)), axis=1)
            return a * x + a2 * swapped + v

        for seg in range(nseg):
            def step(i, x):
                si = i if d == 0 else steps - 1 - i
                rows = pl.ds(pl.multiple_of(seg * nchunks * TB + si * SUBLANES, SUBLANES), SUBLANES)
                v = v_ref[d, rows, :]
                order = range(per) if d == 0 else range(per - 1, -1, -1)
                entering = [None] * per
                for j in order:
                    entering[j] = jnp.concatenate(halves(x, (0,)), axis=1)
                    x = advance(x, v[j * TB:(j + 1) * TB])
                xp_ref[d, rows, :] = jnp.concatenate(entering, axis=0)
                return x

            x0 = jnp.zeros((TB, W), F32) if x0_ref is None else x0_ref[d, seg * TB:(seg + 1) * TB, :]
            x = lax.fori_loop(0, steps, step, x0)
            if xfin_ref is not None:
                xfin_ref[d, seg * TB:(seg + 1) * TB, :] = jnp.concatenate(halves(x, (0,)), axis=1)

    for gl in range(S5_GB):
        cols = slice(gl * S5_IN, (gl + 1) * S5_IN)
        u = zr_ref[0, :, cols]
        y = jnp.dot(u.astype(BF16), t_ref[gl], preferred_element_type=F32)
        for d in range(2):
            y = y + jnp.dot(xp_ref[d, :, gl * S5_ST:(gl + 1) * S5_ST].astype(BF16), m_ref[d, gl],
                            preferred_element_type=F32)
        y_ref[0, :, cols] = jax.nn.gelu(y + d_ref[0, :, cols] * u)


def _s5(p, l, zr, t, m, n, a, dt, x0=None, want_final=False):
    G = S5_GROUPS
    nseg = p.nb // TB
    row = pl.BlockSpec((1, R_PASS, S5_ROW), lambda j: (j, 0, 0))
    dsp = lambda r, c: pl.BlockSpec((None, 2, S5_GB, r, c), lambda j: (l, 0, j, 0, 0))
    lsp = lambda r, w: pl.BlockSpec((2, r, S5_GB * w), lambda j: (0, 0, j))
    in_specs = [row, pl.BlockSpec((None, S5_GB, S5_IN, S5_IN), lambda j: (l, j, 0, 0)),
                dsp(S5_ST, S5_IN), dsp(S5_IN, S5_ST2),
                pl.BlockSpec((None, 2, 2, S5_GB * S5_ST2), lambda j: (l, 0, 0, j)),
                pl.BlockSpec((None, 1, 1, S5_ROW), lambda j: (l, j, 0, 0))]
    args = [zr, t, m, n, a, dt]
    if x0 is not None:
        in_specs.append(lsp(p.nb, S5_ST2))
        args.append(x0)
    out_specs = [row]
    out_shape = [jax.ShapeDtypeStruct((S5_NB, R_PASS, S5_ROW), F32)]
    if want_final:
        out_specs.append(lsp(p.nb, S5_ST))
        out_shape.append(jax.ShapeDtypeStruct((2, p.nb, G * S5_ST), F32))

    def body(zr_ref, t_ref, m_ref, n_ref, a_ref, d_ref, *rest):
        rest = list(rest)
        x0_ref = rest.pop(0) if x0 is not None else None
        y_ref = rest.pop(0)
        xfin_ref = rest.pop(0) if want_final else None
        _s5_kernel(nseg, p.seq // S5_CHUNK, zr_ref, t_ref, m_ref, n_ref, a_ref, x0_ref, d_ref,
                   y_ref, xfin_ref, *rest)

    return pl.pallas_call(
        body,
        grid=(S5_NB,),
        in_specs=in_specs,
        out_specs=out_specs,
        out_shape=out_shape,
        scratch_shapes=[pltpu.VMEM((2, R_PASS, S5_GB * S5_ST2), F32),
                        pltpu.VMEM((2, R_PASS, S5_GB * S5_ST), F32)],
        compiler_params=_cparams(("arbitrary",)),
        name="s5_scan_%d" % p.seq,
    )(*args)


def _mix_kernel(x_ref, mod_ref, fo_ref, mo_ref, ys_ref, wglu_ref, wo_ref, nw_ref, x1_ref, xn_ref, scr_ref):
    mod = mod_ref[...]
    g1 = mod[:, :, 2 * D_MODEL:3 * D_MODEL]
    blocks = []
    for bl in range(S5_NB):
        by_group = [ys_ref[bl, :, gl * S5_IN:(gl + 1) * S5_IN] for gl in range(S5_GB)]
        for t in range(S5_CHUNK):
            lo = t * S5_GROUP_CH
            scr_ref[pl.ds(t, TROWS, stride=S5_CHUNK), :] = jnp.concatenate(
                [y[:, lo:lo + S5_GROUP_CH] for y in by_group], axis=1)
        slabs = [scr_ref[_slab(b, c)[1], :] for b in range(TB) for c in range(TT // S5_CHUNK)]
        blocks.append(jnp.concatenate(slabs, axis=0))
    y = jnp.concatenate(blocks, axis=1).astype(BF16)
    gg = jnp.dot(y, wglu_ref[...], preferred_element_type=F32)
    s_out = (gg[:, :S5_W] * jax.nn.sigmoid(gg[:, S5_W:])).astype(BF16)
    flat = lambda ref: ref[...].reshape(TB * TT, ref.shape[-1])
    o_m, o_s = FOURIER_W, FOURIER_W + HEADS_W
    mix = (jnp.dot(flat(fo_ref), wo_ref[:o_m, :], preferred_element_type=F32)
           + jnp.dot(flat(mo_ref), wo_ref[o_m:o_s, :], preferred_element_type=F32)
           + jnp.dot(s_out, wo_ref[o_s:, :], preferred_element_type=F32))
    x1 = x_ref[...] + g1 * mix.reshape(TB, TT, D_MODEL)
    x1_ref[...] = x1
    xn_ref[...] = _mod_norm(x1, mod, nw_ref[...], 3).reshape(TB, TT, D_MODEL)


def _mix(p, l, x, mods, fo, mo, ys, wglu, wo, nw):
    return pl.pallas_call(
        _mix_kernel,
        grid=(p.nb // TB, p.seq // TT),
        in_specs=[_tile_spec(D_MODEL), _tile_mod_spec(p, l),
                  _tile_spec(FOURIER_W), _tile_spec(HEADS_W), _chunk_rows_spec(p),
                  _layer(wglu, l), _layer(wo, l), _layer(nw, l)],
        out_specs=[_tile_spec(D_MODEL), _tile_spec(D_MODEL)],
        out_shape=[jax.ShapeDtypeStruct((p.nb, p.seq, D_MODEL), F32),
                   jax.ShapeDtypeStruct((p.nb, p.seq, D_MODEL), BF16)],
        scratch_shapes=[pltpu.VMEM((TB * TT, LANES), F32)],
        compiler_params=_cparams(("arbitrary", "arbitrary")),
        name="mix_out_%d" % p.seq,
    )(x, mods, fo, mo, ys, wglu, wo, nw)


def _ffn_kernel(final, xn_ref, x1_ref, mod_ref, wg_ref, wu_ref, wd_ref, nf_ref, o_ref, acc_ref):
    j = pl.program_id(1)
    xn = xn_ref[...]
    a = jnp.dot(xn, wg_ref[...], preferred_element_type=F32)
    u = jnp.dot(xn, wu_ref[...], preferred_element_type=F32)
    h = (a * jax.nn.sigmoid(a) * u).astype(BF16)
    part = jnp.dot(h, wd_ref[...], preferred_element_type=F32)

    @pl.when(j == 0)
    def _():
        acc_ref[...] = part

    @pl.when(j > 0)
    def _():
        acc_ref[...] += part

    @pl.when(j == pl.num_programs(1) - 1)
    def _():
        g2 = mod_ref[0][:, 5 * D_MODEL:6 * D_MODEL]
        x2 = x1_ref[...] + g2 * acc_ref[...]
        if final:
            x2 = x2 * lax.rsqrt(jnp.mean(x2 * x2, axis=-1, keepdims=True) + EPS) * nf_ref[...]
        o_ref[...] = x2


def _ffn(p, l, final, xn, x1, mods, wg, wu, wd, nf):
    tok = pl.BlockSpec((TM, D_MODEL), lambda i, j: (i, 0))
    mod_row = (lambda i: p.mod_first + i // (p.seq // TM)) if p.mod_each else (lambda i: p.mod_first)
    return pl.pallas_call(
        functools.partial(_ffn_kernel, final),
        grid=(T_PASS // TM, D_FF // TF),
        in_specs=[tok, tok,
                  pl.BlockSpec((None, 1, 1, 6 * D_MODEL), lambda i, j: (l, mod_row(i), 0, 0)),
                  pl.BlockSpec((None, D_MODEL, TF), lambda i, j: (l, 0, j)),
                  pl.BlockSpec((None, D_MODEL, TF), lambda i, j: (l, 0, j)),
                  pl.BlockSpec((None, TF, D_MODEL), lambda i, j: (l, j, 0)),
                  pl.BlockSpec((1, D_MODEL), lambda i, j: (0, 0))],
        out_specs=tok,
        out_shape=jax.ShapeDtypeStruct((T_PASS, D_MODEL), F32),
        scratch_shapes=[pltpu.VMEM((TM, D_MODEL), F32)],
        compiler_params=_cparams(("arbitrary", "arbitrary")),
        name="ffn_%d" % p.seq,
    )(xn, x1, mods, wg, wu, wd, nf)


def _pad_heads(a, axis):
    shape = a.shape[:axis] + (MLSTM_HEADS, MLSTM_DH) + a.shape[axis + 1:]
    pad = [(0, 0)] * (a.ndim + 1)
    pad[axis + 1] = (0, HEAD_PAD - MLSTM_DH)
    return jnp.pad(a.reshape(shape), pad).reshape(a.shape[:axis] + (HEADS_W,) + a.shape[axis + 1:])


def _mlstm_state_in(c, n):
    cn = jnp.concatenate([c, n[..., None]], axis=-1)
    return jnp.pad(cn, ((0, 0),) * (c.ndim - 2) + ((0, HEAD_PAD - MLSTM_DH), (0, HEAD_PAD - MLSTM_DH - 1)))


def kernel(x_prompt, x_sample, state_mlstm_C, state_mlstm_n, state_mlstm_m, state_s5_re, state_s5_im,
           c, c_ctx, w_ada, b_ada, norm1_w, norm2_w, w_in, b_gates, w_fourier, mlstm_norm_w,
           s5_lambda_re, s5_lambda_im, s5_log_step, s5_b_re, s5_b_im, s5_c_re, s5_c_im, s5_d,
           w_glu, w_out, w_gate, w_up, w_down, norm_f):
    xs = {PROMPT: x_prompt, SAMPLE: x_sample}
    cc = jnp.concatenate([c, c_ctx[None], jnp.zeros((N_MODS - 1 - DEC_BATCH, D_MODEL), F32)], axis=0)
    mods = _ada(cc, w_ada, b_ada).reshape(DEPTH, N_MODS, 1, 6 * D_MODEL)
    cdsd, cs, ab = (jnp.asarray(a.astype(np.float32)).astype(BF16) for a in _dft_consts())
    s5_t, s5_m, s5_n, s5_a = _s5_prep(s5_lambda_re, s5_lambda_im, s5_log_step, s5_b_re, s5_b_im,
                                      s5_c_re, s5_c_im)

    o_q = FOURIER_W
    o_g = o_q + 3 * MLSTM_W
    o_o = o_g + N_GATES
    o_u = o_o + MLSTM_W
    heads = lambda o: _pad_heads(w_in[:, :, o:o + MLSTM_W], 2)
    w_cat = jnp.concatenate([heads(o_q), heads(o_q + MLSTM_W), heads(o_o), w_in[:, :, :FOURIER_W],
                             w_in[:, :, o_u:]], axis=2).astype(BF16)
    wv_t = jnp.swapaxes(heads(o_q + 2 * MLSTM_W), 1, 2).astype(BF16)
    gate_perm = np.arange(N_GATES).reshape(2, 2, MLSTM_HEADS).transpose(1, 0, 2).reshape(-1)
    wg_t = jnp.swapaxes(w_in[:, :, o_g:o_o], 1, 2)[:, gate_perm].astype(BF16)
    bg = b_gates[:, gate_perm, None]
    wf = w_fourier.astype(BF16)
    nw = _pad_heads(mlstm_norm_w, 1).reshape(DEPTH, MLSTM_HEADS, HEAD_PAD, 1)
    dt = jnp.tile(s5_d[:, :, None, :], (1, 1, S5_CHUNK, 1)).reshape(DEPTH, S5_NB, 1, S5_ROW)
    wo_m = jnp.pad(w_out[:, FOURIER_W:FOURIER_W + MLSTM_W].reshape(DEPTH, MLSTM_HEADS, MLSTM_DH, D_MODEL),
                   ((0, 0), (0, 0), (0, HEAD_PAD - MLSTM_DH), (0, 0))).reshape(DEPTH, HEADS_W, D_MODEL)
    wo = jnp.concatenate([w_out[:, :FOURIER_W], wo_m, w_out[:, FOURIER_W + MLSTM_W:]], axis=1).astype(BF16)
    wglu = w_glu.astype(BF16)
    wg, wu, wd = w_gate.astype(BF16), w_up.astype(BF16), w_down.astype(BF16)
    n1, n2 = norm1_w[:, None, :], norm2_w[:, None, :]

    m0 = jnp.swapaxes(state_mlstm_m, 0, 1)
    c0 = jnp.swapaxes(_mlstm_state_in(state_mlstm_C, state_mlstm_n), 0, 1)
    x0 = jnp.concatenate([state_s5_re, state_s5_im, state_s5_im, state_s5_re], axis=-1)
    x0 = jnp.transpose(x0, (1, 2, 0, 3, 4)).reshape(DEPTH, 2, DEC_BATCH, S5_GROUPS * S5_ST2)

    finals = []
    for l in range(DEPTH):
        for p in (PROMPT, SAMPLE):
            x = xs[p]
            zq, zk, zo, zf, zu, vt, gt = _in_proj(p, l, x, mods, n1, w_cat, wv_t, wg_t, bg)
            pr = _gate_prep(p, gt)
            if p is PROMPT:
                fo = _fourier_prompt(l, zf, cdsd, cs, wf)
                mo, *fin = _mlstm(p, l, zq, zk, vt, zo, pr, nw, want_final=True)
                ys, xfin = _s5(p, l, zu, s5_t, s5_m, s5_n, s5_a, dt, want_final=True)
                finals.append(fin + [xfin])
            else:
                fo = _fourier_sample(l, zf, cdsd, ab, wf)
                mo, = _mlstm(p, l, zq, zk, vt, zo, pr, nw, init=(m0[l], c0[l]))
                ys, = _s5(p, l, zu, s5_t, s5_m, s5_n, s5_a, dt, x0=x0[l])
            x1, xn2 = _mix(p, l, x, mods, fo, mo, ys, wglu, wo, n2)
            x2 = _ffn(p, l, l == DEPTH - 1, xn2.reshape(T_PASS, D_MODEL), x1.reshape(T_PASS, D_MODEL),
                      mods, wg, wu, wd, norm_f[None])
            xs[p] = x2.reshape(p.nb, p.seq, D_MODEL)

    cfin, nfin, mfin, xfin = (jnp.stack(parts, axis=1) for parts in zip(*finals))
    xfin = xfin.reshape(2, DEPTH, BATCH, S5_GROUPS, 2, S5_STATE)
    new_re, new_im = (jnp.transpose(xfin[:, :, :, :, i], (2, 1, 0, 3, 4)) for i in range(2))
    return (xs[PROMPT], xs[SAMPLE], cfin, nfin[:, :, :, :, 0], mfin[:, :, :, :, 0, 0], new_re, new_im)
```

```python
import collections
import functools
import math

import numpy as np
import jax
import jax.numpy as jnp
from jax import lax
from jax.experimental import pallas as pl
from jax.experimental.pallas import tpu as pltpu

F32 = jnp.float32
BF16 = jnp.bfloat16

D_MODEL = 1024
BATCH = 32
SEQ = 256
DEPTH = 2
DEC_BATCH = 4
DEC_SEQ = 2048
GRID_W = 64
FOURIER_W = 256
FOURIER_DH = 64
MLSTM_W = 384
MLSTM_HEADS = 4
MLSTM_DH = 96
S5_W = 384
S5_GROUP_CH = 16
S5_GROUPS = 24
S5_STATE = 64
N_GATES = 16
D_FF = 2816
EPS = 1e-6

LANES = 128
SUBLANES = 8
VMEM_LIMIT = 56 * 1024 * 1024

HEAD_PAD = LANES
HEADS_W = MLSTM_HEADS * HEAD_PAD
N_AUG = MLSTM_DH
Z_W = 3 * HEADS_W + FOURIER_W + S5_W
MLSTM_CHUNK = 256
S5_CHUNK = 16
S5_IN = S5_CHUNK * S5_GROUP_CH
S5_ST = 2 * S5_STATE
S5_ST2 = 2 * S5_ST
S5_GB = LANES // S5_GROUP_CH
S5_NB = S5_W // LANES
S5_ROW = S5_GB * S5_IN
TB = 4
TT = 128
TROWS = TB * TT // S5_CHUNK
TM = 512
TF = D_FF // 2
HPS = 2
N_MODS = 8
NEG = -1e30

Pass = collections.namedtuple("Pass", "nb seq mod_first mod_each")
PROMPT = Pass(BATCH, SEQ, DEC_BATCH, False)
SAMPLE = Pass(DEC_BATCH, DEC_SEQ, 0, True)
T_PASS = BATCH * SEQ
assert T_PASS == DEC_BATCH * DEC_SEQ
R_PASS = T_PASS // S5_CHUNK

_NT = (((1,), (1,)), ((), ()))
_TN = (((0,), (0,)), ((), ()))


def _cparams(sem):
    return pltpu.CompilerParams(dimension_semantics=sem, vmem_limit_bytes=VMEM_LIMIT)


def _full(a):
    return pl.BlockSpec(a.shape, lambda *_: (0,) * a.ndim)


def _layer(a, l):
    return pl.BlockSpec((None,) + a.shape[1:], lambda *_: (l,) + (0,) * (a.ndim - 1))


def _tile_mod_spec(p, l):
    if p.mod_each:
        return pl.BlockSpec((None, TB, 1, 6 * D_MODEL), lambda j, k: (l, p.mod_first // TB + j, 0, 0))
    return pl.BlockSpec((None, 1, 1, 6 * D_MODEL), lambda j, k: (l, p.mod_first, 0, 0))


def _log_sigmoid(x):
    return jnp.minimum(x, 0.0) - jnp.log1p(jnp.exp(-jnp.abs(x)))


def _tile_spec(w):
    return pl.BlockSpec((TB, TT, w), lambda j, k: (j, k, 0))


def _chunk_rows_spec(p):
    per_group = p.seq // TT
    return pl.BlockSpec((S5_NB, TROWS, S5_ROW), lambda j, k: (0, j * per_group + k, 0))


def _slab(b, c):
    tok = slice(b * TT + c * S5_CHUNK, b * TT + (c + 1) * S5_CHUNK)
    chk = slice((c * TB + b) * S5_CHUNK, (c * TB + b + 1) * S5_CHUNK)
    return tok, chk


def _ada_kernel(c_ref, w_ref, b_ref, o_ref):
    a = c_ref[...]
    a = (a * jax.nn.sigmoid(a)).astype(BF16)
    o_ref[0] = jnp.dot(a, w_ref[0].astype(BF16), preferred_element_type=F32) + b_ref[0]


def _ada(cc, w_ada, b_ada):
    tn = 512
    return pl.pallas_call(
        _ada_kernel,
        grid=(DEPTH, 6 * D_MODEL // tn),
        in_specs=[pl.BlockSpec((N_MODS, D_MODEL), lambda l, j: (0, 0)),
                  pl.BlockSpec((1, D_MODEL, tn), lambda l, j: (l, 0, j)),
                  pl.BlockSpec((1, 1, tn), lambda l, j: (l, 0, j))],
        out_specs=pl.BlockSpec((1, N_MODS, tn), lambda l, j: (l, 0, j)),
        out_shape=jax.ShapeDtypeStruct((DEPTH, N_MODS, 6 * D_MODEL), F32),
        compiler_params=_cparams(("arbitrary", "arbitrary")),
        name="ada_mod",
    )(cc, w_ada, b_ada.reshape(DEPTH, 1, 6 * D_MODEL))


def _mod_norm(x3, mod, nw, first):
    sh = mod[:, :, first * D_MODEL:(first + 1) * D_MODEL]
    sc = mod[:, :, (first + 1) * D_MODEL:(first + 2) * D_MODEL]
    y = x3 * lax.rsqrt(jnp.mean(x3 * x3, axis=-1, keepdims=True) + EPS) * nw
    return (y * (1.0 + sc) + sh).reshape(TB * TT, D_MODEL).astype(BF16)


def _in_kernel(x_ref, mod_ref, nw_ref, w_ref, wv_ref, wg_ref, bg_ref,
               zq_ref, zk_ref, zo_ref, zf_ref, zu_ref, vt_ref, gt_ref, scr_ref):
    xn = _mod_norm(x_ref[...], mod_ref[...], nw_ref[...], 0)
    w_main = Z_W - S5_W
    zu = jnp.dot(xn, w_ref[:, w_main:], preferred_element_type=F32)
    for bl in range(S5_NB):
        zb = zu[:, bl * LANES:(bl + 1) * LANES]
        for b in range(TB):
            for c in range(TT // S5_CHUNK):
                tok, chk = _slab(b, c)
                scr_ref[bl, chk, :] = zb[tok]
        by_token = [scr_ref[bl, pl.ds(s, TROWS, stride=S5_CHUNK), :] for s in range(S5_CHUNK)]
        for gl in range(S5_GB):
            lo = gl * S5_GROUP_CH
            zu_ref[bl, :, gl * S5_IN:(gl + 1) * S5_IN] = jnp.concatenate(
                [x[:, lo:lo + S5_GROUP_CH] for x in by_token], axis=1)
    z = jnp.dot(xn, w_ref[:, :w_main], preferred_element_type=F32)
    o = 0
    for ref, w in ((zq_ref, HEADS_W), (zk_ref, HEADS_W), (zo_ref, HEADS_W), (zf_ref, FOURIER_W)):
        ref[...] = z[:, o:o + w].astype(BF16).reshape(TB, TT, w)
        o += w
    vt = lax.dot_general(wv_ref[...], xn, _NT, preferred_element_type=F32).astype(BF16)
    gt = lax.dot_general(wg_ref[...], xn, _NT, preferred_element_type=F32) + bg_ref[...]
    for b in range(TB):
        vt_ref[b] = vt[:, b * TT:(b + 1) * TT]
        gt_ref[b] = gt[:, b * TT:(b + 1) * TT]


def _in_proj(p, l, x, mods, nw, w, wv_t, wg_t, bg):
    chan = lambda c_: pl.BlockSpec((TB, c_, TT), lambda j, k: (j, 0, k))
    outs = [HEADS_W] * 3 + [FOURIER_W]
    return pl.pallas_call(
        _in_kernel,
        grid=(p.nb // TB, p.seq // TT),
        in_specs=[_tile_spec(D_MODEL), _tile_mod_spec(p, l),
                  _layer(nw, l), _layer(w, l), _layer(wv_t, l), _layer(wg_t, l), _layer(bg, l)],
        out_specs=[_tile_spec(w_) for w_ in outs] + [_chunk_rows_spec(p), chan(HEADS_W), chan(N_GATES)],
        out_shape=[jax.ShapeDtypeStruct((p.nb, p.seq, w_), BF16) for w_ in outs]
        + [jax.ShapeDtypeStruct((S5_NB, R_PASS, S5_ROW), F32),
           jax.ShapeDtypeStruct((p.nb, HEADS_W, p.seq), BF16),
           jax.ShapeDtypeStruct((p.nb, N_GATES, p.seq), F32)],
        scratch_shapes=[pltpu.VMEM((S5_NB, TB * TT, LANES), F32)],
        compiler_params=_cparams(("arbitrary", "arbitrary")),
        name="in_proj_%d" % p.seq,
    )(x, mods, nw, w, wv_t, wg_t, bg)


def _dft_consts():
    d = np.arange(FOURIER_DH)
    phi = 2.0 * np.pi * ((d[:, None] * d[None, :]) % FOURIER_DH) / FOURIER_DH
    eye = np.eye(FOURIER_W // FOURIER_DH)
    cd = np.kron(eye, np.cos(phi)) / math.sqrt(FOURIER_DH)
    sd = np.kron(eye, np.sin(phi)) / math.sqrt(FOURIER_DH)
    s = np.arange(SEQ)
    th = 2.0 * np.pi * ((s[:, None] * s[None, :]) % SEQ) / SEQ
    rows = DEC_SEQ // GRID_W
    pos = np.arange(DEC_SEQ)
    r, c = pos // GRID_W, pos % GRID_W
    ph = ((r[:, None] * r[None, :]) * (GRID_W // rows) + c[:, None] * c[None, :]) % GRID_W
    th2 = 2.0 * np.pi * ph / GRID_W
    return (np.concatenate([cd, sd], axis=1),
            np.concatenate([np.cos(th), -np.sin(th)], axis=1) / math.sqrt(SEQ),
            np.concatenate([np.cos(th2), -np.sin(th2)], axis=1) / math.sqrt(DEC_SEQ))


def _fourier_prompt_kernel(nb, zf_ref, cdsd_ref, cs_ref, wf_ref, o_ref):
    t = jnp.dot(zf_ref[...].reshape(nb * SEQ, FOURIER_W), cdsd_ref[...],
                preferred_element_type=F32).astype(BF16)
    for b in range(nb):
        tb = t[b * SEQ:(b + 1) * SEQ]
        st = jnp.concatenate([tb[:, :FOURIER_W], tb[:, FOURIER_W:]], axis=0)
        f = jnp.dot(cs_ref[...], st, preferred_element_type=F32)
        o_ref[b] = jnp.dot(f.astype(BF16), wf_ref[...], preferred_element_type=F32).astype(BF16)


def _fourier_prompt(l, zf, cdsd, cs, wf):
    nb = 4
    blk = pl.BlockSpec((nb, SEQ, FOURIER_W), lambda i: (i, 0, 0))
    return pl.pallas_call(
        functools.partial(_fourier_prompt_kernel, nb),
        grid=(BATCH // nb,),
        in_specs=[blk, _full(cdsd), _full(cs), _layer(wf, l)],
        out_specs=blk,
        out_shape=jax.ShapeDtypeStruct((BATCH, SEQ, FOURIER_W), BF16),
        compiler_params=_cparams(("arbitrary",)),
        name="fourier_prompt",
    )(zf, cdsd, cs, wf)


def _fourier_sample_kernel(zf_ref, cdsd_ref, ab_ref, wf_ref, o_ref, tt_ref):
    @pl.when(pl.program_id(0) == 0)
    def _():
        for b in range(DEC_BATCH):
            t = jnp.dot(zf_ref[b], cdsd_ref[...], preferred_element_type=F32).astype(BF16)
            tt_ref[b, 0:DEC_SEQ, :] = t[:, :FOURIER_W]
            tt_ref[b, DEC_SEQ:2 * DEC_SEQ, :] = t[:, FOURIER_W:]

    for b in range(DEC_BATCH):
        f = jnp.dot(ab_ref[...], tt_ref[b], preferred_element_type=F32)
        o_ref[b] = jnp.dot(f.astype(BF16), wf_ref[...], preferred_element_type=F32).astype(BF16)


def _fourier_sample(l, zf, cdsd, ab, wf):
    tk = 512
    return pl.pallas_call(
        _fourier_sample_kernel,
        grid=(DEC_SEQ // tk,),
        in_specs=[_full(zf), _full(cdsd), pl.BlockSpec((tk, 2 * DEC_SEQ), lambda i: (i, 0)), _layer(wf, l)],
        out_specs=pl.BlockSpec((DEC_BATCH, tk, FOURIER_W), lambda i: (0, i, 0)),
        out_shape=jax.ShapeDtypeStruct((DEC_BATCH, DEC_SEQ, FOURIER_W), BF16),
        scratch_shapes=[pltpu.VMEM((DEC_BATCH, 2 * DEC_SEQ, FOURIER_W), BF16)],
        compiler_params=_cparams(("arbitrary",)),
        name="fourier_sample",
    )(zf, cdsd, ab, wf)


def _split3(x):
    hi = x.astype(BF16).astype(F32)
    mid = (x - hi).astype(BF16).astype(F32)
    lo = (x - hi - mid).astype(BF16).astype(F32)
    return hi, mid, lo


def _gate_kernel(g_ref, o_ref):
    L = MLSTM_CHUNK
    nrow = N_GATES // 2
    row = lax.broadcasted_iota(jnp.int32, (L, L), 0)
    col = lax.broadcasted_iota(jnp.int32, (L, L), 1)
    tri_pre = jnp.where(row <= col, 1.0, 0.0).astype(BF16)
    tri_suf = jnp.where(row >= col, 1.0, 0.0).astype(BF16)
    is_fwd = lax.broadcasted_iota(jnp.int32, (nrow, L), 0) < MLSTM_HEADS
    lane = lax.broadcasted_iota(jnp.int32, (nrow, L), 1)
    for bi in range(g_ref.shape[0]):
        for c in range(g_ref.shape[2] // L):
            cols = slice(c * L, (c + 1) * L)
            ig = g_ref[bi, 0:nrow, cols]
            lf = _log_sigmoid(g_ref[bi, nrow:, cols])
            parts = jnp.concatenate(_split3(lf), axis=0).astype(BF16)
            pre = jnp.dot(parts, tri_pre, preferred_element_type=F32)
            suf = jnp.dot(parts, tri_suf, preferred_element_type=F32)
            fold = lambda a: a[0:nrow] + a[nrow:2 * nrow] + a[2 * nrow:]
            b = jnp.where(is_fwd, fold(pre), fold(suf))
            r = ig - b
            pm = sm = r
            sh = 1
            while sh < L:
                pm = jnp.maximum(pm, jnp.where(lane >= sh, pltpu.roll(pm, sh, 1), NEG))
                sm = jnp.maximum(sm, jnp.where(lane < L - sh, pltpu.roll(sm, L - sh, 1), NEG))
                sh *= 2
            cm = jnp.where(is_fwd, pm, sm)
            for q, val in enumerate((b, r, cm)):
                for dh in range(nrow):
                    o_ref[bi, dh, q:q + 1, cols] = val[dh:dh + 1]


def _gate_prep(p, gt):
    bb = max(1, DEC_SEQ // p.seq)
    nrow = N_GATES // 2
    return pl.pallas_call(
        _gate_kernel,
        grid=(p.nb // bb,),
        in_specs=[pl.BlockSpec((bb, N_GATES, p.seq), lambda i: (i, 0, 0))],
        out_specs=pl.BlockSpec((bb, nrow, 3, p.seq), lambda i: (i, 0, 0, 0)),
        out_shape=jax.ShapeDtypeStruct((p.nb, nrow, 3, p.seq), F32),
        compiler_params=_cparams(("arbitrary",)),
        name="gate_prep_%d" % p.seq,
    )(gt)


def _mlstm_chunk(q, k, vt, pr, ct, m, fwd):
    L = q.shape[0]
    scale = MLSTM_DH ** -0.5
    b, r, cm = pr[0:1], pr[1:2], pr[2:3]
    ones = jnp.ones((3, L), F32)
    zeros = jnp.zeros((SUBLANES - 6, L), F32)
    lhs = jnp.concatenate(_split3(r) + (ones, zeros), axis=0).astype(BF16)
    rhs = jnp.concatenate((ones,) + _split3(-cm) + (zeros,), axis=0).astype(BF16)
    arg = lax.dot_general(lhs, rhs, _TN, preferred_element_type=F32)
    row = lax.broadcasted_iota(jnp.int32, (L, L), 0)
    col = lax.broadcasted_iota(jnp.int32, (L, L), 1)
    e = jnp.where((row <= col) if fwd else (row >= col), jnp.exp(arg), 0.0)
    st = lax.dot_general(k, q, _NT, preferred_element_type=F32)
    num = jnp.dot(vt, (st * e).astype(BF16), preferred_element_type=F32)

    mx = jnp.maximum(m, cm)
    cq = lax.dot_general(ct.astype(BF16), q, _NT, preferred_element_type=F32)
    num = (jnp.exp(cm - mx) * scale) * num + jnp.exp(m - mx) * cq
    den = num[N_AUG:N_AUG + 1, :]
    h = num * (1.0 / jnp.maximum(jnp.abs(den), jnp.exp(-(b + mx))))

    last = L - 1 if fwd else 0
    cm_last = cm[:, last:last + 1]
    mx_last = jnp.maximum(m, cm_last)
    vw = (vt.astype(F32) * jnp.exp(r - cm_last)).astype(BF16)
    ct_new = (jnp.exp(m - mx_last) * ct
              + (jnp.exp(cm_last - mx_last) * scale) * jnp.dot(vw, k, preferred_element_type=F32))
    return h, ct_new, b[:, last:last + 1] + mx_last


def _mlstm_kernel(nc, has_init, want_final, *refs):
    refs = list(refs)
    m0_ref, c0_ref = (refs.pop(0), refs.pop(0)) if has_init else (None, None)
    q_ref, k_ref, vt_ref, zo_ref, prf_ref, prb_ref, nw_ref, o_ref = refs[:8]
    hbuf_ref = refs[-1]
    L = MLSTM_CHUNK
    bi = pl.program_id(0)
    hg = pl.program_id(1)
    vrow = lax.broadcasted_iota(jnp.int32, (HEAD_PAD, L), 0)

    def chunk(hh, ci, pr_ref, ct, m, dr):
        rows = pl.ds(pl.multiple_of(ci * L, L), L)
        lanes = slice(hh * HEAD_PAD, (hh + 1) * HEAD_PAD)
        vt = vt_ref[lanes, rows]
        vt = jnp.where(vrow == N_AUG, jnp.ones_like(vt), vt)
        h, ct, m = _mlstm_chunk(q_ref[rows, lanes], k_ref[rows, lanes], vt, pr_ref[hh, :, rows], ct, m, dr == 0)
        hbuf_ref[hh, dr, ci] = h
        return ct, m

    def step(i, carry):
        out = []
        for hh in range(HPS):
            cf, mf, cb, mb = carry[4 * hh:4 * hh + 4]
            out += [*chunk(hh, i, prf_ref, cf, mf, 0), *chunk(hh, nc - 1 - i, prb_ref, cb, mb, 1)]
        return tuple(out)

    carry = []
    for hh in range(HPS):
        for dr in range(2):
            if has_init:
                carry += [c0_ref[0, dr, hh].T, jnp.full((1, 1), m0_ref[bi, dr, hg * HPS + hh], F32)]
            else:
                carry += [jnp.zeros((HEAD_PAD, HEAD_PAD), F32), jnp.zeros((1, 1), F32)]
    carry = step(0, tuple(carry)) if nc == 1 else lax.fori_loop(0, nc, step, tuple(carry))
    if want_final:
        cfin_ref, nfin_ref, mfin_ref = refs[8:11]
        for hh in range(HPS):
            for dr in range(2):
                ct, m = carry[4 * hh + 2 * dr], carry[4 * hh + 2 * dr + 1]
                cfin_ref[0, dr, hh] = ct.T[:MLSTM_DH, :MLSTM_DH]
                nfin_ref[0, dr, hh] = ct[N_AUG:N_AUG + 1, :MLSTM_DH]
                mfin_ref[0, dr, hh] = jnp.broadcast_to(m, (SUBLANES, LANES))

    def finish(ci, _):
        rows = pl.ds(pl.multiple_of(ci * L, L), L)
        for hh in range(HPS):
            lanes = slice(hh * HEAD_PAD, (hh + 1) * HEAD_PAD)
            h = jnp.where(vrow < MLSTM_DH, hbuf_ref[hh, 0, ci] + hbuf_ref[hh, 1, ci], 0.0)
            ms = jnp.sum(h * h, axis=0, keepdims=True) * (1.0 / MLSTM_DH)
            hn = (h * lax.rsqrt(ms + EPS) * nw_ref[hh]).T
            o_ref[rows, lanes] = (hn * jax.nn.sigmoid(zo_ref[rows, lanes].astype(F32))).astype(BF16)
        return 0

    if nc == 1:
        finish(0, 0)
    else:
        lax.fori_loop(0, nc, finish, 0)


def _mlstm(p, l, q, k, vt, zo, pr, nw, init=None, want_final=False):
    nc = p.seq // MLSTM_CHUNK
    tok = pl.BlockSpec((None, p.seq, HPS * HEAD_PAD), lambda b, h: (b, 0, h))
    st = lambda r, c: pl.BlockSpec((1, 2, HPS, r, c), lambda b, h: (b, 0, h, 0, 0))
    prs = lambda dr: pl.BlockSpec((None, HPS, 3, p.seq), lambda b, h: (b, dr * (MLSTM_HEADS // HPS) + h, 0, 0))
    in_specs, args = [], []
    if init is not None:
        in_specs += [pl.BlockSpec(memory_space=pltpu.SMEM), st(HEAD_PAD, HEAD_PAD)]
        args += list(init)
    in_specs += [tok, tok, pl.BlockSpec((None, HPS * HEAD_PAD, p.seq), lambda b, h: (b, h, 0)), tok, prs(0), prs(1),
                 pl.BlockSpec((None, HPS, HEAD_PAD, 1), lambda b, h: (l, h, 0, 0))]
    args += [q, k, vt, zo, pr, pr, nw]
    out_specs = [tok]
    out_shape = [jax.ShapeDtypeStruct((p.nb, p.seq, HEADS_W), BF16)]
    if want_final:
        out_specs += [st(MLSTM_DH, MLSTM_DH), st(1, MLSTM_DH), st(SUBLANES, LANES)]
        out_shape += [jax.ShapeDtypeStruct((p.nb, 2, MLSTM_HEADS, r, c), F32)
                      for r, c in ((MLSTM_DH, MLSTM_DH), (1, MLSTM_DH), (SUBLANES, LANES))]
    return pl.pallas_call(
        functools.partial(_mlstm_kernel, nc, init is not None, want_final),
        grid=(p.nb, MLSTM_HEADS // HPS),
        in_specs=in_specs,
        out_specs=out_specs,
        out_shape=out_shape,
        scratch_shapes=[pltpu.VMEM((HPS, 2, nc, HEAD_PAD, MLSTM_CHUNK), F32)],
        compiler_params=_cparams(("arbitrary", "arbitrary")),
        name="mlstm_%d" % p.seq,
    )(*args)


def _cpow(br, bi, e, nbits):
    pr = pi = None
    for bit in range(nbits):
        sel = ((e >> bit) & 1) == 1
        if pr is None:
            pr, pi = jnp.where(sel, br, 1.0), jnp.where(sel, bi, 0.0)
        else:
            pr, pi = jnp.where(sel, pr * br - pi * bi, pr), jnp.where(sel, pr * bi + pi * br, pi)
        if bit + 1 < nbits:
            br, bi = br * br - bi * bi, 2.0 * br * bi
    return pr, pi


def _s5_prep_kernel(lamc_re_ref, lamc_im_ref, lamr_re_ref, lamr_im_ref, lstep_ref,
                    bt_re_ref, bt_im_ref, ct_re_ref, ct_im_ref,
                    t_ref, m_ref, n_ref, a_ref):
    C = S5_CHUNK
    nbits = (C - 1).bit_length()
    assert C == 1 << nbits
    kk = lax.broadcasted_iota(jnp.int32, (S5_ST, S5_IN), 1) >> 4
    top = lax.broadcasted_iota(jnp.int32, (S5_ST, S5_IN), 0) < S5_STATE
    srow = lax.broadcasted_iota(jnp.int32, (S5_IN, S5_ST), 0) >> 4
    left = lax.broadcasted_iota(jnp.int32, (S5_IN, S5_ST), 1) < S5_STATE
    left16 = lax.broadcasted_iota(jnp.int32, (S5_GROUP_CH, S5_ST), 1) < S5_STATE
    left1 = lax.broadcasted_iota(jnp.int32, (1, S5_ST), 1) < S5_STATE
    lane = lax.broadcasted_iota(jnp.int32, (S5_GROUP_CH, S5_IN), 1)
    sel = jnp.where((lane & (S5_GROUP_CH - 1)) == lax.broadcasted_iota(jnp.int32, (S5_GROUP_CH, S5_IN), 0),
                    1.0, 0.0)
    spread = lambda ref: jnp.dot(jnp.concatenate([ref[0, 0], ref[0, 0]], axis=0), sel,
                                 precision=lax.Precision.HIGHEST, preferred_element_type=F32)
    ct_re = spread(ct_re_ref)
    ct_im = spread(ct_im_ref)
    resp = []
    for d in range(2):
        step = jnp.exp(lstep_ref[0, d, 0])
        lr_c, li_c = lamc_re_ref[0, d, 0] * step, lamc_im_ref[0, d, 0] * step
        lr_r, li_r = lamr_re_ref[0, d, 0], lamr_im_ref[0, d, 0]
        lbc_re, lbc_im = jnp.exp(lr_c) * jnp.cos(li_c), jnp.exp(lr_c) * jnp.sin(li_c)
        mag = jnp.exp(lr_r * step)
        lb_re, lb_im = mag * jnp.cos(li_r * step), mag * jnp.sin(li_r * step)

        pr, pi = _cpow(lbc_re, lbc_im, kk if d == 0 else (C - 1) - kk, nbits)
        pr1, pi1 = pr * lbc_re - pi * lbc_im, pr * lbc_im + pi * lbc_re
        cpr, cpi = ct_re * pr - ct_im * pi, ct_re * pi + ct_im * pr
        cpr1, cpi1 = ct_re * pr1 - ct_im * pi1, ct_re * pi1 + ct_im * pr1

        nr, ni = lb_re - 1.0, lb_im
        den = lr_r * lr_r + li_r * li_r
        kap_re = (nr * lr_r + ni * li_r) / den
        kap_im = (ni * lr_r - nr * li_r) / den
        bb_re = kap_re * bt_re_ref[0, 0] - kap_im * bt_im_ref[0, 0]
        bb_im = kap_re * bt_im_ref[0, 0] + kap_im * bt_re_ref[0, 0]

        resp.append(jnp.dot(jnp.where(left16, bb_re, -bb_im), jnp.where(top, cpr, cpi),
                            precision=lax.Precision.HIGHEST, preferred_element_type=F32))
        m_ref[0, d, 0] = jnp.where(top, cpr1, -cpi1).astype(BF16)

        pr, pi = _cpow(lb_re, lb_im, (C - 1) - srow if d == 0 else srow, nbits)
        bt_r = jnp.concatenate([bb_re] * C, axis=0)
        bt_i = jnp.concatenate([bb_im] * C, axis=0)
        n_re, n_im = pr * bt_r - pi * bt_i, pr * bt_i + pi * bt_r
        n_ref[0, d, 0] = jnp.concatenate([jnp.where(left, n_re, n_im), jnp.where(left, n_im, n_re)],
                                         axis=1).astype(BF16)

        ar, ai = lb_re, lb_im
        for _ in range(nbits):
            ar, ai = ar * ar - ai * ai, 2.0 * ar * ai
        a2 = jnp.where(left1, -ai, ai)
        a_ref[0, d, 0] = jnp.concatenate([jnp.concatenate([ar, ar], axis=1),
                                          jnp.concatenate([a2, -a2], axis=1)], axis=0)

    rf, rb = resp
    for s in range(C):
        nf = S5_GROUP_CH * s
        blk = jnp.where(lane >= nf, pltpu.roll(rf, nf, 1) if nf else rf, 0.0)
        nb = S5_GROUP_CH * (C - 1 - s)
        blk = blk + jnp.where(lane < S5_IN - nb, pltpu.roll(rb, S5_IN - nb, 1) if nb else rb, 0.0)
        t_ref[0, 0, S5_GROUP_CH * s:S5_GROUP_CH * (s + 1), :] = blk.astype(BF16)


def _s5_prep(lam_re, lam_im, log_step, b_re, b_im, c_re, c_im):
    G = S5_GROUPS
    dup = lambda a: jnp.concatenate([a, a], axis=-1)
    lamc = [dup(a).reshape(DEPTH, 2, G, S5_ST, 1) for a in (lam_re, lam_im)]
    lamr = [dup(a).reshape(DEPTH, 2, G, 1, S5_ST) for a in (lam_re, lam_im)]
    lstep = log_step.reshape(DEPTH, 2, G, 1, 1)
    bt = [dup(jnp.swapaxes(a, 2, 3)) for a in (b_re, b_im)]
    ct = [jnp.swapaxes(a, 2, 3) for a in (c_re, c_im)]
    dspec = lambda r, c: pl.BlockSpec((1, 2, 1, r, c), lambda l, g: (l, 0, g, 0, 0))
    gspec = lambda r, c: pl.BlockSpec((1, 1, r, c), lambda l, g: (l, g, 0, 0))
    t, m, n, a = pl.pallas_call(
        _s5_prep_kernel,
        grid=(DEPTH, G),
        in_specs=[dspec(S5_ST, 1), dspec(S5_ST, 1), dspec(1, S5_ST), dspec(1, S5_ST), dspec(1, 1),
                  gspec(S5_GROUP_CH, S5_ST), gspec(S5_GROUP_CH, S5_ST),
                  gspec(S5_STATE, S5_GROUP_CH), gspec(S5_STATE, S5_GROUP_CH)],
        out_specs=[gspec(S5_IN, S5_IN), dspec(S5_ST, S5_IN), dspec(S5_IN, S5_ST2), dspec(2, S5_ST2)],
        out_shape=[jax.ShapeDtypeStruct((DEPTH, G, S5_IN, S5_IN), BF16),
                   jax.ShapeDtypeStruct((DEPTH, 2, G, S5_ST, S5_IN), BF16),
                   jax.ShapeDtypeStruct((DEPTH, 2, G, S5_IN, S5_ST2), BF16),
                   jax.ShapeDtypeStruct((DEPTH, 2, G, 2, S5_ST2), F32)],
        compiler_params=_cparams(("arbitrary", "arbitrary")),
        name="s5_prep",
    )(*lamc, *lamr, lstep, *bt, *ct)
    return t, m, n, jnp.transpose(a, (0, 1, 3, 2, 4)).reshape(DEPTH, 2, 2, G * S5_ST2)


def _s5_kernel(nseg, nchunks, zr_ref, t_ref, m_ref, n_ref, a_ref, x0_ref, d_ref,
               y_ref, xfin_ref, v_ref, xp_ref):
    W = S5_GB * S5_ST2
    for gl in range(S5_GB):
        u = zr_ref[0, :, gl * S5_IN:(gl + 1) * S5_IN].astype(BF16)
        for d in range(2):
            v_ref[d, :, gl * S5_ST2:(gl + 1) * S5_ST2] = jnp.dot(
                u, n_ref[d, gl], preferred_element_type=F32)

    def halves(x, which):
        return [x[:, g * S5_ST2 + h * S5_ST:g * S5_ST2 + (h + 1) * S5_ST]
                for g in range(S5_GB) for h in which]

    per = SUBLANES // TB
    steps = nchunks // per

    for d in range(2):
        a = a_ref[d, 0:1, :]
        a2 = a_ref[d, 1:2, :]

        def advance(x, v):
            swapped = jnp.concatenate(halves(x, (1, 0)), axis=1)
            return a * x + a2 * swapped + v

        for seg in range(nseg):
            def step(i, x):
                si = i if d == 0 else steps - 1 - i
                rows = pl.ds(pl.multiple_of(seg * nchunks * TB + si * SUBLANES, SUBLANES), SUBLANES)
                v = v_ref[d, rows, :]
                order = range(per) if d == 0 else range(per - 1, -1, -1)
                entering = [None] * per
                for j in order:
                    entering[j] = jnp.concatenate(halves(x, (0,)), axis=1)
                    x = advance(x, v[j * TB:(j + 1) * TB])
                xp_ref[d, rows, :] = jnp.concatenate(entering, axis=0)
                return x

            x0 = jnp.zeros((TB, W), F32) if x0_ref is None else x0_ref[d, seg * TB:(seg + 1) * TB, :]
            x = lax.fori_loop(0, steps, step, x0)
            if xfin_ref is not None:
                xfin_ref[d, seg * TB:(seg + 1) * TB, :] = jnp.concatenate(halves(x, (0,)), axis=1)

    for gl in range(S5_GB):
        cols = slice(gl * S5_IN, (gl + 1) * S5_IN)
        u = zr_ref[0, :, cols]
        y = jnp.dot(u.astype(BF16), t_ref[gl], preferred_element_type=F32)
        for d in range(2):
            y = y + jnp.dot(xp_ref[d, :, gl * S5_ST:(gl + 1) * S5_ST].astype(BF16), m_ref[d, gl],
                            preferred_element_type=F32)
        y_ref[0, :, cols] = jax.nn.gelu(y + d_ref[0, :, cols] * u)


def _s5(p, l, zr, t, m, n, a, dt, x0=None, want_final=False):
    G = S5_GROUPS
    nseg = p.nb // TB
    row = pl.BlockSpec((1, R_PASS, S5_ROW), lambda j: (j, 0, 0))
    dsp = lambda r, c: pl.BlockSpec((None, 2, S5_GB, r, c), lambda j: (l, 0, j, 0, 0))
    lsp = lambda r, w: pl.BlockSpec((2, r, S5_GB * w), lambda j: (0, 0, j))
    in_specs = [row, pl.BlockSpec((None, S5_GB, S5_IN, S5_IN), lambda j: (l, j, 0, 0)),
                dsp(S5_ST, S5_IN), dsp(S5_IN, S5_ST2),
                pl.BlockSpec((None, 2, 2, S5_GB * S5_ST2), lambda j: (l, 0, 0, j)),
                pl.BlockSpec((None, 1, 1, S5_ROW), lambda j: (l, j, 0, 0))]
    args = [zr, t, m, n, a, dt]
    if x0 is not None:
        in_specs.append(lsp(p.nb, S5_ST2))
        args.append(x0)
    out_specs = [row]
    out_shape = [jax.ShapeDtypeStruct((S5_NB, R_PASS, S5_ROW), F32)]
    if want_final:
        out_specs.append(lsp(p.nb, S5_ST))
        out_shape.append(jax.ShapeDtypeStruct((2, p.nb, G * S5_ST), F32))

    def body(zr_ref, t_ref, m_ref, n_ref, a_ref, d_ref, *rest):
        rest = list(rest)
        x0_ref = rest.pop(0) if x0 is not None else None
        y_ref = rest.pop(0)
        xfin_ref = rest.pop(0) if want_final else None
        _s5_kernel(nseg, p.seq // S5_CHUNK, zr_ref, t_ref, m_ref, n_ref, a_ref, x0_ref, d_ref,
                   y_ref, xfin_ref, *rest)

    return pl.pallas_call(
        body,
        grid=(S5_NB,),
        in_specs=in_specs,
        out_specs=out_specs,
        out_shape=out_shape,
        scratch_shapes=[pltpu.VMEM((2, R_PASS, S5_GB * S5_ST2), F32),
                        pltpu.VMEM((2, R_PASS, S5_GB * S5_ST), F32)],
        compiler_params=_cparams(("arbitrary",)),
        name="s5_scan_%d" % p.seq,
    )(*args)


def _mix_kernel(x_ref, mod_ref, fo_ref, mo_ref, ys_ref, wglu_ref, wo_ref, nw_ref, x1_ref, xn_ref, scr_ref):
    mod = mod_ref[...]
    g1 = mod[:, :, 2 * D_MODEL:3 * D_MODEL]
    flat = lambda ref: ref[...].reshape(TB * TT, ref.shape[-1])
    o_m, o_s = FOURIER_W, FOURIER_W + HEADS_W
    mix = (jnp.dot(flat(fo_ref), wo_ref[:o_m, :], preferred_element_type=F32)
           + jnp.dot(flat(mo_ref), wo_ref[o_m:o_s, :], preferred_element_type=F32))
    blocks = []
    for bl in range(S5_NB):
        by_group = [ys_ref[bl, :, gl * S5_IN:(gl + 1) * S5_IN] for gl in range(S5_GB)]
        for t in range(S5_CHUNK):
            lo = t * S5_GROUP_CH
            scr_ref[bl, pl.ds(t, TROWS, stride=S5_CHUNK), :] = jnp.concatenate(
                [y[:, lo:lo + S5_GROUP_CH] for y in by_group], axis=1)
        slabs = [scr_ref[bl, _slab(b, c)[1], :] for b in range(TB) for c in range(TT // S5_CHUNK)]
        blocks.append(jnp.concatenate(slabs, axis=0))
    y = jnp.concatenate(blocks, axis=1).astype(BF16)
    gg = jnp.dot(y, wglu_ref[...], preferred_element_type=F32)
    s_out = (gg[:, :S5_W] * jax.nn.sigmoid(gg[:, S5_W:])).astype(BF16)
    mix = mix + jnp.dot(s_out, wo_ref[o_s:, :], preferred_element_type=F32)
    x1 = x_ref[...] + g1 * mix.reshape(TB, TT, D_MODEL)
    x1_ref[...] = x1
    xn_ref[...] = _mod_norm(x1, mod, nw_ref[...], 3).reshape(TB, TT, D_MODEL)


def _mix(p, l, x, mods, fo, mo, ys, wglu, wo, nw):
    return pl.pallas_call(
        _mix_kernel,
        grid=(p.nb // TB, p.seq // TT),
        in_specs=[_tile_spec(D_MODEL), _tile_mod_spec(p, l),
                  _tile_spec(FOURIER_W), _tile_spec(HEADS_W), _chunk_rows_spec(p),
                  _layer(wglu, l), _layer(wo, l), _layer(nw, l)],
        out_specs=[_tile_spec(D_MODEL), _tile_spec(D_MODEL)],
        out_shape=[jax.ShapeDtypeStruct((p.nb, p.seq, D_MODEL), F32),
                   jax.ShapeDtypeStruct((p.nb, p.seq, D_MODEL), BF16)],
        scratch_shapes=[pltpu.VMEM((S5_NB, TB * TT, LANES), F32)],
        compiler_params=_cparams(("arbitrary", "arbitrary")),
        name="mix_out_%d" % p.seq,
    )(x, mods, fo, mo, ys, wglu, wo, nw)


def _ffn_kernel(final, xn_ref, x1_ref, mod_ref, wg_ref, wu_ref, wd_ref, nf_ref, o_ref, acc_ref):
    j = pl.program_id(1)
    xn = xn_ref[...]
    a = jnp.dot(xn, wg_ref[...], preferred_element_type=F32)
    u = jnp.dot(xn, wu_ref[...], preferred_element_type=F32)
    h = (a * jax.nn.sigmoid(a) * u).astype(BF16)
    part = jnp.dot(h, wd_ref[...], preferred_element_type=F32)

    @pl.when(j == 0)
    def _():
        acc_ref[...] = part

    @pl.when(j > 0)
    def _():
        acc_ref[...] += part

    @pl.when(j == pl.num_programs(1) - 1)
    def _():
        g2 = mod_ref[0][:, 5 * D_MODEL:6 * D_MODEL]
        x2 = x1_ref[...] + g2 * acc_ref[...]
        if final:
            x2 = x2 * lax.rsqrt(jnp.mean(x2 * x2, axis=-1, keepdims=True) + EPS) * nf_ref[...]
        o_ref[...] = x2


def _ffn(p, l, final, xn, x1, mods, wg, wu, wd, nf):
    tok = pl.BlockSpec((TM, D_MODEL), lambda i, j: (i, 0))
    mod_row = (lambda i: p.mod_first + i // (p.seq // TM)) if p.mod_each else (lambda i: p.mod_first)
    return pl.pallas_call(
        functools.partial(_ffn_kernel, final),
        grid=(T_PASS // TM, D_FF // TF),
        in_specs=[tok, tok,
                  pl.BlockSpec((None, 1, 1, 6 * D_MODEL), lambda i, j: (l, mod_row(i), 0, 0)),
                  pl.BlockSpec((None, D_MODEL, TF), lambda i, j: (l, 0, j)),
                  pl.BlockSpec((None, D_MODEL, TF), lambda i, j: (l, 0, j)),
                  pl.BlockSpec((None, TF, D_MODEL), lambda i, j: (l, j, 0)),
                  pl.BlockSpec((1, D_MODEL), lambda i, j: (0, 0))],
        out_specs=tok,
        out_shape=jax.ShapeDtypeStruct((T_PASS, D_MODEL), F32),
        scratch_shapes=[pltpu.VMEM((TM, D_MODEL), F32)],
        compiler_params=_cparams(("arbitrary", "arbitrary")),
        name="ffn_%d" % p.seq,
    )(xn, x1, mods, wg, wu, wd, nf)


def _pad_heads(a, axis):
    shape = a.shape[:axis] + (MLSTM_HEADS, MLSTM_DH) + a.shape[axis + 1:]
    pad = [(0, 0)] * (a.ndim + 1)
    pad[axis + 1] = (0, HEAD_PAD - MLSTM_DH)
    return jnp.pad(a.reshape(shape), pad).reshape(a.shape[:axis] + (HEADS_W,) + a.shape[axis + 1:])


def _mlstm_state_in(c, n):
    cn = jnp.concatenate([c, n[..., None]], axis=-1)
    return jnp.pad(cn, ((0, 0),) * (c.ndim - 2) + ((0, HEAD_PAD - MLSTM_DH), (0, HEAD_PAD - MLSTM_DH - 1)))


def kernel(x_prompt, x_sample, state_mlstm_C, state_mlstm_n, state_mlstm_m, state_s5_re, state_s5_im,
           c, c_ctx, w_ada, b_ada, norm1_w, norm2_w, w_in, b_gates, w_fourier, mlstm_norm_w,
           s5_lambda_re, s5_lambda_im, s5_log_step, s5_b_re, s5_b_im, s5_c_re, s5_c_im, s5_d,
           w_glu, w_out, w_gate, w_up, w_down, norm_f):
    xs = {PROMPT: x_prompt, SAMPLE: x_sample}
    cc = jnp.concatenate([c, c_ctx[None], jnp.zeros((N_MODS - 1 - DEC_BATCH, D_MODEL), F32)], axis=0)
    mods = _ada(cc, w_ada, b_ada).reshape(DEPTH, N_MODS, 1, 6 * D_MODEL)
    cdsd, cs, ab = (jnp.asarray(a.astype(np.float32)).astype(BF16) for a in _dft_consts())
    s5_t, s5_m, s5_n, s5_a = _s5_prep(s5_lambda_re, s5_lambda_im, s5_log_step, s5_b_re, s5_b_im,
                                      s5_c_re, s5_c_im)

    o_q = FOURIER_W
    o_g = o_q + 3 * MLSTM_W
    o_o = o_g + N_GATES
    o_u = o_o + MLSTM_W
    heads = lambda o: _pad_heads(w_in[:, :, o:o + MLSTM_W], 2)
    w_cat = jnp.concatenate([heads(o_q), heads(o_q + MLSTM_W), heads(o_o), w_in[:, :, :FOURIER_W],
                             w_in[:, :, o_u:]], axis=2).astype(BF16)
    wv_t = jnp.swapaxes(heads(o_q + 2 * MLSTM_W), 1, 2).astype(BF16)
    gate_perm = np.arange(N_GATES).reshape(2, 2, MLSTM_HEADS).transpose(1, 0, 2).reshape(-1)
    wg_t = jnp.swapaxes(w_in[:, :, o_g:o_o], 1, 2)[:, gate_perm].astype(BF16)
    bg = b_gates[:, gate_perm, None]
    wf = w_fourier.astype(BF16)
    nw = _pad_heads(mlstm_norm_w, 1).reshape(DEPTH, MLSTM_HEADS, HEAD_PAD, 1)
    dt = jnp.tile(s5_d[:, :, None, :], (1, 1, S5_CHUNK, 1)).reshape(DEPTH, S5_NB, 1, S5_ROW)
    wo_m = jnp.pad(w_out[:, FOURIER_W:FOURIER_W + MLSTM_W].reshape(DEPTH, MLSTM_HEADS, MLSTM_DH, D_MODEL),
                   ((0, 0), (0, 0), (0, HEAD_PAD - MLSTM_DH), (0, 0))).reshape(DEPTH, HEADS_W, D_MODEL)
    wo = jnp.concatenate([w_out[:, :FOURIER_W], wo_m, w_out[:, FOURIER_W + MLSTM_W:]], axis=1).astype(BF16)
    wglu = w_glu.astype(BF16)
    wg, wu, wd = w_gate.astype(BF16), w_up.astype(BF16), w_down.astype(BF16)
    n1, n2 = norm1_w[:, None, :], norm2_w[:, None, :]

    m0 = jnp.swapaxes(state_mlstm_m, 0, 1)
    c0 = jnp.swapaxes(_mlstm_state_in(state_mlstm_C, state_mlstm_n), 0, 1)
    x0 = jnp.concatenate([state_s5_re, state_s5_im, state_s5_im, state_s5_re], axis=-1)
    x0 = jnp.transpose(x0, (1, 2, 0, 3, 4)).reshape(DEPTH, 2, DEC_BATCH, S5_GROUPS * S5_ST2)

    finals = []
    for l in range(DEPTH):
        for p in (PROMPT, SAMPLE):
            x = xs[p]
            zq, zk, zo, zf, zu, vt, gt = _in_proj(p, l, x, mods, n1, w_cat, wv_t, wg_t, bg)
            pr = _gate_prep(p, gt)
            if p is PROMPT:
                fo = _fourier_prompt(l, zf, cdsd, cs, wf)
                mo, *fin = _mlstm(p, l, zq, zk, vt, zo, pr, nw, want_final=True)
                ys, xfin = _s5(p, l, zu, s5_t, s5_m, s5_n, s5_a, dt, want_final=True)
                finals.append(fin + [xfin])
            else:
                fo = _fourier_sample(l, zf, cdsd, ab, wf)
                mo, = _mlstm(p, l, zq, zk, vt, zo, pr, nw, init=(m0[l], c0[l]))
                ys, = _s5(p, l, zu, s5_t, s5_m, s5_n, s5_a, dt, x0=x0[l])
            x1, xn2 = _mix(p, l, x, mods, fo, mo, ys, wglu, wo, n2)
            x2 = _ffn(p, l, l == DEPTH - 1, xn2.reshape(T_PASS, D_MODEL), x1.reshape(T_PASS, D_MODEL),
                      mods, wg, wu, wd, norm_f[None])
            xs[p] = x2.reshape(p.nb, p.seq, D_MODEL)

    cfin, nfin, mfin, xfin = (jnp.stack(parts, axis=1) for parts in zip(*finals))
    xfin = xfin.reshape(2, DEPTH, BATCH, S5_GROUPS, 2, S5_STATE)
    new_re, new_im = (jnp.transpose(xfin[:, :, :, :, i], (2, 1, 0, 3, 4)) for i in range(2))
    return (xs[PROMPT], xs[SAMPLE], cfin, nfin[:, :, :, :, 0], mfin[:, :, :, :, 0, 0], new_re, new_im)
```

```python
import collections
import functools
import math

import numpy as np
import jax
import jax.numpy as jnp
from jax import lax
from jax.experimental import pallas as pl
from jax.experimental.pallas import tpu as pltpu

F32 = jnp.float32
BF16 = jnp.bfloat16

D_MODEL = 1024
BATCH = 32
SEQ = 256
DEPTH = 2
DEC_BATCH = 4
DEC_SEQ = 2048
GRID_W = 64
FOURIER_W = 256
FOURIER_DH = 64
MLSTM_W = 384
MLSTM_HEADS = 4
MLSTM_DH = 96
S5_W = 384
S5_GROUP_CH = 16
S5_GROUPS = 24
S5_STATE = 64
N_GATES = 16
D_FF = 2816
EPS = 1e-6

LANES = 128
SUBLANES = 8
VMEM_LIMIT = 56 * 1024 * 1024

HEAD_PAD = LANES
HEADS_W = MLSTM_HEADS * HEAD_PAD
N_AUG = MLSTM_DH
Z_W = 3 * HEADS_W + FOURIER_W + S5_W
MLSTM_CHUNK = 256
S5_CHUNK = 16
S5_IN = S5_CHUNK * S5_GROUP_CH
S5_ST = 2 * S5_STATE
S5_ST2 = 2 * S5_ST
S5_GB = LANES // S5_GROUP_CH
S5_NB = S5_W // LANES
S5_ROW = S5_GB * S5_IN
TB = 4
TT = 128
TROWS = TB * TT // S5_CHUNK
TM = 512
TF = D_FF // 2
HPS = 4
N_MODS = 8
NEG = -1e30

Pass = collections.namedtuple("Pass", "nb seq mod_first mod_each")
PROMPT = Pass(BATCH, SEQ, DEC_BATCH, False)
SAMPLE = Pass(DEC_BATCH, DEC_SEQ, 0, True)
T_PASS = BATCH * SEQ
assert T_PASS == DEC_BATCH * DEC_SEQ
R_PASS = T_PASS // S5_CHUNK

_NT = (((1,), (1,)), ((), ()))
_TN = (((0,), (0,)), ((), ()))


def _cparams(sem):
    return pltpu.CompilerParams(dimension_semantics=sem, vmem_limit_bytes=VMEM_LIMIT)


def _full(a):
    return pl.BlockSpec(a.shape, lambda *_: (0,) * a.ndim)


def _layer(a, l):
    return pl.BlockSpec((None,) + a.shape[1:], lambda *_: (l,) + (0,) * (a.ndim - 1))


def _tile_mod_spec(p, l):
    if p.mod_each:
        return pl.BlockSpec((None, TB, 1, 6 * D_MODEL), lambda j, k: (l, p.mod_first // TB + j, 0, 0))
    return pl.BlockSpec((None, 1, 1, 6 * D_MODEL), lambda j, k: (l, p.mod_first, 0, 0))


def _log_sigmoid(x):
    return jnp.minimum(x, 0.0) - jnp.log1p(jnp.exp(-jnp.abs(x)))


def _tile_spec(w):
    return pl.BlockSpec((TB, TT, w), lambda j, k: (j, k, 0))


def _chunk_rows_spec(p):
    per_group = p.seq // TT
    return pl.BlockSpec((S5_NB, TROWS, S5_ROW), lambda j, k: (0, j * per_group + k, 0))


def _slab(b, c):
    tok = slice(b * TT + c * S5_CHUNK, b * TT + (c + 1) * S5_CHUNK)
    chk = slice((c * TB + b) * S5_CHUNK, (c * TB + b + 1) * S5_CHUNK)
    return tok, chk


def _ada_kernel(c_ref, w_ref, b_ref, o_ref):
    a = c_ref[...]
    a = (a * jax.nn.sigmoid(a)).astype(BF16)
    o_ref[0] = jnp.dot(a, w_ref[0].astype(BF16), preferred_element_type=F32) + b_ref[0]


def _ada(cc, w_ada, b_ada):
    tn = 512
    return pl.pallas_call(
        _ada_kernel,
        grid=(DEPTH, 6 * D_MODEL // tn),
        in_specs=[pl.BlockSpec((N_MODS, D_MODEL), lambda l, j: (0, 0)),
                  pl.BlockSpec((1, D_MODEL, tn), lambda l, j: (l, 0, j)),
                  pl.BlockSpec((1, 1, tn), lambda l, j: (l, 0, j))],
        out_specs=pl.BlockSpec((1, N_MODS, tn), lambda l, j: (l, 0, j)),
        out_shape=jax.ShapeDtypeStruct((DEPTH, N_MODS, 6 * D_MODEL), F32),
        compiler_params=_cparams(("arbitrary", "arbitrary")),
        name="ada_mod",
    )(cc, w_ada, b_ada.reshape(DEPTH, 1, 6 * D_MODEL))


def _mod_norm(x3, mod, nw, first):
    sh = mod[:, :, first * D_MODEL:(first + 1) * D_MODEL]
    sc = mod[:, :, (first + 1) * D_MODEL:(first + 2) * D_MODEL]
    y = x3 * lax.rsqrt(jnp.mean(x3 * x3, axis=-1, keepdims=True) + EPS) * nw
    return (y * (1.0 + sc) + sh).reshape(TB * TT, D_MODEL).astype(BF16)


def _in_kernel(x_ref, mod_ref, nw_ref, w_ref, wv_ref, wg_ref, bg_ref,
               zq_ref, zk_ref, zo_ref, zf_ref, zu_ref, vt_ref, gt_ref, scr_ref):
    xn = _mod_norm(x_ref[...], mod_ref[...], nw_ref[...], 0)
    w_main = Z_W - S5_W
    zu = jnp.dot(xn, w_ref[:, w_main:], preferred_element_type=F32)
    for bl in range(S5_NB):
        zb = zu[:, bl * LANES:(bl + 1) * LANES]
        for b in range(TB):
            for c in range(TT // S5_CHUNK):
                tok, chk = _slab(b, c)
                scr_ref[bl, chk, :] = zb[tok]
        by_token = [scr_ref[bl, pl.ds(s, TROWS, stride=S5_CHUNK), :] for s in range(S5_CHUNK)]
        for gl in range(S5_GB):
            lo = gl * S5_GROUP_CH
            zu_ref[bl, :, gl * S5_IN:(gl + 1) * S5_IN] = jnp.concatenate(
                [x[:, lo:lo + S5_GROUP_CH] for x in by_token], axis=1)
    z = jnp.dot(xn, w_ref[:, :w_main], preferred_element_type=F32)
    o = 0
    for ref, w in ((zq_ref, HEADS_W), (zk_ref, HEADS_W), (zo_ref, HEADS_W), (zf_ref, FOURIER_W)):
        ref[...] = z[:, o:o + w].astype(BF16).reshape(TB, TT, w)
        o += w
    vt = lax.dot_general(wv_ref[...], xn, _NT, preferred_element_type=F32).astype(BF16)
    gt = lax.dot_general(wg_ref[...], xn, _NT, preferred_element_type=F32) + bg_ref[...]
    for b in range(TB):
        vt_ref[b] = vt[:, b * TT:(b + 1) * TT]
        gt_ref[b] = gt[:, b * TT:(b + 1) * TT]


def _in_proj(p, l, x, mods, nw, w, wv_t, wg_t, bg):
    chan = lambda c_: pl.BlockSpec((TB, c_, TT), lambda j, k: (j, 0, k))
    outs = [HEADS_W] * 3 + [FOURIER_W]
    return pl.pallas_call(
        _in_kernel,
        grid=(p.nb // TB, p.seq // TT),
        in_specs=[_tile_spec(D_MODEL), _tile_mod_spec(p, l),
                  _layer(nw, l), _layer(w, l), _layer(wv_t, l), _layer(wg_t, l), _layer(bg, l)],
        out_specs=[_tile_spec(w_) for w_ in outs] + [_chunk_rows_spec(p), chan(HEADS_W), chan(N_GATES)],
        out_shape=[jax.ShapeDtypeStruct((p.nb, p.seq, w_), BF16) for w_ in outs]
        + [jax.ShapeDtypeStruct((S5_NB, R_PASS, S5_ROW), F32),
           jax.ShapeDtypeStruct((p.nb, HEADS_W, p.seq), BF16),
           jax.ShapeDtypeStruct((p.nb, N_GATES, p.seq), F32)],
        scratch_shapes=[pltpu.VMEM((S5_NB, TB * TT, LANES), F32)],
        compiler_params=_cparams(("arbitrary", "arbitrary")),
        name="in_proj_%d" % p.seq,
    )(x, mods, nw, w, wv_t, wg_t, bg)


def _dft_consts():
    d = np.arange(FOURIER_DH)
    phi = 2.0 * np.pi * ((d[:, None] * d[None, :]) % FOURIER_DH) / FOURIER_DH
    eye = np.eye(FOURIER_W // FOURIER_DH)
    cd = np.kron(eye, np.cos(phi)) / math.sqrt(FOURIER_DH)
    sd = np.kron(eye, np.sin(phi)) / math.sqrt(FOURIER_DH)
    s = np.arange(SEQ)
    th = 2.0 * np.pi * ((s[:, None] * s[None, :]) % SEQ) / SEQ
    rows = DEC_SEQ // GRID_W
    pos = np.arange(DEC_SEQ)
    r, c = pos // GRID_W, pos % GRID_W
    ph = ((r[:, None] * r[None, :]) * (GRID_W // rows) + c[:, None] * c[None, :]) % GRID_W
    th2 = 2.0 * np.pi * ph / GRID_W
    return (np.concatenate([cd, sd], axis=1),
            np.concatenate([np.cos(th), -np.sin(th)], axis=1) / math.sqrt(SEQ),
            np.concatenate([np.cos(th2), -np.sin(th2)], axis=1) / math.sqrt(DEC_SEQ))


def _fourier_prompt_kernel(nb, zf_ref, cdsd_ref, cs_ref, wf_ref, o_ref):
    t = jnp.dot(zf_ref[...].reshape(nb * SEQ, FOURIER_W), cdsd_ref[...],
                preferred_element_type=F32).astype(BF16)
    for b in range(nb):
        tb = t[b * SEQ:(b + 1) * SEQ]
        st = jnp.concatenate([tb[:, :FOURIER_W], tb[:, FOURIER_W:]], axis=0)
        f = jnp.dot(cs_ref[...], st, preferred_element_type=F32)
        o_ref[b] = jnp.dot(f.astype(BF16), wf_ref[...], preferred_element_type=F32).astype(BF16)


def _fourier_prompt(l, zf, cdsd, cs, wf):
    nb = 4
    blk = pl.BlockSpec((nb, SEQ, FOURIER_W), lambda i: (i, 0, 0))
    return pl.pallas_call(
        functools.partial(_fourier_prompt_kernel, nb),
        grid=(BATCH // nb,),
        in_specs=[blk, _full(cdsd), _full(cs), _layer(wf, l)],
        out_specs=blk,
        out_shape=jax.ShapeDtypeStruct((BATCH, SEQ, FOURIER_W), BF16),
        compiler_params=_cparams(("arbitrary",)),
        name="fourier_prompt",
    )(zf, cdsd, cs, wf)


def _fourier_sample_kernel(zf_ref, cdsd_ref, ab_ref, wf_ref, o_ref, tt_ref):
    @pl.when(pl.program_id(0) == 0)
    def _():
        for b in range(DEC_BATCH):
            t = jnp.dot(zf_ref[b], cdsd_ref[...], preferred_element_type=F32).astype(BF16)
            tt_ref[b, 0:DEC_SEQ, :] = t[:, :FOURIER_W]
            tt_ref[b, DEC_SEQ:2 * DEC_SEQ, :] = t[:, FOURIER_W:]

    for b in range(DEC_BATCH):
        f = jnp.dot(ab_ref[...], tt_ref[b], preferred_element_type=F32)
        o_ref[b] = jnp.dot(f.astype(BF16), wf_ref[...], preferred_element_type=F32).astype(BF16)


def _fourier_sample(l, zf, cdsd, ab, wf):
    tk = 512
    return pl.pallas_call(
        _fourier_sample_kernel,
        grid=(DEC_SEQ // tk,),
        in_specs=[_full(zf), _full(cdsd), pl.BlockSpec((tk, 2 * DEC_SEQ), lambda i: (i, 0)), _layer(wf, l)],
        out_specs=pl.BlockSpec((DEC_BATCH, tk, FOURIER_W), lambda i: (0, i, 0)),
        out_shape=jax.ShapeDtypeStruct((DEC_BATCH, DEC_SEQ, FOURIER_W), BF16),
        scratch_shapes=[pltpu.VMEM((DEC_BATCH, 2 * DEC_SEQ, FOURIER_W), BF16)],
        compiler_params=_cparams(("arbitrary",)),
        name="fourier_sample",
    )(zf, cdsd, ab, wf)


def _split3(x):
    hi = x.astype(BF16).astype(F32)
    mid = (x - hi).astype(BF16).astype(F32)
    lo = (x - hi - mid).astype(BF16).astype(F32)
    return hi, mid, lo


def _gate_kernel(g_ref, o_ref):
    L = MLSTM_CHUNK
    nrow = N_GATES // 2
    row = lax.broadcasted_iota(jnp.int32, (L, L), 0)
    col = lax.broadcasted_iota(jnp.int32, (L, L), 1)
    tri_pre = jnp.where(row <= col, 1.0, 0.0).astype(BF16)
    tri_suf = jnp.where(row >= col, 1.0, 0.0).astype(BF16)
    is_fwd = lax.broadcasted_iota(jnp.int32, (nrow, L), 0) < MLSTM_HEADS
    lane = lax.broadcasted_iota(jnp.int32, (nrow, L), 1)
    for bi in range(g_ref.shape[0]):
        for c in range(g_ref.shape[2] // L):
            cols = slice(c * L, (c + 1) * L)
            ig = g_ref[bi, 0:nrow, cols]
            lf = _log_sigmoid(g_ref[bi, nrow:, cols])
            parts = jnp.concatenate(_split3(lf), axis=0).astype(BF16)
            pre = jnp.dot(parts, tri_pre, preferred_element_type=F32)
            suf = jnp.dot(parts, tri_suf, preferred_element_type=F32)
            fold = lambda a: a[0:nrow] + a[nrow:2 * nrow] + a[2 * nrow:]
            b = jnp.where(is_fwd, fold(pre), fold(suf))
            r = ig - b
            pm = sm = r
            sh = 1
            while sh < L:
                pm = jnp.maximum(pm, jnp.where(lane >= sh, pltpu.roll(pm, sh, 1), NEG))
                sm = jnp.maximum(sm, jnp.where(lane < L - sh, pltpu.roll(sm, L - sh, 1), NEG))
                sh *= 2
            cm = jnp.where(is_fwd, pm, sm)
            for q, val in enumerate((b, r, cm)):
                for dh in range(nrow):
                    o_ref[bi, dh, q:q + 1, cols] = val[dh:dh + 1]


def _gate_prep(p, gt):
    bb = max(1, DEC_SEQ // p.seq)
    nrow = N_GATES // 2
    return pl.pallas_call(
        _gate_kernel,
        grid=(p.nb // bb,),
        in_specs=[pl.BlockSpec((bb, N_GATES, p.seq), lambda i: (i, 0, 0))],
        out_specs=pl.BlockSpec((bb, nrow, 3, p.seq), lambda i: (i, 0, 0, 0)),
        out_shape=jax.ShapeDtypeStruct((p.nb, nrow, 3, p.seq), F32),
        compiler_params=_cparams(("arbitrary",)),
        name="gate_prep_%d" % p.seq,
    )(gt)


def _mlstm_chunk(q, k, vt, pr, ct, m, fwd):
    L = q.shape[0]
    scale = MLSTM_DH ** -0.5
    b, r, cm = pr[0:1], pr[1:2], pr[2:3]
    ones = jnp.ones((3, L), F32)
    zeros = jnp.zeros((SUBLANES - 6, L), F32)
    lhs = jnp.concatenate(_split3(r) + (ones, zeros), axis=0).astype(BF16)
    rhs = jnp.concatenate((ones,) + _split3(-cm) + (zeros,), axis=0).astype(BF16)
    arg = lax.dot_general(lhs, rhs, _TN, preferred_element_type=F32)
    row = lax.broadcasted_iota(jnp.int32, (L, L), 0)
    col = lax.broadcasted_iota(jnp.int32, (L, L), 1)
    e = jnp.where((row <= col) if fwd else (row >= col), jnp.exp(arg), 0.0)
    st = lax.dot_general(k, q, _NT, preferred_element_type=F32)
    num = jnp.dot(vt, (st * e).astype(BF16), preferred_element_type=F32)

    mx = jnp.maximum(m, cm)
    cq = lax.dot_general(ct.astype(BF16), q, _NT, preferred_element_type=F32)
    num = (jnp.exp(cm - mx) * scale) * num + jnp.exp(m - mx) * cq
    den = num[N_AUG:N_AUG + 1, :]
    h = num * (1.0 / jnp.maximum(jnp.abs(den), jnp.exp(-(b + mx))))

    last = L - 1 if fwd else 0
    cm_last = cm[:, last:last + 1]
    mx_last = jnp.maximum(m, cm_last)
    vw = (vt.astype(F32) * jnp.exp(r - cm_last)).astype(BF16)
    ct_new = (jnp.exp(m - mx_last) * ct
              + (jnp.exp(cm_last - mx_last) * scale) * jnp.dot(vw, k, preferred_element_type=F32))
    return h, ct_new, b[:, last:last + 1] + mx_last


def _mlstm_kernel(nc, has_init, want_final, *refs):
    refs = list(refs)
    m0_ref, c0_ref = (refs.pop(0), refs.pop(0)) if has_init else (None, None)
    q_ref, k_ref, vt_ref, zo_ref, prf_ref, prb_ref, nw_ref, o_ref = refs[:8]
    hbuf_ref = refs[-1]
    L = MLSTM_CHUNK
    bi = pl.program_id(0)
    hg = pl.program_id(1)
    vrow = lax.broadcasted_iota(jnp.int32, (HEAD_PAD, L), 0)

    def chunk(hh, ci, pr_ref, ct, m, dr):
        rows = pl.ds(pl.multiple_of(ci * L, L), L)
        lanes = slice(hh * HEAD_PAD, (hh + 1) * HEAD_PAD)
        vt = vt_ref[lanes, rows]
        vt = jnp.where(vrow == N_AUG, jnp.ones_like(vt), vt)
        h, ct, m = _mlstm_chunk(q_ref[rows, lanes], k_ref[rows, lanes], vt, pr_ref[hh, :, rows], ct, m, dr == 0)
        hbuf_ref[hh, dr, ci] = h
        return ct, m

    def step(i, carry):
        out = []
        for hh in range(HPS):
            cf, mf, cb, mb = carry[4 * hh:4 * hh + 4]
            out += [*chunk(hh, i, prf_ref, cf, mf, 0), *chunk(hh, nc - 1 - i, prb_ref, cb, mb, 1)]
        return tuple(out)

    carry = []
    for hh in range(HPS):
        for dr in range(2):
            if has_init:
                carry += [c0_ref[0, dr, hh].T, jnp.full((1, 1), m0_ref[bi, dr, hg * HPS + hh], F32)]
            else:
                carry += [jnp.zeros((HEAD_PAD, HEAD_PAD), F32), jnp.zeros((1, 1), F32)]
    carry = step(0, tuple(carry)) if nc == 1 else lax.fori_loop(0, nc, step, tuple(carry))
    if want_final:
        cfin_ref, nfin_ref, mfin_ref = refs[8:11]
        for hh in range(HPS):
            for dr in range(2):
                ct, m = carry[4 * hh + 2 * dr], carry[4 * hh + 2 * dr + 1]
                cfin_ref[0, dr, hh] = ct.T[:MLSTM_DH, :MLSTM_DH]
                nfin_ref[0, dr, hh] = ct[N_AUG:N_AUG + 1, :MLSTM_DH]
                mfin_ref[0, dr, hh] = jnp.broadcast_to(m, (SUBLANES, LANES))

    def finish(ci, _):
        rows = pl.ds(pl.multiple_of(ci * L, L), L)
        for hh in range(HPS):
            lanes = slice(hh * HEAD_PAD, (hh + 1) * HEAD_PAD)
            h = jnp.where(vrow < MLSTM_DH, hbuf_ref[hh, 0, ci] + hbuf_ref[hh, 1, ci], 0.0)
            ms = jnp.sum(h * h, axis=0, keepdims=True) * (1.0 / MLSTM_DH)
            hn = (h * lax.rsqrt(ms + EPS) * nw_ref[hh]).T
            o_ref[rows, lanes] = (hn * jax.nn.sigmoid(zo_ref[rows, lanes].astype(F32))).astype(BF16)
        return 0

    if nc == 1:
        finish(0, 0)
    else:
        lax.fori_loop(0, nc, finish, 0)


def _mlstm(p, l, q, k, vt, zo, pr, nw, init=None, want_final=False):
    nc = p.seq // MLSTM_CHUNK
    tok = pl.BlockSpec((None, p.seq, HPS * HEAD_PAD), lambda b, h: (b, 0, h))
    st = lambda r, c: pl.BlockSpec((1, 2, HPS, r, c), lambda b, h: (b, 0, h, 0, 0))
    prs = lambda dr: pl.BlockSpec((None, HPS, 3, p.seq), lambda b, h: (b, dr * (MLSTM_HEADS // HPS) + h, 0, 0))
    in_specs, args = [], []
    if init is not None:
        in_specs += [pl.BlockSpec(memory_space=pltpu.SMEM), st(HEAD_PAD, HEAD_PAD)]
        args += list(init)
    in_specs += [tok, tok, pl.BlockSpec((None, HPS * HEAD_PAD, p.seq), lambda b, h: (b, h, 0)), tok, prs(0), prs(1),
                 pl.BlockSpec((None, HPS, HEAD_PAD, 1), lambda b, h: (l, h, 0, 0))]
    args += [q, k, vt, zo, pr, pr, nw]
    out_specs = [tok]
    out_shape = [jax.ShapeDtypeStruct((p.nb, p.seq, HEADS_W), BF16)]
    if want_final:
        out_specs += [st(MLSTM_DH, MLSTM_DH), st(1, MLSTM_DH), st(SUBLANES, LANES)]
        out_shape += [jax.ShapeDtypeStruct((p.nb, 2, MLSTM_HEADS, r, c), F32)
                      for r, c in ((MLSTM_DH, MLSTM_DH), (1, MLSTM_DH), (SUBLANES, LANES))]
    return pl.pallas_call(
        functools.partial(_mlstm_kernel, nc, init is not None, want_final),
        grid=(p.nb, MLSTM_HEADS // HPS),
        in_specs=in_specs,
        out_specs=out_specs,
        out_shape=out_shape,
        scratch_shapes=[pltpu.VMEM((HPS, 2, nc, HEAD_PAD, MLSTM_CHUNK), F32)],
        compiler_params=_cparams(("arbitrary", "arbitrary")),
        name="mlstm_%d" % p.seq,
    )(*args)


def _cpow(br, bi, e, nbits):
    pr = pi = None
    for bit in range(nbits):
        sel = ((e >> bit) & 1) == 1
        if pr is None:
            pr, pi = jnp.where(sel, br, 1.0), jnp.where(sel, bi, 0.0)
        else:
            pr, pi = jnp.where(sel, pr * br - pi * bi, pr), jnp.where(sel, pr * bi + pi * br, pi)
        if bit + 1 < nbits:
            br, bi = br * br - bi * bi, 2.0 * br * bi
    return pr, pi


def _s5_prep_kernel(lamc_re_ref, lamc_im_ref, lamr_re_ref, lamr_im_ref, lstep_ref,
                    bt_re_ref, bt_im_ref, ct_re_ref, ct_im_ref,
                    t_ref, m_ref, n_ref, a_ref):
    C = S5_CHUNK
    nbits = (C - 1).bit_length()
    assert C == 1 << nbits
    kk = lax.broadcasted_iota(jnp.int32, (S5_ST, S5_IN), 1) >> 4
    top = lax.broadcasted_iota(jnp.int32, (S5_ST, S5_IN), 0) < S5_STATE
    srow = lax.broadcasted_iota(jnp.int32, (S5_IN, S5_ST), 0) >> 4
    left = lax.broadcasted_iota(jnp.int32, (S5_IN, S5_ST), 1) < S5_STATE
    left16 = lax.broadcasted_iota(jnp.int32, (S5_GROUP_CH, S5_ST), 1) < S5_STATE
    left1 = lax.broadcasted_iota(jnp.int32, (1, S5_ST), 1) < S5_STATE
    lane = lax.broadcasted_iota(jnp.int32, (S5_GROUP_CH, S5_IN), 1)
    sel = jnp.where((lane & (S5_GROUP_CH - 1)) == lax.broadcasted_iota(jnp.int32, (S5_GROUP_CH, S5_IN), 0),
                    1.0, 0.0)
    spread = lambda ref: jnp.dot(jnp.concatenate([ref[0, 0], ref[0, 0]], axis=0), sel,
                                 precision=lax.Precision.HIGHEST, preferred_element_type=F32)
    ct_re = spread(ct_re_ref)
    ct_im = spread(ct_im_ref)
    resp = []
    for d in range(2):
        step = jnp.exp(lstep_ref[0, d, 0])
        lr_c, li_c = lamc_re_ref[0, d, 0] * step, lamc_im_ref[0, d, 0] * step
        lr_r, li_r = lamr_re_ref[0, d, 0], lamr_im_ref[0, d, 0]
        lbc_re, lbc_im = jnp.exp(lr_c) * jnp.cos(li_c), jnp.exp(lr_c) * jnp.sin(li_c)
        mag = jnp.exp(lr_r * step)
        lb_re, lb_im = mag * jnp.cos(li_r * step), mag * jnp.sin(li_r * step)

        pr, pi = _cpow(lbc_re, lbc_im, kk if d == 0 else (C - 1) - kk, nbits)
        pr1, pi1 = pr * lbc_re - pi * lbc_im, pr * lbc_im + pi * lbc_re
        cpr, cpi = ct_re * pr - ct_im * pi, ct_re * pi + ct_im * pr
        cpr1, cpi1 = ct_re * pr1 - ct_im * pi1, ct_re * pi1 + ct_im * pr1

        nr, ni = lb_re - 1.0, lb_im
        den = lr_r * lr_r + li_r * li_r
        kap_re = (nr * lr_r + ni * li_r) / den
        kap_im = (ni * lr_r - nr * li_r) / den
        bb_re = kap_re * bt_re_ref[0, 0] - kap_im * bt_im_ref[0, 0]
        bb_im = kap_re * bt_im_ref[0, 0] + kap_im * bt_re_ref[0, 0]

        resp.append(jnp.dot(jnp.where(left16, bb_re, -bb_im), jnp.where(top, cpr, cpi),
                            precision=lax.Precision.HIGHEST, preferred_element_type=F32))
        m_ref[0, d, 0] = jnp.where(top, cpr1, -cpi1).astype(BF16)

        pr, pi = _cpow(lb_re, lb_im, (C - 1) - srow if d == 0 else srow, nbits)
        bt_r = jnp.concatenate([bb_re] * C, axis=0)
        bt_i = jnp.concatenate([bb_im] * C, axis=0)
        n_re, n_im = pr * bt_r - pi * bt_i, pr * bt_i + pi * bt_r
        n_ref[0, d, 0] = jnp.concatenate([jnp.where(left, n_re, n_im), jnp.where(left, n_im, n_re)],
                                         axis=1).astype(BF16)

        ar, ai = lb_re, lb_im
        for _ in range(nbits):
            ar, ai = ar * ar - ai * ai, 2.0 * ar * ai
        a2 = jnp.where(left1, -ai, ai)
        a_ref[0, d, 0] = jnp.concatenate([jnp.concatenate([ar, ar], axis=1),
                                          jnp.concatenate([a2, -a2], axis=1)], axis=0)

    rf, rb = resp
    for s in range(C):
        nf = S5_GROUP_CH * s
        blk = jnp.where(lane >= nf, pltpu.roll(rf, nf, 1) if nf else rf, 0.0)
        nb = S5_GROUP_CH * (C - 1 - s)
        blk = blk + jnp.where(lane < S5_IN - nb, pltpu.roll(rb, S5_IN - nb, 1) if nb else rb, 0.0)
        t_ref[0, 0, S5_GROUP_CH * s:S5_GROUP_CH * (s + 1), :] = blk.astype(BF16)


def _s5_prep(lam_re, lam_im, log_step, b_re, b_im, c_re, c_im):
    G = S5_GROUPS
    dup = lambda a: jnp.concatenate([a, a], axis=-1)
    lamc = [dup(a).reshape(DEPTH, 2, G, S5_ST, 1) for a in (lam_re, lam_im)]
    lamr = [dup(a).reshape(DEPTH, 2, G, 1, S5_ST) for a in (lam_re, lam_im)]
    lstep = log_step.reshape(DEPTH, 2, G, 1, 1)
    bt = [dup(jnp.swapaxes(a, 2, 3)) for a in (b_re, b_im)]
    ct = [jnp.swapaxes(a, 2, 3) for a in (c_re, c_im)]
    dspec = lambda r, c: pl.BlockSpec((1, 2, 1, r, c), lambda l, g: (l, 0, g, 0, 0))
    gspec = lambda r, c: pl.BlockSpec((1, 1, r, c), lambda l, g: (l, g, 0, 0))
    t, m, n, a = pl.pallas_call(
        _s5_prep_kernel,
        grid=(DEPTH, G),
        in_specs=[dspec(S5_ST, 1), dspec(S5_ST, 1), dspec(1, S5_ST), dspec(1, S5_ST), dspec(1, 1),
                  gspec(S5_GROUP_CH, S5_ST), gspec(S5_GROUP_CH, S5_ST),
                  gspec(S5_STATE, S5_GROUP_CH), gspec(S5_STATE, S5_GROUP_CH)],
        out_specs=[gspec(S5_IN, S5_IN), dspec(S5_ST, S5_IN), dspec(S5_IN, S5_ST2), dspec(2, S5_ST2)],
        out_shape=[jax.ShapeDtypeStruct((DEPTH, G, S5_IN, S5_IN), BF16),
                   jax.ShapeDtypeStruct((DEPTH, 2, G, S5_ST, S5_IN), BF16),
                   jax.ShapeDtypeStruct((DEPTH, 2, G, S5_IN, S5_ST2), BF16),
                   jax.ShapeDtypeStruct((DEPTH, 2, G, 2, S5_ST2), F32)],
        compiler_params=_cparams(("arbitrary", "arbitrary")),
        name="s5_prep",
    )(*lamc, *lamr, lstep, *bt, *ct)
    return t, m, n, jnp.transpose(a, (0, 1, 3, 2, 4)).reshape(DEPTH, 2, 2, G * S5_ST2)


def _s5_kernel(nseg, nchunks, zr_ref, t_ref, m_ref, n_ref, a_ref, x0_ref, d_ref,
               y_ref, xfin_ref, v_ref, xp_ref):
    W = S5_GB * S5_ST2
    for gl in range(S5_GB):
        u = zr_ref[0, :, gl * S5_IN:(gl + 1) * S5_IN].astype(BF16)
        for d in range(2):
            v_ref[d, :, gl * S5_ST2:(gl + 1) * S5_ST2] = jnp.dot(
                u, n_ref[d, gl], preferred_element_type=F32)

    def halves(x, which):
        return [x[:, g * S5_ST2 + h * S5_ST:g * S5_ST2 + (h + 1) * S5_ST]
                for g in range(S5_GB) for h in which]

    per = SUBLANES // TB
    steps = nchunks // per

    for d in range(2):
        a = a_ref[d, 0:1, :]
        a2 = a_ref[d, 1:2, :]

        def advance(x, v):
            swapped = jnp.concatenate(halves(x, (1, 0)), axis=1)
            return a * x + a2 * swapped + v

        for seg in range(nseg):
            def step(i, x):
                si = i if d == 0 else steps - 1 - i
                rows = pl.ds(pl.multiple_of(seg * nchunks * TB + si * SUBLANES, SUBLANES), SUBLANES)
                v = v_ref[d, rows, :]
                order = range(per) if d == 0 else range(per - 1, -1, -1)
                entering = [None] * per
                for j in order:
                    entering[j] = jnp.concatenate(halves(x, (0,)), axis=1)
                    x = advance(x, v[j * TB:(j + 1) * TB])
                xp_ref[d, rows, :] = jnp.concatenate(entering, axis=0)
                return x

            x0 = jnp.zeros((TB, W), F32) if x0_ref is None else x0_ref[d, seg * TB:(seg + 1) * TB, :]
            x = lax.fori_loop(0, steps, step, x0)
            if xfin_ref is not None:
                xfin_ref[d, seg * TB:(seg + 1) * TB, :] = jnp.concatenate(halves(x, (0,)), axis=1)

    for gl in range(S5_GB):
        cols = slice(gl * S5_IN, (gl + 1) * S5_IN)
        u = zr_ref[0, :, cols]
        y = jnp.dot(u.astype(BF16), t_ref[gl], preferred_element_type=F32)
        for d in range(2):
            y = y + jnp.dot(xp_ref[d, :, gl * S5_ST:(gl + 1) * S5_ST].astype(BF16), m_ref[d, gl],
                            preferred_element_type=F32)
        y_ref[0, :, cols] = jax.nn.gelu(y + d_ref[0, :, cols] * u)


def _s5(p, l, zr, t, m, n, a, dt, x0=None, want_final=False):
    G = S5_GROUPS
    nseg = p.nb // TB
    row = pl.BlockSpec((1, R_PASS, S5_ROW), lambda j: (j, 0, 0))
    dsp = lambda r, c: pl.BlockSpec((None, 2, S5_GB, r, c), lambda j: (l, 0, j, 0, 0))
    lsp = lambda r, w: pl.BlockSpec((2, r, S5_GB * w), lambda j: (0, 0, j))
    in_specs = [row, pl.BlockSpec((None, S5_GB, S5_IN, S5_IN), lambda j: (l, j, 0, 0)),
                dsp(S5_ST, S5_IN), dsp(S5_IN, S5_ST2),
                pl.BlockSpec((None, 2, 2, S5_GB * S5_ST2), lambda j: (l, 0, 0, j)),
                pl.BlockSpec((None, 1, 1, S5_ROW), lambda j: (l, j, 0, 0))]
    args = [zr, t, m, n, a, dt]
    if x0 is not None:
        in_specs.append(lsp(p.nb, S5_ST2))
        args.append(x0)
    out_specs = [row]
    out_shape = [jax.ShapeDtypeStruct((S5_NB, R_PASS, S5_ROW), F32)]
    if want_final:
        out_specs.append(lsp(p.nb, S5_ST))
        out_shape.append(jax.ShapeDtypeStruct((2, p.nb, G * S5_ST), F32))

    def body(zr_ref, t_ref, m_ref, n_ref, a_ref, d_ref, *rest):
        rest = list(rest)
        x0_ref = rest.pop(0) if x0 is not None else None
        y_ref = rest.pop(0)
        xfin_ref = rest.pop(0) if want_final else None
        _s5_kernel(nseg, p.seq // S5_CHUNK, zr_ref, t_ref, m_ref, n_ref, a_ref, x0_ref, d_ref,
                   y_ref, xfin_ref, *rest)

    return pl.pallas_call(
        body,
        grid=(S5_NB,),
        in_specs=in_specs,
        out_specs=out_specs,
        out_shape=out_shape,
        scratch_shapes=[pltpu.VMEM((2, R_PASS, S5_GB * S5_ST2), F32),
                        pltpu.VMEM((2, R_PASS, S5_GB * S5_ST), F32)],
        compiler_params=_cparams(("arbitrary",)),
        name="s5_scan_%d" % p.seq,
    )(*args)


def _mix_kernel(x_ref, mod_ref, fo_ref, mo_ref, ys_ref, wglu_ref, wo_ref, nw_ref, x1_ref, xn_ref, scr_ref):
    mod = mod_ref[...]
    g1 = mod[:, :, 2 * D_MODEL:3 * D_MODEL]
    flat = lambda ref: ref[...].reshape(TB * TT, ref.shape[-1])
    o_m, o_s = FOURIER_W, FOURIER_W + HEADS_W
    mix = (jnp.dot(flat(fo_ref), wo_ref[:o_m, :], preferred_element_type=F32)
           + jnp.dot(flat(mo_ref), wo_ref[o_m:o_s, :], preferred_element_type=F32))
    blocks = []
    for bl in range(S5_NB):
        by_group = [ys_ref[bl, :, gl * S5_IN:(gl + 1) * S5_IN] for gl in range(S5_GB)]
        for t in range(S5_CHUNK):
            lo = t * S5_GROUP_CH
            scr_ref[bl, pl.ds(t, TROWS, stride=S5_CHUNK), :] = jnp.concatenate(
                [y[:, lo:lo + S5_GROUP_CH] for y in by_group], axis=1)
        slabs = [scr_ref[bl, _slab(b, c)[1], :] for b in range(TB) for c in range(TT // S5_CHUNK)]
        blocks.append(jnp.concatenate(slabs, axis=0))
    y = jnp.concatenate(blocks, axis=1).astype(BF16)
    gg = jnp.dot(y, wglu_ref[...], preferred_element_type=F32)
    s_out = (gg[:, :S5_W] * jax.nn.sigmoid(gg[:, S5_W:])).astype(BF16)
    mix = mix + jnp.dot(s_out, wo_ref[o_s:, :], preferred_element_type=F32)
    x1 = x_ref[...] + g1 * mix.reshape(TB, TT, D_MODEL)
    x1_ref[...] = x1
    xn_ref[...] = _mod_norm(x1, mod, nw_ref[...], 3).reshape(TB, TT, D_MODEL)


def _mix(p, l, x, mods, fo, mo, ys, wglu, wo, nw):
    return pl.pallas_call(
        _mix_kernel,
        grid=(p.nb // TB, p.seq // TT),
        in_specs=[_tile_spec(D_MODEL), _tile_mod_spec(p, l),
                  _tile_spec(FOURIER_W), _tile_spec(HEADS_W), _chunk_rows_spec(p),
                  _layer(wglu, l), _layer(wo, l), _layer(nw, l)],
        out_specs=[_tile_spec(D_MODEL), _tile_spec(D_MODEL)],
        out_shape=[jax.ShapeDtypeStruct((p.nb, p.seq, D_MODEL), F32),
                   jax.ShapeDtypeStruct((p.nb, p.seq, D_MODEL), BF16)],
        scratch_shapes=[pltpu.VMEM((S5_NB, TB * TT, LANES), F32)],
        compiler_params=_cparams(("arbitrary", "arbitrary")),
        name="mix_out_%d" % p.seq,
    )(x, mods, fo, mo, ys, wglu, wo, nw)


def _ffn_kernel(final, xn_ref, x1_ref, mod_ref, wg_ref, wu_ref, wd_ref, nf_ref, o_ref):
    xn = xn_ref[...]
    ff = None
    for j in range(D_FF // TF):
        cols = slice(j * TF, (j + 1) * TF)
        a = jnp.dot(xn, wg_ref[:, cols], preferred_element_type=F32)
        u = jnp.dot(xn, wu_ref[:, cols], preferred_element_type=F32)
        h = (a * jax.nn.sigmoid(a) * u).astype(BF16)
        part = jnp.dot(h, wd_ref[cols, :], preferred_element_type=F32)
        ff = part if ff is None else ff + part
    g2 = mod_ref[0][:, 5 * D_MODEL:6 * D_MODEL]
    x2 = x1_ref[...] + g2 * ff
    if final:
        x2 = x2 * lax.rsqrt(jnp.mean(x2 * x2, axis=-1, keepdims=True) + EPS) * nf_ref[...]
    o_ref[...] = x2


def _ffn(p, l, final, xn, x1, mods, wg, wu, wd, nf):
    tok = pl.BlockSpec((TM, D_MODEL), lambda i: (i, 0))
    mod_row = (lambda i: p.mod_first + i // (p.seq // TM)) if p.mod_each else (lambda i: p.mod_first)
    resident = lambda a: pl.BlockSpec((None,) + a.shape[1:], lambda i: (l, 0, 0), pipeline_mode=pl.Buffered(1))
    return pl.pallas_call(
        functools.partial(_ffn_kernel, final),
        grid=(T_PASS // TM,),
        in_specs=[tok, tok,
                  pl.BlockSpec((None, 1, 1, 6 * D_MODEL), lambda i: (l, mod_row(i), 0, 0)),
                  resident(wg), resident(wu), resident(wd),
                  pl.BlockSpec((1, D_MODEL), lambda i: (0, 0))],
        out_specs=tok,
        out_shape=jax.ShapeDtypeStruct((T_PASS, D_MODEL), F32),
        compiler_params=_cparams(("arbitrary",)),
        name="ffn_%d" % p.seq,
    )(xn, x1, mods, wg, wu, wd, nf)


def _pad_heads(a, axis):
    shape = a.shape[:axis] + (MLSTM_HEADS, MLSTM_DH) + a.shape[axis + 1:]
    pad = [(0, 0)] * (a.ndim + 1)
    pad[axis + 1] = (0, HEAD_PAD - MLSTM_DH)
    return jnp.pad(a.reshape(shape), pad).reshape(a.shape[:axis] + (HEADS_W,) + a.shape[axis + 1:])


def _mlstm_state_in(c, n):
    cn = jnp.concatenate([c, n[..., None]], axis=-1)
    return jnp.pad(cn, ((0, 0),) * (c.ndim - 2) + ((0, HEAD_PAD - MLSTM_DH), (0, HEAD_PAD - MLSTM_DH - 1)))


def kernel(x_prompt, x_sample, state_mlstm_C, state_mlstm_n, state_mlstm_m, state_s5_re, state_s5_im,
           c, c_ctx, w_ada, b_ada, norm1_w, norm2_w, w_in, b_gates, w_fourier, mlstm_norm_w,
           s5_lambda_re, s5_lambda_im, s5_log_step, s5_b_re, s5_b_im, s5_c_re, s5_c_im, s5_d,
           w_glu, w_out, w_gate, w_up, w_down, norm_f):
    xs = {PROMPT: x_prompt, SAMPLE: x_sample}
    cc = jnp.concatenate([c, c_ctx[None], jnp.zeros((N_MODS - 1 - DEC_BATCH, D_MODEL), F32)], axis=0)
    mods = _ada(cc, w_ada, b_ada).reshape(DEPTH, N_MODS, 1, 6 * D_MODEL)
    cdsd, cs, ab = (jnp.asarray(a.astype(np.float32)).astype(BF16) for a in _dft_consts())
    s5_t, s5_m, s5_n, s5_a = _s5_prep(s5_lambda_re, s5_lambda_im, s5_log_step, s5_b_re, s5_b_im,
                                      s5_c_re, s5_c_im)

    o_q = FOURIER_W
    o_g = o_q + 3 * MLSTM_W
    o_o = o_g + N_GATES
    o_u = o_o + MLSTM_W
    heads = lambda o: _pad_heads(w_in[:, :, o:o + MLSTM_W], 2)
    w_cat = jnp.concatenate([heads(o_q), heads(o_q + MLSTM_W), heads(o_o), w_in[:, :, :FOURIER_W],
                             w_in[:, :, o_u:]], axis=2).astype(BF16)
    wv_t = jnp.swapaxes(heads(o_q + 2 * MLSTM_W), 1, 2).astype(BF16)
    gate_perm = np.arange(N_GATES).reshape(2, 2, MLSTM_HEADS).transpose(1, 0, 2).reshape(-1)
    wg_t = jnp.swapaxes(w_in[:, :, o_g:o_o], 1, 2)[:, gate_perm].astype(BF16)
    bg = b_gates[:, gate_perm, None]
    wf = w_fourier.astype(BF16)
    nw = _pad_heads(mlstm_norm_w, 1).reshape(DEPTH, MLSTM_HEADS, HEAD_PAD, 1)
    dt = jnp.tile(s5_d[:, :, None, :], (1, 1, S5_CHUNK, 1)).reshape(DEPTH, S5_NB, 1, S5_ROW)
    wo_m = jnp.pad(w_out[:, FOURIER_W:FOURIER_W + MLSTM_W].reshape(DEPTH, MLSTM_HEADS, MLSTM_DH, D_MODEL),
                   ((0, 0), (0, 0), (0, HEAD_PAD - MLSTM_DH), (0, 0))).reshape(DEPTH, HEADS_W, D_MODEL)
    wo = jnp.concatenate([w_out[:, :FOURIER_W], wo_m, w_out[:, FOURIER_W + MLSTM_W:]], axis=1).astype(BF16)
    wglu = w_glu.astype(BF16)
    wg, wu, wd = w_gate.astype(BF16), w_up.astype(BF16), w_down.astype(BF16)
    n1, n2 = norm1_w[:, None, :], norm2_w[:, None, :]

    m0 = jnp.swapaxes(state_mlstm_m, 0, 1)
    c0 = jnp.swapaxes(_mlstm_state_in(state_mlstm_C, state_mlstm_n), 0, 1)
    x0 = jnp.concatenate([state_s5_re, state_s5_im, state_s5_im, state_s5_re], axis=-1)
    x0 = jnp.transpose(x0, (1, 2, 0, 3, 4)).reshape(DEPTH, 2, DEC_BATCH, S5_GROUPS * S5_ST2)

    finals = []
    for l in range(DEPTH):
        for p in (PROMPT, SAMPLE):
            x = xs[p]
            zq, zk, zo, zf, zu, vt, gt = _in_proj(p, l, x, mods, n1, w_cat, wv_t, wg_t, bg)
            pr = _gate_prep(p, gt)
            if p is PROMPT:
                fo = _fourier_prompt(l, zf, cdsd, cs, wf)
                mo, *fin = _mlstm(p, l, zq, zk, vt, zo, pr, nw, want_final=True)
                ys, xfin = _s5(p, l, zu, s5_t, s5_m, s5_n, s5_a, dt, want_final=True)
                finals.append(fin + [xfin])
            else:
                fo = _fourier_sample(l, zf, cdsd, ab, wf)
                mo, = _mlstm(p, l, zq, zk, vt, zo, pr, nw, init=(m0[l], c0[l]))
                ys, = _s5(p, l, zu, s5_t, s5_m, s5_n, s5_a, dt, x0=x0[l])
            x1, xn2 = _mix(p, l, x, mods, fo, mo, ys, wglu, wo, n2)
            x2 = _ffn(p, l, l == DEPTH - 1, xn2.reshape(T_PASS, D_MODEL), x1.reshape(T_PASS, D_MODEL),
                      mods, wg, wu, wd, norm_f[None])
            xs[p] = x2.reshape(p.nb, p.seq, D_MODEL)

    cfin, nfin, mfin, xfin = (jnp.stack(parts, axis=1) for parts in zip(*finals))
    xfin = xfin.reshape(2, DEPTH, BATCH, S5_GROUPS, 2, S5_STATE)
    new_re, new_im = (jnp.transpose(xfin[:, :, :, :, i], (2, 1, 0, 3, 4)) for i in range(2))
    return (xs[PROMPT], xs[SAMPLE], cfin, nfin[:, :, :, :, 0], mfin[:, :, :, :, 0, 0], new_re, new_im)
```

```python
import collections
import functools
import math

import numpy as np
import jax
import jax.numpy as jnp
from jax import lax
from jax.experimental import pallas as pl
from jax.experimental.pallas import tpu as pltpu

F32 = jnp.float32
BF16 = jnp.bfloat16

D_MODEL = 1024
BATCH = 32
SEQ = 256
DEPTH = 2
DEC_BATCH = 4
DEC_SEQ = 2048
GRID_W = 64
FOURIER_W = 256
FOURIER_DH = 64
MLSTM_W = 384
MLSTM_HEADS = 4
MLSTM_DH = 96
S5_W = 384
S5_GROUP_CH = 16
S5_GROUPS = 24
S5_STATE = 64
N_GATES = 16
D_FF = 2816
EPS = 1e-6

LANES = 128
SUBLANES = 8
VMEM_LIMIT = 56 * 1024 * 1024

HEAD_PAD = LANES
HEADS_W = MLSTM_HEADS * HEAD_PAD
N_AUG = MLSTM_DH
Z_W = 3 * HEADS_W + FOURIER_W + S5_W
MLSTM_CHUNK = 256
S5_CHUNK = 16
S5_IN = S5_CHUNK * S5_GROUP_CH
S5_ST = 2 * S5_STATE
S5_ST2 = 2 * S5_ST
S5_GB = LANES // S5_GROUP_CH
S5_NB = S5_W // LANES
S5_ROW = S5_GB * S5_IN
TB = 4
TT = 128
TROWS = TB * TT // S5_CHUNK
TM = 512
TF = D_FF // 2
HPS = 4
N_MODS = 8
NEG = -1e30

Pass = collections.namedtuple("Pass", "nb seq mod_first mod_each")
PROMPT = Pass(BATCH, SEQ, DEC_BATCH, False)
SAMPLE = Pass(DEC_BATCH, DEC_SEQ, 0, True)
T_PASS = BATCH * SEQ
assert T_PASS == DEC_BATCH * DEC_SEQ
R_PASS = T_PASS // S5_CHUNK

_NT = (((1,), (1,)), ((), ()))
_TN = (((0,), (0,)), ((), ()))


def _cparams(sem):
    return pltpu.CompilerParams(dimension_semantics=sem, vmem_limit_bytes=VMEM_LIMIT)


def _full(a):
    return pl.BlockSpec(a.shape, lambda *_: (0,) * a.ndim)


def _layer(a, l):
    return pl.BlockSpec((None,) + a.shape[1:], lambda *_: (l,) + (0,) * (a.ndim - 1))


def _tile_mod_spec(p, l):
    if p.mod_each:
        return pl.BlockSpec((None, TB, 1, 6 * D_MODEL), lambda j, k: (l, p.mod_first // TB + j, 0, 0))
    return pl.BlockSpec((None, 1, 1, 6 * D_MODEL), lambda j, k: (l, p.mod_first, 0, 0))


def _log_sigmoid(x):
    return jnp.minimum(x, 0.0) - jnp.log1p(jnp.exp(-jnp.abs(x)))


def _tile_spec(w):
    return pl.BlockSpec((TB, TT, w), lambda j, k: (j, k, 0))


def _chunk_rows_spec(p):
    per_group = p.seq // TT
    return pl.BlockSpec((S5_NB, TROWS, S5_ROW), lambda j, k: (0, j * per_group + k, 0))


def _slab(b, c):
    tok = slice(b * TT + c * S5_CHUNK, b * TT + (c + 1) * S5_CHUNK)
    chk = slice((c * TB + b) * S5_CHUNK, (c * TB + b + 1) * S5_CHUNK)
    return tok, chk


def _ada_kernel(c_ref, w_ref, b_ref, o_ref):
    a = c_ref[...]
    a = (a * jax.nn.sigmoid(a)).astype(BF16)
    o_ref[0] = jnp.dot(a, w_ref[0].astype(BF16), preferred_element_type=F32) + b_ref[0]


def _ada(cc, w_ada, b_ada):
    tn = 512
    return pl.pallas_call(
        _ada_kernel,
        grid=(DEPTH, 6 * D_MODEL // tn),
        in_specs=[pl.BlockSpec((N_MODS, D_MODEL), lambda l, j: (0, 0)),
                  pl.BlockSpec((1, D_MODEL, tn), lambda l, j: (l, 0, j)),
                  pl.BlockSpec((1, 1, tn), lambda l, j: (l, 0, j))],
        out_specs=pl.BlockSpec((1, N_MODS, tn), lambda l, j: (l, 0, j)),
        out_shape=jax.ShapeDtypeStruct((DEPTH, N_MODS, 6 * D_MODEL), F32),
        compiler_params=_cparams(("arbitrary", "arbitrary")),
        name="ada_mod",
    )(cc, w_ada, b_ada.reshape(DEPTH, 1, 6 * D_MODEL))


def _mod_norm(x3, mod, nw, first):
    sh = mod[:, :, first * D_MODEL:(first + 1) * D_MODEL]
    sc = mod[:, :, (first + 1) * D_MODEL:(first + 2) * D_MODEL]
    y = x3 * lax.rsqrt(jnp.mean(x3 * x3, axis=-1, keepdims=True) + EPS) * nw
    return (y * (1.0 + sc) + sh).reshape(TB * TT, D_MODEL).astype(BF16)


def _in_kernel(x_ref, mod_ref, nw_ref, w_ref, wv_ref, wg_ref, bg_ref,
               zq_ref, zk_ref, zo_ref, zf_ref, zu_ref, vt_ref, gt_ref, scr_ref):
    xn = _mod_norm(x_ref[...], mod_ref[...], nw_ref[...], 0)
    w_main = Z_W - S5_W
    zu = jnp.dot(xn, w_ref[:, w_main:], preferred_element_type=F32)
    for bl in range(S5_NB):
        zb = zu[:, bl * LANES:(bl + 1) * LANES]
        for b in range(TB):
            for c in range(TT // S5_CHUNK):
                tok, chk = _slab(b, c)
                scr_ref[bl, chk, :] = zb[tok]
        by_token = [scr_ref[bl, pl.ds(s, TROWS, stride=S5_CHUNK), :] for s in range(S5_CHUNK)]
        for gl in range(S5_GB):
            lo = gl * S5_GROUP_CH
            zu_ref[bl, :, gl * S5_IN:(gl + 1) * S5_IN] = jnp.concatenate(
                [x[:, lo:lo + S5_GROUP_CH] for x in by_token], axis=1)
    z = jnp.dot(xn, w_ref[:, :w_main], preferred_element_type=F32)
    o = 0
    for ref, w in ((zq_ref, HEADS_W), (zk_ref, HEADS_W), (zo_ref, HEADS_W), (zf_ref, FOURIER_W)):
        ref[...] = z[:, o:o + w].astype(BF16).reshape(TB, TT, w)
        o += w
    vt = lax.dot_general(wv_ref[...], xn, _NT, preferred_element_type=F32).astype(BF16)
    gt = lax.dot_general(wg_ref[...], xn, _NT, preferred_element_type=F32) + bg_ref[...]
    for b in range(TB):
        vt_ref[b] = vt[:, b * TT:(b + 1) * TT]
        gt_ref[b] = gt[:, b * TT:(b + 1) * TT]


def _in_proj(p, l, x, mods, nw, w, wv_t, wg_t, bg):
    chan = lambda c_: pl.BlockSpec((TB, c_, TT), lambda j, k: (j, 0, k))
    outs = [HEADS_W] * 3 + [FOURIER_W]
    return pl.pallas_call(
        _in_kernel,
        grid=(p.nb // TB, p.seq // TT),
        in_specs=[_tile_spec(D_MODEL), _tile_mod_spec(p, l),
                  _layer(nw, l), _layer(w, l), _layer(wv_t, l), _layer(wg_t, l), _layer(bg, l)],
        out_specs=[_tile_spec(w_) for w_ in outs] + [_chunk_rows_spec(p), chan(HEADS_W), chan(N_GATES)],
        out_shape=[jax.ShapeDtypeStruct((p.nb, p.seq, w_), BF16) for w_ in outs]
        + [jax.ShapeDtypeStruct((S5_NB, R_PASS, S5_ROW), F32),
           jax.ShapeDtypeStruct((p.nb, HEADS_W, p.seq), BF16),
           jax.ShapeDtypeStruct((p.nb, N_GATES, p.seq), F32)],
        scratch_shapes=[pltpu.VMEM((S5_NB, TB * TT, LANES), F32)],
        compiler_params=_cparams(("arbitrary", "arbitrary")),
        name="in_proj_%d" % p.seq,
    )(x, mods, nw, w, wv_t, wg_t, bg)


def _dft_consts():
    d = np.arange(FOURIER_DH)
    phi = 2.0 * np.pi * ((d[:, None] * d[None, :]) % FOURIER_DH) / FOURIER_DH
    eye = np.eye(FOURIER_W // FOURIER_DH)
    cd = np.kron(eye, np.cos(phi)) / math.sqrt(FOURIER_DH)
    sd = np.kron(eye, np.sin(phi)) / math.sqrt(FOURIER_DH)
    s = np.arange(SEQ)
    th = 2.0 * np.pi * ((s[:, None] * s[None, :]) % SEQ) / SEQ
    rows = DEC_SEQ // GRID_W
    pos = np.arange(DEC_SEQ)
    r, c = pos // GRID_W, pos % GRID_W
    ph = ((r[:, None] * r[None, :]) * (GRID_W // rows) + c[:, None] * c[None, :]) % GRID_W
    th2 = 2.0 * np.pi * ph / GRID_W
    return (np.concatenate([cd, sd], axis=1),
            np.concatenate([np.cos(th), -np.sin(th)], axis=1) / math.sqrt(SEQ),
            np.concatenate([np.cos(th2), -np.sin(th2)], axis=1) / math.sqrt(DEC_SEQ))


def _fourier_prompt_kernel(nb, zf_ref, cdsd_ref, cs_ref, wf_ref, o_ref):
    t = jnp.dot(zf_ref[...].reshape(nb * SEQ, FOURIER_W), cdsd_ref[...],
                preferred_element_type=F32).astype(BF16)
    for b in range(nb):
        tb = t[b * SEQ:(b + 1) * SEQ]
        st = jnp.concatenate([tb[:, :FOURIER_W], tb[:, FOURIER_W:]], axis=0)
        f = jnp.dot(cs_ref[...], st, preferred_element_type=F32)
        o_ref[b] = jnp.dot(f.astype(BF16), wf_ref[...], preferred_element_type=F32).astype(BF16)


def _fourier_prompt(l, zf, cdsd, cs, wf):
    nb = 4
    blk = pl.BlockSpec((nb, SEQ, FOURIER_W), lambda i: (i, 0, 0))
    return pl.pallas_call(
        functools.partial(_fourier_prompt_kernel, nb),
        grid=(BATCH // nb,),
        in_specs=[blk, _full(cdsd), _full(cs), _layer(wf, l)],
        out_specs=blk,
        out_shape=jax.ShapeDtypeStruct((BATCH, SEQ, FOURIER_W), BF16),
        compiler_params=_cparams(("arbitrary",)),
        name="fourier_prompt",
    )(zf, cdsd, cs, wf)


def _fourier_sample_kernel(zf_ref, cdsd_ref, ab_ref, wf_ref, o_ref, tt_ref):
    @pl.when(pl.program_id(0) == 0)
    def _():
        for b in range(DEC_BATCH):
            t = jnp.dot(zf_ref[b], cdsd_ref[...], preferred_element_type=F32).astype(BF16)
            tt_ref[b, 0:DEC_SEQ, :] = t[:, :FOURIER_W]
            tt_ref[b, DEC_SEQ:2 * DEC_SEQ, :] = t[:, FOURIER_W:]

    for b in range(DEC_BATCH):
        f = jnp.dot(ab_ref[...], tt_ref[b], preferred_element_type=F32)
        o_ref[b] = jnp.dot(f.astype(BF16), wf_ref[...], preferred_element_type=F32).astype(BF16)


def _fourier_sample(l, zf, cdsd, ab, wf):
    tk = 512
    return pl.pallas_call(
        _fourier_sample_kernel,
        grid=(DEC_SEQ // tk,),
        in_specs=[_full(zf), _full(cdsd), pl.BlockSpec((tk, 2 * DEC_SEQ), lambda i: (i, 0)), _layer(wf, l)],
        out_specs=pl.BlockSpec((DEC_BATCH, tk, FOURIER_W), lambda i: (0, i, 0)),
        out_shape=jax.ShapeDtypeStruct((DEC_BATCH, DEC_SEQ, FOURIER_W), BF16),
        scratch_shapes=[pltpu.VMEM((DEC_BATCH, 2 * DEC_SEQ, FOURIER_W), BF16)],
        compiler_params=_cparams(("arbitrary",)),
        name="fourier_sample",
    )(zf, cdsd, ab, wf)


def _split3(x):
    hi = x.astype(BF16).astype(F32)
    mid = (x - hi).astype(BF16).astype(F32)
    lo = (x - hi - mid).astype(BF16).astype(F32)
    return hi, mid, lo


def _gate_kernel(g_ref, o_ref):
    L = MLSTM_CHUNK
    nrow = N_GATES // 2
    row = lax.broadcasted_iota(jnp.int32, (L, L), 0)
    col = lax.broadcasted_iota(jnp.int32, (L, L), 1)
    tri_pre = jnp.where(row <= col, 1.0, 0.0).astype(BF16)
    tri_suf = jnp.where(row >= col, 1.0, 0.0).astype(BF16)
    is_fwd = lax.broadcasted_iota(jnp.int32, (nrow, L), 0) < MLSTM_HEADS
    lane = lax.broadcasted_iota(jnp.int32, (nrow, L), 1)
    for bi in range(g_ref.shape[0]):
        for c in range(g_ref.shape[2] // L):
            cols = slice(c * L, (c + 1) * L)
            ig = g_ref[bi, 0:nrow, cols]
            lf = _log_sigmoid(g_ref[bi, nrow:, cols])
            parts = jnp.concatenate(_split3(lf), axis=0).astype(BF16)
            pre = jnp.dot(parts, tri_pre, preferred_element_type=F32)
            suf = jnp.dot(parts, tri_suf, preferred_element_type=F32)
            fold = lambda a: a[0:nrow] + a[nrow:2 * nrow] + a[2 * nrow:]
            b = jnp.where(is_fwd, fold(pre), fold(suf))
            r = ig - b
            pm = sm = r
            sh = 1
            while sh < L:
                pm = jnp.maximum(pm, jnp.where(lane >= sh, pltpu.roll(pm, sh, 1), NEG))
                sm = jnp.maximum(sm, jnp.where(lane < L - sh, pltpu.roll(sm, L - sh, 1), NEG))
                sh *= 2
            cm = jnp.where(is_fwd, pm, sm)
            for q, val in enumerate((b, r, cm)):
                for dh in range(nrow):
                    o_ref[bi, dh, q:q + 1, cols] = val[dh:dh + 1]


def _gate_prep(p, gt):
    bb = max(1, DEC_SEQ // p.seq)
    nrow = N_GATES // 2
    return pl.pallas_call(
        _gate_kernel,
        grid=(p.nb // bb,),
        in_specs=[pl.BlockSpec((bb, N_GATES, p.seq), lambda i: (i, 0, 0))],
        out_specs=pl.BlockSpec((bb, nrow, 3, p.seq), lambda i: (i, 0, 0, 0)),
        out_shape=jax.ShapeDtypeStruct((p.nb, nrow, 3, p.seq), F32),
        compiler_params=_cparams(("arbitrary",)),
        name="gate_prep_%d" % p.seq,
    )(gt)


def _mlstm_chunk(q, k, vt, pr, ct, m, fwd, out):
    L = q.shape[0]
    scale = MLSTM_DH ** -0.5
    b, r, cm = pr[0:1], pr[1:2], pr[2:3]
    ones = jnp.ones((3, L), F32)
    zeros = jnp.zeros((SUBLANES - 6, L), F32)
    lhs = jnp.concatenate(_split3(r) + (ones, zeros), axis=0).astype(BF16)
    rhs = jnp.concatenate((ones,) + _split3(-cm) + (zeros,), axis=0).astype(BF16)
    arg = lax.dot_general(lhs, rhs, _TN, preferred_element_type=F32)
    st = lax.dot_general(k, q, _NT, preferred_element_type=F32)
    cq = lax.dot_general(ct.astype(BF16), q, _NT, preferred_element_type=F32)
    last = L - 1 if fwd else 0
    cm_last = cm[:, last:last + 1]
    mx_last = jnp.maximum(m, cm_last)
    vw = (vt.astype(F32) * jnp.exp(r - cm_last)).astype(BF16)
    dct = jnp.dot(vw, k, preferred_element_type=F32)
    yield

    row = lax.broadcasted_iota(jnp.int32, (L, L), 0)
    col = lax.broadcasted_iota(jnp.int32, (L, L), 1)
    e = jnp.where((row <= col) if fwd else (row >= col), jnp.exp(arg), 0.0)
    num = jnp.dot(vt, (st * e).astype(BF16), preferred_element_type=F32)
    ct_new = jnp.exp(m - mx_last) * ct + (jnp.exp(cm_last - mx_last) * scale) * dct
    yield

    mx = jnp.maximum(m, cm)
    num = (jnp.exp(cm - mx) * scale) * num + jnp.exp(m - mx) * cq
    den = num[N_AUG:N_AUG + 1, :]
    h = num * (1.0 / jnp.maximum(jnp.abs(den), jnp.exp(-(b + mx))))
    out += [h, ct_new, b[:, last:last + 1] + mx_last]


def _mlstm_kernel(nc, has_init, want_final, *refs):
    refs = list(refs)
    m0_ref, c0_ref = (refs.pop(0), refs.pop(0)) if has_init else (None, None)
    q_ref, k_ref, vt_ref, zo_ref, prf_ref, prb_ref, nw_ref, o_ref = refs[:8]
    hbuf_ref = refs[-1]
    L = MLSTM_CHUNK
    bi = pl.program_id(0)
    hg = pl.program_id(1)
    vrow = lax.broadcasted_iota(jnp.int32, (HEAD_PAD, L), 0)

    def chunk(hh, ci, pr_ref, ct, m, dr, out):
        rows = pl.ds(pl.multiple_of(ci * L, L), L)
        lanes = slice(hh * HEAD_PAD, (hh + 1) * HEAD_PAD)
        vt = vt_ref[lanes, rows]
        vt = jnp.where(vrow == N_AUG, jnp.ones_like(vt), vt)
        res = []
        yield from _mlstm_chunk(q_ref[rows, lanes], k_ref[rows, lanes], vt, pr_ref[hh, :, rows], ct, m,
                                dr == 0, res)
        hbuf_ref[hh, dr, ci] = res[0]
        out += res[1:]

    def step(i, carry):
        outs = [[] for _ in range(2 * HPS)]
        gens = []
        for hh in range(HPS):
            cf, mf, cb, mb = carry[4 * hh:4 * hh + 4]
            gens += [chunk(hh, i, prf_ref, cf, mf, 0, outs[2 * hh]),
                     chunk(hh, nc - 1 - i, prb_ref, cb, mb, 1, outs[2 * hh + 1])]
        while gens:
            alive = []
            for g in gens:
                try:
                    next(g)
                    alive.append(g)
                except StopIteration:
                    pass
            gens = alive
        return tuple(x for o in outs for x in o)

    carry = []
    for hh in range(HPS):
        for dr in range(2):
            if has_init:
                carry += [c0_ref[0, dr, hh].T, jnp.full((1, 1), m0_ref[bi, dr, hg * HPS + hh], F32)]
            else:
                carry += [jnp.zeros((HEAD_PAD, HEAD_PAD), F32), jnp.zeros((1, 1), F32)]
    carry = step(0, tuple(carry)) if nc == 1 else lax.fori_loop(0, nc, step, tuple(carry))
    if want_final:
        cfin_ref, nfin_ref, mfin_ref = refs[8:11]
        for hh in range(HPS):
            for dr in range(2):
                ct, m = carry[4 * hh + 2 * dr], carry[4 * hh + 2 * dr + 1]
                cfin_ref[0, dr, hh] = ct.T[:MLSTM_DH, :MLSTM_DH]
                nfin_ref[0, dr, hh] = ct[N_AUG:N_AUG + 1, :MLSTM_DH]
                mfin_ref[0, dr, hh] = jnp.broadcast_to(m, (SUBLANES, LANES))

    def finish(ci, _):
        rows = pl.ds(pl.multiple_of(ci * L, L), L)
        for hh in range(HPS):
            lanes = slice(hh * HEAD_PAD, (hh + 1) * HEAD_PAD)
            h = jnp.where(vrow < MLSTM_DH, hbuf_ref[hh, 0, ci] + hbuf_ref[hh, 1, ci], 0.0)
            ms = jnp.sum(h * h, axis=0, keepdims=True) * (1.0 / MLSTM_DH)
            hn = (h * lax.rsqrt(ms + EPS) * nw_ref[hh]).T
            o_ref[rows, lanes] = (hn * jax.nn.sigmoid(zo_ref[rows, lanes].astype(F32))).astype(BF16)
        return 0

    if nc == 1:
        finish(0, 0)
    else:
        lax.fori_loop(0, nc, finish, 0)


def _mlstm(p, l, q, k, vt, zo, pr, nw, init=None, want_final=False):
    nc = p.seq // MLSTM_CHUNK
    tok = pl.BlockSpec((None, p.seq, HPS * HEAD_PAD), lambda b, h: (b, 0, h))
    st = lambda r, c: pl.BlockSpec((1, 2, HPS, r, c), lambda b, h: (b, 0, h, 0, 0))
    prs = lambda dr: pl.BlockSpec((None, HPS, 3, p.seq), lambda b, h: (b, dr * (MLSTM_HEADS // HPS) + h, 0, 0))
    in_specs, args = [], []
    if init is not None:
        in_specs += [pl.BlockSpec(memory_space=pltpu.SMEM), st(HEAD_PAD, HEAD_PAD)]
        args += list(init)
    in_specs += [tok, tok, pl.BlockSpec((None, HPS * HEAD_PAD, p.seq), lambda b, h: (b, h, 0)), tok, prs(0), prs(1),
                 pl.BlockSpec((None, HPS, HEAD_PAD, 1), lambda b, h: (l, h, 0, 0))]
    args += [q, k, vt, zo, pr, pr, nw]
    out_specs = [tok]
    out_shape = [jax.ShapeDtypeStruct((p.nb, p.seq, HEADS_W), BF16)]
    if want_final:
        out_specs += [st(MLSTM_DH, MLSTM_DH), st(1, MLSTM_DH), st(SUBLANES, LANES)]
        out_shape += [jax.ShapeDtypeStruct((p.nb, 2, MLSTM_HEADS, r, c), F32)
                      for r, c in ((MLSTM_DH, MLSTM_DH), (1, MLSTM_DH), (SUBLANES, LANES))]
    return pl.pallas_call(
        functools.partial(_mlstm_kernel, nc, init is not None, want_final),
        grid=(p.nb, MLSTM_HEADS // HPS),
        in_specs=in_specs,
        out_specs=out_specs,
        out_shape=out_shape,
        scratch_shapes=[pltpu.VMEM((HPS, 2, nc, HEAD_PAD, MLSTM_CHUNK), F32)],
        compiler_params=_cparams(("arbitrary", "arbitrary")),
        name="mlstm_%d" % p.seq,
    )(*args)


def _cpow(br, bi, e, nbits):
    pr = pi = None
    for bit in range(nbits):
        sel = ((e >> bit) & 1) == 1
        if pr is None:
            pr, pi = jnp.where(sel, br, 1.0), jnp.where(sel, bi, 0.0)
        else:
            pr, pi = jnp.where(sel, pr * br - pi * bi, pr), jnp.where(sel, pr * bi + pi * br, pi)
        if bit + 1 < nbits:
            br, bi = br * br - bi * bi, 2.0 * br * bi
    return pr, pi


def _s5_prep_kernel(lamc_re_ref, lamc_im_ref, lamr_re_ref, lamr_im_ref, lstep_ref,
                    bt_re_ref, bt_im_ref, ct_re_ref, ct_im_ref,
                    t_ref, m_ref, n_ref, a_ref):
    C = S5_CHUNK
    nbits = (C - 1).bit_length()
    assert C == 1 << nbits
    kk = lax.broadcasted_iota(jnp.int32, (S5_ST, S5_IN), 1) >> 4
    top = lax.broadcasted_iota(jnp.int32, (S5_ST, S5_IN), 0) < S5_STATE
    srow = lax.broadcasted_iota(jnp.int32, (S5_IN, S5_ST), 0) >> 4
    left = lax.broadcasted_iota(jnp.int32, (S5_IN, S5_ST), 1) < S5_STATE
    left16 = lax.broadcasted_iota(jnp.int32, (S5_GROUP_CH, S5_ST), 1) < S5_STATE
    left1 = lax.broadcasted_iota(jnp.int32, (1, S5_ST), 1) < S5_STATE
    lane = lax.broadcasted_iota(jnp.int32, (S5_GROUP_CH, S5_IN), 1)
    sel = jnp.where((lane & (S5_GROUP_CH - 1)) == lax.broadcasted_iota(jnp.int32, (S5_GROUP_CH, S5_IN), 0),
                    1.0, 0.0)
    spread = lambda ref: jnp.dot(jnp.concatenate([ref[0, 0], ref[0, 0]], axis=0), sel,
                                 precision=lax.Precision.HIGHEST, preferred_element_type=F32)
    ct_re = spread(ct_re_ref)
    ct_im = spread(ct_im_ref)
    resp = []
    for d in range(2):
        step = jnp.exp(lstep_ref[0, d, 0])
        lr_c, li_c = lamc_re_ref[0, d, 0] * step, lamc_im_ref[0, d, 0] * step
        lr_r, li_r = lamr_re_ref[0, d, 0], lamr_im_ref[0, d, 0]
        lbc_re, lbc_im = jnp.exp(lr_c) * jnp.cos(li_c), jnp.exp(lr_c) * jnp.sin(li_c)
        mag = jnp.exp(lr_r * step)
        lb_re, lb_im = mag * jnp.cos(li_r * step), mag * jnp.sin(li_r * step)

        pr, pi = _cpow(lbc_re, lbc_im, kk if d == 0 else (C - 1) - kk, nbits)
        pr1, pi1 = pr * lbc_re - pi * lbc_im, pr * lbc_im + pi * lbc_re
        cpr, cpi = ct_re * pr - ct_im * pi, ct_re * pi + ct_im * pr
        cpr1, cpi1 = ct_re * pr1 - ct_im * pi1, ct_re * pi1 + ct_im * pr1

        nr, ni = lb_re - 1.0, lb_im
        den = lr_r * lr_r + li_r * li_r
        kap_re = (nr * lr_r + ni * li_r) / den
        kap_im = (ni * lr_r - nr * li_r) / den
        bb_re = kap_re * bt_re_ref[0, 0] - kap_im * bt_im_ref[0, 0]
        bb_im = kap_re * bt_im_ref[0, 0] + kap_im * bt_re_ref[0, 0]

        resp.append(jnp.dot(jnp.where(left16, bb_re, -bb_im), jnp.where(top, cpr, cpi),
                            precision=lax.Precision.HIGHEST, preferred_element_type=F32))
        m_ref[0, d, 0] = jnp.where(top, cpr1, -cpi1).astype(BF16)

        pr, pi = _cpow(lb_re, lb_im, (C - 1) - srow if d == 0 else srow, nbits)
        bt_r = jnp.concatenate([bb_re] * C, axis=0)
        bt_i = jnp.concatenate([bb_im] * C, axis=0)
        n_re, n_im = pr * bt_r - pi * bt_i, pr * bt_i + pi * bt_r
        n_ref[0, d, 0] = jnp.concatenate([jnp.where(left, n_re, n_im), jnp.where(left, n_im, n_re)],
                                         axis=1).astype(BF16)

        ar, ai = lb_re, lb_im
        for _ in range(nbits):
            ar, ai = ar * ar - ai * ai, 2.0 * ar * ai
        a2 = jnp.where(left1, -ai, ai)
        a_ref[0, d, 0] = jnp.concatenate([jnp.concatenate([ar, ar], axis=1),
                                          jnp.concatenate([a2, -a2], axis=1)], axis=0)

    rf, rb = resp
    for s in range(C):
        nf = S5_GROUP_CH * s
        blk = jnp.where(lane >= nf, pltpu.roll(rf, nf, 1) if nf else rf, 0.0)
        nb = S5_GROUP_CH * (C - 1 - s)
        blk = blk + jnp.where(lane < S5_IN - nb, pltpu.roll(rb, S5_IN - nb, 1) if nb else rb, 0.0)
        t_ref[0, 0, S5_GROUP_CH * s:S5_GROUP_CH * (s + 1), :] = blk.astype(BF16)


def _s5_prep(lam_re, lam_im, log_step, b_re, b_im, c_re, c_im):
    G = S5_GROUPS
    dup = lambda a: jnp.concatenate([a, a], axis=-1)
    lamc = [dup(a).reshape(DEPTH, 2, G, S5_ST, 1) for a in (lam_re, lam_im)]
    lamr = [dup(a).reshape(DEPTH, 2, G, 1, S5_ST) for a in (lam_re, lam_im)]
    lstep = log_step.reshape(DEPTH, 2, G, 1, 1)
    bt = [dup(jnp.swapaxes(a, 2, 3)) for a in (b_re, b_im)]
    ct = [jnp.swapaxes(a, 2, 3) for a in (c_re, c_im)]
    dspec = lambda r, c: pl.BlockSpec((1, 2, 1, r, c), lambda l, g: (l, 0, g, 0, 0))
    gspec = lambda r, c: pl.BlockSpec((1, 1, r, c), lambda l, g: (l, g, 0, 0))
    t, m, n, a = pl.pallas_call(
        _s5_prep_kernel,
        grid=(DEPTH, G),
        in_specs=[dspec(S5_ST, 1), dspec(S5_ST, 1), dspec(1, S5_ST), dspec(1, S5_ST), dspec(1, 1),
                  gspec(S5_GROUP_CH, S5_ST), gspec(S5_GROUP_CH, S5_ST),
                  gspec(S5_STATE, S5_GROUP_CH), gspec(S5_STATE, S5_GROUP_CH)],
        out_specs=[gspec(S5_IN, S5_IN), dspec(S5_ST, S5_IN), dspec(S5_IN, S5_ST2), dspec(2, S5_ST2)],
        out_shape=[jax.ShapeDtypeStruct((DEPTH, G, S5_IN, S5_IN), BF16),
                   jax.ShapeDtypeStruct((DEPTH, 2, G, S5_ST, S5_IN), BF16),
                   jax.ShapeDtypeStruct((DEPTH, 2, G, S5_IN, S5_ST2), BF16),
                   jax.ShapeDtypeStruct((DEPTH, 2, G, 2, S5_ST2), F32)],
        compiler_params=_cparams(("arbitrary", "arbitrary")),
        name="s5_prep",
    )(*lamc, *lamr, lstep, *bt, *ct)
    return t, m, n, jnp.transpose(a, (0, 1, 3, 2, 4)).reshape(DEPTH, 2, 2, G * S5_ST2)


def _s5_kernel(nseg, nchunks, zr_ref, t_ref, m_ref, n_ref, a_ref, x0_ref, d_ref,
               y_ref, xfin_ref, v_ref, xp_ref):
    W = S5_GB * S5_ST2
    for gl in range(S5_GB):
        u = zr_ref[0, :, gl * S5_IN:(gl + 1) * S5_IN].astype(BF16)
        for d in range(2):
            v_ref[d, :, gl * S5_ST2:(gl + 1) * S5_ST2] = jnp.dot(
                u, n_ref[d, gl], preferred_element_type=F32)

    def halves(x, which):
        return [x[:, g * S5_ST2 + h * S5_ST:g * S5_ST2 + (h + 1) * S5_ST]
                for g in range(S5_GB) for h in which]

    per = SUBLANES // TB
    steps = nchunks // per

    for d in range(2):
        a = a_ref[d, 0:1, :]
        a2 = a_ref[d, 1:2, :]

        def advance(x, v):
            swapped = jnp.concatenate(halves(x, (1, 0)), axis=1)
            return a * x + a2 * swapped + v

        for seg in range(nseg):
            def step(i, x):
                si = i if d == 0 else steps - 1 - i
                rows = pl.ds(pl.multiple_of(seg * nchunks * TB + si * SUBLANES, SUBLANES), SUBLANES)
                v = v_ref[d, rows, :]
                order = range(per) if d == 0 else range(per - 1, -1, -1)
                entering = [None] * per
                for j in order:
                    entering[j] = jnp.concatenate(halves(x, (0,)), axis=1)
                    x = advance(x, v[j * TB:(j + 1) * TB])
                xp_ref[d, rows, :] = jnp.concatenate(entering, axis=0)
                return x

            x0 = jnp.zeros((TB, W), F32) if x0_ref is None else x0_ref[d, seg * TB:(seg + 1) * TB, :]
            x = lax.fori_loop(0, steps, step, x0)
            if xfin_ref is not None:
                xfin_ref[d, seg * TB:(seg + 1) * TB, :] = jnp.concatenate(halves(x, (0,)), axis=1)

    for gl in range(S5_GB):
        cols = slice(gl * S5_IN, (gl + 1) * S5_IN)
        u = zr_ref[0, :, cols]
        y = jnp.dot(u.astype(BF16), t_ref[gl], preferred_element_type=F32)
        for d in range(2):
            y = y + jnp.dot(xp_ref[d, :, gl * S5_ST:(gl + 1) * S5_ST].astype(BF16), m_ref[d, gl],
                            preferred_element_type=F32)
        y_ref[0, :, cols] = jax.nn.gelu(y + d_ref[0, :, cols] * u)


def _s5(p, l, zr, t, m, n, a, dt, x0=None, want_final=False):
    G = S5_GROUPS
    nseg = p.nb // TB
    row = pl.BlockSpec((1, R_PASS, S5_ROW), lambda j: (j, 0, 0))
    dsp = lambda r, c: pl.BlockSpec((None, 2, S5_GB, r, c), lambda j: (l, 0, j, 0, 0))
    lsp = lambda r, w: pl.BlockSpec((2, r, S5_GB * w), lambda j: (0, 0, j))
    in_specs = [row, pl.BlockSpec((None, S5_GB, S5_IN, S5_IN), lambda j: (l, j, 0, 0)),
                dsp(S5_ST, S5_IN), dsp(S5_IN, S5_ST2),
                pl.BlockSpec((None, 2, 2, S5_GB * S5_ST2), lambda j: (l, 0, 0, j)),
                pl.BlockSpec((None, 1, 1, S5_ROW), lambda j: (l, j, 0, 0))]
    args = [zr, t, m, n, a, dt]
    if x0 is not None:
        in_specs.append(lsp(p.nb, S5_ST2))
        args.append(x0)
    out_specs = [row]
    out_shape = [jax.ShapeDtypeStruct((S5_NB, R_PASS, S5_ROW), F32)]
    if want_final:
        out_specs.append(lsp(p.nb, S5_ST))
        out_shape.append(jax.ShapeDtypeStruct((2, p.nb, G * S5_ST), F32))

    def body(zr_ref, t_ref, m_ref, n_ref, a_ref, d_ref, *rest):
        rest = list(rest)
        x0_ref = rest.pop(0) if x0 is not None else None
        y_ref = rest.pop(0)
        xfin_ref = rest.pop(0) if want_final else None
        _s5_kernel(nseg, p.seq // S5_CHUNK, zr_ref, t_ref, m_ref, n_ref, a_ref, x0_ref, d_ref,
                   y_ref, xfin_ref, *rest)

    return pl.pallas_call(
        body,
        grid=(S5_NB,),
        in_specs=in_specs,
        out_specs=out_specs,
        out_shape=out_shape,
        scratch_shapes=[pltpu.VMEM((2, R_PASS, S5_GB * S5_ST2), F32),
                        pltpu.VMEM((2, R_PASS, S5_GB * S5_ST), F32)],
        compiler_params=_cparams(("arbitrary",)),
        name="s5_scan_%d" % p.seq,
    )(*args)


def _mix_kernel(x_ref, mod_ref, fo_ref, mo_ref, ys_ref, wglu_ref, wo_ref, nw_ref, x1_ref, xn_ref, scr_ref):
    mod = mod_ref[...]
    g1 = mod[:, :, 2 * D_MODEL:3 * D_MODEL]
    flat = lambda ref: ref[...].reshape(TB * TT, ref.shape[-1])
    o_m, o_s = FOURIER_W, FOURIER_W + HEADS_W
    mix = (jnp.dot(flat(fo_ref), wo_ref[:o_m, :], preferred_element_type=F32)
           + jnp.dot(flat(mo_ref), wo_ref[o_m:o_s, :], preferred_element_type=F32))
    blocks = []
    for bl in range(S5_NB):
        by_group = [ys_ref[bl, :, gl * S5_IN:(gl + 1) * S5_IN] for gl in range(S5_GB)]
        for t in range(S5_CHUNK):
            lo = t * S5_GROUP_CH
            scr_ref[bl, pl.ds(t, TROWS, stride=S5_CHUNK), :] = jnp.concatenate(
                [y[:, lo:lo + S5_GROUP_CH] for y in by_group], axis=1)
        slabs = [scr_ref[bl, _slab(b, c)[1], :] for b in range(TB) for c in range(TT // S5_CHUNK)]
        blocks.append(jnp.concatenate(slabs, axis=0))
    y = jnp.concatenate(blocks, axis=1).astype(BF16)
    gg = jnp.dot(y, wglu_ref[...], preferred_element_type=F32)
    s_out = (gg[:, :S5_W] * jax.nn.sigmoid(gg[:, S5_W:])).astype(BF16)
    mix = mix + jnp.dot(s_out, wo_ref[o_s:, :], preferred_element_type=F32)
    x1 = x_ref[...] + g1 * mix.reshape(TB, TT, D_MODEL)
    x1_ref[...] = x1
    xn_ref[...] = _mod_norm(x1, mod, nw_ref[...], 3).reshape(TB, TT, D_MODEL)


def _mix(p, l, x, mods, fo, mo, ys, wglu, wo, nw):
    return pl.pallas_call(
        _mix_kernel,
        grid=(p.nb // TB, p.seq // TT),
        in_specs=[_tile_spec(D_MODEL), _tile_mod_spec(p, l),
                  _tile_spec(FOURIER_W), _tile_spec(HEADS_W), _chunk_rows_spec(p),
                  _layer(wglu, l), _layer(wo, l), _layer(nw, l)],
        out_specs=[_tile_spec(D_MODEL), _tile_spec(D_MODEL)],
        out_shape=[jax.ShapeDtypeStruct((p.nb, p.seq, D_MODEL), F32),
                   jax.ShapeDtypeStruct((p.nb, p.seq, D_MODEL), BF16)],
        scratch_shapes=[pltpu.VMEM((S5_NB, TB * TT, LANES), F32)],
        compiler_params=_cparams(("arbitrary", "arbitrary")),
        name="mix_out_%d" % p.seq,
    )(x, mods, fo, mo, ys, wglu, wo, nw)


def _ffn_kernel(final, xn_ref, x1_ref, mod_ref, wg_ref, wu_ref, wd_ref, nf_ref, o_ref):
    xn = xn_ref[...]
    ff = None
    for j in range(D_FF // TF):
        cols = slice(j * TF, (j + 1) * TF)
        a = jnp.dot(xn, wg_ref[:, cols], preferred_element_type=F32)
        u = jnp.dot(xn, wu_ref[:, cols], preferred_element_type=F32)
        h = (a * jax.nn.sigmoid(a) * u).astype(BF16)
        part = jnp.dot(h, wd_ref[cols, :], preferred_element_type=F32)
        ff = part if ff is None else ff + part
    g2 = mod_ref[0][:, 5 * D_MODEL:6 * D_MODEL]
    x2 = x1_ref[...] + g2 * ff
    if final:
        x2 = x2 * lax.rsqrt(jnp.mean(x2 * x2, axis=-1, keepdims=True) + EPS) * nf_ref[...]
    o_ref[...] = x2


def _ffn(p, l, final, xn, x1, mods, wg, wu, wd, nf):
    tok = pl.BlockSpec((TM, D_MODEL), lambda i: (i, 0))
    mod_row = (lambda i: p.mod_first + i // (p.seq // TM)) if p.mod_each else (lambda i: p.mod_first)
    resident = lambda a: pl.BlockSpec((None,) + a.shape[1:], lambda i: (l, 0, 0), pipeline_mode=pl.Buffered(1))
    return pl.pallas_call(
        functools.partial(_ffn_kernel, final),
        grid=(T_PASS // TM,),
        in_specs=[tok, tok,
                  pl.BlockSpec((None, 1, 1, 6 * D_MODEL), lambda i: (l, mod_row(i), 0, 0)),
                  resident(wg), resident(wu), resident(wd),
                  pl.BlockSpec((1, D_MODEL), lambda i: (0, 0))],
        out_specs=tok,
        out_shape=jax.ShapeDtypeStruct((T_PASS, D_MODEL), F32),
        compiler_params=_cparams(("arbitrary",)),
        name="ffn_%d" % p.seq,
    )(xn, x1, mods, wg, wu, wd, nf)


def _pad_heads(a, axis):
    shape = a.shape[:axis] + (MLSTM_HEADS, MLSTM_DH) + a.shape[axis + 1:]
    pad = [(0, 0)] * (a.ndim + 1)
    pad[axis + 1] = (0, HEAD_PAD - MLSTM_DH)
    return jnp.pad(a.reshape(shape), pad).reshape(a.shape[:axis] + (HEADS_W,) + a.shape[axis + 1:])


def _mlstm_state_in(c, n):
    cn = jnp.concatenate([c, n[..., None]], axis=-1)
    return jnp.pad(cn, ((0, 0),) * (c.ndim - 2) + ((0, HEAD_PAD - MLSTM_DH), (0, HEAD_PAD - MLSTM_DH - 1)))


def kernel(x_prompt, x_sample, state_mlstm_C, state_mlstm_n, state_mlstm_m, state_s5_re, state_s5_im,
           c, c_ctx, w_ada, b_ada, norm1_w, norm2_w, w_in, b_gates, w_fourier, mlstm_norm_w,
           s5_lambda_re, s5_lambda_im, s5_log_step, s5_b_re, s5_b_im, s5_c_re, s5_c_im, s5_d,
           w_glu, w_out, w_gate, w_up, w_down, norm_f):
    xs = {PROMPT: x_prompt, SAMPLE: x_sample}
    cc = jnp.concatenate([c, c_ctx[None], jnp.zeros((N_MODS - 1 - DEC_BATCH, D_MODEL), F32)], axis=0)
    mods = _ada(cc, w_ada, b_ada).reshape(DEPTH, N_MODS, 1, 6 * D_MODEL)
    cdsd, cs, ab = (jnp.asarray(a.astype(np.float32)).astype(BF16) for a in _dft_consts())
    s5_t, s5_m, s5_n, s5_a = _s5_prep(s5_lambda_re, s5_lambda_im, s5_log_step, s5_b_re, s5_b_im,
                                      s5_c_re, s5_c_im)

    o_q = FOURIER_W
    o_g = o_q + 3 * MLSTM_W
    o_o = o_g + N_GATES
    o_u = o_o + MLSTM_W
    heads = lambda o: _pad_heads(w_in[:, :, o:o + MLSTM_W], 2)
    w_cat = jnp.concatenate([heads(o_q), heads(o_q + MLSTM_W), heads(o_o), w_in[:, :, :FOURIER_W],
                             w_in[:, :, o_u:]], axis=2).astype(BF16)
    wv_t = jnp.swapaxes(heads(o_q + 2 * MLSTM_W), 1, 2).astype(BF16)
    gate_perm = np.arange(N_GATES).reshape(2, 2, MLSTM_HEADS).transpose(1, 0, 2).reshape(-1)
    wg_t = jnp.swapaxes(w_in[:, :, o_g:o_o], 1, 2)[:, gate_perm].astype(BF16)
    bg = b_gates[:, gate_perm, None]
    wf = w_fourier.astype(BF16)
    nw = _pad_heads(mlstm_norm_w, 1).reshape(DEPTH, MLSTM_HEADS, HEAD_PAD, 1)
    dt = jnp.tile(s5_d[:, :, None, :], (1, 1, S5_CHUNK, 1)).reshape(DEPTH, S5_NB, 1, S5_ROW)
    wo_m = jnp.pad(w_out[:, FOURIER_W:FOURIER_W + MLSTM_W].reshape(DEPTH, MLSTM_HEADS, MLSTM_DH, D_MODEL),
                   ((0, 0), (0, 0), (0, HEAD_PAD - MLSTM_DH), (0, 0))).reshape(DEPTH, HEADS_W, D_MODEL)
    wo = jnp.concatenate([w_out[:, :FOURIER_W], wo_m, w_out[:, FOURIER_W + MLSTM_W:]], axis=1).astype(BF16)
    wglu = w_glu.astype(BF16)
    wg, wu, wd = w_gate.astype(BF16), w_up.astype(BF16), w_down.astype(BF16)
    n1, n2 = norm1_w[:, None, :], norm2_w[:, None, :]

    m0 = jnp.swapaxes(state_mlstm_m, 0, 1)
    c0 = jnp.swapaxes(_mlstm_state_in(state_mlstm_C, state_mlstm_n), 0, 1)
    x0 = jnp.concatenate([state_s5_re, state_s5_im, state_s5_im, state_s5_re], axis=-1)
    x0 = jnp.transpose(x0, (1, 2, 0, 3, 4)).reshape(DEPTH, 2, DEC_BATCH, S5_GROUPS * S5_ST2)

    finals = []
    for l in range(DEPTH):
        for p in (PROMPT, SAMPLE):
            x = xs[p]
            zq, zk, zo, zf, zu, vt, gt = _in_proj(p, l, x, mods, n1, w_cat, wv_t, wg_t, bg)
            pr = _gate_prep(p, gt)
            if p is PROMPT:
                fo = _fourier_prompt(l, zf, cdsd, cs, wf)
                mo, *fin = _mlstm(p, l, zq, zk, vt, zo, pr, nw, want_final=True)
                ys, xfin = _s5(p, l, zu, s5_t, s5_m, s5_n, s5_a, dt, want_final=True)
                finals.append(fin + [xfin])
            else:
                fo = _fourier_sample(l, zf, cdsd, ab, wf)
                mo, = _mlstm(p, l, zq, zk, vt, zo, pr, nw, init=(m0[l], c0[l]))
                ys, = _s5(p, l, zu, s5_t, s5_m, s5_n, s5_a, dt, x0=x0[l])
            x1, xn2 = _mix(p, l, x, mods, fo, mo, ys, wglu, wo, n2)
            x2 = _ffn(p, l, l == DEPTH - 1, xn2.reshape(T_PASS, D_MODEL), x1.reshape(T_PASS, D_MODEL),
                      mods, wg, wu, wd, norm_f[None])
            xs[p] = x2.reshape(p.nb, p.seq, D_MODEL)

    cfin, nfin, mfin, xfin = (jnp.stack(parts, axis=1) for parts in zip(*finals))
    xfin = xfin.reshape(2, DEPTH, BATCH, S5_GROUPS, 2, S5_STATE)
    new_re, new_im = (jnp.transpose(xfin[:, :, :, :, i], (2, 1, 0, 3, 4)) for i in range(2))
    return (xs[PROMPT], xs[SAMPLE], cfin, nfin[:, :, :, :, 0], mfin[:, :, :, :, 0, 0], new_re, new_im)
```

```python
import collections
import functools
import math

import numpy as np
import jax
import jax.numpy as jnp
from jax import lax
from jax.experimental import pallas as pl
from jax.experimental.pallas import tpu as pltpu

F32 = jnp.float32
BF16 = jnp.bfloat16

D_MODEL = 1024
BATCH = 32
SEQ = 256
DEPTH = 2
DEC_BATCH = 4
DEC_SEQ = 2048
GRID_W = 64
FOURIER_W = 256
FOURIER_DH = 64
MLSTM_W = 384
MLSTM_HEADS = 4
MLSTM_DH = 96
S5_W = 384
S5_GROUP_CH = 16
S5_GROUPS = 24
S5_STATE = 64
N_GATES = 16
D_FF = 2816
EPS = 1e-6

LANES = 128
SUBLANES = 8
VMEM_LIMIT = 56 * 1024 * 1024

HEAD_PAD = LANES
HEADS_W = MLSTM_HEADS * HEAD_PAD
N_AUG = MLSTM_DH
Z_W = 3 * HEADS_W + FOURIER_W + S5_W
MLSTM_CHUNK = 256
S5_CHUNK = 16
S5_IN = S5_CHUNK * S5_GROUP_CH
S5_ST = 2 * S5_STATE
S5_ST2 = 2 * S5_ST
S5_GB = LANES // S5_GROUP_CH
S5_NB = S5_W // LANES
S5_ROW = S5_GB * S5_IN
TB = 4
TT = 128
TROWS = TB * TT // S5_CHUNK
TM = 512
TF = D_FF // 2
HPS = 4
N_MODS = 8
NEG = -1e30

Pass = collections.namedtuple("Pass", "nb seq mod_first mod_each")
PROMPT = Pass(BATCH, SEQ, DEC_BATCH, False)
SAMPLE = Pass(DEC_BATCH, DEC_SEQ, 0, True)
T_PASS = BATCH * SEQ
assert T_PASS == DEC_BATCH * DEC_SEQ
R_PASS = T_PASS // S5_CHUNK

_NT = (((1,), (1,)), ((), ()))
_TN = (((0,), (0,)), ((), ()))


def _cparams(sem):
    return pltpu.CompilerParams(dimension_semantics=sem, vmem_limit_bytes=VMEM_LIMIT)


def _full(a):
    return pl.BlockSpec(a.shape, lambda *_: (0,) * a.ndim)


def _layer(a, l):
    return pl.BlockSpec((None,) + a.shape[1:], lambda *_: (l,) + (0,) * (a.ndim - 1))


def _tile_mod_spec(p, l):
    if p.mod_each:
        return pl.BlockSpec((None, TB, 1, 6 * D_MODEL), lambda j, k: (l, p.mod_first // TB + j, 0, 0))
    return pl.BlockSpec((None, 1, 1, 6 * D_MODEL), lambda j, k: (l, p.mod_first, 0, 0))


def _log_sigmoid(x):
    return jnp.minimum(x, 0.0) - jnp.log1p(jnp.exp(-jnp.abs(x)))


def _tile_spec(w):
    return pl.BlockSpec((TB, TT, w), lambda j, k: (j, k, 0))


def _chunk_rows_spec(p):
    per_group = p.seq // TT
    return pl.BlockSpec((S5_NB, TROWS, S5_ROW), lambda j, k: (0, j * per_group + k, 0))


def _slab(b, c):
    tok = slice(b * TT + c * S5_CHUNK, b * TT + (c + 1) * S5_CHUNK)
    chk = slice((c * TB + b) * S5_CHUNK, (c * TB + b + 1) * S5_CHUNK)
    return tok, chk


def _ada_kernel(c_ref, w_ref, b_ref, o_ref):
    a = c_ref[...]
    a = (a * jax.nn.sigmoid(a)).astype(BF16)
    o_ref[0] = jnp.dot(a, w_ref[0].astype(BF16), preferred_element_type=F32) + b_ref[0]


def _ada(cc, w_ada, b_ada):
    tn = 512
    return pl.pallas_call(
        _ada_kernel,
        grid=(DEPTH, 6 * D_MODEL // tn),
        in_specs=[pl.BlockSpec((N_MODS, D_MODEL), lambda l, j: (0, 0)),
                  pl.BlockSpec((1, D_MODEL, tn), lambda l, j: (l, 0, j)),
                  pl.BlockSpec((1, 1, tn), lambda l, j: (l, 0, j))],
        out_specs=pl.BlockSpec((1, N_MODS, tn), lambda l, j: (l, 0, j)),
        out_shape=jax.ShapeDtypeStruct((DEPTH, N_MODS, 6 * D_MODEL), F32),
        compiler_params=_cparams(("arbitrary", "arbitrary")),
        name="ada_mod",
    )(cc, w_ada, b_ada.reshape(DEPTH, 1, 6 * D_MODEL))


def _mod_norm(x3, mod, nw, first):
    sh = mod[:, :, first * D_MODEL:(first + 1) * D_MODEL]
    sc = mod[:, :, (first + 1) * D_MODEL:(first + 2) * D_MODEL]
    y = x3 * lax.rsqrt(jnp.mean(x3 * x3, axis=-1, keepdims=True) + EPS) * nw
    return (y * (1.0 + sc) + sh).reshape(TB * TT, D_MODEL).astype(BF16)


def _in_kernel(x_ref, mod_ref, nw_ref, w_ref, wv_ref, wg_ref, bg_ref,
               zq_ref, zk_ref, zo_ref, zf_ref, zu_ref, vt_ref, gt_ref, scr_ref):
    xn = _mod_norm(x_ref[...], mod_ref[...], nw_ref[...], 0)
    w_main = Z_W - S5_W
    zu = jnp.dot(xn, w_ref[:, w_main:], preferred_element_type=F32)
    for bl in range(S5_NB):
        zb = zu[:, bl * LANES:(bl + 1) * LANES]
        for b in range(TB):
            for c in range(TT // S5_CHUNK):
                tok, chk = _slab(b, c)
                scr_ref[bl, chk, :] = zb[tok]
        by_token = [scr_ref[bl, pl.ds(s, TROWS, stride=S5_CHUNK), :] for s in range(S5_CHUNK)]
        for gl in range(S5_GB):
            lo = gl * S5_GROUP_CH
            zu_ref[bl, :, gl * S5_IN:(gl + 1) * S5_IN] = jnp.concatenate(
                [x[:, lo:lo + S5_GROUP_CH] for x in by_token], axis=1)
    z = jnp.dot(xn, w_ref[:, :w_main], preferred_element_type=F32)
    o = 0
    for ref, w in ((zq_ref, HEADS_W), (zk_ref, HEADS_W), (zo_ref, HEADS_W), (zf_ref, FOURIER_W)):
        ref[...] = z[:, o:o + w].astype(BF16).reshape(TB, TT, w)
        o += w
    vt = lax.dot_general(wv_ref[...], xn, _NT, preferred_element_type=F32).astype(BF16)
    gt = lax.dot_general(wg_ref[...], xn, _NT, preferred_element_type=F32) + bg_ref[...]
    for b in range(TB):
        vt_ref[b] = vt[:, b * TT:(b + 1) * TT]
        gt_ref[b] = gt[:, b * TT:(b + 1) * TT]


def _in_proj(p, l, x, mods, nw, w, wv_t, wg_t, bg):
    chan = lambda c_: pl.BlockSpec((TB, c_, TT), lambda j, k: (j, 0, k))
    outs = [HEADS_W] * 3 + [FOURIER_W]
    return pl.pallas_call(
        _in_kernel,
        grid=(p.nb // TB, p.seq // TT),
        in_specs=[_tile_spec(D_MODEL), _tile_mod_spec(p, l),
                  _layer(nw, l), _layer(w, l), _layer(wv_t, l), _layer(wg_t, l), _layer(bg, l)],
        out_specs=[_tile_spec(w_) for w_ in outs] + [_chunk_rows_spec(p), chan(HEADS_W), chan(N_GATES)],
        out_shape=[jax.ShapeDtypeStruct((p.nb, p.seq, w_), BF16) for w_ in outs]
        + [jax.ShapeDtypeStruct((S5_NB, R_PASS, S5_ROW), F32),
           jax.ShapeDtypeStruct((p.nb, HEADS_W, p.seq), BF16),
           jax.ShapeDtypeStruct((p.nb, N_GATES, p.seq), F32)],
        scratch_shapes=[pltpu.VMEM((S5_NB, TB * TT, LANES), F32)],
        compiler_params=_cparams(("arbitrary", "arbitrary")),
        name="in_proj_%d" % p.seq,
    )(x, mods, nw, w, wv_t, wg_t, bg)


def _dft_consts():
    d = np.arange(FOURIER_DH)
    phi = 2.0 * np.pi * ((d[:, None] * d[None, :]) % FOURIER_DH) / FOURIER_DH
    eye = np.eye(FOURIER_W // FOURIER_DH)
    cd = np.kron(eye, np.cos(phi)) / math.sqrt(FOURIER_DH)
    sd = np.kron(eye, np.sin(phi)) / math.sqrt(FOURIER_DH)
    s = np.arange(SEQ)
    th = 2.0 * np.pi * ((s[:, None] * s[None, :]) % SEQ) / SEQ
    rows = DEC_SEQ // GRID_W
    pos = np.arange(DEC_SEQ)
    r, c = pos // GRID_W, pos % GRID_W
    ph = ((r[:, None] * r[None, :]) * (GRID_W // rows) + c[:, None] * c[None, :]) % GRID_W
    th2 = 2.0 * np.pi * ph / GRID_W
    return (np.concatenate([cd, sd], axis=1),
            np.concatenate([np.cos(th), -np.sin(th)], axis=1) / math.sqrt(SEQ),
            np.concatenate([np.cos(th2), -np.sin(th2)], axis=1) / math.sqrt(DEC_SEQ))


def _fourier_prompt_kernel(nb, zf_ref, cdsd_ref, cs_ref, wf_ref, o_ref):
    t = jnp.dot(zf_ref[...].reshape(nb * SEQ, FOURIER_W), cdsd_ref[...],
                preferred_element_type=F32).astype(BF16)
    fs = []
    for b in range(nb):
        tb = t[b * SEQ:(b + 1) * SEQ]
        st = jnp.concatenate([tb[:, :FOURIER_W], tb[:, FOURIER_W:]], axis=0)
        fs.append(jnp.dot(cs_ref[...], st, preferred_element_type=F32))
    for b in range(nb):
        o_ref[b] = jnp.dot(fs[b].astype(BF16), wf_ref[...], preferred_element_type=F32).astype(BF16)


def _fourier_prompt(l, zf, cdsd, cs, wf):
    nb = 4
    blk = pl.BlockSpec((nb, SEQ, FOURIER_W), lambda i: (i, 0, 0))
    return pl.pallas_call(
        functools.partial(_fourier_prompt_kernel, nb),
        grid=(BATCH // nb,),
        in_specs=[blk, _full(cdsd), _full(cs), _layer(wf, l)],
        out_specs=blk,
        out_shape=jax.ShapeDtypeStruct((BATCH, SEQ, FOURIER_W), BF16),
        compiler_params=_cparams(("arbitrary",)),
        name="fourier_prompt",
    )(zf, cdsd, cs, wf)


def _fourier_sample_kernel(zf_ref, cdsd_ref, ab_ref, wf_ref, o_ref, tt_ref):
    @pl.when(pl.program_id(0) == 0)
    def _():
        for b in range(DEC_BATCH):
            t = jnp.dot(zf_ref[b], cdsd_ref[...], preferred_element_type=F32).astype(BF16)
            tt_ref[b, 0:DEC_SEQ, :] = t[:, :FOURIER_W]
            tt_ref[b, DEC_SEQ:2 * DEC_SEQ, :] = t[:, FOURIER_W:]

    fs = [jnp.dot(ab_ref[...], tt_ref[b], preferred_element_type=F32) for b in range(DEC_BATCH)]
    for b in range(DEC_BATCH):
        o_ref[b] = jnp.dot(fs[b].astype(BF16), wf_ref[...], preferred_element_type=F32).astype(BF16)


def _fourier_sample(l, zf, cdsd, ab, wf):
    tk = 512
    return pl.pallas_call(
        _fourier_sample_kernel,
        grid=(DEC_SEQ // tk,),
        in_specs=[_full(zf), _full(cdsd), pl.BlockSpec((tk, 2 * DEC_SEQ), lambda i: (i, 0)), _layer(wf, l)],
        out_specs=pl.BlockSpec((DEC_BATCH, tk, FOURIER_W), lambda i: (0, i, 0)),
        out_shape=jax.ShapeDtypeStruct((DEC_BATCH, DEC_SEQ, FOURIER_W), BF16),
        scratch_shapes=[pltpu.VMEM((DEC_BATCH, 2 * DEC_SEQ, FOURIER_W), BF16)],
        compiler_params=_cparams(("arbitrary",)),
        name="fourier_sample",
    )(zf, cdsd, ab, wf)


def _split3(x):
    hi = x.astype(BF16).astype(F32)
    mid = (x - hi).astype(BF16).astype(F32)
    lo = (x - hi - mid).astype(BF16).astype(F32)
    return hi, mid, lo


def _gate_kernel(g_ref, o_ref):
    L = MLSTM_CHUNK
    nrow = N_GATES // 2
    row = lax.broadcasted_iota(jnp.int32, (L, L), 0)
    col = lax.broadcasted_iota(jnp.int32, (L, L), 1)
    tri_pre = jnp.where(row <= col, 1.0, 0.0).astype(BF16)
    tri_suf = jnp.where(row >= col, 1.0, 0.0).astype(BF16)
    is_fwd = lax.broadcasted_iota(jnp.int32, (nrow, L), 0) < MLSTM_HEADS
    lane = lax.broadcasted_iota(jnp.int32, (nrow, L), 1)
    chunks = [(bi, slice(c * L, (c + 1) * L)) for bi in range(g_ref.shape[0]) for c in range(g_ref.shape[2] // L)]
    fold = lambda a: a[0:nrow] + a[nrow:2 * nrow] + a[2 * nrow:]
    bs, rs = [], []
    for bi, cols in chunks:
        lf = _log_sigmoid(g_ref[bi, nrow:, cols])
        parts = jnp.concatenate(_split3(lf), axis=0).astype(BF16)
        pre = jnp.dot(parts, tri_pre, preferred_element_type=F32)
        suf = jnp.dot(parts, tri_suf, preferred_element_type=F32)
        bs.append(jnp.where(is_fwd, fold(pre), fold(suf)))
        rs.append(g_ref[bi, 0:nrow, cols] - bs[-1])
    pms, sms = list(rs), list(rs)
    sh = 1
    while sh < L:
        pms = [jnp.maximum(x, jnp.where(lane >= sh, pltpu.roll(x, sh, 1), NEG)) for x in pms]
        sms = [jnp.maximum(x, jnp.where(lane < L - sh, pltpu.roll(x, L - sh, 1), NEG)) for x in sms]
        sh *= 2
    for (bi, cols), b, r, pm, sm in zip(chunks, bs, rs, pms, sms):
        for q, val in enumerate((b, r, jnp.where(is_fwd, pm, sm))):
            for dh in range(nrow):
                o_ref[bi, dh, q:q + 1, cols] = val[dh:dh + 1]


def _gate_prep(p, gt):
    bb = max(1, DEC_SEQ // p.seq)
    nrow = N_GATES // 2
    return pl.pallas_call(
        _gate_kernel,
        grid=(p.nb // bb,),
        in_specs=[pl.BlockSpec((bb, N_GATES, p.seq), lambda i: (i, 0, 0))],
        out_specs=pl.BlockSpec((bb, nrow, 3, p.seq), lambda i: (i, 0, 0, 0)),
        out_shape=jax.ShapeDtypeStruct((p.nb, nrow, 3, p.seq), F32),
        compiler_params=_cparams(("arbitrary",)),
        name="gate_prep_%d" % p.seq,
    )(gt)


def _mlstm_chunk(q, k, vt, pr, ct, m, fwd, out):
    L = q.shape[0]
    scale = MLSTM_DH ** -0.5
    b, r, cm = pr[0:1], pr[1:2], pr[2:3]
    ones = jnp.ones((3, L), F32)
    zeros = jnp.zeros((SUBLANES - 6, L), F32)
    lhs = jnp.concatenate(_split3(r) + (ones, zeros), axis=0).astype(BF16)
    rhs = jnp.concatenate((ones,) + _split3(-cm) + (zeros,), axis=0).astype(BF16)
    arg = lax.dot_general(lhs, rhs, _TN, preferred_element_type=F32)
    st = lax.dot_general(k, q, _NT, preferred_element_type=F32)
    cq = lax.dot_general(ct.astype(BF16), q, _NT, preferred_element_type=F32)
    last = L - 1 if fwd else 0
    cm_last = cm[:, last:last + 1]
    mx_last = jnp.maximum(m, cm_last)
    vw = (vt.astype(F32) * jnp.exp(r - cm_last)).astype(BF16)
    dct = jnp.dot(vw, k, preferred_element_type=F32)
    yield

    row = lax.broadcasted_iota(jnp.int32, (L, L), 0)
    col = lax.broadcasted_iota(jnp.int32, (L, L), 1)
    e = jnp.where((row <= col) if fwd else (row >= col), jnp.exp(arg), 0.0)
    num = jnp.dot(vt, (st * e).astype(BF16), preferred_element_type=F32)
    ct_new = jnp.exp(m - mx_last) * ct + (jnp.exp(cm_last - mx_last) * scale) * dct
    yield

    mx = jnp.maximum(m, cm)
    num = (jnp.exp(cm - mx) * scale) * num + jnp.exp(m - mx) * cq
    den = num[N_AUG:N_AUG + 1, :]
    h = num * (1.0 / jnp.maximum(jnp.abs(den), jnp.exp(-(b + mx))))
    out += [h, ct_new, b[:, last:last + 1] + mx_last]


def _mlstm_kernel(nc, has_init, want_final, *refs):
    refs = list(refs)
    m0_ref, c0_ref = (refs.pop(0), refs.pop(0)) if has_init else (None, None)
    q_ref, k_ref, vt_ref, zo_ref, prf_ref, prb_ref, nw_ref, o_ref = refs[:8]
    hbuf_ref = refs[-1]
    L = MLSTM_CHUNK
    bi = pl.program_id(0)
    hg = pl.program_id(1)
    vrow = lax.broadcasted_iota(jnp.int32, (HEAD_PAD, L), 0)

    def chunk(hh, ci, pr_ref, ct, m, dr, out):
        rows = pl.ds(pl.multiple_of(ci * L, L), L)
        lanes = slice(hh * HEAD_PAD, (hh + 1) * HEAD_PAD)
        vt = vt_ref[lanes, rows]
        vt = jnp.where(vrow == N_AUG, jnp.ones_like(vt), vt)
        res = []
        yield from _mlstm_chunk(q_ref[rows, lanes], k_ref[rows, lanes], vt, pr_ref[hh, :, rows], ct, m,
                                dr == 0, res)
        hbuf_ref[hh, dr, ci] = res[0]
        out += res[1:]

    def step(i, carry):
        outs = [[] for _ in range(2 * HPS)]
        gens = []
        for hh in range(HPS):
            cf, mf, cb, mb = carry[4 * hh:4 * hh + 4]
            gens += [chunk(hh, i, prf_ref, cf, mf, 0, outs[2 * hh]),
                     chunk(hh, nc - 1 - i, prb_ref, cb, mb, 1, outs[2 * hh + 1])]
        while gens:
            alive = []
            for g in gens:
                try:
                    next(g)
                    alive.append(g)
                except StopIteration:
                    pass
            gens = alive
        return tuple(x for o in outs for x in o)

    carry = []
    for hh in range(HPS):
        for dr in range(2):
            if has_init:
                carry += [c0_ref[0, dr, hh].T, jnp.full((1, 1), m0_ref[bi, dr, hg * HPS + hh], F32)]
            else:
                carry += [jnp.zeros((HEAD_PAD, HEAD_PAD), F32), jnp.zeros((1, 1), F32)]
    carry = step(0, tuple(carry)) if nc == 1 else lax.fori_loop(0, nc, step, tuple(carry))
    if want_final:
        cfin_ref, nfin_ref, mfin_ref = refs[8:11]
        for hh in range(HPS):
            for dr in range(2):
                ct, m = carry[4 * hh + 2 * dr], carry[4 * hh + 2 * dr + 1]
                cfin_ref[0, dr, hh] = ct.T[:MLSTM_DH, :MLSTM_DH]
                nfin_ref[0, dr, hh] = ct[N_AUG:N_AUG + 1, :MLSTM_DH]
                mfin_ref[0, dr, hh] = jnp.broadcast_to(m, (SUBLANES, LANES))

    def finish(ci, _):
        rows = pl.ds(pl.multiple_of(ci * L, L), L)
        for hh in range(HPS):
            lanes = slice(hh * HEAD_PAD, (hh + 1) * HEAD_PAD)
            h = jnp.where(vrow < MLSTM_DH, hbuf_ref[hh, 0, ci] + hbuf_ref[hh, 1, ci], 0.0)
            ms = jnp.sum(h * h, axis=0, keepdims=True) * (1.0 / MLSTM_DH)
            hn = (h * lax.rsqrt(ms + EPS) * nw_ref[hh]).T
            o_ref[rows, lanes] = (hn * jax.nn.sigmoid(zo_ref[rows, lanes].astype(F32))).astype(BF16)
        return 0

    if nc == 1:
        finish(0, 0)
    else:
        lax.fori_loop(0, nc, finish, 0)


def _mlstm(p, l, q, k, vt, zo, pr, nw, init=None, want_final=False):
    nc = p.seq // MLSTM_CHUNK
    tok = pl.BlockSpec((None, p.seq, HPS * HEAD_PAD), lambda b, h: (b, 0, h))
    st = lambda r, c: pl.BlockSpec((1, 2, HPS, r, c), lambda b, h: (b, 0, h, 0, 0))
    prs = lambda dr: pl.BlockSpec((None, HPS, 3, p.seq), lambda b, h: (b, dr * (MLSTM_HEADS // HPS) + h, 0, 0))
    in_specs, args = [], []
    if init is not None:
        in_specs += [pl.BlockSpec(memory_space=pltpu.SMEM), st(HEAD_PAD, HEAD_PAD)]
        args += list(init)
    in_specs += [tok, tok, pl.BlockSpec((None, HPS * HEAD_PAD, p.seq), lambda b, h: (b, h, 0)), tok, prs(0), prs(1),
                 pl.BlockSpec((None, HPS, HEAD_PAD, 1), lambda b, h: (l, h, 0, 0))]
    args += [q, k, vt, zo, pr, pr, nw]
    out_specs = [tok]
    out_shape = [jax.ShapeDtypeStruct((p.nb, p.seq, HEADS_W), BF16)]
    if want_final:
        out_specs += [st(MLSTM_DH, MLSTM_DH), st(1, MLSTM_DH), st(SUBLANES, LANES)]
        out_shape += [jax.ShapeDtypeStruct((p.nb, 2, MLSTM_HEADS, r, c), F32)
                      for r, c in ((MLSTM_DH, MLSTM_DH), (1, MLSTM_DH), (SUBLANES, LANES))]
    return pl.pallas_call(
        functools.partial(_mlstm_kernel, nc, init is not None, want_final),
        grid=(p.nb, MLSTM_HEADS // HPS),
        in_specs=in_specs,
        out_specs=out_specs,
        out_shape=out_shape,
        scratch_shapes=[pltpu.VMEM((HPS, 2, nc, HEAD_PAD, MLSTM_CHUNK), F32)],
        compiler_params=_cparams(("arbitrary", "arbitrary")),
        name="mlstm_%d" % p.seq,
    )(*args)


def _cpow(br, bi, e, nbits):
    pr = pi = None
    for bit in range(nbits):
        sel = ((e >> bit) & 1) == 1
        if pr is None:
            pr, pi = jnp.where(sel, br, 1.0), jnp.where(sel, bi, 0.0)
        else:
            pr, pi = jnp.where(sel, pr * br - pi * bi, pr), jnp.where(sel, pr * bi + pi * br, pi)
        if bit + 1 < nbits:
            br, bi = br * br - bi * bi, 2.0 * br * bi
    return pr, pi


def _s5_prep_kernel(lamc_re_ref, lamc_im_ref, lamr_re_ref, lamr_im_ref, lstep_ref,
                    bt_re_ref, bt_im_ref, ct_re_ref, ct_im_ref,
                    t_ref, m_ref, n_ref, a_ref):
    C = S5_CHUNK
    nbits = (C - 1).bit_length()
    assert C == 1 << nbits
    kk = lax.broadcasted_iota(jnp.int32, (S5_STATE, S5_IN), 1) >> 4
    srow = lax.broadcasted_iota(jnp.int32, (S5_IN, S5_ST), 0) >> 4
    left = lax.broadcasted_iota(jnp.int32, (S5_IN, S5_ST), 1) < S5_STATE
    left16 = lax.broadcasted_iota(jnp.int32, (S5_GROUP_CH, S5_ST), 1) < S5_STATE
    left1 = lax.broadcasted_iota(jnp.int32, (1, S5_ST), 1) < S5_STATE
    lane = lax.broadcasted_iota(jnp.int32, (S5_GROUP_CH, S5_IN), 1)
    sel = jnp.where((lane & (S5_GROUP_CH - 1)) == lax.broadcasted_iota(jnp.int32, (S5_GROUP_CH, S5_IN), 0),
                    1.0, 0.0)
    spread = lambda ref: jnp.dot(ref[0, 0], sel, precision=lax.Precision.HIGHEST, preferred_element_type=F32)
    ct_re = spread(ct_re_ref)
    ct_im = spread(ct_im_ref)
    resp = []
    for d in range(2):
        step = jnp.exp(lstep_ref[0, d, 0])
        lr_c, li_c = lamc_re_ref[0, d, 0] * step, lamc_im_ref[0, d, 0] * step
        lr_r, li_r = lamr_re_ref[0, d, 0], lamr_im_ref[0, d, 0]
        lbc_re, lbc_im = jnp.exp(lr_c) * jnp.cos(li_c), jnp.exp(lr_c) * jnp.sin(li_c)
        mag = jnp.exp(lr_r * step)
        lb_re, lb_im = mag * jnp.cos(li_r * step), mag * jnp.sin(li_r * step)

        pr, pi = _cpow(lbc_re, lbc_im, kk if d == 0 else (C - 1) - kk, nbits)
        pr1, pi1 = pr * lbc_re - pi * lbc_im, pr * lbc_im + pi * lbc_re
        cpr, cpi = ct_re * pr - ct_im * pi, ct_re * pi + ct_im * pr
        cpr1, cpi1 = ct_re * pr1 - ct_im * pi1, ct_re * pi1 + ct_im * pr1

        nr, ni = lb_re - 1.0, lb_im
        den = lr_r * lr_r + li_r * li_r
        kap_re = (nr * lr_r + ni * li_r) / den
        kap_im = (ni * lr_r - nr * li_r) / den
        bb_re = kap_re * bt_re_ref[0, 0] - kap_im * bt_im_ref[0, 0]
        bb_im = kap_re * bt_im_ref[0, 0] + kap_im * bt_re_ref[0, 0]

        resp.append(jnp.dot(jnp.where(left16, bb_re, -bb_im), jnp.concatenate([cpr, cpi], axis=0),
                            precision=lax.Precision.HIGHEST, preferred_element_type=F32))
        m_ref[0, d, 0] = jnp.concatenate([cpr1, -cpi1], axis=0).astype(BF16)

        pr, pi = _cpow(lb_re, lb_im, (C - 1) - srow if d == 0 else srow, nbits)
        bt_r = jnp.concatenate([bb_re] * C, axis=0)
        bt_i = jnp.concatenate([bb_im] * C, axis=0)
        n_re, n_im = pr * bt_r - pi * bt_i, pr * bt_i + pi * bt_r
        n_ref[0, d, 0] = jnp.concatenate([jnp.where(left, n_re, n_im), jnp.where(left, n_im, n_re)],
                                         axis=1).astype(BF16)

        ar, ai = lb_re, lb_im
        for _ in range(nbits):
            ar, ai = ar * ar - ai * ai, 2.0 * ar * ai
        a2 = jnp.where(left1, -ai, ai)
        a_ref[0, d, 0] = jnp.concatenate([jnp.concatenate([ar, ar], axis=1),
                                          jnp.concatenate([a2, -a2], axis=1)], axis=0)

    rf, rb = resp
    for s in range(C):
        nf = S5_GROUP_CH * s
        blk = jnp.where(lane >= nf, pltpu.roll(rf, nf, 1) if nf else rf, 0.0)
        nb = S5_GROUP_CH * (C - 1 - s)
        blk = blk + jnp.where(lane < S5_IN - nb, pltpu.roll(rb, S5_IN - nb, 1) if nb else rb, 0.0)
        t_ref[0, 0, S5_GROUP_CH * s:S5_GROUP_CH * (s + 1), :] = blk.astype(BF16)


def _s5_prep(lam_re, lam_im, log_step, b_re, b_im, c_re, c_im):
    G = S5_GROUPS
    dup = lambda a: jnp.concatenate([a, a], axis=-1)
    lamc = [a.reshape(DEPTH, 2, G, S5_STATE, 1) for a in (lam_re, lam_im)]
    lamr = [dup(a).reshape(DEPTH, 2, G, 1, S5_ST) for a in (lam_re, lam_im)]
    lstep = log_step.reshape(DEPTH, 2, G, 1, 1)
    bt = [dup(jnp.swapaxes(a, 2, 3)) for a in (b_re, b_im)]
    ct = [jnp.swapaxes(a, 2, 3) for a in (c_re, c_im)]
    dspec = lambda r, c: pl.BlockSpec((1, 2, 1, r, c), lambda l, g: (l, 0, g, 0, 0))
    gspec = lambda r, c: pl.BlockSpec((1, 1, r, c), lambda l, g: (l, g, 0, 0))
    t, m, n, a = pl.pallas_call(
        _s5_prep_kernel,
        grid=(DEPTH, G),
        in_specs=[dspec(S5_STATE, 1), dspec(S5_STATE, 1), dspec(1, S5_ST), dspec(1, S5_ST), dspec(1, 1),
                  gspec(S5_GROUP_CH, S5_ST), gspec(S5_GROUP_CH, S5_ST),
                  gspec(S5_STATE, S5_GROUP_CH), gspec(S5_STATE, S5_GROUP_CH)],
        out_specs=[gspec(S5_IN, S5_IN), dspec(S5_ST, S5_IN), dspec(S5_IN, S5_ST2), dspec(2, S5_ST2)],
        out_shape=[jax.ShapeDtypeStruct((DEPTH, G, S5_IN, S5_IN), BF16),
                   jax.ShapeDtypeStruct((DEPTH, 2, G, S5_ST, S5_IN), BF16),
                   jax.ShapeDtypeStruct((DEPTH, 2, G, S5_IN, S5_ST2), BF16),
                   jax.ShapeDtypeStruct((DEPTH, 2, G, 2, S5_ST2), F32)],
        compiler_params=_cparams(("arbitrary", "arbitrary")),
        name="s5_prep",
    )(*lamc, *lamr, lstep, *bt, *ct)
    return t, m, n, jnp.transpose(a, (0, 1, 3, 2, 4)).reshape(DEPTH, 2, 2, G * S5_ST2)


def _s5_kernel(nseg, nchunks, zr_ref, t_ref, m_ref, n_ref, a_ref, x0_ref, d_ref,
               y_ref, xfin_ref, v_ref, xp_ref):
    W = S5_GB * S5_ST2
    for gl in range(S5_GB):
        u = zr_ref[0, :, gl * S5_IN:(gl + 1) * S5_IN].astype(BF16)
        for d in range(2):
            v_ref[d, :, gl * S5_ST2:(gl + 1) * S5_ST2] = jnp.dot(
                u, n_ref[d, gl], preferred_element_type=F32)

    def halves(x, which):
        return [x[:, g * S5_ST2 + h * S5_ST:g * S5_ST2 + (h + 1) * S5_ST]
                for g in range(S5_GB) for h in which]

    per = SUBLANES // TB
    steps = nchunks // per

    for d in range(2):
        a = a_ref[d, 0:1, :]
        a2 = a_ref[d, 1:2, :]

        def advance(x, v):
            swapped = jnp.concatenate(halves(x, (1, 0)), axis=1)
            return a * x + a2 * swapped + v

        for seg in range(nseg):
            def step(i, x):
                si = i if d == 0 else steps - 1 - i
                rows = pl.ds(pl.multiple_of(seg * nchunks * TB + si * SUBLANES, SUBLANES), SUBLANES)
                v = v_ref[d, rows, :]
                order = range(per) if d == 0 else range(per - 1, -1, -1)
                entering = [None] * per
                for j in order:
                    entering[j] = jnp.concatenate(halves(x, (0,)), axis=1)
                    x = advance(x, v[j * TB:(j + 1) * TB])
                xp_ref[d, rows, :] = jnp.concatenate(entering, axis=0)
                return x

            x0 = jnp.zeros((TB, W), F32) if x0_ref is None else x0_ref[d, seg * TB:(seg + 1) * TB, :]
            x = lax.fori_loop(0, steps, step, x0)
            if xfin_ref is not None:
                xfin_ref[d, seg * TB:(seg + 1) * TB, :] = jnp.concatenate(halves(x, (0,)), axis=1)

    for gl in range(S5_GB):
        cols = slice(gl * S5_IN, (gl + 1) * S5_IN)
        u = zr_ref[0, :, cols]
        y = jnp.dot(u.astype(BF16), t_ref[gl], preferred_element_type=F32)
        for d in range(2):
            y = y + jnp.dot(xp_ref[d, :, gl * S5_ST:(gl + 1) * S5_ST].astype(BF16), m_ref[d, gl],
                            preferred_element_type=F32)
        y_ref[0, :, cols] = jax.nn.gelu(y + d_ref[0, :, cols] * u)


def _s5(p, l, zr, t, m, n, a, dt, x0=None, want_final=False):
    G = S5_GROUPS
    nseg = p.nb // TB
    row = pl.BlockSpec((1, R_PASS, S5_ROW), lambda j: (j, 0, 0))
    dsp = lambda r, c: pl.BlockSpec((None, 2, S5_GB, r, c), lambda j: (l, 0, j, 0, 0))
    lsp = lambda r, w: pl.BlockSpec((2, r, S5_GB * w), lambda j: (0, 0, j))
    in_specs = [row, pl.BlockSpec((None, S5_GB, S5_IN, S5_IN), lambda j: (l, j, 0, 0)),
                dsp(S5_ST, S5_IN), dsp(S5_IN, S5_ST2),
                pl.BlockSpec((None, 2, 2, S5_GB * S5_ST2), lambda j: (l, 0, 0, j)),
                pl.BlockSpec((None, 1, 1, S5_ROW), lambda j: (l, j, 0, 0))]
    args = [zr, t, m, n, a, dt]
    if x0 is not None:
        in_specs.append(lsp(p.nb, S5_ST2))
        args.append(x0)
    out_specs = [row]
    out_shape = [jax.ShapeDtypeStruct((S5_NB, R_PASS, S5_ROW), F32)]
    if want_final:
        out_specs.append(lsp(p.nb, S5_ST))
        out_shape.append(jax.ShapeDtypeStruct((2, p.nb, G * S5_ST), F32))

    def body(zr_ref, t_ref, m_ref, n_ref, a_ref, d_ref, *rest):
        rest = list(rest)
        x0_ref = rest.pop(0) if x0 is not None else None
        y_ref = rest.pop(0)
        xfin_ref = rest.pop(0) if want_final else None
        _s5_kernel(nseg, p.seq // S5_CHUNK, zr_ref, t_ref, m_ref, n_ref, a_ref, x0_ref, d_ref,
                   y_ref, xfin_ref, *rest)

    return pl.pallas_call(
        body,
        grid=(S5_NB,),
        in_specs=in_specs,
        out_specs=out_specs,
        out_shape=out_shape,
        scratch_shapes=[pltpu.VMEM((2, R_PASS, S5_GB * S5_ST2), F32),
                        pltpu.VMEM((2, R_PASS, S5_GB * S5_ST), F32)],
        compiler_params=_cparams(("arbitrary",)),
        name="s5_scan_%d" % p.seq,
    )(*args)


def _mix_kernel(x_ref, mod_ref, fo_ref, mo_ref, ys_ref, wglu_ref, wo_ref, nw_ref, x1_ref, xn_ref, scr_ref):
    mod = mod_ref[...]
    g1 = mod[:, :, 2 * D_MODEL:3 * D_MODEL]
    flat = lambda ref: ref[...].reshape(TB * TT, ref.shape[-1])
    o_m, o_s = FOURIER_W, FOURIER_W + HEADS_W
    mix = (jnp.dot(flat(fo_ref), wo_ref[:o_m, :], preferred_element_type=F32)
           + jnp.dot(flat(mo_ref), wo_ref[o_m:o_s, :], preferred_element_type=F32))
    blocks = []
    for bl in range(S5_NB):
        by_group = [ys_ref[bl, :, gl * S5_IN:(gl + 1) * S5_IN] for gl in range(S5_GB)]
        for t in range(S5_CHUNK):
            lo = t * S5_GROUP_CH
            scr_ref[bl, pl.ds(t, TROWS, stride=S5_CHUNK), :] = jnp.concatenate(
                [y[:, lo:lo + S5_GROUP_CH] for y in by_group], axis=1)
        slabs = [scr_ref[bl, _slab(b, c)[1], :] for b in range(TB) for c in range(TT // S5_CHUNK)]
        blocks.append(jnp.concatenate(slabs, axis=0))
    y = jnp.concatenate(blocks, axis=1).astype(BF16)
    gg = jnp.dot(y, wglu_ref[...], preferred_element_type=F32)
    s_out = (gg[:, :S5_W] * jax.nn.sigmoid(gg[:, S5_W:])).astype(BF16)
    mix = mix + jnp.dot(s_out, wo_ref[o_s:, :], preferred_element_type=F32)
    x1 = x_ref[...] + g1 * mix.reshape(TB, TT, D_MODEL)
    x1_ref[...] = x1
    xn_ref[...] = _mod_norm(x1, mod, nw_ref[...], 3).reshape(TB, TT, D_MODEL)


def _mix(p, l, x, mods, fo, mo, ys, wglu, wo, nw):
    return pl.pallas_call(
        _mix_kernel,
        grid=(p.nb // TB, p.seq // TT),
        in_specs=[_tile_spec(D_MODEL), _tile_mod_spec(p, l),
                  _tile_spec(FOURIER_W), _tile_spec(HEADS_W), _chunk_rows_spec(p),
                  _layer(wglu, l), _layer(wo, l), _layer(nw, l)],
        out_specs=[_tile_spec(D_MODEL), _tile_spec(D_MODEL)],
        out_shape=[jax.ShapeDtypeStruct((p.nb, p.seq, D_MODEL), F32),
                   jax.ShapeDtypeStruct((p.nb, p.seq, D_MODEL), BF16)],
        scratch_shapes=[pltpu.VMEM((S5_NB, TB * TT, LANES), F32)],
        compiler_params=_cparams(("arbitrary", "arbitrary")),
        name="mix_out_%d" % p.seq,
    )(x, mods, fo, mo, ys, wglu, wo, nw)


def _ffn_kernel(final, xn_ref, x1_ref, mod_ref, wg_ref, wu_ref, wd_ref, nf_ref, o_ref):
    xn = xn_ref[...]
    ff = None
    for j in range(D_FF // TF):
        cols = slice(j * TF, (j + 1) * TF)
        a = jnp.dot(xn, wg_ref[:, cols], preferred_element_type=F32)
        u = jnp.dot(xn, wu_ref[:, cols], preferred_element_type=F32)
        h = (a * jax.nn.sigmoid(a) * u).astype(BF16)
        part = jnp.dot(h, wd_ref[cols, :], preferred_element_type=F32)
        ff = part if ff is None else ff + part
    g2 = mod_ref[0][:, 5 * D_MODEL:6 * D_MODEL]
    x2 = x1_ref[...] + g2 * ff
    if final:
        x2 = x2 * lax.rsqrt(jnp.mean(x2 * x2, axis=-1, keepdims=True) + EPS) * nf_ref[...]
    o_ref[...] = x2


def _ffn(p, l, final, xn, x1, mods, wg, wu, wd, nf):
    tok = pl.BlockSpec((TM, D_MODEL), lambda i: (i, 0))
    mod_row = (lambda i: p.mod_first + i // (p.seq // TM)) if p.mod_each else (lambda i: p.mod_first)
    resident = lambda a: pl.BlockSpec((None,) + a.shape[1:], lambda i: (l, 0, 0), pipeline_mode=pl.Buffered(1))
    return pl.pallas_call(
        functools.partial(_ffn_kernel, final),
        grid=(T_PASS // TM,),
        in_specs=[tok, tok,
                  pl.BlockSpec((None, 1, 1, 6 * D_MODEL), lambda i: (l, mod_row(i), 0, 0)),
                  resident(wg), resident(wu), resident(wd),
                  pl.BlockSpec((1, D_MODEL), lambda i: (0, 0))],
        out_specs=tok,
        out_shape=jax.ShapeDtypeStruct((T_PASS, D_MODEL), F32),
        compiler_params=_cparams(("arbitrary",)),
        name="ffn_%d" % p.seq,
    )(xn, x1, mods, wg, wu, wd, nf)


def _pad_heads(a, axis):
    shape = a.shape[:axis] + (MLSTM_HEADS, MLSTM_DH) + a.shape[axis + 1:]
    pad = [(0, 0)] * (a.ndim + 1)
    pad[axis + 1] = (0, HEAD_PAD - MLSTM_DH)
    return jnp.pad(a.reshape(shape), pad).reshape(a.shape[:axis] + (HEADS_W,) + a.shape[axis + 1:])


def _mlstm_state_in(c, n):
    cn = jnp.concatenate([c, n[..., None]], axis=-1)
    return jnp.pad(cn, ((0, 0),) * (c.ndim - 2) + ((0, HEAD_PAD - MLSTM_DH), (0, HEAD_PAD - MLSTM_DH - 1)))


def kernel(x_prompt, x_sample, state_mlstm_C, state_mlstm_n, state_mlstm_m, state_s5_re, state_s5_im,
           c, c_ctx, w_ada, b_ada, norm1_w, norm2_w, w_in, b_gates, w_fourier, mlstm_norm_w,
           s5_lambda_re, s5_lambda_im, s5_log_step, s5_b_re, s5_b_im, s5_c_re, s5_c_im, s5_d,
           w_glu, w_out, w_gate, w_up, w_down, norm_f):
    xs = {PROMPT: x_prompt, SAMPLE: x_sample}
    cc = jnp.concatenate([c, c_ctx[None], jnp.zeros((N_MODS - 1 - DEC_BATCH, D_MODEL), F32)], axis=0)
    mods = _ada(cc, w_ada, b_ada).reshape(DEPTH, N_MODS, 1, 6 * D_MODEL)
    cdsd, cs, ab = (jnp.asarray(a.astype(np.float32)).astype(BF16) for a in _dft_consts())
    s5_t, s5_m, s5_n, s5_a = _s5_prep(s5_lambda_re, s5_lambda_im, s5_log_step, s5_b_re, s5_b_im,
                                      s5_c_re, s5_c_im)

    o_q = FOURIER_W
    o_g = o_q + 3 * MLSTM_W
    o_o = o_g + N_GATES
    o_u = o_o + MLSTM_W
    heads = lambda o: _pad_heads(w_in[:, :, o:o + MLSTM_W], 2)
    w_cat = jnp.concatenate([heads(o_q), heads(o_q + MLSTM_W), heads(o_o), w_in[:, :, :FOURIER_W],
                             w_in[:, :, o_u:]], axis=2).astype(BF16)
    wv_t = jnp.swapaxes(heads(o_q + 2 * MLSTM_W), 1, 2).astype(BF16)
    gate_perm = np.arange(N_GATES).reshape(2, 2, MLSTM_HEADS).transpose(1, 0, 2).reshape(-1)
    wg_t = jnp.swapaxes(w_in[:, :, o_g:o_o], 1, 2)[:, gate_perm].astype(BF16)
    bg = b_gates[:, gate_perm, None]
    wf = w_fourier.astype(BF16)
    nw = _pad_heads(mlstm_norm_w, 1).reshape(DEPTH, MLSTM_HEADS, HEAD_PAD, 1)
    dt = jnp.tile(s5_d[:, :, None, :], (1, 1, S5_CHUNK, 1)).reshape(DEPTH, S5_NB, 1, S5_ROW)
    wo_m = jnp.pad(w_out[:, FOURIER_W:FOURIER_W + MLSTM_W].reshape(DEPTH, MLSTM_HEADS, MLSTM_DH, D_MODEL),
                   ((0, 0), (0, 0), (0, HEAD_PAD - MLSTM_DH), (0, 0))).reshape(DEPTH, HEADS_W, D_MODEL)
    wo = jnp.concatenate([w_out[:, :FOURIER_W], wo_m, w_out[:, FOURIER_W + MLSTM_W:]], axis=1).astype(BF16)
    wglu = w_glu.astype(BF16)
    wg, wu, wd = w_gate.astype(BF16), w_up.astype(BF16), w_down.astype(BF16)
    n1, n2 = norm1_w[:, None, :], norm2_w[:, None, :]

    m0 = jnp.swapaxes(state_mlstm_m, 0, 1)
    c0 = jnp.swapaxes(_mlstm_state_in(state_mlstm_C, state_mlstm_n), 0, 1)
    x0 = jnp.concatenate([state_s5_re, state_s5_im, state_s5_im, state_s5_re], axis=-1)
    x0 = jnp.transpose(x0, (1, 2, 0, 3, 4)).reshape(DEPTH, 2, DEC_BATCH, S5_GROUPS * S5_ST2)

    finals = []
    for l in range(DEPTH):
        for p in (PROMPT, SAMPLE):
            x = xs[p]
            zq, zk, zo, zf, zu, vt, gt = _in_proj(p, l, x, mods, n1, w_cat, wv_t, wg_t, bg)
            pr = _gate_prep(p, gt)
            if p is PROMPT:
                fo = _fourier_prompt(l, zf, cdsd, cs, wf)
                mo, *fin = _mlstm(p, l, zq, zk, vt, zo, pr, nw, want_final=True)
                ys, xfin = _s5(p, l, zu, s5_t, s5_m, s5_n, s5_a, dt, want_final=True)
                finals.append(fin + [xfin])
            else:
                fo = _fourier_sample(l, zf, cdsd, ab, wf)
                mo, = _mlstm(p, l, zq, zk, vt, zo, pr, nw, init=(m0[l], c0[l]))
                ys, = _s5(p, l, zu, s5_t, s5_m, s5_n, s5_a, dt, x0=x0[l])
            x1, xn2 = _mix(p, l, x, mods, fo, mo, ys, wglu, wo, n2)
            x2 = _ffn(p, l, l == DEPTH - 1, xn2.reshape(T_PASS, D_MODEL), x1.reshape(T_PASS, D_MODEL),
                      mods, wg, wu, wd, norm_f[None])
            xs[p] = x2.reshape(p.nb, p.seq, D_MODEL)

    cfin, nfin, mfin, xfin = (jnp.stack(parts, axis=1) for parts in zip(*finals))
    xfin = xfin.reshape(2, DEPTH, BATCH, S5_GROUPS, 2, S5_STATE)
    new_re, new_im = (jnp.transpose(xfin[:, :, :, :, i], (2, 1, 0, 3, 4)) for i in range(2))
    return (xs[PROMPT], xs[SAMPLE], cfin, nfin[:, :, :, :, 0], mfin[:, :, :, :, 0, 0], new_re, new_im)
```

```python
import collections
import functools
import math

import numpy as np
import jax
import jax.numpy as jnp
from jax import lax
from jax.experimental import pallas as pl
from jax.experimental.pallas import tpu as pltpu

F32 = jnp.float32
BF16 = jnp.bfloat16

D_MODEL = 1024
BATCH = 32
SEQ = 256
DEPTH = 2
DEC_BATCH = 4
DEC_SEQ = 2048
GRID_W = 64
FOURIER_W = 256
FOURIER_DH = 64
MLSTM_W = 384
MLSTM_HEADS = 4
MLSTM_DH = 96
S5_W = 384
S5_GROUP_CH = 16
S5_GROUPS = 24
S5_STATE = 64
N_GATES = 16
D_FF = 2816
EPS = 1e-6

LANES = 128
SUBLANES = 8
VMEM_LIMIT = 56 * 1024 * 1024
POST_VMEM_LIMIT = 60 * 1024 * 1024

HEAD_PAD = LANES
HEADS_W = MLSTM_HEADS * HEAD_PAD
N_AUG = MLSTM_DH
Z_W = 3 * HEADS_W + FOURIER_W + S5_W
MLSTM_CHUNK = 256
S5_CHUNK = 16
S5_IN = S5_CHUNK * S5_GROUP_CH
S5_ST = 2 * S5_STATE
S5_ST2 = 2 * S5_ST
S5_GB = LANES // S5_GROUP_CH
S5_NB = S5_W // LANES
S5_ROW = S5_GB * S5_IN
TB = 4
TT = 128
TROWS = TB * TT // S5_CHUNK
HPS = 4
N_MODS = 8
NEG = -1e30

Pass = collections.namedtuple("Pass", "nb seq mod_first mod_each")
PROMPT = Pass(BATCH, SEQ, DEC_BATCH, False)
SAMPLE = Pass(DEC_BATCH, DEC_SEQ, 0, True)
T_PASS = BATCH * SEQ
assert T_PASS == DEC_BATCH * DEC_SEQ
R_PASS = T_PASS // S5_CHUNK

_NT = (((1,), (1,)), ((), ()))
_TN = (((0,), (0,)), ((), ()))


def _cparams(sem):
    return pltpu.CompilerParams(dimension_semantics=sem, vmem_limit_bytes=VMEM_LIMIT)


def _full(a):
    return pl.BlockSpec(a.shape, lambda *_: (0,) * a.ndim)


def _layer(a, l):
    return pl.BlockSpec((None,) + a.shape[1:], lambda *_: (l,) + (0,) * (a.ndim - 1))


def _tile_mod_spec(p, l):
    if p.mod_each:
        return pl.BlockSpec((None, TB, 1, 6 * D_MODEL), lambda j, k: (l, p.mod_first // TB + j, 0, 0))
    return pl.BlockSpec((None, 1, 1, 6 * D_MODEL), lambda j, k: (l, p.mod_first, 0, 0))


def _log_sigmoid(x):
    return jnp.minimum(x, 0.0) - jnp.log1p(jnp.exp(-jnp.abs(x)))


def _tile_spec(w):
    return pl.BlockSpec((TB, TT, w), lambda j, k: (j, k, 0))


def _chunk_rows_spec(p):
    per_group = p.seq // TT
    return pl.BlockSpec((S5_NB, TROWS, S5_ROW), lambda j, k: (0, j * per_group + k, 0))


def _slab(b, c):
    tok = slice(b * TT + c * S5_CHUNK, b * TT + (c + 1) * S5_CHUNK)
    chk = slice((c * TB + b) * S5_CHUNK, (c * TB + b + 1) * S5_CHUNK)
    return tok, chk


def _ada_kernel(c_ref, w_ref, b_ref, o_ref):
    a = c_ref[...]
    a = (a * jax.nn.sigmoid(a)).astype(BF16)
    o_ref[0] = jnp.dot(a, w_ref[0].astype(BF16), preferred_element_type=F32) + b_ref[0]


def _ada(cc, w_ada, b_ada):
    tn = 512
    return pl.pallas_call(
        _ada_kernel,
        grid=(DEPTH, 6 * D_MODEL // tn),
        in_specs=[pl.BlockSpec((N_MODS, D_MODEL), lambda l, j: (0, 0)),
                  pl.BlockSpec((1, D_MODEL, tn), lambda l, j: (l, 0, j)),
                  pl.BlockSpec((1, 1, tn), lambda l, j: (l, 0, j))],
        out_specs=pl.BlockSpec((1, N_MODS, tn), lambda l, j: (l, 0, j)),
        out_shape=jax.ShapeDtypeStruct((DEPTH, N_MODS, 6 * D_MODEL), F32),
        compiler_params=_cparams(("arbitrary", "arbitrary")),
        name="ada_mod",
    )(cc, w_ada, b_ada.reshape(DEPTH, 1, 6 * D_MODEL))


def _mod_norm(x3, mod, nw, first):
    sh = mod[:, :, first * D_MODEL:(first + 1) * D_MODEL]
    sc = mod[:, :, (first + 1) * D_MODEL:(first + 2) * D_MODEL]
    y = x3 * lax.rsqrt(jnp.mean(x3 * x3, axis=-1, keepdims=True) + EPS) * nw
    return (y * (1.0 + sc) + sh).reshape(TB * TT, D_MODEL).astype(BF16)


def _in_kernel(x_ref, mod_ref, nw_ref, w_ref, wv_ref, wg_ref, bg_ref,
               zq_ref, zk_ref, zo_ref, zf_ref, zu_ref, vt_ref, gt_ref, scr_ref):
    xn = _mod_norm(x_ref[...], mod_ref[...], nw_ref[...], 0)
    w_main = Z_W - S5_W
    zu = jnp.dot(xn, w_ref[:, w_main:], preferred_element_type=F32)
    for bl in range(S5_NB):
        zb = zu[:, bl * LANES:(bl + 1) * LANES]
        for b in range(TB):
            for c in range(TT // S5_CHUNK):
                tok, chk = _slab(b, c)
                scr_ref[bl, chk, :] = zb[tok]
        by_token = [scr_ref[bl, pl.ds(s, TROWS, stride=S5_CHUNK), :] for s in range(S5_CHUNK)]
        for gl in range(S5_GB):
            lo = gl * S5_GROUP_CH
            zu_ref[bl, :, gl * S5_IN:(gl + 1) * S5_IN] = jnp.concatenate(
                [x[:, lo:lo + S5_GROUP_CH] for x in by_token], axis=1)
    z = jnp.dot(xn, w_ref[:, :w_main], preferred_element_type=F32)
    o = 0
    for ref, w in ((zq_ref, HEADS_W), (zk_ref, HEADS_W), (zo_ref, HEADS_W), (zf_ref, FOURIER_W)):
        ref[...] = z[:, o:o + w].astype(BF16).reshape(TB, TT, w)
        o += w
    vt = lax.dot_general(wv_ref[...], xn, _NT, preferred_element_type=F32).astype(BF16)
    gt = lax.dot_general(wg_ref[...], xn, _NT, preferred_element_type=F32) + bg_ref[...]
    for b in range(TB):
        vt_ref[b] = vt[:, b * TT:(b + 1) * TT]
        gt_ref[b] = gt[:, b * TT:(b + 1) * TT]


def _in_proj(p, l, x, mods, nw, w, wv_t, wg_t, bg):
    chan = lambda c_: pl.BlockSpec((TB, c_, TT), lambda j, k: (j, 0, k))
    outs = [HEADS_W] * 3 + [FOURIER_W]
    return pl.pallas_call(
        _in_kernel,
        grid=(p.nb // TB, p.seq // TT),
        in_specs=[_tile_spec(D_MODEL), _tile_mod_spec(p, l),
                  _layer(nw, l), _layer(w, l), _layer(wv_t, l), _layer(wg_t, l), _layer(bg, l)],
        out_specs=[_tile_spec(w_) for w_ in outs] + [_chunk_rows_spec(p), chan(HEADS_W), chan(N_GATES)],
        out_shape=[jax.ShapeDtypeStruct((p.nb, p.seq, w_), BF16) for w_ in outs]
        + [jax.ShapeDtypeStruct((S5_NB, R_PASS, S5_ROW), F32),
           jax.ShapeDtypeStruct((p.nb, HEADS_W, p.seq), BF16),
           jax.ShapeDtypeStruct((p.nb, N_GATES, p.seq), F32)],
        scratch_shapes=[pltpu.VMEM((S5_NB, TB * TT, LANES), F32)],
        compiler_params=_cparams(("arbitrary", "arbitrary")),
        name="in_proj_%d" % p.seq,
    )(x, mods, nw, w, wv_t, wg_t, bg)


def _dft_consts():
    d = np.arange(FOURIER_DH)
    phi = 2.0 * np.pi * ((d[:, None] * d[None, :]) % FOURIER_DH) / FOURIER_DH
    eye = np.eye(FOURIER_W // FOURIER_DH)
    cd = np.kron(eye, np.cos(phi)) / math.sqrt(FOURIER_DH)
    sd = np.kron(eye, np.sin(phi)) / math.sqrt(FOURIER_DH)
    s = np.arange(SEQ)
    th = 2.0 * np.pi * ((s[:, None] * s[None, :]) % SEQ) / SEQ
    rows = DEC_SEQ // GRID_W
    pos = np.arange(DEC_SEQ)
    r, c = pos // GRID_W, pos % GRID_W
    ph = ((r[:, None] * r[None, :]) * (GRID_W // rows) + c[:, None] * c[None, :]) % GRID_W
    th2 = 2.0 * np.pi * ph / GRID_W
    return (np.concatenate([cd, sd], axis=1),
            np.concatenate([np.cos(th), -np.sin(th)], axis=1) / math.sqrt(SEQ),
            np.concatenate([np.cos(th2), -np.sin(th2)], axis=1) / math.sqrt(DEC_SEQ))


def _fourier_prompt_kernel(nb, zf_ref, cdsd_ref, cs_ref, wf_ref, o_ref):
    t = jnp.dot(zf_ref[...].reshape(nb * SEQ, FOURIER_W), cdsd_ref[...],
                preferred_element_type=F32).astype(BF16)
    fs = []
    for b in range(nb):
        tb = t[b * SEQ:(b + 1) * SEQ]
        st = jnp.concatenate([tb[:, :FOURIER_W], tb[:, FOURIER_W:]], axis=0)
        fs.append(jnp.dot(cs_ref[...], st, preferred_element_type=F32))
    for b in range(nb):
        o_ref[b] = jnp.dot(fs[b].astype(BF16), wf_ref[...], preferred_element_type=F32).astype(BF16)


def _fourier_prompt(l, zf, cdsd, cs, wf):
    nb = 4
    blk = pl.BlockSpec((nb, SEQ, FOURIER_W), lambda i: (i, 0, 0))
    return pl.pallas_call(
        functools.partial(_fourier_prompt_kernel, nb),
        grid=(BATCH // nb,),
        in_specs=[blk, _full(cdsd), _full(cs), _layer(wf, l)],
        out_specs=blk,
        out_shape=jax.ShapeDtypeStruct((BATCH, SEQ, FOURIER_W), BF16),
        compiler_params=_cparams(("arbitrary",)),
        name="fourier_prompt",
    )(zf, cdsd, cs, wf)


def _fourier_sample_kernel(zf_ref, cdsd_ref, ab_ref, wf_ref, o_ref, tt_ref):
    @pl.when(pl.program_id(0) == 0)
    def _():
        for b in range(DEC_BATCH):
            t = jnp.dot(zf_ref[b], cdsd_ref[...], preferred_element_type=F32).astype(BF16)
            tt_ref[b, 0:DEC_SEQ, :] = t[:, :FOURIER_W]
            tt_ref[b, DEC_SEQ:2 * DEC_SEQ, :] = t[:, FOURIER_W:]

    fs = [jnp.dot(ab_ref[...], tt_ref[b], preferred_element_type=F32) for b in range(DEC_BATCH)]
    for b in range(DEC_BATCH):
        o_ref[b] = jnp.dot(fs[b].astype(BF16), wf_ref[...], preferred_element_type=F32).astype(BF16)


def _fourier_sample(l, zf, cdsd, ab, wf):
    tk = 512
    return pl.pallas_call(
        _fourier_sample_kernel,
        grid=(DEC_SEQ // tk,),
        in_specs=[_full(zf), _full(cdsd), pl.BlockSpec((tk, 2 * DEC_SEQ), lambda i: (i, 0)), _layer(wf, l)],
        out_specs=pl.BlockSpec((DEC_BATCH, tk, FOURIER_W), lambda i: (0, i, 0)),
        out_shape=jax.ShapeDtypeStruct((DEC_BATCH, DEC_SEQ, FOURIER_W), BF16),
        scratch_shapes=[pltpu.VMEM((DEC_BATCH, 2 * DEC_SEQ, FOURIER_W), BF16)],
        compiler_params=_cparams(("arbitrary",)),
        name="fourier_sample",
    )(zf, cdsd, ab, wf)


def _split3(x):
    hi = x.astype(BF16).astype(F32)
    mid = (x - hi).astype(BF16).astype(F32)
    lo = (x - hi - mid).astype(BF16).astype(F32)
    return hi, mid, lo


def _gate_kernel(g_ref, o_ref):
    L = MLSTM_CHUNK
    nrow = N_GATES // 2
    row = lax.broadcasted_iota(jnp.int32, (L, L), 0)
    col = lax.broadcasted_iota(jnp.int32, (L, L), 1)
    tri_pre = jnp.where(row <= col, 1.0, 0.0).astype(BF16)
    tri_suf = jnp.where(row >= col, 1.0, 0.0).astype(BF16)
    is_fwd = lax.broadcasted_iota(jnp.int32, (nrow, L), 0) < MLSTM_HEADS
    lane = lax.broadcasted_iota(jnp.int32, (nrow, L), 1)
    chunks = [(bi, slice(c * L, (c + 1) * L)) for bi in range(g_ref.shape[0]) for c in range(g_ref.shape[2] // L)]
    fold = lambda a: a[0:nrow] + a[nrow:2 * nrow] + a[2 * nrow:]
    bs, rs = [], []
    for bi, cols in chunks:
        lf = _log_sigmoid(g_ref[bi, nrow:, cols])
        parts = jnp.concatenate(_split3(lf), axis=0).astype(BF16)
        pre = jnp.dot(parts, tri_pre, preferred_element_type=F32)
        suf = jnp.dot(parts, tri_suf, preferred_element_type=F32)
        bs.append(jnp.where(is_fwd, fold(pre), fold(suf)))
        rs.append(g_ref[bi, 0:nrow, cols] - bs[-1])
    pms, sms = list(rs), list(rs)
    sh = 1
    while sh < L:
        pms = [jnp.maximum(x, jnp.where(lane >= sh, pltpu.roll(x, sh, 1), NEG)) for x in pms]
        sms = [jnp.maximum(x, jnp.where(lane < L - sh, pltpu.roll(x, L - sh, 1), NEG)) for x in sms]
        sh *= 2
    for (bi, cols), b, r, pm, sm in zip(chunks, bs, rs, pms, sms):
        for q, val in enumerate((b, r, jnp.where(is_fwd, pm, sm))):
            for dh in range(nrow):
                o_ref[bi, dh, q:q + 1, cols] = val[dh:dh + 1]


def _gate_prep(p, gt):
    bb = max(1, DEC_SEQ // p.seq)
    nrow = N_GATES // 2
    return pl.pallas_call(
        _gate_kernel,
        grid=(p.nb // bb,),
        in_specs=[pl.BlockSpec((bb, N_GATES, p.seq), lambda i: (i, 0, 0))],
        out_specs=pl.BlockSpec((bb, nrow, 3, p.seq), lambda i: (i, 0, 0, 0)),
        out_shape=jax.ShapeDtypeStruct((p.nb, nrow, 3, p.seq), F32),
        compiler_params=_cparams(("arbitrary",)),
        name="gate_prep_%d" % p.seq,
    )(gt)


def _mlstm_chunk(q, k, vt, pr, ct, m, fwd, out):
    L = q.shape[0]
    scale = MLSTM_DH ** -0.5
    b, r, cm = pr[0:1], pr[1:2], pr[2:3]
    ones = jnp.ones((3, L), F32)
    zeros = jnp.zeros((SUBLANES - 6, L), F32)
    lhs = jnp.concatenate(_split3(r) + (ones, zeros), axis=0).astype(BF16)
    rhs = jnp.concatenate((ones,) + _split3(-cm) + (zeros,), axis=0).astype(BF16)
    arg = lax.dot_general(lhs, rhs, _TN, preferred_element_type=F32)
    st = lax.dot_general(k, q, _NT, preferred_element_type=F32)
    cq = lax.dot_general(ct.astype(BF16), q, _NT, preferred_element_type=F32)
    last = L - 1 if fwd else 0
    cm_last = cm[:, last:last + 1]
    mx_last = jnp.maximum(m, cm_last)
    vw = (vt.astype(F32) * jnp.exp(r - cm_last)).astype(BF16)
    dct = jnp.dot(vw, k, preferred_element_type=F32)
    yield

    row = lax.broadcasted_iota(jnp.int32, (L, L), 0)
    col = lax.broadcasted_iota(jnp.int32, (L, L), 1)
    e = jnp.where((row <= col) if fwd else (row >= col), jnp.exp(arg), 0.0)
    num = jnp.dot(vt, (st * e).astype(BF16), preferred_element_type=F32)
    ct_new = jnp.exp(m - mx_last) * ct + (jnp.exp(cm_last - mx_last) * scale) * dct
    yield

    mx = jnp.maximum(m, cm)
    num = (jnp.exp(cm - mx) * scale) * num + jnp.exp(m - mx) * cq
    den = num[N_AUG:N_AUG + 1, :]
    h = num * (1.0 / jnp.maximum(jnp.abs(den), jnp.exp(-(b + mx))))
    out += [h, ct_new, b[:, last:last + 1] + mx_last]


def _mlstm_kernel(nc, has_init, want_final, *refs):
    refs = list(refs)
    m0_ref, c0_ref = (refs.pop(0), refs.pop(0)) if has_init else (None, None)
    q_ref, k_ref, vt_ref, zo_ref, prf_ref, prb_ref, nw_ref, o_ref = refs[:8]
    hbuf_ref = refs[-1]
    L = MLSTM_CHUNK
    bi = pl.program_id(0)
    hg = pl.program_id(1)
    vrow = lax.broadcasted_iota(jnp.int32, (HEAD_PAD, L), 0)

    def chunk(hh, ci, pr_ref, ct, m, dr, out):
        rows = pl.ds(pl.multiple_of(ci * L, L), L)
        lanes = slice(hh * HEAD_PAD, (hh + 1) * HEAD_PAD)
        vt = vt_ref[lanes, rows]
        vt = jnp.where(vrow == N_AUG, jnp.ones_like(vt), vt)
        res = []
        yield from _mlstm_chunk(q_ref[rows, lanes], k_ref[rows, lanes], vt, pr_ref[hh, :, rows], ct, m,
                                dr == 0, res)
        hbuf_ref[hh, dr, ci] = res[0]
        out += res[1:]

    def step(i, carry):
        outs = [[] for _ in range(2 * HPS)]
        gens = []
        for hh in range(HPS):
            cf, mf, cb, mb = carry[4 * hh:4 * hh + 4]
            gens += [chunk(hh, i, prf_ref, cf, mf, 0, outs[2 * hh]),
                     chunk(hh, nc - 1 - i, prb_ref, cb, mb, 1, outs[2 * hh + 1])]
        while gens:
            alive = []
            for g in gens:
                try:
                    next(g)
                    alive.append(g)
                except StopIteration:
                    pass
            gens = alive
        return tuple(x for o in outs for x in o)

    carry = []
    for hh in range(HPS):
        for dr in range(2):
            if has_init:
                carry += [c0_ref[0, dr, hh].T, jnp.full((1, 1), m0_ref[bi, dr, hg * HPS + hh], F32)]
            else:
                carry += [jnp.zeros((HEAD_PAD, HEAD_PAD), F32), jnp.zeros((1, 1), F32)]
    carry = step(0, tuple(carry)) if nc == 1 else lax.fori_loop(0, nc, step, tuple(carry))
    if want_final:
        cfin_ref, nfin_ref, mfin_ref = refs[8:11]
        for hh in range(HPS):
            for dr in range(2):
                ct, m = carry[4 * hh + 2 * dr], carry[4 * hh + 2 * dr + 1]
                cfin_ref[0, dr, hh] = ct.T[:MLSTM_DH, :MLSTM_DH]
                nfin_ref[0, dr, hh] = ct[N_AUG:N_AUG + 1, :MLSTM_DH]
                mfin_ref[0, dr, hh] = jnp.broadcast_to(m, (SUBLANES, LANES))

    def finish(ci, _):
        rows = pl.ds(pl.multiple_of(ci * L, L), L)
        for hh in range(HPS):
            lanes = slice(hh * HEAD_PAD, (hh + 1) * HEAD_PAD)
            h = jnp.where(vrow < MLSTM_DH, hbuf_ref[hh, 0, ci] + hbuf_ref[hh, 1, ci], 0.0)
            ms = jnp.sum(h * h, axis=0, keepdims=True) * (1.0 / MLSTM_DH)
            hn = (h * lax.rsqrt(ms + EPS) * nw_ref[hh]).T
            o_ref[rows, lanes] = (hn * jax.nn.sigmoid(zo_ref[rows, lanes].astype(F32))).astype(BF16)
        return 0

    if nc == 1:
        finish(0, 0)
    else:
        lax.fori_loop(0, nc, finish, 0)


def _mlstm(p, l, q, k, vt, zo, pr, nw, init=None, want_final=False):
    nc = p.seq // MLSTM_CHUNK
    tok = pl.BlockSpec((None, p.seq, HPS * HEAD_PAD), lambda b, h: (b, 0, h))
    st = lambda r, c: pl.BlockSpec((1, 2, HPS, r, c), lambda b, h: (b, 0, h, 0, 0))
    prs = lambda dr: pl.BlockSpec((None, HPS, 3, p.seq), lambda b, h: (b, dr * (MLSTM_HEADS // HPS) + h, 0, 0))
    in_specs, args = [], []
    if init is not None:
        in_specs += [pl.BlockSpec(memory_space=pltpu.SMEM), st(HEAD_PAD, HEAD_PAD)]
        args += list(init)
    in_specs += [tok, tok, pl.BlockSpec((None, HPS * HEAD_PAD, p.seq), lambda b, h: (b, h, 0)), tok, prs(0), prs(1),
                 pl.BlockSpec((None, HPS, HEAD_PAD, 1), lambda b, h: (l, h, 0, 0))]
    args += [q, k, vt, zo, pr, pr, nw]
    out_specs = [tok]
    out_shape = [jax.ShapeDtypeStruct((p.nb, p.seq, HEADS_W), BF16)]
    if want_final:
        out_specs += [st(MLSTM_DH, MLSTM_DH), st(1, MLSTM_DH), st(SUBLANES, LANES)]
        out_shape += [jax.ShapeDtypeStruct((p.nb, 2, MLSTM_HEADS, r, c), F32)
                      for r, c in ((MLSTM_DH, MLSTM_DH), (1, MLSTM_DH), (SUBLANES, LANES))]
    return pl.pallas_call(
        functools.partial(_mlstm_kernel, nc, init is not None, want_final),
        grid=(p.nb, MLSTM_HEADS // HPS),
        in_specs=in_specs,
        out_specs=out_specs,
        out_shape=out_shape,
        scratch_shapes=[pltpu.VMEM((HPS, 2, nc, HEAD_PAD, MLSTM_CHUNK), F32)],
        compiler_params=_cparams(("arbitrary", "arbitrary")),
        name="mlstm_%d" % p.seq,
    )(*args)


def _cpow(br, bi, e, nbits):
    pr = pi = None
    for bit in range(nbits):
        sel = ((e >> bit) & 1) == 1
        if pr is None:
            pr, pi = jnp.where(sel, br, 1.0), jnp.where(sel, bi, 0.0)
        else:
            pr, pi = jnp.where(sel, pr * br - pi * bi, pr), jnp.where(sel, pr * bi + pi * br, pi)
        if bit + 1 < nbits:
            br, bi = br * br - bi * bi, 2.0 * br * bi
    return pr, pi


def _s5_prep_kernel(lamc_re_ref, lamc_im_ref, lamr_re_ref, lamr_im_ref, lstep_ref,
                    bt_re_ref, bt_im_ref, ct_re_ref, ct_im_ref,
                    t_ref, m_ref, n_ref, a_ref):
    C = S5_CHUNK
    nbits = (C - 1).bit_length()
    assert C == 1 << nbits
    kk = lax.broadcasted_iota(jnp.int32, (S5_STATE, S5_IN), 1) >> 4
    srow = lax.broadcasted_iota(jnp.int32, (S5_IN, S5_ST), 0) >> 4
    left = lax.broadcasted_iota(jnp.int32, (S5_IN, S5_ST), 1) < S5_STATE
    left16 = lax.broadcasted_iota(jnp.int32, (S5_GROUP_CH, S5_ST), 1) < S5_STATE
    left1 = lax.broadcasted_iota(jnp.int32, (1, S5_ST), 1) < S5_STATE
    lane = lax.broadcasted_iota(jnp.int32, (S5_GROUP_CH, S5_IN), 1)
    sel = jnp.where((lane & (S5_GROUP_CH - 1)) == lax.broadcasted_iota(jnp.int32, (S5_GROUP_CH, S5_IN), 0),
                    1.0, 0.0)
    spread = lambda ref: jnp.dot(ref[0, 0], sel, precision=lax.Precision.HIGHEST, preferred_element_type=F32)
    ct_re = spread(ct_re_ref)
    ct_im = spread(ct_im_ref)
    resp = []
    for d in range(2):
        step = jnp.exp(lstep_ref[0, d, 0])
        lr_c, li_c = lamc_re_ref[0, d, 0] * step, lamc_im_ref[0, d, 0] * step
        lr_r, li_r = lamr_re_ref[0, d, 0], lamr_im_ref[0, d, 0]
        lbc_re, lbc_im = jnp.exp(lr_c) * jnp.cos(li_c), jnp.exp(lr_c) * jnp.sin(li_c)
        mag = jnp.exp(lr_r * step)
        lb_re, lb_im = mag * jnp.cos(li_r * step), mag * jnp.sin(li_r * step)

        pr, pi = _cpow(lbc_re, lbc_im, kk if d == 0 else (C - 1) - kk, nbits)
        pr1, pi1 = pr * lbc_re - pi * lbc_im, pr * lbc_im + pi * lbc_re
        cpr, cpi = ct_re * pr - ct_im * pi, ct_re * pi + ct_im * pr
        cpr1, cpi1 = ct_re * pr1 - ct_im * pi1, ct_re * pi1 + ct_im * pr1

        nr, ni = lb_re - 1.0, lb_im
        den = lr_r * lr_r + li_r * li_r
        kap_re = (nr * lr_r + ni * li_r) / den
        kap_im = (ni * lr_r - nr * li_r) / den
        bb_re = kap_re * bt_re_ref[0, 0] - kap_im * bt_im_ref[0, 0]
        bb_im = kap_re * bt_im_ref[0, 0] + kap_im * bt_re_ref[0, 0]

        resp.append(jnp.dot(jnp.where(left16, bb_re, -bb_im), jnp.concatenate([cpr, cpi], axis=0),
                            precision=lax.Precision.HIGHEST, preferred_element_type=F32))
        m_ref[0, d, 0] = jnp.concatenate([cpr1, -cpi1], axis=0).astype(BF16)

        pr, pi = _cpow(lb_re, lb_im, (C - 1) - srow if d == 0 else srow, nbits)
        bt_r = jnp.concatenate([bb_re] * C, axis=0)
        bt_i = jnp.concatenate([bb_im] * C, axis=0)
        n_re, n_im = pr * bt_r - pi * bt_i, pr * bt_i + pi * bt_r
        n_ref[0, d, 0] = jnp.concatenate([jnp.where(left, n_re, n_im), jnp.where(left, n_im, n_re)],
                                         axis=1).astype(BF16)

        ar, ai = lb_re, lb_im
        for _ in range(nbits):
            ar, ai = ar * ar - ai * ai, 2.0 * ar * ai
        a2 = jnp.where(left1, -ai, ai)
        a_ref[0, d, 0] = jnp.concatenate([jnp.concatenate([ar, ar], axis=1),
                                          jnp.concatenate([a2, -a2], axis=1)], axis=0)

    rf, rb = resp
    for s in range(C):
        nf = S5_GROUP_CH * s
        blk = jnp.where(lane >= nf, pltpu.roll(rf, nf, 1) if nf else rf, 0.0)
        nb = S5_GROUP_CH * (C - 1 - s)
        blk = blk + jnp.where(lane < S5_IN - nb, pltpu.roll(rb, S5_IN - nb, 1) if nb else rb, 0.0)
        t_ref[0, 0, S5_GROUP_CH * s:S5_GROUP_CH * (s + 1), :] = blk.astype(BF16)


def _s5_prep(lam_re, lam_im, log_step, b_re, b_im, c_re, c_im):
    G = S5_GROUPS
    dup = lambda a: jnp.concatenate([a, a], axis=-1)
    lamc = [a.reshape(DEPTH, 2, G, S5_STATE, 1) for a in (lam_re, lam_im)]
    lamr = [dup(a).reshape(DEPTH, 2, G, 1, S5_ST) for a in (lam_re, lam_im)]
    lstep = log_step.reshape(DEPTH, 2, G, 1, 1)
    bt = [dup(jnp.swapaxes(a, 2, 3)) for a in (b_re, b_im)]
    ct = [jnp.swapaxes(a, 2, 3) for a in (c_re, c_im)]
    dspec = lambda r, c: pl.BlockSpec((1, 2, 1, r, c), lambda l, g: (l, 0, g, 0, 0))
    gspec = lambda r, c: pl.BlockSpec((1, 1, r, c), lambda l, g: (l, g, 0, 0))
    t, m, n, a = pl.pallas_call(
        _s5_prep_kernel,
        grid=(DEPTH, G),
        in_specs=[dspec(S5_STATE, 1), dspec(S5_STATE, 1), dspec(1, S5_ST), dspec(1, S5_ST), dspec(1, 1),
                  gspec(S5_GROUP_CH, S5_ST), gspec(S5_GROUP_CH, S5_ST),
                  gspec(S5_STATE, S5_GROUP_CH), gspec(S5_STATE, S5_GROUP_CH)],
        out_specs=[gspec(S5_IN, S5_IN), dspec(S5_ST, S5_IN), dspec(S5_IN, S5_ST2), dspec(2, S5_ST2)],
        out_shape=[jax.ShapeDtypeStruct((DEPTH, G, S5_IN, S5_IN), BF16),
                   jax.ShapeDtypeStruct((DEPTH, 2, G, S5_ST, S5_IN), BF16),
                   jax.ShapeDtypeStruct((DEPTH, 2, G, S5_IN, S5_ST2), BF16),
                   jax.ShapeDtypeStruct((DEPTH, 2, G, 2, S5_ST2), F32)],
        compiler_params=_cparams(("arbitrary", "arbitrary")),
        name="s5_prep",
    )(*lamc, *lamr, lstep, *bt, *ct)
    return t, m, n, jnp.transpose(a, (0, 1, 3, 2, 4)).reshape(DEPTH, 2, 2, G * S5_ST2)


def _s5_kernel(nseg, nchunks, zr_ref, t_ref, m_ref, n_ref, a_ref, x0_ref, d_ref,
               y_ref, xfin_ref, v_ref, xp_ref):
    W = S5_GB * S5_ST2
    for gl in range(S5_GB):
        u = zr_ref[0, :, gl * S5_IN:(gl + 1) * S5_IN].astype(BF16)
        for d in range(2):
            v_ref[d, :, gl * S5_ST2:(gl + 1) * S5_ST2] = jnp.dot(
                u, n_ref[d, gl], preferred_element_type=F32)

    def halves(x, which):
        return [x[:, g * S5_ST2 + h * S5_ST:g * S5_ST2 + (h + 1) * S5_ST]
                for g in range(S5_GB) for h in which]

    per = SUBLANES // TB
    steps = nchunks // per

    for d in range(2):
        a = a_ref[d, 0:1, :]
        a2 = a_ref[d, 1:2, :]

        def advance(x, v):
            swapped = jnp.concatenate(halves(x, (1, 0)), axis=1)
            return a * x + a2 * swapped + v

        for seg in range(nseg):
            def step(i, x):
                si = i if d == 0 else steps - 1 - i
                rows = pl.ds(pl.multiple_of(seg * nchunks * TB + si * SUBLANES, SUBLANES), SUBLANES)
                v = v_ref[d, rows, :]
                order = range(per) if d == 0 else range(per - 1, -1, -1)
                entering = [None] * per
                for j in order:
                    entering[j] = jnp.concatenate(halves(x, (0,)), axis=1)
                    x = advance(x, v[j * TB:(j + 1) * TB])
                xp_ref[d, rows, :] = jnp.concatenate(entering, axis=0)
                return x

            x0 = jnp.zeros((TB, W), F32) if x0_ref is None else x0_ref[d, seg * TB:(seg + 1) * TB, :]
            x = lax.fori_loop(0, steps, step, x0)
            if xfin_ref is not None:
                xfin_ref[d, seg * TB:(seg + 1) * TB, :] = jnp.concatenate(halves(x, (0,)), axis=1)

    for gl in range(S5_GB):
        cols = slice(gl * S5_IN, (gl + 1) * S5_IN)
        u = zr_ref[0, :, cols]
        y = jnp.dot(u.astype(BF16), t_ref[gl], preferred_element_type=F32)
        for d in range(2):
            y = y + jnp.dot(xp_ref[d, :, gl * S5_ST:(gl + 1) * S5_ST].astype(BF16), m_ref[d, gl],
                            preferred_element_type=F32)
        y_ref[0, :, cols] = jax.nn.gelu(y + d_ref[0, :, cols] * u)


def _s5(p, l, zr, t, m, n, a, dt, x0=None, want_final=False):
    G = S5_GROUPS
    nseg = p.nb // TB
    row = pl.BlockSpec((1, R_PASS, S5_ROW), lambda j: (j, 0, 0))
    dsp = lambda r, c: pl.BlockSpec((None, 2, S5_GB, r, c), lambda j: (l, 0, j, 0, 0))
    lsp = lambda r, w: pl.BlockSpec((2, r, S5_GB * w), lambda j: (0, 0, j))
    in_specs = [row, pl.BlockSpec((None, S5_GB, S5_IN, S5_IN), lambda j: (l, j, 0, 0)),
                dsp(S5_ST, S5_IN), dsp(S5_IN, S5_ST2),
                pl.BlockSpec((None, 2, 2, S5_GB * S5_ST2), lambda j: (l, 0, 0, j)),
                pl.BlockSpec((None, 1, 1, S5_ROW), lambda j: (l, j, 0, 0))]
    args = [zr, t, m, n, a, dt]
    if x0 is not None:
        in_specs.append(lsp(p.nb, S5_ST2))
        args.append(x0)
    out_specs = [row]
    out_shape = [jax.ShapeDtypeStruct((S5_NB, R_PASS, S5_ROW), F32)]
    if want_final:
        out_specs.append(lsp(p.nb, S5_ST))
        out_shape.append(jax.ShapeDtypeStruct((2, p.nb, G * S5_ST), F32))

    def body(zr_ref, t_ref, m_ref, n_ref, a_ref, d_ref, *rest):
        rest = list(rest)
        x0_ref = rest.pop(0) if x0 is not None else None
        y_ref = rest.pop(0)
        xfin_ref = rest.pop(0) if want_final else None
        _s5_kernel(nseg, p.seq // S5_CHUNK, zr_ref, t_ref, m_ref, n_ref, a_ref, x0_ref, d_ref,
                   y_ref, xfin_ref, *rest)

    return pl.pallas_call(
        body,
        grid=(S5_NB,),
        in_specs=in_specs,
        out_specs=out_specs,
        out_shape=out_shape,
        scratch_shapes=[pltpu.VMEM((2, R_PASS, S5_GB * S5_ST2), F32),
                        pltpu.VMEM((2, R_PASS, S5_GB * S5_ST), F32)],
        compiler_params=_cparams(("arbitrary",)),
        name="s5_scan_%d" % p.seq,
    )(*args)


def _mix_stages(x_ref, mod_ref, fo_ref, mo_ref, ys_ref, wglu_ref, wo_ref, nw_ref, scr_ref, out):
    mod = mod_ref[...]
    g1 = mod[:, :, 2 * D_MODEL:3 * D_MODEL]
    flat = lambda ref: ref[...].reshape(TB * TT, ref.shape[-1])
    o_m, o_s = FOURIER_W, FOURIER_W + HEADS_W
    mix = (jnp.dot(flat(fo_ref), wo_ref[:o_m, :], preferred_element_type=F32)
           + jnp.dot(flat(mo_ref), wo_ref[o_m:o_s, :], preferred_element_type=F32))
    yield
    blocks = []
    for bl in range(S5_NB):
        by_group = [ys_ref[bl, :, gl * S5_IN:(gl + 1) * S5_IN] for gl in range(S5_GB)]
        for t in range(S5_CHUNK):
            lo = t * S5_GROUP_CH
            scr_ref[bl, pl.ds(t, TROWS, stride=S5_CHUNK), :] = jnp.concatenate(
                [y[:, lo:lo + S5_GROUP_CH] for y in by_group], axis=1)
        slabs = [scr_ref[bl, _slab(b, c)[1], :] for b in range(TB) for c in range(TT // S5_CHUNK)]
        blocks.append(jnp.concatenate(slabs, axis=0))
        yield
    y = jnp.concatenate(blocks, axis=1).astype(BF16)
    gg = jnp.dot(y, wglu_ref[...], preferred_element_type=F32)
    s_out = (gg[:, :S5_W] * jax.nn.sigmoid(gg[:, S5_W:])).astype(BF16)
    mix = mix + jnp.dot(s_out, wo_ref[o_s:, :], preferred_element_type=F32)
    yield
    x1 = x_ref[...] + g1 * mix.reshape(TB, TT, D_MODEL)
    out += [x1, _mod_norm(x1, mod, nw_ref[...], 3)]


FF_SPLIT = (768, 768, 640, 640)
assert sum(FF_SPLIT) == D_FF


def _ffn_stages(final, xn, x1, mod_ref, wg_ref, wu_ref, wd_ref, nf_ref, out):
    ff = None
    o = 0
    for w in FF_SPLIT:
        cols = slice(o, o + w)
        o += w
        a = jnp.dot(xn, wg_ref[:, cols], preferred_element_type=F32)
        u = jnp.dot(xn, wu_ref[:, cols], preferred_element_type=F32)
        h = (a * jax.nn.sigmoid(a) * u).astype(BF16)
        part = jnp.dot(h, wd_ref[cols, :], preferred_element_type=F32)
        ff = part if ff is None else ff + part
        yield
    g2 = mod_ref[...][:, :, 5 * D_MODEL:6 * D_MODEL]
    x2 = x1 + g2 * ff.reshape(TB, TT, D_MODEL)
    if final:
        x2 = x2 * lax.rsqrt(jnp.mean(x2 * x2, axis=-1, keepdims=True) + EPS) * nf_ref[...]
    out.append(x2)


def _post_kernel(final, x_ref, moda_ref, fo_ref, mo_ref, ys_ref, modb_ref, wglu_ref, wo_ref, nw_ref,
                 wg_ref, wu_ref, wd_ref, nf_ref, o_ref, x1_ref, xn_ref, scr_ref):
    i = pl.program_id(0)
    cur = i % 2
    prev = 1 - cur

    @pl.when(i == 0)
    def _():
        x1_ref[1] = jnp.zeros((TB, TT, D_MODEL), F32)
        xn_ref[1] = jnp.zeros((TB * TT, D_MODEL), BF16)

    res_a, res_b = [], []
    gens = [_ffn_stages(final, xn_ref[prev], x1_ref[prev], modb_ref, wg_ref, wu_ref, wd_ref, nf_ref, res_b),
            _mix_stages(x_ref, moda_ref, fo_ref, mo_ref, ys_ref, wglu_ref, wo_ref, nw_ref, scr_ref, res_a)]
    while gens:
        alive = []
        for g in gens:
            try:
                next(g)
                alive.append(g)
            except StopIteration:
                pass
        gens = alive
    o_ref[...] = res_b[0]
    x1_ref[cur] = res_a[0]
    xn_ref[cur] = res_a[1]


def _post(p, l, final, x, mods, fo, mo, ys, wglu, wo, nw, wg, wu, wd, nf):
    nk = p.seq // TT
    n = (p.nb // TB) * nk
    tile_a = lambda i: jnp.minimum(i, n - 1)
    tile_b = lambda i: jnp.maximum(i - 1, 0)

    def tok(w, tile):
        return pl.BlockSpec((TB, TT, w), lambda i: (tile(i) // nk, tile(i) % nk, 0))

    def mod(tile):
        if p.mod_each:
            return pl.BlockSpec((None, TB, 1, 6 * D_MODEL), lambda i: (l, p.mod_first // TB + tile(i) // nk, 0, 0))
        return pl.BlockSpec((None, 1, 1, 6 * D_MODEL), lambda i: (l, p.mod_first, 0, 0))

    once = lambda a: pl.BlockSpec((None,) + a.shape[1:], lambda i: (l,) + (0,) * (a.ndim - 1),
                                  pipeline_mode=pl.Buffered(1))
    return pl.pallas_call(
        functools.partial(_post_kernel, final),
        grid=(n + 1,),
        in_specs=[tok(D_MODEL, tile_a), mod(tile_a), tok(FOURIER_W, tile_a), tok(HEADS_W, tile_a),
                  pl.BlockSpec((S5_NB, TROWS, S5_ROW), lambda i: (0, tile_a(i), 0)), mod(tile_b),
                  once(wglu), once(wo), once(nw), once(wg), once(wu), once(wd),
                  pl.BlockSpec((1, D_MODEL), lambda i: (0, 0))],
        out_specs=tok(D_MODEL, tile_b),
        out_shape=jax.ShapeDtypeStruct((p.nb, p.seq, D_MODEL), F32),
        scratch_shapes=[pltpu.VMEM((2, TB, TT, D_MODEL), F32), pltpu.VMEM((2, TB * TT, D_MODEL), BF16),
                        pltpu.VMEM((S5_NB, TB * TT, LANES), F32)],
        compiler_params=pltpu.CompilerParams(dimension_semantics=("arbitrary",),
                                             vmem_limit_bytes=POST_VMEM_LIMIT),
        name="post_%d" % p.seq,
    )(x, mods, fo, mo, ys, mods, wglu, wo, nw, wg, wu, wd, nf)


def _pad_heads(a, axis):
    shape = a.shape[:axis] + (MLSTM_HEADS, MLSTM_DH) + a.shape[axis + 1:]
    pad = [(0, 0)] * (a.ndim + 1)
    pad[axis + 1] = (0, HEAD_PAD - MLSTM_DH)
    return jnp.pad(a.reshape(shape), pad).reshape(a.shape[:axis] + (HEADS_W,) + a.shape[axis + 1:])


def _mlstm_state_in(c, n):
    cn = jnp.concatenate([c, n[..., None]], axis=-1)
    return jnp.pad(cn, ((0, 0),) * (c.ndim - 2) + ((0, HEAD_PAD - MLSTM_DH), (0, HEAD_PAD - MLSTM_DH - 1)))


def kernel(x_prompt, x_sample, state_mlstm_C, state_mlstm_n, state_mlstm_m, state_s5_re, state_s5_im,
           c, c_ctx, w_ada, b_ada, norm1_w, norm2_w, w_in, b_gates, w_fourier, mlstm_norm_w,
           s5_lambda_re, s5_lambda_im, s5_log_step, s5_b_re, s5_b_im, s5_c_re, s5_c_im, s5_d,
           w_glu, w_out, w_gate, w_up, w_down, norm_f):
    xs = {PROMPT: x_prompt, SAMPLE: x_sample}
    cc = jnp.concatenate([c, c_ctx[None], jnp.zeros((N_MODS - 1 - DEC_BATCH, D_MODEL), F32)], axis=0)
    mods = _ada(cc, w_ada, b_ada).reshape(DEPTH, N_MODS, 1, 6 * D_MODEL)
    cdsd, cs, ab = (jnp.asarray(a.astype(np.float32)).astype(BF16) for a in _dft_consts())
    s5_t, s5_m, s5_n, s5_a = _s5_prep(s5_lambda_re, s5_lambda_im, s5_log_step, s5_b_re, s5_b_im,
                                      s5_c_re, s5_c_im)

    o_q = FOURIER_W
    o_g = o_q + 3 * MLSTM_W
    o_o = o_g + N_GATES
    o_u = o_o + MLSTM_W
    heads = lambda o: _pad_heads(w_in[:, :, o:o + MLSTM_W], 2)
    w_cat = jnp.concatenate([heads(o_q), heads(o_q + MLSTM_W), heads(o_o), w_in[:, :, :FOURIER_W],
                             w_in[:, :, o_u:]], axis=2).astype(BF16)
    wv_t = jnp.swapaxes(heads(o_q + 2 * MLSTM_W), 1, 2).astype(BF16)
    gate_perm = np.arange(N_GATES).reshape(2, 2, MLSTM_HEADS).transpose(1, 0, 2).reshape(-1)
    wg_t = jnp.swapaxes(w_in[:, :, o_g:o_o], 1, 2)[:, gate_perm].astype(BF16)
    bg = b_gates[:, gate_perm, None]
    wf = w_fourier.astype(BF16)
    nw = _pad_heads(mlstm_norm_w, 1).reshape(DEPTH, MLSTM_HEADS, HEAD_PAD, 1)
    dt = jnp.tile(s5_d[:, :, None, :], (1, 1, S5_CHUNK, 1)).reshape(DEPTH, S5_NB, 1, S5_ROW)
    wo_m = jnp.pad(w_out[:, FOURIER_W:FOURIER_W + MLSTM_W].reshape(DEPTH, MLSTM_HEADS, MLSTM_DH, D_MODEL),
                   ((0, 0), (0, 0), (0, HEAD_PAD - MLSTM_DH), (0, 0))).reshape(DEPTH, HEADS_W, D_MODEL)
    wo = jnp.concatenate([w_out[:, :FOURIER_W], wo_m, w_out[:, FOURIER_W + MLSTM_W:]], axis=1).astype(BF16)
    wglu = w_glu.astype(BF16)
    wg, wu, wd = w_gate.astype(BF16), w_up.astype(BF16), w_down.astype(BF16)
    n1, n2 = norm1_w[:, None, :], norm2_w[:, None, :]

    m0 = jnp.swapaxes(state_mlstm_m, 0, 1)
    c0 = jnp.swapaxes(_mlstm_state_in(state_mlstm_C, state_mlstm_n), 0, 1)
    x0 = jnp.concatenate([state_s5_re, state_s5_im, state_s5_im, state_s5_re], axis=-1)
    x0 = jnp.transpose(x0, (1, 2, 0, 3, 4)).reshape(DEPTH, 2, DEC_BATCH, S5_GROUPS * S5_ST2)

    finals = []
    for l in range(DEPTH):
        for p in (PROMPT, SAMPLE):
            x = xs[p]
            zq, zk, zo, zf, zu, vt, gt = _in_proj(p, l, x, mods, n1, w_cat, wv_t, wg_t, bg)
            pr = _gate_prep(p, gt)
            if p is PROMPT:
                fo = _fourier_prompt(l, zf, cdsd, cs, wf)
                mo, *fin = _mlstm(p, l, zq, zk, vt, zo, pr, nw, want_final=True)
                ys, xfin = _s5(p, l, zu, s5_t, s5_m, s5_n, s5_a, dt, want_final=True)
                finals.append(fin + [xfin])
            else:
                fo = _fourier_sample(l, zf, cdsd, ab, wf)
                mo, = _mlstm(p, l, zq, zk, vt, zo, pr, nw, init=(m0[l], c0[l]))
                ys, = _s5(p, l, zu, s5_t, s5_m, s5_n, s5_a, dt, x0=x0[l])
            xs[p] = _post(p, l, l == DEPTH - 1, x, mods, fo, mo, ys, wglu, wo, n2, wg, wu, wd, norm_f[None])

    cfin, nfin, mfin, xfin = (jnp.stack(parts, axis=1) for parts in zip(*finals))
    xfin = xfin.reshape(2, DEPTH, BATCH, S5_GROUPS, 2, S5_STATE)
    new_re, new_im = (jnp.transpose(xfin[:, :, :, :, i], (2, 1, 0, 3, 4)) for i in range(2))
    return (xs[PROMPT], xs[SAMPLE], cfin, nfin[:, :, :, :, 0], mfin[:, :, :, :, 0, 0], new_re, new_im)
```

```python
import collections
import functools
import math

import numpy as np
import jax
import jax.numpy as jnp
from jax import lax
from jax.experimental import pallas as pl
from jax.experimental.pallas import tpu as pltpu

F32 = jnp.float32
BF16 = jnp.bfloat16

D_MODEL = 1024
BATCH = 32
SEQ = 256
DEPTH = 2
DEC_BATCH = 4
DEC_SEQ = 2048
GRID_W = 64
FOURIER_W = 256
FOURIER_DH = 64
MLSTM_W = 384
MLSTM_HEADS = 4
MLSTM_DH = 96
S5_W = 384
S5_GROUP_CH = 16
S5_GROUPS = 24
S5_STATE = 64
N_GATES = 16
D_FF = 2816
EPS = 1e-6

LANES = 128
SUBLANES = 8
VMEM_LIMIT = 56 * 1024 * 1024
POST_VMEM_LIMIT = 60 * 1024 * 1024

HEAD_PAD = LANES
HEADS_W = MLSTM_HEADS * HEAD_PAD
N_AUG = MLSTM_DH
Z_W = 3 * HEADS_W + FOURIER_W + S5_W
MLSTM_CHUNK = 256
S5_CHUNK = 16
S5_IN = S5_CHUNK * S5_GROUP_CH
S5_ST = 2 * S5_STATE
S5_ST2 = 2 * S5_ST
S5_GB = LANES // S5_GROUP_CH
S5_NB = S5_W // LANES
S5_ROW = S5_GB * S5_IN
TB = 4
TT = 128
TROWS = TB * TT // S5_CHUNK
HPS = 4
N_MODS = 8
NEG = -1e30

Pass = collections.namedtuple("Pass", "nb seq mod_first mod_each")
PROMPT = Pass(BATCH, SEQ, DEC_BATCH, False)
SAMPLE = Pass(DEC_BATCH, DEC_SEQ, 0, True)
T_PASS = BATCH * SEQ
assert T_PASS == DEC_BATCH * DEC_SEQ
R_PASS = T_PASS // S5_CHUNK

_NT = (((1,), (1,)), ((), ()))
_TN = (((0,), (0,)), ((), ()))


def _cparams(sem):
    return pltpu.CompilerParams(dimension_semantics=sem, vmem_limit_bytes=VMEM_LIMIT)


def _full(a):
    return pl.BlockSpec(a.shape, lambda *_: (0,) * a.ndim)


def _layer(a, l):
    return pl.BlockSpec((None,) + a.shape[1:], lambda *_: (l,) + (0,) * (a.ndim - 1))


def _tile_mod_spec(p, l):
    if p.mod_each:
        return pl.BlockSpec((None, TB, 1, 6 * D_MODEL), lambda j, k: (l, p.mod_first // TB + j, 0, 0))
    return pl.BlockSpec((None, 1, 1, 6 * D_MODEL), lambda j, k: (l, p.mod_first, 0, 0))


def _log_sigmoid(x):
    return jnp.minimum(x, 0.0) - jnp.log1p(jnp.exp(-jnp.abs(x)))


def _tile_spec(w):
    return pl.BlockSpec((TB, TT, w), lambda j, k: (j, k, 0))


def _chunk_rows_spec(p):
    per_group = p.seq // TT
    return pl.BlockSpec((S5_NB, TROWS, S5_ROW), lambda j, k: (0, j * per_group + k, 0))


def _slab(b, c):
    tok = slice(b * TT + c * S5_CHUNK, b * TT + (c + 1) * S5_CHUNK)
    chk = slice((c * TB + b) * S5_CHUNK, (c * TB + b + 1) * S5_CHUNK)
    return tok, chk


def _ada_kernel(c_ref, w_ref, b_ref, o_ref):
    a = c_ref[...]
    a = (a * jax.nn.sigmoid(a)).astype(BF16)
    o_ref[0] = jnp.dot(a, w_ref[0].astype(BF16), preferred_element_type=F32) + b_ref[0]


def _ada(cc, w_ada, b_ada):
    tn = 1536
    return pl.pallas_call(
        _ada_kernel,
        grid=(DEPTH, 6 * D_MODEL // tn),
        in_specs=[pl.BlockSpec((N_MODS, D_MODEL), lambda l, j: (0, 0)),
                  pl.BlockSpec((1, D_MODEL, tn), lambda l, j: (l, 0, j)),
                  pl.BlockSpec((1, 1, tn), lambda l, j: (l, 0, j))],
        out_specs=pl.BlockSpec((1, N_MODS, tn), lambda l, j: (l, 0, j)),
        out_shape=jax.ShapeDtypeStruct((DEPTH, N_MODS, 6 * D_MODEL), F32),
        compiler_params=_cparams(("arbitrary", "arbitrary")),
        name="ada_mod",
    )(cc, w_ada, b_ada.reshape(DEPTH, 1, 6 * D_MODEL))


def _mod_norm(x3, mod, nw, first):
    sh = mod[:, :, first * D_MODEL:(first + 1) * D_MODEL]
    sc = mod[:, :, (first + 1) * D_MODEL:(first + 2) * D_MODEL]
    y = x3 * lax.rsqrt(jnp.mean(x3 * x3, axis=-1, keepdims=True) + EPS) * nw
    return (y * (1.0 + sc) + sh).reshape(TB * TT, D_MODEL).astype(BF16)


def _in_kernel(x_ref, mod_ref, nw_ref, w_ref, wv_ref, wg_ref, bg_ref,
               zq_ref, zk_ref, zo_ref, zf_ref, zu_ref, vt_ref, gt_ref, scr_ref):
    xn = _mod_norm(x_ref[...], mod_ref[...], nw_ref[...], 0)
    w_main = Z_W - S5_W
    zu = jnp.dot(xn, w_ref[:, w_main:], preferred_element_type=F32)
    for bl in range(S5_NB):
        zb = zu[:, bl * LANES:(bl + 1) * LANES]
        for b in range(TB):
            for c in range(TT // S5_CHUNK):
                tok, chk = _slab(b, c)
                scr_ref[bl, chk, :] = zb[tok]
        by_token = [scr_ref[bl, pl.ds(s, TROWS, stride=S5_CHUNK), :] for s in range(S5_CHUNK)]
        for gl in range(S5_GB):
            lo = gl * S5_GROUP_CH
            zu_ref[bl, :, gl * S5_IN:(gl + 1) * S5_IN] = jnp.concatenate(
                [x[:, lo:lo + S5_GROUP_CH] for x in by_token], axis=1)
    z = jnp.dot(xn, w_ref[:, :w_main], preferred_element_type=F32)
    o = 0
    for ref, w in ((zq_ref, HEADS_W), (zk_ref, HEADS_W), (zo_ref, HEADS_W), (zf_ref, FOURIER_W)):
        ref[...] = z[:, o:o + w].astype(BF16).reshape(TB, TT, w)
        o += w
    vt = lax.dot_general(wv_ref[...], xn, _NT, preferred_element_type=F32).astype(BF16)
    gt = lax.dot_general(wg_ref[...], xn, _NT, preferred_element_type=F32) + bg_ref[...]
    for b in range(TB):
        vt_ref[b] = vt[:, b * TT:(b + 1) * TT]
        gt_ref[b] = gt[:, b * TT:(b + 1) * TT]


def _in_proj(p, l, x, mods, nw, w, wv_t, wg_t, bg):
    chan = lambda c_: pl.BlockSpec((TB, c_, TT), lambda j, k: (j, 0, k))
    outs = [HEADS_W] * 3 + [FOURIER_W]
    return pl.pallas_call(
        _in_kernel,
        grid=(p.nb // TB, p.seq // TT),
        in_specs=[_tile_spec(D_MODEL), _tile_mod_spec(p, l),
                  _layer(nw, l), _layer(w, l), _layer(wv_t, l), _layer(wg_t, l), _layer(bg, l)],
        out_specs=[_tile_spec(w_) for w_ in outs] + [_chunk_rows_spec(p), chan(HEADS_W), chan(N_GATES)],
        out_shape=[jax.ShapeDtypeStruct((p.nb, p.seq, w_), BF16) for w_ in outs]
        + [jax.ShapeDtypeStruct((S5_NB, R_PASS, S5_ROW), F32),
           jax.ShapeDtypeStruct((p.nb, HEADS_W, p.seq), BF16),
           jax.ShapeDtypeStruct((p.nb, N_GATES, p.seq), F32)],
        scratch_shapes=[pltpu.VMEM((S5_NB, TB * TT, LANES), F32)],
        compiler_params=_cparams(("arbitrary", "arbitrary")),
        name="in_proj_%d" % p.seq,
    )(x, mods, nw, w, wv_t, wg_t, bg)


def _dft_consts():
    d = np.arange(FOURIER_DH)
    phi = 2.0 * np.pi * ((d[:, None] * d[None, :]) % FOURIER_DH) / FOURIER_DH
    eye = np.eye(FOURIER_W // FOURIER_DH)
    cd = np.kron(eye, np.cos(phi)) / math.sqrt(FOURIER_DH)
    sd = np.kron(eye, np.sin(phi)) / math.sqrt(FOURIER_DH)
    s = np.arange(SEQ)
    th = 2.0 * np.pi * ((s[:, None] * s[None, :]) % SEQ) / SEQ
    rows = DEC_SEQ // GRID_W
    pos = np.arange(DEC_SEQ)
    r, c = pos // GRID_W, pos % GRID_W
    ph = ((r[:, None] * r[None, :]) * (GRID_W // rows) + c[:, None] * c[None, :]) % GRID_W
    th2 = 2.0 * np.pi * ph / GRID_W
    return (np.concatenate([cd, sd], axis=1),
            np.concatenate([np.cos(th), -np.sin(th)], axis=1) / math.sqrt(SEQ),
            np.concatenate([np.cos(th2), -np.sin(th2)], axis=1) / math.sqrt(DEC_SEQ))


def _fourier_prompt_kernel(nb, zf_ref, cdsd_ref, cs_ref, wf_ref, o_ref):
    t = jnp.dot(zf_ref[...].reshape(nb * SEQ, FOURIER_W), cdsd_ref[...],
                preferred_element_type=F32).astype(BF16)
    fs = []
    for b in range(nb):
        tb = t[b * SEQ:(b + 1) * SEQ]
        st = jnp.concatenate([tb[:, :FOURIER_W], tb[:, FOURIER_W:]], axis=0)
        fs.append(jnp.dot(cs_ref[...], st, preferred_element_type=F32))
    for b in range(nb):
        o_ref[b] = jnp.dot(fs[b].astype(BF16), wf_ref[...], preferred_element_type=F32).astype(BF16)


def _fourier_prompt(l, zf, cdsd, cs, wf):
    nb = 4
    blk = pl.BlockSpec((nb, SEQ, FOURIER_W), lambda i: (i, 0, 0))
    return pl.pallas_call(
        functools.partial(_fourier_prompt_kernel, nb),
        grid=(BATCH // nb,),
        in_specs=[blk, _full(cdsd), _full(cs), _layer(wf, l)],
        out_specs=blk,
        out_shape=jax.ShapeDtypeStruct((BATCH, SEQ, FOURIER_W), BF16),
        compiler_params=_cparams(("arbitrary",)),
        name="fourier_prompt",
    )(zf, cdsd, cs, wf)


def _fourier_sample_kernel(zf_ref, cdsd_ref, ab_ref, wf_ref, o_ref, tt_ref):
    @pl.when(pl.program_id(0) == 0)
    def _():
        for b in range(DEC_BATCH):
            t = jnp.dot(zf_ref[b], cdsd_ref[...], preferred_element_type=F32).astype(BF16)
            tt_ref[b, 0:DEC_SEQ, :] = t[:, :FOURIER_W]
            tt_ref[b, DEC_SEQ:2 * DEC_SEQ, :] = t[:, FOURIER_W:]

    fs = [jnp.dot(ab_ref[...], tt_ref[b], preferred_element_type=F32) for b in range(DEC_BATCH)]
    for b in range(DEC_BATCH):
        o_ref[b] = jnp.dot(fs[b].astype(BF16), wf_ref[...], preferred_element_type=F32).astype(BF16)


def _fourier_sample(l, zf, cdsd, ab, wf):
    tk = 512
    return pl.pallas_call(
        _fourier_sample_kernel,
        grid=(DEC_SEQ // tk,),
        in_specs=[_full(zf), _full(cdsd), pl.BlockSpec((tk, 2 * DEC_SEQ), lambda i: (i, 0)), _layer(wf, l)],
        out_specs=pl.BlockSpec((DEC_BATCH, tk, FOURIER_W), lambda i: (0, i, 0)),
        out_shape=jax.ShapeDtypeStruct((DEC_BATCH, DEC_SEQ, FOURIER_W), BF16),
        scratch_shapes=[pltpu.VMEM((DEC_BATCH, 2 * DEC_SEQ, FOURIER_W), BF16)],
        compiler_params=_cparams(("arbitrary",)),
        name="fourier_sample",
    )(zf, cdsd, ab, wf)


def _split3(x):
    hi = x.astype(BF16).astype(F32)
    mid = (x - hi).astype(BF16).astype(F32)
    lo = (x - hi - mid).astype(BF16).astype(F32)
    return hi, mid, lo


def _gate_kernel(g_ref, o_ref):
    L = MLSTM_CHUNK
    nrow = N_GATES // 2
    row = lax.broadcasted_iota(jnp.int32, (L, L), 0)
    col = lax.broadcasted_iota(jnp.int32, (L, L), 1)
    tri_pre = jnp.where(row <= col, 1.0, 0.0).astype(BF16)
    tri_suf = jnp.where(row >= col, 1.0, 0.0).astype(BF16)
    is_fwd = lax.broadcasted_iota(jnp.int32, (nrow, L), 0) < MLSTM_HEADS
    lane = lax.broadcasted_iota(jnp.int32, (nrow, L), 1)
    chunks = [(bi, slice(c * L, (c + 1) * L)) for bi in range(g_ref.shape[0]) for c in range(g_ref.shape[2] // L)]
    fold = lambda a: a[0:nrow] + a[nrow:2 * nrow] + a[2 * nrow:]
    bs, rs = [], []
    for bi, cols in chunks:
        lf = _log_sigmoid(g_ref[bi, nrow:, cols])
        parts = jnp.concatenate(_split3(lf), axis=0).astype(BF16)
        pre = jnp.dot(parts, tri_pre, preferred_element_type=F32)
        suf = jnp.dot(parts, tri_suf, preferred_element_type=F32)
        bs.append(jnp.where(is_fwd, fold(pre), fold(suf)))
        rs.append(g_ref[bi, 0:nrow, cols] - bs[-1])
    pms, sms = list(rs), list(rs)
    sh = 1
    while sh < L:
        pms = [jnp.maximum(x, jnp.where(lane >= sh, pltpu.roll(x, sh, 1), NEG)) for x in pms]
        sms = [jnp.maximum(x, jnp.where(lane < L - sh, pltpu.roll(x, L - sh, 1), NEG)) for x in sms]
        sh *= 2
    for (bi, cols), b, r, pm, sm in zip(chunks, bs, rs, pms, sms):
        for q, val in enumerate((b, r, jnp.where(is_fwd, pm, sm))):
            for dh in range(nrow):
                o_ref[bi, dh, q:q + 1, cols] = val[dh:dh + 1]


def _gate_prep(p, gt):
    bb = max(1, DEC_SEQ // p.seq)
    nrow = N_GATES // 2
    return pl.pallas_call(
        _gate_kernel,
        grid=(p.nb // bb,),
        in_specs=[pl.BlockSpec((bb, N_GATES, p.seq), lambda i: (i, 0, 0))],
        out_specs=pl.BlockSpec((bb, nrow, 3, p.seq), lambda i: (i, 0, 0, 0)),
        out_shape=jax.ShapeDtypeStruct((p.nb, nrow, 3, p.seq), F32),
        compiler_params=_cparams(("arbitrary",)),
        name="gate_prep_%d" % p.seq,
    )(gt)


def _mlstm_chunk(q, k, vt, pr, ct, m, fwd, out):
    L = q.shape[0]
    scale = MLSTM_DH ** -0.5
    b, r, cm = pr[0:1], pr[1:2], pr[2:3]
    ones = jnp.ones((3, L), F32)
    zeros = jnp.zeros((SUBLANES - 6, L), F32)
    lhs = jnp.concatenate(_split3(r) + (ones, zeros), axis=0).astype(BF16)
    rhs = jnp.concatenate((ones,) + _split3(-cm) + (zeros,), axis=0).astype(BF16)
    arg = lax.dot_general(lhs, rhs, _TN, preferred_element_type=F32)
    st = lax.dot_general(k, q, _NT, preferred_element_type=F32)
    cq = lax.dot_general(ct.astype(BF16), q, _NT, preferred_element_type=F32)
    last = L - 1 if fwd else 0
    cm_last = cm[:, last:last + 1]
    mx_last = jnp.maximum(m, cm_last)
    vw = (vt.astype(F32) * jnp.exp(r - cm_last)).astype(BF16)
    dct = jnp.dot(vw, k, preferred_element_type=F32)
    yield

    row = lax.broadcasted_iota(jnp.int32, (L, L), 0)
    col = lax.broadcasted_iota(jnp.int32, (L, L), 1)
    e = jnp.where((row <= col) if fwd else (row >= col), jnp.exp(arg), 0.0)
    num = jnp.dot(vt, (st * e).astype(BF16), preferred_element_type=F32)
    ct_new = jnp.exp(m - mx_last) * ct + (jnp.exp(cm_last - mx_last) * scale) * dct
    yield

    mx = jnp.maximum(m, cm)
    num = (jnp.exp(cm - mx) * scale) * num + jnp.exp(m - mx) * cq
    den = num[N_AUG:N_AUG + 1, :]
    h = num * (1.0 / jnp.maximum(jnp.abs(den), jnp.exp(-(b + mx))))
    out += [h, ct_new, b[:, last:last + 1] + mx_last]


def _mlstm_kernel(nc, has_init, want_final, *refs):
    refs = list(refs)
    m0_ref, c0_ref = (refs.pop(0), refs.pop(0)) if has_init else (None, None)
    q_ref, k_ref, vt_ref, zo_ref, prf_ref, prb_ref, nw_ref, o_ref = refs[:8]
    hbuf_ref = refs[-1]
    L = MLSTM_CHUNK
    bi = pl.program_id(0)
    hg = pl.program_id(1)
    vrow = lax.broadcasted_iota(jnp.int32, (HEAD_PAD, L), 0)

    def chunk(hh, ci, pr_ref, ct, m, dr, out):
        rows = pl.ds(pl.multiple_of(ci * L, L), L)
        lanes = slice(hh * HEAD_PAD, (hh + 1) * HEAD_PAD)
        vt = vt_ref[lanes, rows]
        vt = jnp.where(vrow == N_AUG, jnp.ones_like(vt), vt)
        res = []
        yield from _mlstm_chunk(q_ref[rows, lanes], k_ref[rows, lanes], vt, pr_ref[hh, :, rows], ct, m,
                                dr == 0, res)
        hbuf_ref[hh, dr, ci] = res[0]
        out += res[1:]

    def step(i, carry):
        outs = [[] for _ in range(2 * HPS)]
        gens = []
        for hh in range(HPS):
            cf, mf, cb, mb = carry[4 * hh:4 * hh + 4]
            gens += [chunk(hh, i, prf_ref, cf, mf, 0, outs[2 * hh]),
                     chunk(hh, nc - 1 - i, prb_ref, cb, mb, 1, outs[2 * hh + 1])]
        while gens:
            alive = []
            for g in gens:
                try:
                    next(g)
                    alive.append(g)
                except StopIteration:
                    pass
            gens = alive
        return tuple(x for o in outs for x in o)

    carry = []
    for hh in range(HPS):
        for dr in range(2):
            if has_init:
                carry += [c0_ref[0, dr, hh].T, jnp.full((1, 1), m0_ref[bi, dr, hg * HPS + hh], F32)]
            else:
                carry += [jnp.zeros((HEAD_PAD, HEAD_PAD), F32), jnp.zeros((1, 1), F32)]
    carry = step(0, tuple(carry)) if nc == 1 else lax.fori_loop(0, nc, step, tuple(carry))
    if want_final:
        cfin_ref, nfin_ref, mfin_ref = refs[8:11]
        for hh in range(HPS):
            for dr in range(2):
                ct, m = carry[4 * hh + 2 * dr], carry[4 * hh + 2 * dr + 1]
                cfin_ref[0, dr, hh] = ct.T[:MLSTM_DH, :MLSTM_DH]
                nfin_ref[0, dr, hh] = ct[N_AUG:N_AUG + 1, :MLSTM_DH]
                mfin_ref[0, dr, hh] = jnp.broadcast_to(m, (SUBLANES, LANES))

    def finish(ci, _):
        rows = pl.ds(pl.multiple_of(ci * L, L), L)
        for hh in range(HPS):
            lanes = slice(hh * HEAD_PAD, (hh + 1) * HEAD_PAD)
            h = jnp.where(vrow < MLSTM_DH, hbuf_ref[hh, 0, ci] + hbuf_ref[hh, 1, ci], 0.0)
            ms = jnp.sum(h * h, axis=0, keepdims=True) * (1.0 / MLSTM_DH)
            hn = (h * lax.rsqrt(ms + EPS) * nw_ref[hh]).T
            o_ref[rows, lanes] = (hn * jax.nn.sigmoid(zo_ref[rows, lanes].astype(F32))).astype(BF16)
        return 0

    if nc == 1:
        finish(0, 0)
    else:
        lax.fori_loop(0, nc, finish, 0)


def _mlstm(p, l, q, k, vt, zo, pr, nw, init=None, want_final=False):
    nc = p.seq // MLSTM_CHUNK
    tok = pl.BlockSpec((None, p.seq, HPS * HEAD_PAD), lambda b, h: (b, 0, h))
    st = lambda r, c: pl.BlockSpec((1, 2, HPS, r, c), lambda b, h: (b, 0, h, 0, 0))
    prs = lambda dr: pl.BlockSpec((None, HPS, 3, p.seq), lambda b, h: (b, dr * (MLSTM_HEADS // HPS) + h, 0, 0))
    in_specs, args = [], []
    if init is not None:
        in_specs += [pl.BlockSpec(memory_space=pltpu.SMEM), st(HEAD_PAD, HEAD_PAD)]
        args += list(init)
    in_specs += [tok, tok, pl.BlockSpec((None, HPS * HEAD_PAD, p.seq), lambda b, h: (b, h, 0)), tok, prs(0), prs(1),
                 pl.BlockSpec((None, HPS, HEAD_PAD, 1), lambda b, h: (l, h, 0, 0))]
    args += [q, k, vt, zo, pr, pr, nw]
    out_specs = [tok]
    out_shape = [jax.ShapeDtypeStruct((p.nb, p.seq, HEADS_W), BF16)]
    if want_final:
        out_specs += [st(MLSTM_DH, MLSTM_DH), st(1, MLSTM_DH), st(SUBLANES, LANES)]
        out_shape += [jax.ShapeDtypeStruct((p.nb, 2, MLSTM_HEADS, r, c), F32)
                      for r, c in ((MLSTM_DH, MLSTM_DH), (1, MLSTM_DH), (SUBLANES, LANES))]
    return pl.pallas_call(
        functools.partial(_mlstm_kernel, nc, init is not None, want_final),
        grid=(p.nb, MLSTM_HEADS // HPS),
        in_specs=in_specs,
        out_specs=out_specs,
        out_shape=out_shape,
        scratch_shapes=[pltpu.VMEM((HPS, 2, nc, HEAD_PAD, MLSTM_CHUNK), F32)],
        compiler_params=_cparams(("arbitrary", "arbitrary")),
        name="mlstm_%d" % p.seq,
    )(*args)


def _cpow(br, bi, e, nbits):
    pr = pi = None
    for bit in range(nbits):
        sel = ((e >> bit) & 1) == 1
        if pr is None:
            pr, pi = jnp.where(sel, br, 1.0), jnp.where(sel, bi, 0.0)
        else:
            pr, pi = jnp.where(sel, pr * br - pi * bi, pr), jnp.where(sel, pr * bi + pi * br, pi)
        if bit + 1 < nbits:
            br, bi = br * br - bi * bi, 2.0 * br * bi
    return pr, pi


def _s5_prep_kernel(lamc_re_ref, lamc_im_ref, lamr_re_ref, lamr_im_ref, lstep_ref,
                    bt_re_ref, bt_im_ref, ct_re_ref, ct_im_ref,
                    t_ref, m_ref, n_ref, a_ref):
    C = S5_CHUNK
    nbits = (C - 1).bit_length()
    assert C == 1 << nbits
    kk = lax.broadcasted_iota(jnp.int32, (S5_STATE, S5_IN), 1) >> 4
    srow = lax.broadcasted_iota(jnp.int32, (S5_IN, S5_ST), 0) >> 4
    left = lax.broadcasted_iota(jnp.int32, (S5_IN, S5_ST), 1) < S5_STATE
    left16 = lax.broadcasted_iota(jnp.int32, (S5_GROUP_CH, S5_ST), 1) < S5_STATE
    left1 = lax.broadcasted_iota(jnp.int32, (1, S5_ST), 1) < S5_STATE
    lane = lax.broadcasted_iota(jnp.int32, (S5_GROUP_CH, S5_IN), 1)
    sel = jnp.where((lane & (S5_GROUP_CH - 1)) == lax.broadcasted_iota(jnp.int32, (S5_GROUP_CH, S5_IN), 0),
                    1.0, 0.0)
    spread = lambda ref: jnp.dot(ref[0, 0], sel, precision=lax.Precision.HIGHEST, preferred_element_type=F32)
    ct_re = spread(ct_re_ref)
    ct_im = spread(ct_im_ref)
    resp = []
    for d in range(2):
        step = jnp.exp(lstep_ref[0, d, 0])
        lr_c, li_c = lamc_re_ref[0, d, 0] * step, lamc_im_ref[0, d, 0] * step
        lr_r, li_r = lamr_re_ref[0, d, 0], lamr_im_ref[0, d, 0]
        lbc_re, lbc_im = jnp.exp(lr_c) * jnp.cos(li_c), jnp.exp(lr_c) * jnp.sin(li_c)
        mag = jnp.exp(lr_r * step)
        lb_re, lb_im = mag * jnp.cos(li_r * step), mag * jnp.sin(li_r * step)

        pr, pi = _cpow(lbc_re, lbc_im, kk if d == 0 else (C - 1) - kk, nbits)
        pr1, pi1 = pr * lbc_re - pi * lbc_im, pr * lbc_im + pi * lbc_re
        cpr, cpi = ct_re * pr - ct_im * pi, ct_re * pi + ct_im * pr
        cpr1, cpi1 = ct_re * pr1 - ct_im * pi1, ct_re * pi1 + ct_im * pr1

        nr, ni = lb_re - 1.0, lb_im
        den = lr_r * lr_r + li_r * li_r
        kap_re = (nr * lr_r + ni * li_r) / den
        kap_im = (ni * lr_r - nr * li_r) / den
        bb_re = kap_re * bt_re_ref[0, 0] - kap_im * bt_im_ref[0, 0]
        bb_im = kap_re * bt_im_ref[0, 0] + kap_im * bt_re_ref[0, 0]

        resp.append(jnp.dot(jnp.where(left16, bb_re, -bb_im), jnp.concatenate([cpr, cpi], axis=0),
                            precision=lax.Precision.HIGHEST, preferred_element_type=F32))
        m_ref[0, d, 0] = jnp.concatenate([cpr1, -cpi1], axis=0).astype(BF16)

        pr, pi = _cpow(lb_re, lb_im, (C - 1) - srow if d == 0 else srow, nbits)
        bt_r = jnp.concatenate([bb_re] * C, axis=0)
        bt_i = jnp.concatenate([bb_im] * C, axis=0)
        n_re, n_im = pr * bt_r - pi * bt_i, pr * bt_i + pi * bt_r
        n_ref[0, d, 0] = jnp.concatenate([jnp.where(left, n_re, n_im), jnp.where(left, n_im, n_re)],
                                         axis=1).astype(BF16)

        ar, ai = lb_re, lb_im
        for _ in range(nbits):
            ar, ai = ar * ar - ai * ai, 2.0 * ar * ai
        a2 = jnp.where(left1, -ai, ai)
        a_ref[0, d, 0] = jnp.concatenate([jnp.concatenate([ar, ar], axis=1),
                                          jnp.concatenate([a2, -a2], axis=1)], axis=0)

    rf, rb = resp
    for s in range(C):
        nf = S5_GROUP_CH * s
        blk = jnp.where(lane >= nf, pltpu.roll(rf, nf, 1) if nf else rf, 0.0)
        nb = S5_GROUP_CH * (C - 1 - s)
        blk = blk + jnp.where(lane < S5_IN - nb, pltpu.roll(rb, S5_IN - nb, 1) if nb else rb, 0.0)
        t_ref[0, 0, S5_GROUP_CH * s:S5_GROUP_CH * (s + 1), :] = blk.astype(BF16)


def _s5_prep(lam_re, lam_im, log_step, b_re, b_im, c_re, c_im):
    G = S5_GROUPS
    dup = lambda a: jnp.concatenate([a, a], axis=-1)
    lamc = [a.reshape(DEPTH, 2, G, S5_STATE, 1) for a in (lam_re, lam_im)]
    lamr = [dup(a).reshape(DEPTH, 2, G, 1, S5_ST) for a in (lam_re, lam_im)]
    lstep = log_step.reshape(DEPTH, 2, G, 1, 1)
    bt = [dup(jnp.swapaxes(a, 2, 3)) for a in (b_re, b_im)]
    ct = [jnp.swapaxes(a, 2, 3) for a in (c_re, c_im)]
    dspec = lambda r, c: pl.BlockSpec((1, 2, 1, r, c), lambda l, g: (l, 0, g, 0, 0))
    gspec = lambda r, c: pl.BlockSpec((1, 1, r, c), lambda l, g: (l, g, 0, 0))
    t, m, n, a = pl.pallas_call(
        _s5_prep_kernel,
        grid=(DEPTH, G),
        in_specs=[dspec(S5_STATE, 1), dspec(S5_STATE, 1), dspec(1, S5_ST), dspec(1, S5_ST), dspec(1, 1),
                  gspec(S5_GROUP_CH, S5_ST), gspec(S5_GROUP_CH, S5_ST),
                  gspec(S5_STATE, S5_GROUP_CH), gspec(S5_STATE, S5_GROUP_CH)],
        out_specs=[gspec(S5_IN, S5_IN), dspec(S5_ST, S5_IN), dspec(S5_IN, S5_ST2), dspec(2, S5_ST2)],
        out_shape=[jax.ShapeDtypeStruct((DEPTH, G, S5_IN, S5_IN), BF16),
                   jax.ShapeDtypeStruct((DEPTH, 2, G, S5_ST, S5_IN), BF16),
                   jax.ShapeDtypeStruct((DEPTH, 2, G, S5_IN, S5_ST2), BF16),
                   jax.ShapeDtypeStruct((DEPTH, 2, G, 2, S5_ST2), F32)],
        compiler_params=_cparams(("arbitrary", "arbitrary")),
        name="s5_prep",
    )(*lamc, *lamr, lstep, *bt, *ct)
    return t, m, n, jnp.transpose(a, (0, 1, 3, 2, 4)).reshape(DEPTH, 2, 2, G * S5_ST2)


def _s5_kernel(nseg, nchunks, zr_ref, t_ref, m_ref, n_ref, a_ref, x0_ref, d_ref,
               y_ref, xfin_ref, v_ref, xp_ref):
    W = S5_GB * S5_ST2
    for gl in range(S5_GB):
        u = zr_ref[0, :, gl * S5_IN:(gl + 1) * S5_IN].astype(BF16)
        for d in range(2):
            v_ref[d, :, gl * S5_ST2:(gl + 1) * S5_ST2] = jnp.dot(
                u, n_ref[d, gl], preferred_element_type=F32)

    def halves(x, which):
        return [x[:, g * S5_ST2 + h * S5_ST:g * S5_ST2 + (h + 1) * S5_ST]
                for g in range(S5_GB) for h in which]

    per = SUBLANES // TB
    steps = nchunks // per

    for d in range(2):
        a = a_ref[d, 0:1, :]
        a2 = a_ref[d, 1:2, :]

        def advance(x, v):
            swapped = jnp.concatenate(halves(x, (1, 0)), axis=1)
            return a * x + a2 * swapped + v

        for seg in range(nseg):
            def step(i, x):
                si = i if d == 0 else steps - 1 - i
                rows = pl.ds(pl.multiple_of(seg * nchunks * TB + si * SUBLANES, SUBLANES), SUBLANES)
                v = v_ref[d, rows, :]
                order = range(per) if d == 0 else range(per - 1, -1, -1)
                entering = [None] * per
                for j in order:
                    entering[j] = jnp.concatenate(halves(x, (0,)), axis=1)
                    x = advance(x, v[j * TB:(j + 1) * TB])
                xp_ref[d, rows, :] = jnp.concatenate(entering, axis=0)
                return x

            x0 = jnp.zeros((TB, W), F32) if x0_ref is None else x0_ref[d, seg * TB:(seg + 1) * TB, :]
            x = lax.fori_loop(0, steps, step, x0)
            if xfin_ref is not None:
                xfin_ref[d, seg * TB:(seg + 1) * TB, :] = jnp.concatenate(halves(x, (0,)), axis=1)

    for gl in range(S5_GB):
        cols = slice(gl * S5_IN, (gl + 1) * S5_IN)
        u = zr_ref[0, :, cols]
        y = jnp.dot(u.astype(BF16), t_ref[gl], preferred_element_type=F32)
        for d in range(2):
            y = y + jnp.dot(xp_ref[d, :, gl * S5_ST:(gl + 1) * S5_ST].astype(BF16), m_ref[d, gl],
                            preferred_element_type=F32)
        y_ref[0, :, cols] = jax.nn.gelu(y + d_ref[0, :, cols] * u)


def _s5(p, l, zr, t, m, n, a, dt, x0=None, want_final=False):
    G = S5_GROUPS
    nseg = p.nb // TB
    row = pl.BlockSpec((1, R_PASS, S5_ROW), lambda j: (j, 0, 0))
    dsp = lambda r, c: pl.BlockSpec((None, 2, S5_GB, r, c), lambda j: (l, 0, j, 0, 0))
    lsp = lambda r, w: pl.BlockSpec((2, r, S5_GB * w), lambda j: (0, 0, j))
    in_specs = [row, pl.BlockSpec((None, S5_GB, S5_IN, S5_IN), lambda j: (l, j, 0, 0)),
                dsp(S5_ST, S5_IN), dsp(S5_IN, S5_ST2),
                pl.BlockSpec((None, 2, 2, S5_GB * S5_ST2), lambda j: (l, 0, 0, j)),
                pl.BlockSpec((None, 1, 1, S5_ROW), lambda j: (l, j, 0, 0))]
    args = [zr, t, m, n, a, dt]
    if x0 is not None:
        in_specs.append(lsp(p.nb, S5_ST2))
        args.append(x0)
    out_specs = [row]
    out_shape = [jax.ShapeDtypeStruct((S5_NB, R_PASS, S5_ROW), F32)]
    if want_final:
        out_specs.append(lsp(p.nb, S5_ST))
        out_shape.append(jax.ShapeDtypeStruct((2, p.nb, G * S5_ST), F32))

    def body(zr_ref, t_ref, m_ref, n_ref, a_ref, d_ref, *rest):
        rest = list(rest)
        x0_ref = rest.pop(0) if x0 is not None else None
        y_ref = rest.pop(0)
        xfin_ref = rest.pop(0) if want_final else None
        _s5_kernel(nseg, p.seq // S5_CHUNK, zr_ref, t_ref, m_ref, n_ref, a_ref, x0_ref, d_ref,
                   y_ref, xfin_ref, *rest)

    return pl.pallas_call(
        body,
        grid=(S5_NB,),
        in_specs=in_specs,
        out_specs=out_specs,
        out_shape=out_shape,
        scratch_shapes=[pltpu.VMEM((2, R_PASS, S5_GB * S5_ST2), F32),
                        pltpu.VMEM((2, R_PASS, S5_GB * S5_ST), F32)],
        compiler_params=_cparams(("arbitrary",)),
        name="s5_scan_%d" % p.seq,
    )(*args)


def _mix_stages(x_ref, mod_ref, fo_ref, mo_ref, ys_ref, wglu_ref, wo_ref, nw_ref, scr_ref, out):
    mod = mod_ref[...]
    g1 = mod[:, :, 2 * D_MODEL:3 * D_MODEL]
    flat = lambda ref: ref[...].reshape(TB * TT, ref.shape[-1])
    o_m, o_s = FOURIER_W, FOURIER_W + HEADS_W
    mix = (jnp.dot(flat(fo_ref), wo_ref[:o_m, :], preferred_element_type=F32)
           + jnp.dot(flat(mo_ref), wo_ref[o_m:o_s, :], preferred_element_type=F32))
    yield
    blocks = []
    for bl in range(S5_NB):
        by_group = [ys_ref[bl, :, gl * S5_IN:(gl + 1) * S5_IN] for gl in range(S5_GB)]
        for t in range(S5_CHUNK):
            lo = t * S5_GROUP_CH
            scr_ref[bl, pl.ds(t, TROWS, stride=S5_CHUNK), :] = jnp.concatenate(
                [y[:, lo:lo + S5_GROUP_CH] for y in by_group], axis=1)
        slabs = [scr_ref[bl, _slab(b, c)[1], :] for b in range(TB) for c in range(TT // S5_CHUNK)]
        blocks.append(jnp.concatenate(slabs, axis=0))
        yield
    y = jnp.concatenate(blocks, axis=1).astype(BF16)
    gg = jnp.dot(y, wglu_ref[...], preferred_element_type=F32)
    s_out = (gg[:, :S5_W] * jax.nn.sigmoid(gg[:, S5_W:])).astype(BF16)
    mix = mix + jnp.dot(s_out, wo_ref[o_s:, :], preferred_element_type=F32)
    yield
    x1 = x_ref[...] + g1 * mix.reshape(TB, TT, D_MODEL)
    out += [x1, _mod_norm(x1, mod, nw_ref[...], 3)]


FF_SPLIT = (768, 768, 640, 640)
assert sum(FF_SPLIT) == D_FF


def _ffn_stages(final, xn, x1, mod_ref, wg_ref, wu_ref, wd_ref, nf_ref, out):
    ff = None
    o = 0
    for w in FF_SPLIT:
        cols = slice(o, o + w)
        o += w
        a = jnp.dot(xn, wg_ref[:, cols], preferred_element_type=F32)
        u = jnp.dot(xn, wu_ref[:, cols], preferred_element_type=F32)
        h = (a * jax.nn.sigmoid(a) * u).astype(BF16)
        part = jnp.dot(h, wd_ref[cols, :], preferred_element_type=F32)
        ff = part if ff is None else ff + part
        yield
    g2 = mod_ref[...][:, :, 5 * D_MODEL:6 * D_MODEL]
    x2 = x1 + g2 * ff.reshape(TB, TT, D_MODEL)
    if final:
        x2 = x2 * lax.rsqrt(jnp.mean(x2 * x2, axis=-1, keepdims=True) + EPS) * nf_ref[...]
    out.append(x2)


def _post_kernel(final, n, x_ref, moda_ref, fo_ref, mo_ref, ys_ref, modb_ref, wglu_ref, wo_ref, nw_ref,
                 wg_ref, wu_ref, wd_ref, nf_ref, o_ref, x1_ref, xn_ref, scr_ref):
    i = pl.program_id(0)
    cur = i % 2
    prev = 1 - cur

    def run(do_ffn, do_mix):
        res_a, res_b, gens = [], [], []
        if do_ffn:
            gens.append(_ffn_stages(final, xn_ref[prev], x1_ref[prev], modb_ref, wg_ref, wu_ref, wd_ref,
                                    nf_ref, res_b))
        if do_mix:
            gens.append(_mix_stages(x_ref, moda_ref, fo_ref, mo_ref, ys_ref, wglu_ref, wo_ref, nw_ref,
                                    scr_ref, res_a))
        while gens:
            alive = []
            for g in gens:
                try:
                    next(g)
                    alive.append(g)
                except StopIteration:
                    pass
            gens = alive
        if do_ffn:
            o_ref[...] = res_b[0]
        if do_mix:
            x1_ref[cur] = res_a[0]
            xn_ref[cur] = res_a[1]

    pl.when(i == 0)(lambda: run(False, True))
    pl.when(jnp.logical_and(i > 0, i < n))(lambda: run(True, True))
    pl.when(i == n)(lambda: run(True, False))


def _post(p, l, final, x, mods, fo, mo, ys, wglu, wo, nw, wg, wu, wd, nf):
    nk = p.seq // TT
    n = (p.nb // TB) * nk
    tile_a = lambda i: jnp.minimum(i, n - 1)
    tile_b = lambda i: jnp.maximum(i - 1, 0)

    def tok(w, tile):
        return pl.BlockSpec((TB, TT, w), lambda i: (tile(i) // nk, tile(i) % nk, 0))

    def mod(tile):
        if p.mod_each:
            return pl.BlockSpec((None, TB, 1, 6 * D_MODEL), lambda i: (l, p.mod_first // TB + tile(i) // nk, 0, 0))
        return pl.BlockSpec((None, 1, 1, 6 * D_MODEL), lambda i: (l, p.mod_first, 0, 0))

    once = lambda a: pl.BlockSpec((None,) + a.shape[1:], lambda i: (l,) + (0,) * (a.ndim - 1),
                                  pipeline_mode=pl.Buffered(1))
    return pl.pallas_call(
        functools.partial(_post_kernel, final, n),
        grid=(n + 1,),
        in_specs=[tok(D_MODEL, tile_a), mod(tile_a), tok(FOURIER_W, tile_a), tok(HEADS_W, tile_a),
                  pl.BlockSpec((S5_NB, TROWS, S5_ROW), lambda i: (0, tile_a(i), 0)), mod(tile_b),
                  once(wglu), once(wo), once(nw), once(wg), once(wu), once(wd),
                  pl.BlockSpec((1, D_MODEL), lambda i: (0, 0))],
        out_specs=tok(D_MODEL, tile_b),
        out_shape=jax.ShapeDtypeStruct((p.nb, p.seq, D_MODEL), F32),
        scratch_shapes=[pltpu.VMEM((2, TB, TT, D_MODEL), F32), pltpu.VMEM((2, TB * TT, D_MODEL), BF16),
                        pltpu.VMEM((S5_NB, TB * TT, LANES), F32)],
        compiler_params=pltpu.CompilerParams(dimension_semantics=("arbitrary",),
                                             vmem_limit_bytes=POST_VMEM_LIMIT),
        name="post_%d" % p.seq,
    )(x, mods, fo, mo, ys, mods, wglu, wo, nw, wg, wu, wd, nf)


def _pad_heads(a, axis):
    shape = a.shape[:axis] + (MLSTM_HEADS, MLSTM_DH) + a.shape[axis + 1:]
    pad = [(0, 0)] * (a.ndim + 1)
    pad[axis + 1] = (0, HEAD_PAD - MLSTM_DH)
    return jnp.pad(a.reshape(shape), pad).reshape(a.shape[:axis] + (HEADS_W,) + a.shape[axis + 1:])


def _mlstm_state_in(c, n):
    cn = jnp.concatenate([c, n[..., None]], axis=-1)
    return jnp.pad(cn, ((0, 0),) * (c.ndim - 2) + ((0, HEAD_PAD - MLSTM_DH), (0, HEAD_PAD - MLSTM_DH - 1)))


def kernel(x_prompt, x_sample, state_mlstm_C, state_mlstm_n, state_mlstm_m, state_s5_re, state_s5_im,
           c, c_ctx, w_ada, b_ada, norm1_w, norm2_w, w_in, b_gates, w_fourier, mlstm_norm_w,
           s5_lambda_re, s5_lambda_im, s5_log_step, s5_b_re, s5_b_im, s5_c_re, s5_c_im, s5_d,
           w_glu, w_out, w_gate, w_up, w_down, norm_f):
    xs = {PROMPT: x_prompt, SAMPLE: x_sample}
    cc = jnp.concatenate([c, c_ctx[None], jnp.zeros((N_MODS - 1 - DEC_BATCH, D_MODEL), F32)], axis=0)
    mods = _ada(cc, w_ada, b_ada).reshape(DEPTH, N_MODS, 1, 6 * D_MODEL)
    cdsd, cs, ab = (jnp.asarray(a.astype(np.float32)).astype(BF16) for a in _dft_consts())
    s5_t, s5_m, s5_n, s5_a = _s5_prep(s5_lambda_re, s5_lambda_im, s5_log_step, s5_b_re, s5_b_im,
                                      s5_c_re, s5_c_im)

    o_q = FOURIER_W
    o_g = o_q + 3 * MLSTM_W
    o_o = o_g + N_GATES
    o_u = o_o + MLSTM_W
    heads = lambda o: _pad_heads(w_in[:, :, o:o + MLSTM_W], 2)
    w_cat = jnp.concatenate([heads(o_q), heads(o_q + MLSTM_W), heads(o_o), w_in[:, :, :FOURIER_W],
                             w_in[:, :, o_u:]], axis=2).astype(BF16)
    wv_t = jnp.swapaxes(heads(o_q + 2 * MLSTM_W), 1, 2).astype(BF16)
    gate_perm = np.arange(N_GATES).reshape(2, 2, MLSTM_HEADS).transpose(1, 0, 2).reshape(-1)
    wg_t = jnp.swapaxes(w_in[:, :, o_g:o_o], 1, 2)[:, gate_perm].astype(BF16)
    bg = b_gates[:, gate_perm, None]
    wf = w_fourier.astype(BF16)
    nw = _pad_heads(mlstm_norm_w, 1).reshape(DEPTH, MLSTM_HEADS, HEAD_PAD, 1)
    dt = jnp.tile(s5_d[:, :, None, :], (1, 1, S5_CHUNK, 1)).reshape(DEPTH, S5_NB, 1, S5_ROW)
    wo_m = jnp.pad(w_out[:, FOURIER_W:FOURIER_W + MLSTM_W].reshape(DEPTH, MLSTM_HEADS, MLSTM_DH, D_MODEL),
                   ((0, 0), (0, 0), (0, HEAD_PAD - MLSTM_DH), (0, 0))).reshape(DEPTH, HEADS_W, D_MODEL)
    wo = jnp.concatenate([w_out[:, :FOURIER_W], wo_m, w_out[:, FOURIER_W + MLSTM_W:]], axis=1).astype(BF16)
    wglu = w_glu.astype(BF16)
    wg, wu, wd = w_gate.astype(BF16), w_up.astype(BF16), w_down.astype(BF16)
    n1, n2 = norm1_w[:, None, :], norm2_w[:, None, :]

    m0 = jnp.swapaxes(state_mlstm_m, 0, 1)
    c0 = jnp.swapaxes(_mlstm_state_in(state_mlstm_C, state_mlstm_n), 0, 1)
    x0 = jnp.concatenate([state_s5_re, state_s5_im, state_s5_im, state_s5_re], axis=-1)
    x0 = jnp.transpose(x0, (1, 2, 0, 3, 4)).reshape(DEPTH, 2, DEC_BATCH, S5_GROUPS * S5_ST2)

    finals = []
    for l in range(DEPTH):
        for p in (PROMPT, SAMPLE):
            x = xs[p]
            zq, zk, zo, zf, zu, vt, gt = _in_proj(p, l, x, mods, n1, w_cat, wv_t, wg_t, bg)
            pr = _gate_prep(p, gt)
            if p is PROMPT:
                fo = _fourier_prompt(l, zf, cdsd, cs, wf)
                mo, *fin = _mlstm(p, l, zq, zk, vt, zo, pr, nw, want_final=True)
                ys, xfin = _s5(p, l, zu, s5_t, s5_m, s5_n, s5_a, dt, want_final=True)
                finals.append(fin + [xfin])
            else:
                fo = _fourier_sample(l, zf, cdsd, ab, wf)
                mo, = _mlstm(p, l, zq, zk, vt, zo, pr, nw, init=(m0[l], c0[l]))
                ys, = _s5(p, l, zu, s5_t, s5_m, s5_n, s5_a, dt, x0=x0[l])
            xs[p] = _post(p, l, l == DEPTH - 1, x, mods, fo, mo, ys, wglu, wo, n2, wg, wu, wd, norm_f[None])

    cfin, nfin, mfin, xfin = (jnp.stack(parts, axis=1) for parts in zip(*finals))
    xfin = xfin.reshape(2, DEPTH, BATCH, S5_GROUPS, 2, S5_STATE)
    new_re, new_im = (jnp.transpose(xfin[:, :, :, :, i], (2, 1, 0, 3, 4)) for i in range(2))
    return (xs[PROMPT], xs[SAMPLE], cfin, nfin[:, :, :, :, 0], mfin[:, :, :, :, 0, 0], new_re, new_im)
```

```python
import collections
import functools
import math

import numpy as np
import jax
import jax.numpy as jnp
from jax import lax
from jax.experimental import pallas as pl
from jax.experimental.pallas import tpu as pltpu

F32 = jnp.float32
BF16 = jnp.bfloat16

D_MODEL = 1024
BATCH = 32
SEQ = 256
DEPTH = 2
DEC_BATCH = 4
DEC_SEQ = 2048
GRID_W = 64
FOURIER_W = 256
FOURIER_DH = 64
MLSTM_W = 384
MLSTM_HEADS = 4
MLSTM_DH = 96
S5_W = 384
S5_GROUP_CH = 16
S5_GROUPS = 24
S5_STATE = 64
N_GATES = 16
D_FF = 2816
EPS = 1e-6

LANES = 128
SUBLANES = 8
VMEM_LIMIT = 56 * 1024 * 1024
POST_VMEM_LIMIT = 60 * 1024 * 1024

HEAD_PAD = LANES
HEADS_W = MLSTM_HEADS * HEAD_PAD
N_AUG = MLSTM_DH
Z_W = S5_W + FOURIER_W + 3 * MLSTM_W
MLSTM_CHUNK = 256
S5_CHUNK = 16
S5_IN = S5_CHUNK * S5_GROUP_CH
S5_ST = 2 * S5_STATE
S5_ST2 = 2 * S5_ST
S5_GB = LANES // S5_GROUP_CH
S5_NB = S5_W // LANES
S5_ROW = S5_GB * S5_IN
TB = 4
TT = 128
TROWS = TB * TT // S5_CHUNK
HPS = 4
N_MODS = 8
NEG = -1e30

Pass = collections.namedtuple("Pass", "nb seq mod_first mod_each")
PROMPT = Pass(BATCH, SEQ, DEC_BATCH, False)
SAMPLE = Pass(DEC_BATCH, DEC_SEQ, 0, True)
T_PASS = BATCH * SEQ
assert T_PASS == DEC_BATCH * DEC_SEQ
R_PASS = T_PASS // S5_CHUNK

_NT = (((1,), (1,)), ((), ()))
_TN = (((0,), (0,)), ((), ()))


def _cparams(sem):
    return pltpu.CompilerParams(dimension_semantics=sem, vmem_limit_bytes=VMEM_LIMIT)


def _full(a):
    return pl.BlockSpec(a.shape, lambda *_: (0,) * a.ndim)


def _layer(a, l):
    return pl.BlockSpec((None,) + a.shape[1:], lambda *_: (l,) + (0,) * (a.ndim - 1))


def _log_sigmoid(x):
    return jnp.minimum(x, 0.0) - jnp.log1p(jnp.exp(-jnp.abs(x)))


TileSpecs = collections.namedtuple("TileSpecs", "n head tail tok rows mod")


def _tile_specs(p, l):
    nk = p.seq // TT
    n = (p.nb // TB) * nk
    head = lambda i: jnp.minimum(i, n - 1)
    tail = lambda i: jnp.maximum(i - 1, 0)
    tok = lambda w, tile: pl.BlockSpec((TB, TT, w), lambda i: (tile(i) // nk, tile(i) % nk, 0))
    rows = lambda tile: pl.BlockSpec((S5_NB, TROWS, S5_ROW), lambda i: (0, tile(i), 0))

    def mod(tile):
        if p.mod_each:
            return pl.BlockSpec((None, TB, 1, 6 * D_MODEL), lambda i: (l, p.mod_first // TB + tile(i) // nk, 0, 0))
        return pl.BlockSpec((None, 1, 1, 6 * D_MODEL), lambda i: (l, p.mod_first, 0, 0))

    return TileSpecs(n, head, tail, tok, rows, mod)


def _slab(b, c):
    tok = slice(b * TT + c * S5_CHUNK, b * TT + (c + 1) * S5_CHUNK)
    chk = slice((c * TB + b) * S5_CHUNK, (c * TB + b + 1) * S5_CHUNK)
    return tok, chk


def _ada_kernel(c_ref, w_ref, b_ref, o_ref):
    a = c_ref[...]
    a = (a * jax.nn.sigmoid(a)).astype(BF16)
    o_ref[0] = jnp.dot(a, w_ref[0].astype(BF16), preferred_element_type=F32) + b_ref[0]


def _ada(cc, w_ada, b_ada):
    tn = 1536
    return pl.pallas_call(
        _ada_kernel,
        grid=(DEPTH, 6 * D_MODEL // tn),
        in_specs=[pl.BlockSpec((N_MODS, D_MODEL), lambda l, j: (0, 0)),
                  pl.BlockSpec((1, D_MODEL, tn), lambda l, j: (l, 0, j)),
                  pl.BlockSpec((1, 1, tn), lambda l, j: (l, 0, j))],
        out_specs=pl.BlockSpec((1, N_MODS, tn), lambda l, j: (l, 0, j)),
        out_shape=jax.ShapeDtypeStruct((DEPTH, N_MODS, 6 * D_MODEL), F32),
        compiler_params=_cparams(("arbitrary", "arbitrary")),
        name="ada_mod",
    )(cc, w_ada, b_ada.reshape(DEPTH, 1, 6 * D_MODEL))


def _mod_norm(x3, mod, nw, first):
    sh = mod[:, :, first * D_MODEL:(first + 1) * D_MODEL]
    sc = mod[:, :, (first + 1) * D_MODEL:(first + 2) * D_MODEL]
    y = x3 * lax.rsqrt(jnp.mean(x3 * x3, axis=-1, keepdims=True) + EPS) * nw
    return (y * (1.0 + sc) + sh).reshape(-1, D_MODEL).astype(BF16)


def _alternate(gens):
    gens = list(gens)
    while gens:
        alive = []
        for g in gens:
            try:
                next(g)
                alive.append(g)
            except StopIteration:
                pass
        gens = alive


def _in_kernel(x_ref, mod_ref, nw_ref, w_ref, wv_ref, wg_ref, bg_ref,
               zq_ref, zk_ref, zo_ref, zf_ref, zu_ref, vt_ref, gt_ref, scr_ref):
    xn = _mod_norm(x_ref[...], mod_ref[...], nw_ref[...], 0)
    zu = jnp.dot(xn, w_ref[:, :S5_W], preferred_element_type=F32)
    for bl in range(S5_NB):
        zb = zu[:, bl * LANES:(bl + 1) * LANES]
        for b in range(TB):
            for c in range(TT // S5_CHUNK):
                tok, chk = _slab(b, c)
                scr_ref[bl, chk, :] = zb[tok]
        by_token = [scr_ref[bl, pl.ds(s, TROWS, stride=S5_CHUNK), :] for s in range(S5_CHUNK)]
        for gl in range(S5_GB):
            lo = gl * S5_GROUP_CH
            zu_ref[bl, :, gl * S5_IN:(gl + 1) * S5_IN] = jnp.concatenate(
                [x[:, lo:lo + S5_GROUP_CH] for x in by_token], axis=1)
        if bl == 0:
            z = jnp.dot(xn, w_ref[:, S5_W:], preferred_element_type=F32)
    zf_ref[...] = z[:, :FOURIER_W].astype(BF16).reshape(TB, TT, FOURIER_W)
    gap = jnp.zeros((TB * TT, HEAD_PAD - MLSTM_DH), BF16)
    for j, ref in enumerate((zq_ref, zk_ref, zo_ref)):
        t = z[:, FOURIER_W + j * MLSTM_W:FOURIER_W + (j + 1) * MLSTM_W].astype(BF16)
        ref[...] = jnp.concatenate(
            [piece for h in range(MLSTM_HEADS) for piece in (t[:, h * MLSTM_DH:(h + 1) * MLSTM_DH], gap)],
            axis=1).reshape(TB, TT, HEADS_W)
    vt = lax.dot_general(wv_ref[...], xn, _NT, preferred_element_type=F32).astype(BF16)
    gt = lax.dot_general(wg_ref[...], xn, _NT, preferred_element_type=F32) + bg_ref[...]
    for b in range(TB):
        toks = slice(b * TT, (b + 1) * TT)
        for h in range(MLSTM_HEADS):
            vt_ref[b, h * HEAD_PAD:h * HEAD_PAD + MLSTM_DH, :] = vt[h * MLSTM_DH:(h + 1) * MLSTM_DH, toks]
            vt_ref[b, h * HEAD_PAD + MLSTM_DH:(h + 1) * HEAD_PAD, :] = jnp.zeros((HEAD_PAD - MLSTM_DH, TT), BF16)
        gt_ref[b] = gt[:, toks]


def _in_proj(p, l, x, mods, nw, w, wv_t, wg_t, bg):
    ts = _tile_specs(p, l)
    nk = p.seq // TT
    tile = lambda i: i
    chan = lambda c_: pl.BlockSpec((TB, c_, TT), lambda i: (i // nk, 0, i % nk))
    outs = [HEADS_W] * 3 + [FOURIER_W]
    return pl.pallas_call(
        _in_kernel,
        grid=(ts.n,),
        in_specs=[ts.tok(D_MODEL, tile), ts.mod(tile),
                  _layer(nw, l), _layer(w, l), _layer(wv_t, l), _layer(wg_t, l), _layer(bg, l)],
        out_specs=[ts.tok(w_, tile) for w_ in outs] + [ts.rows(tile), chan(HEADS_W), chan(N_GATES)],
        out_shape=[jax.ShapeDtypeStruct((p.nb, p.seq, w_), BF16) for w_ in outs]
        + [jax.ShapeDtypeStruct((S5_NB, R_PASS, S5_ROW), F32),
           jax.ShapeDtypeStruct((p.nb, HEADS_W, p.seq), BF16),
           jax.ShapeDtypeStruct((p.nb, N_GATES, p.seq), F32)],
        scratch_shapes=[pltpu.VMEM((S5_NB, TB * TT, LANES), F32)],
        compiler_params=_cparams(("arbitrary",)),
        name="in_proj_%d" % p.seq,
    )(x, mods, nw, w, wv_t, wg_t, bg)


def _dft_consts():
    d = np.arange(FOURIER_DH)
    phi = 2.0 * np.pi * ((d[:, None] * d[None, :]) % FOURIER_DH) / FOURIER_DH
    eye = np.eye(FOURIER_W // FOURIER_DH)
    cd = np.kron(eye, np.cos(phi)) / math.sqrt(FOURIER_DH)
    sd = np.kron(eye, np.sin(phi)) / math.sqrt(FOURIER_DH)
    s = np.arange(SEQ)
    th = 2.0 * np.pi * ((s[:, None] * s[None, :]) % SEQ) / SEQ
    rows = DEC_SEQ // GRID_W
    pos = np.arange(DEC_SEQ)
    r, c = pos // GRID_W, pos % GRID_W
    ph = ((r[:, None] * r[None, :]) * (GRID_W // rows) + c[:, None] * c[None, :]) % GRID_W
    th2 = 2.0 * np.pi * ph / GRID_W
    return (np.concatenate([cd, sd], axis=1),
            np.concatenate([np.cos(th), -np.sin(th)], axis=1) / math.sqrt(SEQ),
            np.concatenate([np.cos(th2), -np.sin(th2)], axis=1) / math.sqrt(DEC_SEQ))


def _fourier_prompt_kernel(nb, zf_ref, cdsd_ref, cs_ref, wf_ref, o_ref):
    t = jnp.dot(zf_ref[...].reshape(nb * SEQ, FOURIER_W), cdsd_ref[...],
                preferred_element_type=F32).astype(BF16)
    fs = []
    for b in range(nb):
        tb = t[b * SEQ:(b + 1) * SEQ]
        st = jnp.concatenate([tb[:, :FOURIER_W], tb[:, FOURIER_W:]], axis=0)
        fs.append(jnp.dot(cs_ref[...], st, preferred_element_type=F32))
    for b in range(nb):
        o_ref[b] = jnp.dot(fs[b].astype(BF16), wf_ref[...], preferred_element_type=F32).astype(BF16)


def _fourier_prompt(l, zf, cdsd, cs, wf):
    nb = 4
    blk = pl.BlockSpec((nb, SEQ, FOURIER_W), lambda i: (i, 0, 0))
    return pl.pallas_call(
        functools.partial(_fourier_prompt_kernel, nb),
        grid=(BATCH // nb,),
        in_specs=[blk, _full(cdsd), _full(cs), _layer(wf, l)],
        out_specs=blk,
        out_shape=jax.ShapeDtypeStruct((BATCH, SEQ, FOURIER_W), BF16),
        compiler_params=_cparams(("arbitrary",)),
        name="fourier_prompt",
    )(zf, cdsd, cs, wf)


def _fourier_sample_kernel(zf_ref, cdsd_ref, ab_ref, wf_ref, o_ref, tt_ref):
    @pl.when(pl.program_id(0) == 0)
    def _():
        for b in range(DEC_BATCH):
            t = jnp.dot(zf_ref[b], cdsd_ref[...], preferred_element_type=F32).astype(BF16)
            tt_ref[b, 0:DEC_SEQ, :] = t[:, :FOURIER_W]
            tt_ref[b, DEC_SEQ:2 * DEC_SEQ, :] = t[:, FOURIER_W:]

    fs = [jnp.dot(ab_ref[...], tt_ref[b], preferred_element_type=F32) for b in range(DEC_BATCH)]
    for b in range(DEC_BATCH):
        o_ref[b] = jnp.dot(fs[b].astype(BF16), wf_ref[...], preferred_element_type=F32).astype(BF16)


def _fourier_sample(l, zf, cdsd, ab, wf):
    tk = 512
    return pl.pallas_call(
        _fourier_sample_kernel,
        grid=(DEC_SEQ // tk,),
        in_specs=[_full(zf), _full(cdsd), pl.BlockSpec((tk, 2 * DEC_SEQ), lambda i: (i, 0)), _layer(wf, l)],
        out_specs=pl.BlockSpec((DEC_BATCH, tk, FOURIER_W), lambda i: (0, i, 0)),
        out_shape=jax.ShapeDtypeStruct((DEC_BATCH, DEC_SEQ, FOURIER_W), BF16),
        scratch_shapes=[pltpu.VMEM((DEC_BATCH, 2 * DEC_SEQ, FOURIER_W), BF16)],
        compiler_params=_cparams(("arbitrary",)),
        name="fourier_sample",
    )(zf, cdsd, ab, wf)


def _split3(x):
    hi = x.astype(BF16).astype(F32)
    mid = (x - hi).astype(BF16).astype(F32)
    lo = (x - hi - mid).astype(BF16).astype(F32)
    return hi, mid, lo


def _gate_kernel(g_ref, o_ref):
    L = MLSTM_CHUNK
    nrow = N_GATES // 2
    row = lax.broadcasted_iota(jnp.int32, (L, L), 0)
    col = lax.broadcasted_iota(jnp.int32, (L, L), 1)
    tri_pre = jnp.where(row <= col, 1.0, 0.0).astype(BF16)
    tri_suf = jnp.where(row >= col, 1.0, 0.0).astype(BF16)
    is_fwd = lax.broadcasted_iota(jnp.int32, (nrow, L), 0) < MLSTM_HEADS
    lane = lax.broadcasted_iota(jnp.int32, (nrow, L), 1)
    chunks = [(bi, slice(c * L, (c + 1) * L)) for bi in range(g_ref.shape[0]) for c in range(g_ref.shape[2] // L)]
    fold = lambda a: a[0:nrow] + a[nrow:2 * nrow] + a[2 * nrow:]
    bs, rs = [], []
    for bi, cols in chunks:
        lf = _log_sigmoid(g_ref[bi, nrow:, cols])
        parts = jnp.concatenate(_split3(lf), axis=0).astype(BF16)
        pre = jnp.dot(parts, tri_pre, preferred_element_type=F32)
        suf = jnp.dot(parts, tri_suf, preferred_element_type=F32)
        bs.append(jnp.where(is_fwd, fold(pre), fold(suf)))
        rs.append(g_ref[bi, 0:nrow, cols] - bs[-1])
    pms, sms = list(rs), list(rs)
    sh = 1
    while sh < L:
        pms = [jnp.maximum(x, jnp.where(lane >= sh, pltpu.roll(x, sh, 1), NEG)) for x in pms]
        sms = [jnp.maximum(x, jnp.where(lane < L - sh, pltpu.roll(x, L - sh, 1), NEG)) for x in sms]
        sh *= 2
    for (bi, cols), b, r, pm, sm in zip(chunks, bs, rs, pms, sms):
        for q, val in enumerate((b, r, jnp.where(is_fwd, pm, sm))):
            for dh in range(nrow):
                o_ref[bi, dh, q:q + 1, cols] = val[dh:dh + 1]


def _gate_prep(p, gt):
    bb = max(1, DEC_SEQ // p.seq)
    nrow = N_GATES // 2
    return pl.pallas_call(
        _gate_kernel,
        grid=(p.nb // bb,),
        in_specs=[pl.BlockSpec((bb, N_GATES, p.seq), lambda i: (i, 0, 0))],
        out_specs=pl.BlockSpec((bb, nrow, 3, p.seq), lambda i: (i, 0, 0, 0)),
        out_shape=jax.ShapeDtypeStruct((p.nb, nrow, 3, p.seq), F32),
        compiler_params=_cparams(("arbitrary",)),
        name="gate_prep_%d" % p.seq,
    )(gt)


def _mlstm_chunk(q, k, vt, pr, ct, m, fwd, out):
    L = q.shape[0]
    scale = MLSTM_DH ** -0.5
    b, r, cm = pr[0:1], pr[1:2], pr[2:3]
    ones = jnp.ones((3, L), F32)
    zeros = jnp.zeros((SUBLANES - 6, L), F32)
    lhs = jnp.concatenate(_split3(r) + (ones, zeros), axis=0).astype(BF16)
    rhs = jnp.concatenate((ones,) + _split3(-cm) + (zeros,), axis=0).astype(BF16)
    arg = lax.dot_general(lhs, rhs, _TN, preferred_element_type=F32)
    st = lax.dot_general(k, q, _NT, preferred_element_type=F32)
    cq = lax.dot_general(ct.astype(BF16), q, _NT, preferred_element_type=F32)
    last = L - 1 if fwd else 0
    cm_last = cm[:, last:last + 1]
    mx_last = jnp.maximum(m, cm_last)
    vw = (vt.astype(F32) * jnp.exp(r - cm_last)).astype(BF16)
    dct = jnp.dot(vw, k, preferred_element_type=F32)
    yield

    row = lax.broadcasted_iota(jnp.int32, (L, L), 0)
    col = lax.broadcasted_iota(jnp.int32, (L, L), 1)
    e = jnp.where((row <= col) if fwd else (row >= col), jnp.exp(arg), 0.0)
    num = jnp.dot(vt, (st * e).astype(BF16), preferred_element_type=F32)
    ct_new = jnp.exp(m - mx_last) * ct + (jnp.exp(cm_last - mx_last) * scale) * dct
    yield

    mx = jnp.maximum(m, cm)
    num = (jnp.exp(cm - mx) * scale) * num + jnp.exp(m - mx) * cq
    den = num[N_AUG:N_AUG + 1, :]
    h = num * (1.0 / jnp.maximum(jnp.abs(den), jnp.exp(-(b + mx))))
    out += [h, ct_new, b[:, last:last + 1] + mx_last]


def _mlstm_kernel(nc, has_init, want_final, *refs):
    refs = list(refs)
    m0_ref, c0_ref = (refs.pop(0), refs.pop(0)) if has_init else (None, None)
    if want_final:
        del refs[7]
    q_ref, k_ref, vt_ref, zo_ref, prf_ref, prb_ref, nw_ref, o_ref = refs[:8]
    hbuf_ref = refs[-1]
    L = MLSTM_CHUNK
    bi = pl.program_id(0)
    hg = pl.program_id(1)
    vrow = lax.broadcasted_iota(jnp.int32, (HEAD_PAD, L), 0)

    def chunk(hh, ci, pr_ref, ct, m, dr, out):
        rows = pl.ds(pl.multiple_of(ci * L, L), L)
        lanes = slice(hh * HEAD_PAD, (hh + 1) * HEAD_PAD)
        vt = vt_ref[lanes, rows]
        vt = jnp.where(vrow == N_AUG, jnp.ones_like(vt), vt)
        res = []
        yield from _mlstm_chunk(q_ref[rows, lanes], k_ref[rows, lanes], vt, pr_ref[hh, :, rows], ct, m,
                                dr == 0, res)
        hbuf_ref[hh, dr, ci] = res[0]
        out += res[1:]

    def step(i, carry):
        outs = [[] for _ in range(2 * HPS)]
        gens = []
        for hh in range(HPS):
            cf, mf, cb, mb = carry[4 * hh:4 * hh + 4]
            gens += [chunk(hh, i, prf_ref, cf, mf, 0, outs[2 * hh]),
                     chunk(hh, nc - 1 - i, prb_ref, cb, mb, 1, outs[2 * hh + 1])]
        _alternate(gens)
        return tuple(x for o in outs for x in o)

    carry = []
    for hh in range(HPS):
        for dr in range(2):
            if has_init:
                carry += [c0_ref[0, dr, hh].T, jnp.full((1, 1), m0_ref[bi, dr, hg * HPS + hh], F32)]
            else:
                carry += [jnp.zeros((HEAD_PAD, HEAD_PAD), F32), jnp.zeros((1, 1), F32)]
    carry = step(0, tuple(carry)) if nc == 1 else lax.fori_loop(0, nc, step, tuple(carry))
    if want_final:
        cfin_ref, nfin_ref, mfin_ref = refs[8:11]
        for hh in range(HPS):
            for dr in range(2):
                ct, m = carry[4 * hh + 2 * dr], carry[4 * hh + 2 * dr + 1]
                cfin_ref[0, dr, hh] = ct.T[:MLSTM_DH, :MLSTM_DH]
                nfin_ref[0, dr, hh] = ct[N_AUG:N_AUG + 1, :MLSTM_DH]
                mfin_ref[0, dr, hh] = jnp.broadcast_to(m, (SUBLANES, LANES))

    def finish(ci, _):
        rows = pl.ds(pl.multiple_of(ci * L, L), L)
        for hh in range(HPS):
            lanes = slice(hh * HEAD_PAD, (hh + 1) * HEAD_PAD)
            h = jnp.where(vrow < MLSTM_DH, hbuf_ref[hh, 0, ci] + hbuf_ref[hh, 1, ci], 0.0)
            ms = jnp.sum(h * h, axis=0, keepdims=True) * (1.0 / MLSTM_DH)
            hn = (h * lax.rsqrt(ms + EPS) * nw_ref[hh]).T
            o_ref[rows, lanes] = (hn * jax.nn.sigmoid(zo_ref[rows, lanes].astype(F32))).astype(BF16)
        return 0

    if nc == 1:
        finish(0, 0)
    else:
        lax.fori_loop(0, nc, finish, 0)


def _mlstm(p, l, q, k, vt, zo, pr, nw, init=None, c_all=None):
    want_final = c_all is not None
    nc = p.seq // MLSTM_CHUNK
    tok = pl.BlockSpec((None, p.seq, HPS * HEAD_PAD), lambda b, h: (b, 0, h))
    st = lambda r, c: pl.BlockSpec((1, 2, HPS, r, c), lambda b, h: (b, 0, h, 0, 0))
    prs = lambda dr: pl.BlockSpec((None, HPS, 3, p.seq), lambda b, h: (b, dr * (MLSTM_HEADS // HPS) + h, 0, 0))
    in_specs, args = [], []
    if init is not None:
        in_specs += [pl.BlockSpec(memory_space=pltpu.SMEM), st(HEAD_PAD, HEAD_PAD)]
        args += list(init)
    in_specs += [tok, tok, pl.BlockSpec((None, HPS * HEAD_PAD, p.seq), lambda b, h: (b, h, 0)), tok, prs(0), prs(1),
                 pl.BlockSpec((None, HPS, HEAD_PAD, 1), lambda b, h: (l, h, 0, 0))]
    args += [q, k, vt, zo, pr, pr, nw]
    out_specs = [tok]
    out_shape = [jax.ShapeDtypeStruct((p.nb, p.seq, HEADS_W), BF16)]
    aliases = {}
    if want_final:
        in_specs.append(pl.BlockSpec(memory_space=pl.ANY))
        args.append(c_all)
        aliases = {len(args) - 1: 1}
        out_specs += [pl.BlockSpec((1, None, 2, HPS, MLSTM_DH, MLSTM_DH), lambda b, h: (b, l, 0, h, 0, 0)),
                      st(1, MLSTM_DH), st(SUBLANES, LANES)]
        out_shape += [jax.ShapeDtypeStruct(c_all.shape, F32)]
        out_shape += [jax.ShapeDtypeStruct((p.nb, 2, MLSTM_HEADS, r, c), F32)
                      for r, c in ((1, MLSTM_DH), (SUBLANES, LANES))]
    return pl.pallas_call(
        functools.partial(_mlstm_kernel, nc, init is not None, want_final),
        grid=(p.nb, MLSTM_HEADS // HPS),
        in_specs=in_specs,
        out_specs=out_specs,
        out_shape=out_shape,
        input_output_aliases=aliases,
        scratch_shapes=[pltpu.VMEM((HPS, 2, nc, HEAD_PAD, MLSTM_CHUNK), F32)],
        compiler_params=_cparams(("arbitrary", "arbitrary")),
        name="mlstm_%d" % p.seq,
    )(*args)


def _cpow(br, bi, e, nbits):
    pr = pi = None
    for bit in range(nbits):
        sel = ((e >> bit) & 1) == 1
        if pr is None:
            pr, pi = jnp.where(sel, br, 1.0), jnp.where(sel, bi, 0.0)
        else:
            pr, pi = jnp.where(sel, pr * br - pi * bi, pr), jnp.where(sel, pr * bi + pi * br, pi)
        if bit + 1 < nbits:
            br, bi = br * br - bi * bi, 2.0 * br * bi
    return pr, pi


def _s5_prep_kernel(lamc_re_ref, lamc_im_ref, lamr_re_ref, lamr_im_ref, lstep_ref,
                    bt_re_ref, bt_im_ref, ct_re_ref, ct_im_ref,
                    t_ref, m_ref, n_ref, a_ref):
    C = S5_CHUNK
    nbits = (C - 1).bit_length()
    assert C == 1 << nbits
    kk = lax.broadcasted_iota(jnp.int32, (S5_STATE, S5_IN), 1) >> 4
    srow = lax.broadcasted_iota(jnp.int32, (S5_IN, S5_ST), 0) >> 4
    left = lax.broadcasted_iota(jnp.int32, (S5_IN, S5_ST), 1) < S5_STATE
    left16 = lax.broadcasted_iota(jnp.int32, (S5_GROUP_CH, S5_ST), 1) < S5_STATE
    left1 = lax.broadcasted_iota(jnp.int32, (1, S5_ST), 1) < S5_STATE
    lane = lax.broadcasted_iota(jnp.int32, (S5_GROUP_CH, S5_IN), 1)
    sel = jnp.where((lane & (S5_GROUP_CH - 1)) == lax.broadcasted_iota(jnp.int32, (S5_GROUP_CH, S5_IN), 0),
                    1.0, 0.0)
    spread = lambda ref: jnp.dot(ref[0, 0], sel, precision=lax.Precision.HIGHEST, preferred_element_type=F32)
    ct_re = spread(ct_re_ref)
    ct_im = spread(ct_im_ref)
    resp = []
    for d in range(2):
        step = jnp.exp(lstep_ref[0, d, 0])
        lr_c, li_c = lamc_re_ref[0, d, 0] * step, lamc_im_ref[0, d, 0] * step
        lr_r, li_r = lamr_re_ref[0, d, 0], lamr_im_ref[0, d, 0]
        lbc_re, lbc_im = jnp.exp(lr_c) * jnp.cos(li_c), jnp.exp(lr_c) * jnp.sin(li_c)
        mag = jnp.exp(lr_r * step)
        lb_re, lb_im = mag * jnp.cos(li_r * step), mag * jnp.sin(li_r * step)

        pr, pi = _cpow(lbc_re, lbc_im, kk if d == 0 else (C - 1) - kk, nbits)
        pr1, pi1 = pr * lbc_re - pi * lbc_im, pr * lbc_im + pi * lbc_re
        cpr, cpi = ct_re * pr - ct_im * pi, ct_re * pi + ct_im * pr
        cpr1, cpi1 = ct_re * pr1 - ct_im * pi1, ct_re * pi1 + ct_im * pr1

        nr, ni = lb_re - 1.0, lb_im
        den = lr_r * lr_r + li_r * li_r
        kap_re = (nr * lr_r + ni * li_r) / den
        kap_im = (ni * lr_r - nr * li_r) / den
        bb_re = kap_re * bt_re_ref[0, 0] - kap_im * bt_im_ref[0, 0]
        bb_im = kap_re * bt_im_ref[0, 0] + kap_im * bt_re_ref[0, 0]

        resp.append(jnp.dot(jnp.where(left16, bb_re, -bb_im), jnp.concatenate([cpr, cpi], axis=0),
                            precision=lax.Precision.HIGHEST, preferred_element_type=F32))
        m_ref[0, d, 0] = jnp.concatenate([cpr1, -cpi1], axis=0).astype(BF16)

        pr, pi = _cpow(lb_re, lb_im, (C - 1) - srow if d == 0 else srow, nbits)
        bt_r = jnp.concatenate([bb_re] * C, axis=0)
        bt_i = jnp.concatenate([bb_im] * C, axis=0)
        n_re, n_im = pr * bt_r - pi * bt_i, pr * bt_i + pi * bt_r
        n_ref[0, d, 0] = jnp.concatenate([jnp.where(left, n_re, n_im), jnp.where(left, n_im, n_re)],
                                         axis=1).astype(BF16)

        ar, ai = lb_re, lb_im
        for _ in range(nbits):
            ar, ai = ar * ar - ai * ai, 2.0 * ar * ai
        a2 = jnp.where(left1, -ai, ai)
        a_ref[0, d, 0] = jnp.concatenate([jnp.concatenate([ar, ar], axis=1),
                                          jnp.concatenate([a2, -a2], axis=1)], axis=0)

    rf, rb = resp
    for s in range(C):
        nf = S5_GROUP_CH * s
        blk = jnp.where(lane >= nf, pltpu.roll(rf, nf, 1) if nf else rf, 0.0)
        nb = S5_GROUP_CH * (C - 1 - s)
        blk = blk + jnp.where(lane < S5_IN - nb, pltpu.roll(rb, S5_IN - nb, 1) if nb else rb, 0.0)
        t_ref[0, 0, S5_GROUP_CH * s:S5_GROUP_CH * (s + 1), :] = blk.astype(BF16)


def _s5_prep(lam_re, lam_im, log_step, b_re, b_im, c_re, c_im):
    G = S5_GROUPS
    dup = lambda a: jnp.concatenate([a, a], axis=-1)
    lamc = [a.reshape(DEPTH, 2, G, S5_STATE, 1) for a in (lam_re, lam_im)]
    lamr = [dup(a).reshape(DEPTH, 2, G, 1, S5_ST) for a in (lam_re, lam_im)]
    lstep = log_step.reshape(DEPTH, 2, G, 1, 1)
    bt = [dup(jnp.swapaxes(a, 2, 3)) for a in (b_re, b_im)]
    ct = [jnp.swapaxes(a, 2, 3) for a in (c_re, c_im)]
    dspec = lambda r, c: pl.BlockSpec((1, 2, 1, r, c), lambda l, g: (l, 0, g, 0, 0))
    gspec = lambda r, c: pl.BlockSpec((1, 1, r, c), lambda l, g: (l, g, 0, 0))
    t, m, n, a = pl.pallas_call(
        _s5_prep_kernel,
        grid=(DEPTH, G),
        in_specs=[dspec(S5_STATE, 1), dspec(S5_STATE, 1), dspec(1, S5_ST), dspec(1, S5_ST), dspec(1, 1),
                  gspec(S5_GROUP_CH, S5_ST), gspec(S5_GROUP_CH, S5_ST),
                  gspec(S5_STATE, S5_GROUP_CH), gspec(S5_STATE, S5_GROUP_CH)],
        out_specs=[gspec(S5_IN, S5_IN), dspec(S5_ST, S5_IN), dspec(S5_IN, S5_ST2), dspec(2, S5_ST2)],
        out_shape=[jax.ShapeDtypeStruct((DEPTH, G, S5_IN, S5_IN), BF16),
                   jax.ShapeDtypeStruct((DEPTH, 2, G, S5_ST, S5_IN), BF16),
                   jax.ShapeDtypeStruct((DEPTH, 2, G, S5_IN, S5_ST2), BF16),
                   jax.ShapeDtypeStruct((DEPTH, 2, G, 2, S5_ST2), F32)],
        compiler_params=_cparams(("arbitrary", "arbitrary")),
        name="s5_prep",
    )(*lamc, *lamr, lstep, *bt, *ct)
    return t, m, n, jnp.transpose(a, (0, 1, 3, 2, 4)).reshape(DEPTH, 2, 2, G * S5_ST2)


def _s5_kernel(nseg, nchunks, zr_ref, t_ref, m_ref, n_ref, a_ref, x0_ref, d_ref,
               y_ref, xfin_ref, v_ref, xp_ref):
    W = S5_GB * S5_ST2
    for gl in range(S5_GB):
        u = zr_ref[0, :, gl * S5_IN:(gl + 1) * S5_IN].astype(BF16)
        for d in range(2):
            v_ref[d, :, gl * S5_ST2:(gl + 1) * S5_ST2] = jnp.dot(
                u, n_ref[d, gl], preferred_element_type=F32)

    def halves(x, which):
        return [x[:, g * S5_ST2 + h * S5_ST:g * S5_ST2 + (h + 1) * S5_ST]
                for g in range(S5_GB) for h in which]

    per = SUBLANES // TB
    steps = nchunks // per

    for d in range(2):
        a = a_ref[d, 0:1, :]
        a2 = a_ref[d, 1:2, :]

        def advance(x, v):
            swapped = jnp.concatenate(halves(x, (1, 0)), axis=1)
            return a * x + a2 * swapped + v

        for seg in range(nseg):
            def step(i, x):
                si = i if d == 0 else steps - 1 - i
                rows = pl.ds(pl.multiple_of(seg * nchunks * TB + si * SUBLANES, SUBLANES), SUBLANES)
                v = v_ref[d, rows, :]
                order = range(per) if d == 0 else range(per - 1, -1, -1)
                entering = [None] * per
                for j in order:
                    entering[j] = jnp.concatenate(halves(x, (0,)), axis=1)
                    x = advance(x, v[j * TB:(j + 1) * TB])
                xp_ref[d, rows, :] = jnp.concatenate(entering, axis=0)
                return x

            x0 = jnp.zeros((TB, W), F32) if x0_ref is None else x0_ref[d, seg * TB:(seg + 1) * TB, :]
            x = lax.fori_loop(0, steps, step, x0)
            if xfin_ref is not None:
                xfin_ref[d, seg * TB:(seg + 1) * TB, :] = jnp.concatenate(halves(x, (0,)), axis=1)

    for gl in range(S5_GB):
        cols = slice(gl * S5_IN, (gl + 1) * S5_IN)
        u = zr_ref[0, :, cols]
        y = jnp.dot(u.astype(BF16), t_ref[gl], preferred_element_type=F32)
        for d in range(2):
            y = y + jnp.dot(xp_ref[d, :, gl * S5_ST:(gl + 1) * S5_ST].astype(BF16), m_ref[d, gl],
                            preferred_element_type=F32)
        y_ref[0, :, cols] = jax.nn.gelu(y + d_ref[0, :, cols] * u)


def _s5(p, l, zr, t, m, n, a, dt, x0=None, want_final=False):
    G = S5_GROUPS
    nseg = p.nb // TB
    row = pl.BlockSpec((1, R_PASS, S5_ROW), lambda j: (j, 0, 0))
    dsp = lambda r, c: pl.BlockSpec((None, 2, S5_GB, r, c), lambda j: (l, 0, j, 0, 0))
    lsp = lambda r, w: pl.BlockSpec((2, r, S5_GB * w), lambda j: (0, 0, j))
    in_specs = [row, pl.BlockSpec((None, S5_GB, S5_IN, S5_IN), lambda j: (l, j, 0, 0)),
                dsp(S5_ST, S5_IN), dsp(S5_IN, S5_ST2),
                pl.BlockSpec((None, 2, 2, S5_GB * S5_ST2), lambda j: (l, 0, 0, j)),
                pl.BlockSpec((None, 1, 1, S5_ROW), lambda j: (l, j, 0, 0))]
    args = [zr, t, m, n, a, dt]
    if x0 is not None:
        in_specs.append(lsp(p.nb, S5_ST2))
        args.append(x0)
    out_specs = [row]
    out_shape = [jax.ShapeDtypeStruct((S5_NB, R_PASS, S5_ROW), F32)]
    if want_final:
        out_specs.append(lsp(p.nb, S5_ST))
        out_shape.append(jax.ShapeDtypeStruct((2, p.nb, G * S5_ST), F32))

    def body(zr_ref, t_ref, m_ref, n_ref, a_ref, d_ref, *rest):
        rest = list(rest)
        x0_ref = rest.pop(0) if x0 is not None else None
        y_ref = rest.pop(0)
        xfin_ref = rest.pop(0) if want_final else None
        _s5_kernel(nseg, p.seq // S5_CHUNK, zr_ref, t_ref, m_ref, n_ref, a_ref, x0_ref, d_ref,
                   y_ref, xfin_ref, *rest)

    return pl.pallas_call(
        body,
        grid=(S5_NB,),
        in_specs=in_specs,
        out_specs=out_specs,
        out_shape=out_shape,
        scratch_shapes=[pltpu.VMEM((2, R_PASS, S5_GB * S5_ST2), F32),
                        pltpu.VMEM((2, R_PASS, S5_GB * S5_ST), F32)],
        compiler_params=_cparams(("arbitrary",)),
        name="s5_scan_%d" % p.seq,
    )(*args)


def _mix_stages(x_ref, mod_ref, fo_ref, mo_ref, ys_ref, wglu_ref, wo_ref, nw_ref, scr_ref, out):
    mod = mod_ref[...]
    g1 = mod[:, :, 2 * D_MODEL:3 * D_MODEL]
    flat = lambda ref: ref[...].reshape(TB * TT, ref.shape[-1])
    o_m, o_s = FOURIER_W, FOURIER_W + HEADS_W
    mix = (jnp.dot(flat(fo_ref), wo_ref[:o_m, :], preferred_element_type=F32)
           + jnp.dot(flat(mo_ref), wo_ref[o_m:o_s, :], preferred_element_type=F32))
    yield
    blocks = []
    for bl in range(S5_NB):
        by_group = [ys_ref[bl, :, gl * S5_IN:(gl + 1) * S5_IN] for gl in range(S5_GB)]
        for t in range(S5_CHUNK):
            lo = t * S5_GROUP_CH
            scr_ref[bl, pl.ds(t, TROWS, stride=S5_CHUNK), :] = jnp.concatenate(
                [y[:, lo:lo + S5_GROUP_CH] for y in by_group], axis=1)
        slabs = [scr_ref[bl, _slab(b, c)[1], :] for b in range(TB) for c in range(TT // S5_CHUNK)]
        blocks.append(jnp.concatenate(slabs, axis=0))
        yield
    y = jnp.concatenate(blocks, axis=1).astype(BF16)
    gg = jnp.dot(y, wglu_ref[...], preferred_element_type=F32)
    s_out = (gg[:, :S5_W] * jax.nn.sigmoid(gg[:, S5_W:])).astype(BF16)
    mix = mix + jnp.dot(s_out, wo_ref[o_s:, :], preferred_element_type=F32)
    yield
    x1 = x_ref[...] + g1 * mix.reshape(TB, TT, D_MODEL)
    out += [x1, _mod_norm(x1, mod, nw_ref[...], 3)]


FF_SPLIT = (768, 768, 640, 640)
assert sum(FF_SPLIT) == D_FF


def _ffn_stages(final, xn, x1, mod_ref, wg_ref, wu_ref, wd_ref, nf_ref, out):
    ff = None
    o = 0
    for w in FF_SPLIT:
        cols = slice(o, o + w)
        o += w
        a = jnp.dot(xn, wg_ref[:, cols], preferred_element_type=F32)
        u = jnp.dot(xn, wu_ref[:, cols], preferred_element_type=F32)
        h = (a * jax.nn.sigmoid(a) * u).astype(BF16)
        part = jnp.dot(h, wd_ref[cols, :], preferred_element_type=F32)
        ff = part if ff is None else ff + part
        yield
    g2 = mod_ref[...][:, :, 5 * D_MODEL:6 * D_MODEL]
    x2 = x1 + g2 * ff.reshape(TB, TT, D_MODEL)
    if final:
        x2 = x2 * lax.rsqrt(jnp.mean(x2 * x2, axis=-1, keepdims=True) + EPS) * nf_ref[...]
    out.append(x2)


def _post_kernel(final, n, x_ref, moda_ref, fo_ref, mo_ref, ys_ref, modb_ref, wglu_ref, wo_ref, nw_ref,
                 wg_ref, wu_ref, wd_ref, nf_ref, o_ref, x1_ref, xn_ref, scr_ref):
    i = pl.program_id(0)
    cur = i % 2
    prev = 1 - cur

    def run(do_ffn, do_mix):
        res_a, res_b, gens = [], [], []
        if do_ffn:
            gens.append(_ffn_stages(final, xn_ref[prev], x1_ref[prev], modb_ref, wg_ref, wu_ref, wd_ref,
                                    nf_ref, res_b))
        if do_mix:
            gens.append(_mix_stages(x_ref, moda_ref, fo_ref, mo_ref, ys_ref, wglu_ref, wo_ref, nw_ref,
                                    scr_ref, res_a))
        _alternate(gens)
        if do_ffn:
            o_ref[...] = res_b[0]
        if do_mix:
            x1_ref[cur] = res_a[0]
            xn_ref[cur] = res_a[1]

    pl.when(i == 0)(lambda: run(False, True))
    pl.when(jnp.logical_and(i > 0, i < n))(lambda: run(True, True))
    pl.when(i == n)(lambda: run(True, False))


def _post(p, l, final, x, mods, fo, mo, ys, wglu, wo, nw, wg, wu, wd, nf):
    ts = _tile_specs(p, l)
    once = lambda a: pl.BlockSpec((None,) + a.shape[1:], lambda i: (l,) + (0,) * (a.ndim - 1),
                                  pipeline_mode=pl.Buffered(1))
    return pl.pallas_call(
        functools.partial(_post_kernel, final, ts.n),
        grid=(ts.n + 1,),
        in_specs=[ts.tok(D_MODEL, ts.head), ts.mod(ts.head), ts.tok(FOURIER_W, ts.head),
                  ts.tok(HEADS_W, ts.head), ts.rows(ts.head), ts.mod(ts.tail),
                  once(wglu), once(wo), once(nw), once(wg), once(wu), once(wd),
                  pl.BlockSpec((1, D_MODEL), lambda i: (0, 0))],
        out_specs=ts.tok(D_MODEL, ts.tail),
        out_shape=jax.ShapeDtypeStruct((p.nb, p.seq, D_MODEL), F32),
        scratch_shapes=[pltpu.VMEM((2, TB, TT, D_MODEL), F32), pltpu.VMEM((2, TB * TT, D_MODEL), BF16),
                        pltpu.VMEM((S5_NB, TB * TT, LANES), F32)],
        compiler_params=pltpu.CompilerParams(dimension_semantics=("arbitrary",),
                                             vmem_limit_bytes=POST_VMEM_LIMIT),
        name="post_%d" % p.seq,
    )(x, mods, fo, mo, ys, mods, wglu, wo, nw, wg, wu, wd, nf)


def _pad_heads(a, axis):
    shape = a.shape[:axis] + (MLSTM_HEADS, MLSTM_DH) + a.shape[axis + 1:]
    pad = [(0, 0)] * (a.ndim + 1)
    pad[axis + 1] = (0, HEAD_PAD - MLSTM_DH)
    return jnp.pad(a.reshape(shape), pad).reshape(a.shape[:axis] + (HEADS_W,) + a.shape[axis + 1:])


def _mlstm_state_in(c, n):
    cn = jnp.concatenate([c, n[..., None]], axis=-1)
    return jnp.pad(cn, ((0, 0),) * (c.ndim - 2) + ((0, HEAD_PAD - MLSTM_DH), (0, HEAD_PAD - MLSTM_DH - 1)))


def kernel(x_prompt, x_sample, state_mlstm_C, state_mlstm_n, state_mlstm_m, state_s5_re, state_s5_im,
           c, c_ctx, w_ada, b_ada, norm1_w, norm2_w, w_in, b_gates, w_fourier, mlstm_norm_w,
           s5_lambda_re, s5_lambda_im, s5_log_step, s5_b_re, s5_b_im, s5_c_re, s5_c_im, s5_d,
           w_glu, w_out, w_gate, w_up, w_down, norm_f):
    xs = {PROMPT: x_prompt, SAMPLE: x_sample}
    cc = jnp.concatenate([c, c_ctx[None], jnp.zeros((N_MODS - 1 - DEC_BATCH, D_MODEL), F32)], axis=0)
    mods = _ada(cc, w_ada, b_ada).reshape(DEPTH, N_MODS, 1, 6 * D_MODEL)
    cdsd, cs, ab = (jnp.asarray(a.astype(np.float32)).astype(BF16) for a in _dft_consts())
    s5_t, s5_m, s5_n, s5_a = _s5_prep(s5_lambda_re, s5_lambda_im, s5_log_step, s5_b_re, s5_b_im,
                                      s5_c_re, s5_c_im)

    o_q = FOURIER_W
    o_g = o_q + 3 * MLSTM_W
    o_o = o_g + N_GATES
    o_u = o_o + MLSTM_W
    cols = lambda o, w: w_in[:, :, o:o + w]
    w_cat = jnp.concatenate([cols(o_u, S5_W), cols(0, FOURIER_W), cols(o_q, 2 * MLSTM_W), cols(o_o, MLSTM_W)],
                            axis=2).astype(BF16)
    wv_t = jnp.swapaxes(cols(o_q + 2 * MLSTM_W, MLSTM_W), 1, 2).astype(BF16)
    gate_perm = np.arange(N_GATES).reshape(2, 2, MLSTM_HEADS).transpose(1, 0, 2).reshape(-1)
    wg_t = jnp.swapaxes(w_in[:, :, o_g:o_o], 1, 2)[:, gate_perm].astype(BF16)
    bg = b_gates[:, gate_perm, None]
    wf = w_fourier.astype(BF16)
    nw = _pad_heads(mlstm_norm_w, 1).reshape(DEPTH, MLSTM_HEADS, HEAD_PAD, 1)
    dt = jnp.tile(s5_d[:, :, None, :], (1, 1, S5_CHUNK, 1)).reshape(DEPTH, S5_NB, 1, S5_ROW)
    wo_m = jnp.pad(w_out[:, FOURIER_W:FOURIER_W + MLSTM_W].reshape(DEPTH, MLSTM_HEADS, MLSTM_DH, D_MODEL),
                   ((0, 0), (0, 0), (0, HEAD_PAD - MLSTM_DH), (0, 0))).reshape(DEPTH, HEADS_W, D_MODEL)
    wo = jnp.concatenate([w_out[:, :FOURIER_W], wo_m, w_out[:, FOURIER_W + MLSTM_W:]], axis=1).astype(BF16)
    wglu = w_glu.astype(BF16)
    wg, wu, wd = w_gate.astype(BF16), w_up.astype(BF16), w_down.astype(BF16)
    n1, n2 = norm1_w[:, None, :], norm2_w[:, None, :]

    m0 = jnp.swapaxes(state_mlstm_m, 0, 1)
    c0 = jnp.swapaxes(_mlstm_state_in(state_mlstm_C, state_mlstm_n), 0, 1)
    x0 = jnp.concatenate([state_s5_re, state_s5_im, state_s5_im, state_s5_re], axis=-1)
    x0 = jnp.transpose(x0, (1, 2, 0, 3, 4)).reshape(DEPTH, 2, DEC_BATCH, S5_GROUPS * S5_ST2)

    finals = []
    cfin = jnp.zeros((BATCH, DEPTH, 2, MLSTM_HEADS, MLSTM_DH, MLSTM_DH), F32)
    for l in range(DEPTH):
        for p in (PROMPT, SAMPLE):
            x = xs[p]
            zq, zk, zo, zf, zu, vt, gt = _in_proj(p, l, x, mods, n1, w_cat, wv_t, wg_t, bg)
            pr = _gate_prep(p, gt)
            if p is PROMPT:
                fo = _fourier_prompt(l, zf, cdsd, cs, wf)
                mo, cfin, nfin, mfin = _mlstm(p, l, zq, zk, vt, zo, pr, nw, c_all=cfin)
                ys, xfin = _s5(p, l, zu, s5_t, s5_m, s5_n, s5_a, dt, want_final=True)
                finals.append([nfin, mfin, xfin])
            else:
                fo = _fourier_sample(l, zf, cdsd, ab, wf)
                mo, = _mlstm(p, l, zq, zk, vt, zo, pr, nw, init=(m0[l], c0[l]))
                ys, = _s5(p, l, zu, s5_t, s5_m, s5_n, s5_a, dt, x0=x0[l])
            xs[p] = _post(p, l, l == DEPTH - 1, x, mods, fo, mo, ys, wglu, wo, n2, wg, wu, wd, norm_f[None])

    nfin, mfin, xfin = (jnp.stack(parts, axis=1) for parts in zip(*finals))
    xfin = xfin.reshape(2, DEPTH, BATCH, S5_GROUPS, 2, S5_STATE)
    new_re, new_im = (jnp.transpose(xfin[:, :, :, :, i], (2, 1, 0, 3, 4)) for i in range(2))
    return (xs[PROMPT], xs[SAMPLE], cfin, nfin[:, :, :, :, 0], mfin[:, :, :, :, 0, 0], new_re, new_im)
```

```python
import collections
import functools
import math

import numpy as np
import jax
import jax.numpy as jnp
from jax import lax
from jax.experimental import pallas as pl
from jax.experimental.pallas import tpu as pltpu

F32 = jnp.float32
BF16 = jnp.bfloat16

D_MODEL = 1024
BATCH = 32
SEQ = 256
DEPTH = 2
DEC_BATCH = 4
DEC_SEQ = 2048
GRID_W = 64
FOURIER_W = 256
FOURIER_DH = 64
MLSTM_W = 384
MLSTM_HEADS = 4
MLSTM_DH = 96
S5_W = 384
S5_GROUP_CH = 16
S5_GROUPS = 24
S5_STATE = 64
N_GATES = 16
D_FF = 2816
EPS = 1e-6

LANES = 128
SUBLANES = 8
VMEM_LIMIT = 56 * 1024 * 1024
POST_VMEM_LIMIT = 60 * 1024 * 1024

HEAD_PAD = LANES
HEADS_W = MLSTM_HEADS * HEAD_PAD
N_AUG = MLSTM_DH
Z_W = S5_W + FOURIER_W + 3 * MLSTM_W
MLSTM_CHUNK = 256
S5_CHUNK = 16
S5_IN = S5_CHUNK * S5_GROUP_CH
S5_ST = 2 * S5_STATE
S5_ST2 = 2 * S5_ST
S5_GB = LANES // S5_GROUP_CH
S5_NB = S5_W // LANES
S5_ROW = S5_GB * S5_IN
TB = 4
TT = 128
TROWS = TB * TT // S5_CHUNK
HPS = 4
N_MODS = 8
NEG = -1e30

Pass = collections.namedtuple("Pass", "nb seq mod_first mod_each")
PROMPT = Pass(BATCH, SEQ, DEC_BATCH, False)
SAMPLE = Pass(DEC_BATCH, DEC_SEQ, 0, True)
T_PASS = BATCH * SEQ
assert T_PASS == DEC_BATCH * DEC_SEQ
R_PASS = T_PASS // S5_CHUNK

_NT = (((1,), (1,)), ((), ()))
_TN = (((0,), (0,)), ((), ()))


def _cparams(sem):
    return pltpu.CompilerParams(dimension_semantics=sem, vmem_limit_bytes=VMEM_LIMIT)


def _full(a):
    return pl.BlockSpec(a.shape, lambda *_: (0,) * a.ndim)


def _layer(a, l):
    return pl.BlockSpec((None,) + a.shape[1:], lambda *_: (l,) + (0,) * (a.ndim - 1))


def _log_sigmoid(x):
    return jnp.minimum(x, 0.0) - jnp.log1p(jnp.exp(-jnp.abs(x)))


TileSpecs = collections.namedtuple("TileSpecs", "n head tail tok rows mod")


def _tile_specs(p, l):
    nk = p.seq // TT
    n = (p.nb // TB) * nk
    head = lambda i: jnp.minimum(i, n - 1)
    tail = lambda i: jnp.maximum(i - 1, 0)
    tok = lambda w, tile: pl.BlockSpec((TB, TT, w), lambda i: (tile(i) // nk, tile(i) % nk, 0))
    rows = lambda tile: pl.BlockSpec((S5_NB, TROWS, S5_ROW), lambda i: (0, tile(i), 0))

    def mod(tile):
        if p.mod_each:
            return pl.BlockSpec((None, TB, 1, 6 * D_MODEL), lambda i: (l, p.mod_first // TB + tile(i) // nk, 0, 0))
        return pl.BlockSpec((None, 1, 1, 6 * D_MODEL), lambda i: (l, p.mod_first, 0, 0))

    return TileSpecs(n, head, tail, tok, rows, mod)


def _slab(b, c):
    tok = slice(b * TT + c * S5_CHUNK, b * TT + (c + 1) * S5_CHUNK)
    chk = slice((c * TB + b) * S5_CHUNK, (c * TB + b + 1) * S5_CHUNK)
    return tok, chk


def _ada_kernel(c_ref, w_ref, b_ref, o_ref):
    a = c_ref[...]
    a = (a * jax.nn.sigmoid(a)).astype(BF16)
    o_ref[0] = jnp.dot(a, w_ref[0].astype(BF16), preferred_element_type=F32) + b_ref[0]


def _ada(cc, w_ada, b_ada):
    tn = 1536
    return pl.pallas_call(
        _ada_kernel,
        grid=(DEPTH, 6 * D_MODEL // tn),
        in_specs=[pl.BlockSpec((N_MODS, D_MODEL), lambda l, j: (0, 0)),
                  pl.BlockSpec((1, D_MODEL, tn), lambda l, j: (l, 0, j)),
                  pl.BlockSpec((1, 1, tn), lambda l, j: (l, 0, j))],
        out_specs=pl.BlockSpec((1, N_MODS, tn), lambda l, j: (l, 0, j)),
        out_shape=jax.ShapeDtypeStruct((DEPTH, N_MODS, 6 * D_MODEL), F32),
        compiler_params=_cparams(("arbitrary", "arbitrary")),
        name="ada_mod",
    )(cc, w_ada, b_ada.reshape(DEPTH, 1, 6 * D_MODEL))


def _mod_norm(x3, mod, nw, first):
    sh = mod[:, :, first * D_MODEL:(first + 1) * D_MODEL]
    sc = mod[:, :, (first + 1) * D_MODEL:(first + 2) * D_MODEL]
    y = x3 * lax.rsqrt(jnp.mean(x3 * x3, axis=-1, keepdims=True) + EPS) * nw
    return (y * (1.0 + sc) + sh).reshape(-1, D_MODEL).astype(BF16)


def _alternate(gens):
    gens = list(gens)
    while gens:
        alive = []
        for g in gens:
            try:
                next(g)
                alive.append(g)
            except StopIteration:
                pass
        gens = alive


def _in_kernel(x_ref, mod_ref, nw_ref, w_ref, wv_ref, wg_ref, bg_ref,
               zq_ref, zk_ref, zo_ref, zf_ref, zu_ref, vt_ref, gt_ref, scr_ref):
    xn = _mod_norm(x_ref[...], mod_ref[...], nw_ref[...], 0)
    zu = jnp.dot(xn, w_ref[:, :S5_W], preferred_element_type=F32)
    for bl in range(S5_NB):
        zb = zu[:, bl * LANES:(bl + 1) * LANES]
        for b in range(TB):
            for c in range(TT // S5_CHUNK):
                tok, chk = _slab(b, c)
                scr_ref[bl, chk, :] = zb[tok]
        by_token = [scr_ref[bl, pl.ds(s, TROWS, stride=S5_CHUNK), :] for s in range(S5_CHUNK)]
        for gl in range(S5_GB):
            lo = gl * S5_GROUP_CH
            zu_ref[bl, :, gl * S5_IN:(gl + 1) * S5_IN] = jnp.concatenate(
                [x[:, lo:lo + S5_GROUP_CH] for x in by_token], axis=1)
        if bl == 0:
            z = jnp.dot(xn, w_ref[:, S5_W:], preferred_element_type=F32)
    zf_ref[...] = z[:, :FOURIER_W].astype(BF16).reshape(TB, TT, FOURIER_W)
    gap = jnp.zeros((TB * TT, HEAD_PAD - MLSTM_DH), BF16)
    for j, ref in enumerate((zq_ref, zk_ref, zo_ref)):
        t = z[:, FOURIER_W + j * MLSTM_W:FOURIER_W + (j + 1) * MLSTM_W].astype(BF16)
        ref[...] = jnp.concatenate(
            [piece for h in range(MLSTM_HEADS) for piece in (t[:, h * MLSTM_DH:(h + 1) * MLSTM_DH], gap)],
            axis=1).reshape(TB, TT, HEADS_W)
    vt = lax.dot_general(wv_ref[...], xn, _NT, preferred_element_type=F32).astype(BF16)
    gt = lax.dot_general(wg_ref[...], xn, _NT, preferred_element_type=F32) + bg_ref[...]
    for b in range(TB):
        toks = slice(b * TT, (b + 1) * TT)
        for h in range(MLSTM_HEADS):
            vt_ref[b, h * HEAD_PAD:h * HEAD_PAD + MLSTM_DH, :] = vt[h * MLSTM_DH:(h + 1) * MLSTM_DH, toks]
            vt_ref[b, h * HEAD_PAD + MLSTM_DH:(h + 1) * HEAD_PAD, :] = jnp.zeros((HEAD_PAD - MLSTM_DH, TT), BF16)
        gt_ref[b] = gt[:, toks]


def _in_proj(p, l, x, mods, nw, w, wv_t, wg_t, bg):
    ts = _tile_specs(p, l)
    nk = p.seq // TT
    tile = lambda i: i
    chan = lambda c_: pl.BlockSpec((TB, c_, TT), lambda i: (i // nk, 0, i % nk))
    outs = [HEADS_W] * 3 + [FOURIER_W]
    return pl.pallas_call(
        _in_kernel,
        grid=(ts.n,),
        in_specs=[ts.tok(D_MODEL, tile), ts.mod(tile),
                  _layer(nw, l), _layer(w, l), _layer(wv_t, l), _layer(wg_t, l), _layer(bg, l)],
        out_specs=[ts.tok(w_, tile) for w_ in outs] + [ts.rows(tile), chan(HEADS_W), chan(N_GATES)],
        out_shape=[jax.ShapeDtypeStruct((p.nb, p.seq, w_), BF16) for w_ in outs]
        + [jax.ShapeDtypeStruct((S5_NB, R_PASS, S5_ROW), F32),
           jax.ShapeDtypeStruct((p.nb, HEADS_W, p.seq), BF16),
           jax.ShapeDtypeStruct((p.nb, N_GATES, p.seq), F32)],
        scratch_shapes=[pltpu.VMEM((S5_NB, TB * TT, LANES), F32)],
        compiler_params=_cparams(("arbitrary",)),
        name="in_proj_%d" % p.seq,
    )(x, mods, nw, w, wv_t, wg_t, bg)


def _dft_consts():
    d = np.arange(FOURIER_DH)
    phi = 2.0 * np.pi * ((d[:, None] * d[None, :]) % FOURIER_DH) / FOURIER_DH
    eye = np.eye(FOURIER_W // FOURIER_DH)
    cd = np.kron(eye, np.cos(phi)) / math.sqrt(FOURIER_DH)
    sd = np.kron(eye, np.sin(phi)) / math.sqrt(FOURIER_DH)
    s = np.arange(SEQ)
    th = 2.0 * np.pi * ((s[:, None] * s[None, :]) % SEQ) / SEQ
    rows = DEC_SEQ // GRID_W
    pos = np.arange(DEC_SEQ)
    r, c = pos // GRID_W, pos % GRID_W
    ph = ((r[:, None] * r[None, :]) * (GRID_W // rows) + c[:, None] * c[None, :]) % GRID_W
    th2 = 2.0 * np.pi * ph / GRID_W
    return (np.concatenate([cd, sd], axis=1),
            np.concatenate([np.cos(th), -np.sin(th)], axis=1) / math.sqrt(SEQ),
            np.concatenate([np.cos(th2), -np.sin(th2)], axis=1) / math.sqrt(DEC_SEQ))


def _fourier_prompt_kernel(nb, zf_ref, cdsd_ref, cs_ref, wf_ref, o_ref):
    t = jnp.dot(zf_ref[...].reshape(nb * SEQ, FOURIER_W), cdsd_ref[...],
                preferred_element_type=F32).astype(BF16)
    fs = []
    for b in range(nb):
        tb = t[b * SEQ:(b + 1) * SEQ]
        st = jnp.concatenate([tb[:, :FOURIER_W], tb[:, FOURIER_W:]], axis=0)
        fs.append(jnp.dot(cs_ref[...], st, preferred_element_type=F32))
    for b in range(nb):
        o_ref[b] = jnp.dot(fs[b].astype(BF16), wf_ref[...], preferred_element_type=F32).astype(BF16)


def _fourier_prompt(l, zf, cdsd, cs, wf):
    nb = 4
    blk = pl.BlockSpec((nb, SEQ, FOURIER_W), lambda i: (i, 0, 0))
    return pl.pallas_call(
        functools.partial(_fourier_prompt_kernel, nb),
        grid=(BATCH // nb,),
        in_specs=[blk, _full(cdsd), _full(cs), _layer(wf, l)],
        out_specs=blk,
        out_shape=jax.ShapeDtypeStruct((BATCH, SEQ, FOURIER_W), BF16),
        compiler_params=_cparams(("arbitrary",)),
        name="fourier_prompt",
    )(zf, cdsd, cs, wf)


def _fourier_sample_kernel(zf_ref, cdsd_ref, ab_ref, wf_ref, o_ref, tt_ref):
    @pl.when(pl.program_id(0) == 0)
    def _():
        for b in range(DEC_BATCH):
            t = jnp.dot(zf_ref[b], cdsd_ref[...], preferred_element_type=F32).astype(BF16)
            tt_ref[b, 0:DEC_SEQ, :] = t[:, :FOURIER_W]
            tt_ref[b, DEC_SEQ:2 * DEC_SEQ, :] = t[:, FOURIER_W:]

    fs = [jnp.dot(ab_ref[...], tt_ref[b], preferred_element_type=F32) for b in range(DEC_BATCH)]
    for b in range(DEC_BATCH):
        o_ref[b] = jnp.dot(fs[b].astype(BF16), wf_ref[...], preferred_element_type=F32).astype(BF16)


def _fourier_sample(l, zf, cdsd, ab, wf):
    tk = 512
    return pl.pallas_call(
        _fourier_sample_kernel,
        grid=(DEC_SEQ // tk,),
        in_specs=[_full(zf), _full(cdsd), pl.BlockSpec((tk, 2 * DEC_SEQ), lambda i: (i, 0)), _layer(wf, l)],
        out_specs=pl.BlockSpec((DEC_BATCH, tk, FOURIER_W), lambda i: (0, i, 0)),
        out_shape=jax.ShapeDtypeStruct((DEC_BATCH, DEC_SEQ, FOURIER_W), BF16),
        scratch_shapes=[pltpu.VMEM((DEC_BATCH, 2 * DEC_SEQ, FOURIER_W), BF16)],
        compiler_params=_cparams(("arbitrary",)),
        name="fourier_sample",
    )(zf, cdsd, ab, wf)


def _split3(x):
    hi = x.astype(BF16).astype(F32)
    mid = (x - hi).astype(BF16).astype(F32)
    lo = (x - hi - mid).astype(BF16).astype(F32)
    return hi, mid, lo


def _gate_kernel(g_ref, o_ref):
    L = MLSTM_CHUNK
    nrow = N_GATES // 2
    row = lax.broadcasted_iota(jnp.int32, (L, L), 0)
    col = lax.broadcasted_iota(jnp.int32, (L, L), 1)
    tri_pre = jnp.where(row <= col, 1.0, 0.0).astype(BF16)
    tri_suf = jnp.where(row >= col, 1.0, 0.0).astype(BF16)
    is_fwd = lax.broadcasted_iota(jnp.int32, (nrow, L), 0) < MLSTM_HEADS
    lane = lax.broadcasted_iota(jnp.int32, (nrow, L), 1)
    chunks = [(bi, slice(c * L, (c + 1) * L)) for bi in range(g_ref.shape[0]) for c in range(g_ref.shape[2] // L)]
    fold = lambda a: a[0:nrow] + a[nrow:2 * nrow] + a[2 * nrow:]
    bs, rs = [], []
    for bi, cols in chunks:
        lf = _log_sigmoid(g_ref[bi, nrow:, cols])
        parts = jnp.concatenate(_split3(lf), axis=0).astype(BF16)
        pre = jnp.dot(parts, tri_pre, preferred_element_type=F32)
        suf = jnp.dot(parts, tri_suf, preferred_element_type=F32)
        bs.append(jnp.where(is_fwd, fold(pre), fold(suf)))
        rs.append(g_ref[bi, 0:nrow, cols] - bs[-1])
    pms, sms = list(rs), list(rs)
    sh = 1
    while sh < L:
        pms = [jnp.maximum(x, jnp.where(lane >= sh, pltpu.roll(x, sh, 1), NEG)) for x in pms]
        sms = [jnp.maximum(x, jnp.where(lane < L - sh, pltpu.roll(x, L - sh, 1), NEG)) for x in sms]
        sh *= 2
    for (bi, cols), b, r, pm, sm in zip(chunks, bs, rs, pms, sms):
        for q, val in enumerate((b, r, jnp.where(is_fwd, pm, sm))):
            for dh in range(nrow):
                o_ref[bi, dh, q:q + 1, cols] = val[dh:dh + 1]


def _gate_prep(p, gt):
    bb = max(1, DEC_SEQ // p.seq)
    nrow = N_GATES // 2
    return pl.pallas_call(
        _gate_kernel,
        grid=(p.nb // bb,),
        in_specs=[pl.BlockSpec((bb, N_GATES, p.seq), lambda i: (i, 0, 0))],
        out_specs=pl.BlockSpec((bb, nrow, 3, p.seq), lambda i: (i, 0, 0, 0)),
        out_shape=jax.ShapeDtypeStruct((p.nb, nrow, 3, p.seq), F32),
        compiler_params=_cparams(("arbitrary",)),
        name="gate_prep_%d" % p.seq,
    )(gt)


def _mlstm_chunk(q, k, vt, pr, ct, m, fwd, out):
    L = q.shape[0]
    scale = MLSTM_DH ** -0.5
    b, r, cm = pr[0:1], pr[1:2], pr[2:3]
    ones = jnp.ones((3, L), F32)
    zeros = jnp.zeros((SUBLANES - 6, L), F32)
    lhs = jnp.concatenate(_split3(r) + (ones, zeros), axis=0).astype(BF16)
    rhs = jnp.concatenate((ones,) + _split3(-cm) + (zeros,), axis=0).astype(BF16)
    arg = lax.dot_general(lhs, rhs, _TN, preferred_element_type=F32)
    st = lax.dot_general(k, q, _NT, preferred_element_type=F32)
    cq = lax.dot_general(ct.astype(BF16), q, _NT, preferred_element_type=F32)
    last = L - 1 if fwd else 0
    cm_last = cm[:, last:last + 1]
    mx_last = jnp.maximum(m, cm_last)
    vw = (vt.astype(F32) * jnp.exp(r - cm_last)).astype(BF16)
    dct = jnp.dot(vw, k, preferred_element_type=F32)
    yield

    row = lax.broadcasted_iota(jnp.int32, (L, L), 0)
    col = lax.broadcasted_iota(jnp.int32, (L, L), 1)
    e = jnp.where((row <= col) if fwd else (row >= col), jnp.exp(arg), 0.0)
    num = jnp.dot(vt, (st * e).astype(BF16), preferred_element_type=F32)
    ct_new = jnp.exp(m - mx_last) * ct + (jnp.exp(cm_last - mx_last) * scale) * dct
    yield

    mx = jnp.maximum(m, cm)
    num = (jnp.exp(cm - mx) * scale) * num + jnp.exp(m - mx) * cq
    den = num[N_AUG:N_AUG + 1, :]
    h = num * (1.0 / jnp.maximum(jnp.abs(den), jnp.exp(-(b + mx))))
    out += [h, ct_new, b[:, last:last + 1] + mx_last]


def _mlstm_kernel(nc, has_init, want_final, n_prev, *refs):
    refs = list(refs)
    m0_ref, c0_ref = (refs.pop(0), refs.pop(0)) if has_init else (None, None)
    prev_refs = [refs.pop(7) for _ in range(n_prev)]
    q_ref, k_ref, vt_ref, zo_ref, prf_ref, prb_ref, nw_ref, o_ref = refs[:8]
    hbuf_ref = refs[-1]
    L = MLSTM_CHUNK
    bi = pl.program_id(0)
    hg = pl.program_id(1)
    vrow = lax.broadcasted_iota(jnp.int32, (HEAD_PAD, L), 0)

    def chunk(hh, ci, pr_ref, ct, m, dr, out):
        rows = pl.ds(pl.multiple_of(ci * L, L), L)
        lanes = slice(hh * HEAD_PAD, (hh + 1) * HEAD_PAD)
        vt = vt_ref[lanes, rows]
        vt = jnp.where(vrow == N_AUG, jnp.ones_like(vt), vt)
        res = []
        yield from _mlstm_chunk(q_ref[rows, lanes], k_ref[rows, lanes], vt, pr_ref[hh, :, rows], ct, m,
                                dr == 0, res)
        hbuf_ref[hh, dr, ci] = res[0]
        out += res[1:]

    def step(i, carry):
        outs = [[] for _ in range(2 * HPS)]
        gens = []
        for hh in range(HPS):
            cf, mf, cb, mb = carry[4 * hh:4 * hh + 4]
            gens += [chunk(hh, i, prf_ref, cf, mf, 0, outs[2 * hh]),
                     chunk(hh, nc - 1 - i, prb_ref, cb, mb, 1, outs[2 * hh + 1])]
        _alternate(gens)
        return tuple(x for o in outs for x in o)

    carry = []
    for hh in range(HPS):
        for dr in range(2):
            if has_init:
                carry += [c0_ref[0, dr, hh].T, jnp.full((1, 1), m0_ref[bi, dr, hg * HPS + hh], F32)]
            else:
                carry += [jnp.zeros((HEAD_PAD, HEAD_PAD), F32), jnp.zeros((1, 1), F32)]
    carry = step(0, tuple(carry)) if nc == 1 else lax.fori_loop(0, nc, step, tuple(carry))
    if want_final:
        cfin_ref, nfin_ref, mfin_ref = refs[8:11]
        if n_prev:
            for j, ref in enumerate(prev_refs):
                cfin_ref[0, j] = ref[0]
            cfin_ref = cfin_ref.at[:, n_prev]
        for hh in range(HPS):
            for dr in range(2):
                ct, m = carry[4 * hh + 2 * dr], carry[4 * hh + 2 * dr + 1]
                cfin_ref[0, dr, hh] = ct.T[:MLSTM_DH, :MLSTM_DH]
                nfin_ref[0, dr, hh] = ct[N_AUG:N_AUG + 1, :MLSTM_DH]
                mfin_ref[0, dr, hh] = jnp.broadcast_to(m, (SUBLANES, LANES))

    def finish(ci, _):
        rows = pl.ds(pl.multiple_of(ci * L, L), L)
        for hh in range(HPS):
            lanes = slice(hh * HEAD_PAD, (hh + 1) * HEAD_PAD)
            h = jnp.where(vrow < MLSTM_DH, hbuf_ref[hh, 0, ci] + hbuf_ref[hh, 1, ci], 0.0)
            ms = jnp.sum(h * h, axis=0, keepdims=True) * (1.0 / MLSTM_DH)
            hn = (h * lax.rsqrt(ms + EPS) * nw_ref[hh]).T
            o_ref[rows, lanes] = (hn * jax.nn.sigmoid(zo_ref[rows, lanes].astype(F32))).astype(BF16)
        return 0

    if nc == 1:
        finish(0, 0)
    else:
        lax.fori_loop(0, nc, finish, 0)


def _mlstm(p, l, q, k, vt, zo, pr, nw, init=None, c_prev=None):
    want_final = c_prev is not None
    stack = want_final and l == DEPTH - 1
    assert not want_final or len(c_prev) == l
    nc = p.seq // MLSTM_CHUNK
    tok = pl.BlockSpec((None, p.seq, HPS * HEAD_PAD), lambda b, h: (b, 0, h))
    st = lambda r, c: pl.BlockSpec((1, 2, HPS, r, c), lambda b, h: (b, 0, h, 0, 0))
    prs = lambda dr: pl.BlockSpec((None, HPS, 3, p.seq), lambda b, h: (b, dr * (MLSTM_HEADS // HPS) + h, 0, 0))
    in_specs, args = [], []
    if init is not None:
        in_specs += [pl.BlockSpec(memory_space=pltpu.SMEM), st(HEAD_PAD, HEAD_PAD)]
        args += list(init)
    in_specs += [tok, tok, pl.BlockSpec((None, HPS * HEAD_PAD, p.seq), lambda b, h: (b, h, 0)), tok, prs(0), prs(1),
                 pl.BlockSpec((None, HPS, HEAD_PAD, 1), lambda b, h: (l, h, 0, 0))]
    args += [q, k, vt, zo, pr, pr, nw]
    out_specs = [tok]
    out_shape = [jax.ShapeDtypeStruct((p.nb, p.seq, HEADS_W), BF16)]
    if want_final:
        if stack:
            in_specs += [st(MLSTM_DH, MLSTM_DH)] * l
            args += list(c_prev)
            out_specs.append(pl.BlockSpec((1, DEPTH, 2, HPS, MLSTM_DH, MLSTM_DH), lambda b, h: (b, 0, 0, h, 0, 0)))
            out_shape.append(jax.ShapeDtypeStruct((p.nb, DEPTH, 2, MLSTM_HEADS, MLSTM_DH, MLSTM_DH), F32))
        else:
            out_specs.append(st(MLSTM_DH, MLSTM_DH))
            out_shape.append(jax.ShapeDtypeStruct((p.nb, 2, MLSTM_HEADS, MLSTM_DH, MLSTM_DH), F32))
        out_specs += [st(1, MLSTM_DH), st(SUBLANES, LANES)]
        out_shape += [jax.ShapeDtypeStruct((p.nb, 2, MLSTM_HEADS, r, c), F32)
                      for r, c in ((1, MLSTM_DH), (SUBLANES, LANES))]
    return pl.pallas_call(
        functools.partial(_mlstm_kernel, nc, init is not None, want_final, l if stack else 0),
        grid=(p.nb, MLSTM_HEADS // HPS),
        in_specs=in_specs,
        out_specs=out_specs,
        out_shape=out_shape,
        scratch_shapes=[pltpu.VMEM((HPS, 2, nc, HEAD_PAD, MLSTM_CHUNK), F32)],
        compiler_params=_cparams(("arbitrary", "arbitrary")),
        name="mlstm_%d" % p.seq,
    )(*args)


def _cpow(br, bi, e, nbits):
    pr = pi = None
    for bit in range(nbits):
        sel = ((e >> bit) & 1) == 1
        if pr is None:
            pr, pi = jnp.where(sel, br, 1.0), jnp.where(sel, bi, 0.0)
        else:
            pr, pi = jnp.where(sel, pr * br - pi * bi, pr), jnp.where(sel, pr * bi + pi * br, pi)
        if bit + 1 < nbits:
            br, bi = br * br - bi * bi, 2.0 * br * bi
    return pr, pi


def _s5_prep_kernel(lamc_re_ref, lamc_im_ref, lamr_re_ref, lamr_im_ref, lstep_ref,
                    bt_re_ref, bt_im_ref, ct_re_ref, ct_im_ref,
                    t_ref, m_ref, n_ref, a_ref):
    C = S5_CHUNK
    nbits = (C - 1).bit_length()
    assert C == 1 << nbits
    kk = lax.broadcasted_iota(jnp.int32, (S5_STATE, S5_IN), 1) >> 4
    left = lax.broadcasted_iota(jnp.int32, (S5_IN, S5_ST), 1) < S5_STATE
    left16 = lax.broadcasted_iota(jnp.int32, (S5_GROUP_CH, S5_ST), 1) < S5_STATE
    left1 = lax.broadcasted_iota(jnp.int32, (1, S5_ST), 1) < S5_STATE
    lane = lax.broadcasted_iota(jnp.int32, (S5_GROUP_CH, S5_IN), 1)
    sel = jnp.where((lane & (S5_GROUP_CH - 1)) == lax.broadcasted_iota(jnp.int32, (S5_GROUP_CH, S5_IN), 0),
                    1.0, 0.0)
    spread = lambda ref: jnp.dot(ref[0, 0], sel, precision=lax.Precision.HIGHEST, preferred_element_type=F32)
    ct_re = spread(ct_re_ref)
    ct_im = spread(ct_im_ref)
    resp = []
    for d in range(2):
        step = jnp.exp(lstep_ref[0, d, 0])
        lr_c, li_c = lamc_re_ref[0, d, 0] * step, lamc_im_ref[0, d, 0] * step
        lr_r, li_r = lamr_re_ref[0, d, 0], lamr_im_ref[0, d, 0]
        lbc_re, lbc_im = jnp.exp(lr_c) * jnp.cos(li_c), jnp.exp(lr_c) * jnp.sin(li_c)
        mag = jnp.exp(lr_r * step)
        lb_re, lb_im = mag * jnp.cos(li_r * step), mag * jnp.sin(li_r * step)

        pr, pi = _cpow(lbc_re, lbc_im, kk if d == 0 else (C - 1) - kk, nbits)
        pr1, pi1 = pr * lbc_re - pi * lbc_im, pr * lbc_im + pi * lbc_re
        cpr, cpi = ct_re * pr - ct_im * pi, ct_re * pi + ct_im * pr
        cpr1, cpi1 = ct_re * pr1 - ct_im * pi1, ct_re * pi1 + ct_im * pr1

        nr, ni = lb_re - 1.0, lb_im
        den = lr_r * lr_r + li_r * li_r
        kap_re = (nr * lr_r + ni * li_r) / den
        kap_im = (ni * lr_r - nr * li_r) / den
        bb_re = kap_re * bt_re_ref[0, 0] - kap_im * bt_im_ref[0, 0]
        bb_im = kap_re * bt_im_ref[0, 0] + kap_im * bt_re_ref[0, 0]

        resp.append(jnp.dot(jnp.where(left16, bb_re, -bb_im), jnp.concatenate([cpr, cpi], axis=0),
                            precision=lax.Precision.HIGHEST, preferred_element_type=F32))
        m_ref[0, d, 0] = jnp.concatenate([cpr1, -cpi1], axis=0).astype(BF16)

        blocks = [(bb_re, bb_im)]
        for _ in range(C - 1):
            br_, bi_ = blocks[-1]
            blocks.append((br_ * lb_re - bi_ * lb_im, br_ * lb_im + bi_ * lb_re))
        if d == 0:
            blocks.reverse()
        n_re = jnp.concatenate([b_[0] for b_ in blocks], axis=0)
        n_im = jnp.concatenate([b_[1] for b_ in blocks], axis=0)
        n_ref[0, d, 0] = jnp.concatenate([jnp.where(left, n_re, n_im), jnp.where(left, n_im, n_re)],
                                         axis=1).astype(BF16)

        ar, ai = lb_re, lb_im
        for _ in range(nbits):
            ar, ai = ar * ar - ai * ai, 2.0 * ar * ai
        a2 = jnp.where(left1, -ai, ai)
        a_ref[0, d, 0] = jnp.concatenate([jnp.concatenate([ar, ar], axis=1),
                                          jnp.concatenate([a2, -a2], axis=1)], axis=0)

    rf, rb = resp
    for s in range(C):
        nf = S5_GROUP_CH * s
        blk = jnp.where(lane >= nf, pltpu.roll(rf, nf, 1) if nf else rf, 0.0)
        nb = S5_GROUP_CH * (C - 1 - s)
        blk = blk + jnp.where(lane < S5_IN - nb, pltpu.roll(rb, S5_IN - nb, 1) if nb else rb, 0.0)
        t_ref[0, 0, S5_GROUP_CH * s:S5_GROUP_CH * (s + 1), :] = blk.astype(BF16)


def _s5_prep(lam_re, lam_im, log_step, b_re, b_im, c_re, c_im):
    G = S5_GROUPS
    dup = lambda a: jnp.concatenate([a, a], axis=-1)
    lamc = [a.reshape(DEPTH, 2, G, S5_STATE, 1) for a in (lam_re, lam_im)]
    lamr = [dup(a).reshape(DEPTH, 2, G, 1, S5_ST) for a in (lam_re, lam_im)]
    lstep = log_step.reshape(DEPTH, 2, G, 1, 1)
    bt = [dup(jnp.swapaxes(a, 2, 3)) for a in (b_re, b_im)]
    ct = [jnp.swapaxes(a, 2, 3) for a in (c_re, c_im)]
    dspec = lambda r, c: pl.BlockSpec((1, 2, 1, r, c), lambda l, g: (l, 0, g, 0, 0))
    gspec = lambda r, c: pl.BlockSpec((1, 1, r, c), lambda l, g: (l, g, 0, 0))
    t, m, n, a = pl.pallas_call(
        _s5_prep_kernel,
        grid=(DEPTH, G),
        in_specs=[dspec(S5_STATE, 1), dspec(S5_STATE, 1), dspec(1, S5_ST), dspec(1, S5_ST), dspec(1, 1),
                  gspec(S5_GROUP_CH, S5_ST), gspec(S5_GROUP_CH, S5_ST),
                  gspec(S5_STATE, S5_GROUP_CH), gspec(S5_STATE, S5_GROUP_CH)],
        out_specs=[gspec(S5_IN, S5_IN), dspec(S5_ST, S5_IN), dspec(S5_IN, S5_ST2), dspec(2, S5_ST2)],
        out_shape=[jax.ShapeDtypeStruct((DEPTH, G, S5_IN, S5_IN), BF16),
                   jax.ShapeDtypeStruct((DEPTH, 2, G, S5_ST, S5_IN), BF16),
                   jax.ShapeDtypeStruct((DEPTH, 2, G, S5_IN, S5_ST2), BF16),
                   jax.ShapeDtypeStruct((DEPTH, 2, G, 2, S5_ST2), F32)],
        compiler_params=_cparams(("arbitrary", "arbitrary")),
        name="s5_prep",
    )(*lamc, *lamr, lstep, *bt, *ct)
    return t, m, n, jnp.transpose(a, (0, 1, 3, 2, 4)).reshape(DEPTH, 2, 2, G * S5_ST2)


def _s5_kernel(nseg, nchunks, zr_ref, t_ref, m_ref, n_ref, a_ref, x0_ref, d_ref,
               y_ref, xfin_ref, v_ref, xp_ref):
    W = S5_GB * S5_ST2
    for gl in range(S5_GB):
        u = zr_ref[0, :, gl * S5_IN:(gl + 1) * S5_IN].astype(BF16)
        for d in range(2):
            v_ref[d, :, gl * S5_ST2:(gl + 1) * S5_ST2] = jnp.dot(
                u, n_ref[d, gl], preferred_element_type=F32)

    def halves(x, which):
        return [x[:, g * S5_ST2 + h * S5_ST:g * S5_ST2 + (h + 1) * S5_ST]
                for g in range(S5_GB) for h in which]

    per = SUBLANES // TB
    steps = nchunks // per

    for d in range(2):
        a = a_ref[d, 0:1, :]
        a2 = a_ref[d, 1:2, :]

        def advance(x, v):
            swapped = jnp.concatenate(halves(x, (1, 0)), axis=1)
            return a * x + a2 * swapped + v

        for seg in range(nseg):
            def step(i, x):
                si = i if d == 0 else steps - 1 - i
                rows = pl.ds(pl.multiple_of(seg * nchunks * TB + si * SUBLANES, SUBLANES), SUBLANES)
                v = v_ref[d, rows, :]
                order = range(per) if d == 0 else range(per - 1, -1, -1)
                entering = [None] * per
                for j in order:
                    entering[j] = jnp.concatenate(halves(x, (0,)), axis=1)
                    x = advance(x, v[j * TB:(j + 1) * TB])
                xp_ref[d, rows, :] = jnp.concatenate(entering, axis=0)
                return x

            x0 = jnp.zeros((TB, W), F32) if x0_ref is None else x0_ref[d, seg * TB:(seg + 1) * TB, :]
            x = lax.fori_loop(0, steps, step, x0)
            if xfin_ref is not None:
                xfin_ref[d, seg * TB:(seg + 1) * TB, :] = jnp.concatenate(halves(x, (0,)), axis=1)

    for gl in range(S5_GB):
        cols = slice(gl * S5_IN, (gl + 1) * S5_IN)
        u = zr_ref[0, :, cols]
        y = jnp.dot(u.astype(BF16), t_ref[gl], preferred_element_type=F32)
        for d in range(2):
            y = y + jnp.dot(xp_ref[d, :, gl * S5_ST:(gl + 1) * S5_ST].astype(BF16), m_ref[d, gl],
                            preferred_element_type=F32)
        y_ref[0, :, cols] = jax.nn.gelu(y + d_ref[0, :, cols] * u)


def _s5(p, l, zr, t, m, n, a, dt, x0=None, want_final=False):
    G = S5_GROUPS
    nseg = p.nb // TB
    row = pl.BlockSpec((1, R_PASS, S5_ROW), lambda j: (j, 0, 0))
    dsp = lambda r, c: pl.BlockSpec((None, 2, S5_GB, r, c), lambda j: (l, 0, j, 0, 0))
    lsp = lambda r, w: pl.BlockSpec((2, r, S5_GB * w), lambda j: (0, 0, j))
    in_specs = [row, pl.BlockSpec((None, S5_GB, S5_IN, S5_IN), lambda j: (l, j, 0, 0)),
                dsp(S5_ST, S5_IN), dsp(S5_IN, S5_ST2),
                pl.BlockSpec((None, 2, 2, S5_GB * S5_ST2), lambda j: (l, 0, 0, j)),
                pl.BlockSpec((None, 1, 1, S5_ROW), lambda j: (l, j, 0, 0))]
    args = [zr, t, m, n, a, dt]
    if x0 is not None:
        in_specs.append(lsp(p.nb, S5_ST2))
        args.append(x0)
    out_specs = [row]
    out_shape = [jax.ShapeDtypeStruct((S5_NB, R_PASS, S5_ROW), F32)]
    if want_final:
        out_specs.append(lsp(p.nb, S5_ST))
        out_shape.append(jax.ShapeDtypeStruct((2, p.nb, G * S5_ST), F32))

    def body(zr_ref, t_ref, m_ref, n_ref, a_ref, d_ref, *rest):
        rest = list(rest)
        x0_ref = rest.pop(0) if x0 is not None else None
        y_ref = rest.pop(0)
        xfin_ref = rest.pop(0) if want_final else None
        _s5_kernel(nseg, p.seq // S5_CHUNK, zr_ref, t_ref, m_ref, n_ref, a_ref, x0_ref, d_ref,
                   y_ref, xfin_ref, *rest)

    return pl.pallas_call(
        body,
        grid=(S5_NB,),
        in_specs=in_specs,
        out_specs=out_specs,
        out_shape=out_shape,
        scratch_shapes=[pltpu.VMEM((2, R_PASS, S5_GB * S5_ST2), F32),
                        pltpu.VMEM((2, R_PASS, S5_GB * S5_ST), F32)],
        compiler_params=_cparams(("arbitrary",)),
        name="s5_scan_%d" % p.seq,
    )(*args)


def _mix_stages(x_ref, mod_ref, fo_ref, mo_ref, ys_ref, wglu_ref, wo_ref, nw_ref, scr_ref, out):
    mod = mod_ref[...]
    g1 = mod[:, :, 2 * D_MODEL:3 * D_MODEL]
    flat = lambda ref: ref[...].reshape(TB * TT, ref.shape[-1])
    o_m, o_s = FOURIER_W, FOURIER_W + HEADS_W
    mix = (jnp.dot(flat(fo_ref), wo_ref[:o_m, :], preferred_element_type=F32)
           + jnp.dot(flat(mo_ref), wo_ref[o_m:o_s, :], preferred_element_type=F32))
    yield
    blocks = []
    for bl in range(S5_NB):
        by_group = [ys_ref[bl, :, gl * S5_IN:(gl + 1) * S5_IN] for gl in range(S5_GB)]
        for t in range(S5_CHUNK):
            lo = t * S5_GROUP_CH
            scr_ref[bl, pl.ds(t, TROWS, stride=S5_CHUNK), :] = jnp.concatenate(
                [y[:, lo:lo + S5_GROUP_CH] for y in by_group], axis=1)
        slabs = [scr_ref[bl, _slab(b, c)[1], :] for b in range(TB) for c in range(TT // S5_CHUNK)]
        blocks.append(jnp.concatenate(slabs, axis=0))
        yield
    y = jnp.concatenate(blocks, axis=1).astype(BF16)
    gg = jnp.dot(y, wglu_ref[...], preferred_element_type=F32)
    s_out = (gg[:, :S5_W] * jax.nn.sigmoid(gg[:, S5_W:])).astype(BF16)
    mix = mix + jnp.dot(s_out, wo_ref[o_s:, :], preferred_element_type=F32)
    yield
    x1 = x_ref[...] + g1 * mix.reshape(TB, TT, D_MODEL)
    out += [x1, _mod_norm(x1, mod, nw_ref[...], 3)]


FF_SPLIT = (768, 768, 640, 640)
assert sum(FF_SPLIT) == D_FF


def _ffn_stages(final, xn, x1, mod_ref, wg_ref, wu_ref, wd_ref, nf_ref, out):
    ff = None
    o = 0
    for w in FF_SPLIT:
        cols = slice(o, o + w)
        o += w
        a = jnp.dot(xn, wg_ref[:, cols], preferred_element_type=F32)
        u = jnp.dot(xn, wu_ref[:, cols], preferred_element_type=F32)
        h = (a * jax.nn.sigmoid(a) * u).astype(BF16)
        part = jnp.dot(h, wd_ref[cols, :], preferred_element_type=F32)
        ff = part if ff is None else ff + part
        yield
    g2 = mod_ref[...][:, :, 5 * D_MODEL:6 * D_MODEL]
    x2 = x1 + g2 * ff.reshape(TB, TT, D_MODEL)
    if final:
        x2 = x2 * lax.rsqrt(jnp.mean(x2 * x2, axis=-1, keepdims=True) + EPS) * nf_ref[...]
    out.append(x2)


def _post_kernel(final, n, x_ref, moda_ref, fo_ref, mo_ref, ys_ref, modb_ref, wglu_ref, wo_ref, nw_ref,
                 wg_ref, wu_ref, wd_ref, nf_ref, o_ref, x1_ref, xn_ref, scr_ref):
    i = pl.program_id(0)
    cur = i % 2
    prev = 1 - cur

    def run(do_ffn, do_mix):
        res_a, res_b, gens = [], [], []
        if do_ffn:
            gens.append(_ffn_stages(final, xn_ref[prev], x1_ref[prev], modb_ref, wg_ref, wu_ref, wd_ref,
                                    nf_ref, res_b))
        if do_mix:
            gens.append(_mix_stages(x_ref, moda_ref, fo_ref, mo_ref, ys_ref, wglu_ref, wo_ref, nw_ref,
                                    scr_ref, res_a))
        _alternate(gens)
        if do_ffn:
            o_ref[...] = res_b[0]
        if do_mix:
            x1_ref[cur] = res_a[0]
            xn_ref[cur] = res_a[1]

    pl.when(i == 0)(lambda: run(False, True))
    pl.when(jnp.logical_and(i > 0, i < n))(lambda: run(True, True))
    pl.when(i == n)(lambda: run(True, False))


def _post(p, l, final, x, mods, fo, mo, ys, wglu, wo, nw, wg, wu, wd, nf):
    ts = _tile_specs(p, l)
    once = lambda a: pl.BlockSpec((None,) + a.shape[1:], lambda i: (l,) + (0,) * (a.ndim - 1),
                                  pipeline_mode=pl.Buffered(1))
    return pl.pallas_call(
        functools.partial(_post_kernel, final, ts.n),
        grid=(ts.n + 1,),
        in_specs=[ts.tok(D_MODEL, ts.head), ts.mod(ts.head), ts.tok(FOURIER_W, ts.head),
                  ts.tok(HEADS_W, ts.head), ts.rows(ts.head), ts.mod(ts.tail),
                  once(wglu), once(wo), once(nw), once(wg), once(wu), once(wd),
                  pl.BlockSpec((1, D_MODEL), lambda i: (0, 0))],
        out_specs=ts.tok(D_MODEL, ts.tail),
        out_shape=jax.ShapeDtypeStruct((p.nb, p.seq, D_MODEL), F32),
        scratch_shapes=[pltpu.VMEM((2, TB, TT, D_MODEL), F32), pltpu.VMEM((2, TB * TT, D_MODEL), BF16),
                        pltpu.VMEM((S5_NB, TB * TT, LANES), F32)],
        compiler_params=pltpu.CompilerParams(dimension_semantics=("arbitrary",),
                                             vmem_limit_bytes=POST_VMEM_LIMIT),
        name="post_%d" % p.seq,
    )(x, mods, fo, mo, ys, mods, wglu, wo, nw, wg, wu, wd, nf)


def _pad_heads(a, axis):
    shape = a.shape[:axis] + (MLSTM_HEADS, MLSTM_DH) + a.shape[axis + 1:]
    pad = [(0, 0)] * (a.ndim + 1)
    pad[axis + 1] = (0, HEAD_PAD - MLSTM_DH)
    return jnp.pad(a.reshape(shape), pad).reshape(a.shape[:axis] + (HEADS_W,) + a.shape[axis + 1:])


def _mlstm_state_in(c, n):
    cn = jnp.concatenate([c, n[..., None]], axis=-1)
    return jnp.pad(cn, ((0, 0),) * (c.ndim - 2) + ((0, HEAD_PAD - MLSTM_DH), (0, HEAD_PAD - MLSTM_DH - 1)))


def kernel(x_prompt, x_sample, state_mlstm_C, state_mlstm_n, state_mlstm_m, state_s5_re, state_s5_im,
           c, c_ctx, w_ada, b_ada, norm1_w, norm2_w, w_in, b_gates, w_fourier, mlstm_norm_w,
           s5_lambda_re, s5_lambda_im, s5_log_step, s5_b_re, s5_b_im, s5_c_re, s5_c_im, s5_d,
           w_glu, w_out, w_gate, w_up, w_down, norm_f):
    xs = {PROMPT: x_prompt, SAMPLE: x_sample}
    cc = jnp.concatenate([c, c_ctx[None], jnp.zeros((N_MODS - 1 - DEC_BATCH, D_MODEL), F32)], axis=0)
    mods = _ada(cc, w_ada, b_ada).reshape(DEPTH, N_MODS, 1, 6 * D_MODEL)
    cdsd, cs, ab = (jnp.asarray(a.astype(np.float32)).astype(BF16) for a in _dft_consts())
    s5_t, s5_m, s5_n, s5_a = _s5_prep(s5_lambda_re, s5_lambda_im, s5_log_step, s5_b_re, s5_b_im,
                                      s5_c_re, s5_c_im)

    o_q = FOURIER_W
    o_g = o_q + 3 * MLSTM_W
    o_o = o_g + N_GATES
    o_u = o_o + MLSTM_W
    cols = lambda o, w: w_in[:, :, o:o + w]
    w_cat = jnp.concatenate([cols(o_u, S5_W), cols(0, FOURIER_W), cols(o_q, 2 * MLSTM_W), cols(o_o, MLSTM_W)],
                            axis=2).astype(BF16)
    wv_t = jnp.swapaxes(cols(o_q + 2 * MLSTM_W, MLSTM_W), 1, 2).astype(BF16)
    gate_perm = np.arange(N_GATES).reshape(2, 2, MLSTM_HEADS).transpose(1, 0, 2).reshape(-1)
    wg_t = jnp.swapaxes(w_in[:, :, o_g:o_o], 1, 2)[:, gate_perm].astype(BF16)
    bg = b_gates[:, gate_perm, None]
    wf = w_fourier.astype(BF16)
    nw = _pad_heads(mlstm_norm_w, 1).reshape(DEPTH, MLSTM_HEADS, HEAD_PAD, 1)
    dt = jnp.tile(s5_d[:, :, None, :], (1, 1, S5_CHUNK, 1)).reshape(DEPTH, S5_NB, 1, S5_ROW)
    wo_m = jnp.pad(w_out[:, FOURIER_W:FOURIER_W + MLSTM_W].reshape(DEPTH, MLSTM_HEADS, MLSTM_DH, D_MODEL),
                   ((0, 0), (0, 0), (0, HEAD_PAD - MLSTM_DH), (0, 0))).reshape(DEPTH, HEADS_W, D_MODEL)
    wo = jnp.concatenate([w_out[:, :FOURIER_W], wo_m, w_out[:, FOURIER_W + MLSTM_W:]], axis=1).astype(BF16)
    wglu = w_glu.astype(BF16)
    wg, wu, wd = w_gate.astype(BF16), w_up.astype(BF16), w_down.astype(BF16)
    n1, n2 = norm1_w[:, None, :], norm2_w[:, None, :]

    m0 = jnp.swapaxes(state_mlstm_m, 0, 1)
    c0 = jnp.swapaxes(_mlstm_state_in(state_mlstm_C, state_mlstm_n), 0, 1)
    x0 = jnp.concatenate([state_s5_re, state_s5_im, state_s5_im, state_s5_re], axis=-1)
    x0 = jnp.transpose(x0, (1, 2, 0, 3, 4)).reshape(DEPTH, 2, DEC_BATCH, S5_GROUPS * S5_ST2)

    finals, cfins = [], []
    for l in range(DEPTH):
        for p in (PROMPT, SAMPLE):
            x = xs[p]
            zq, zk, zo, zf, zu, vt, gt = _in_proj(p, l, x, mods, n1, w_cat, wv_t, wg_t, bg)
            pr = _gate_prep(p, gt)
            if p is PROMPT:
                fo = _fourier_prompt(l, zf, cdsd, cs, wf)
                mo, cfin, nfin, mfin = _mlstm(p, l, zq, zk, vt, zo, pr, nw, c_prev=cfins)
                cfins.append(cfin)
                ys, xfin = _s5(p, l, zu, s5_t, s5_m, s5_n, s5_a, dt, want_final=True)
                finals.append([nfin, mfin, xfin])
            else:
                fo = _fourier_sample(l, zf, cdsd, ab, wf)
                mo, = _mlstm(p, l, zq, zk, vt, zo, pr, nw, init=(m0[l], c0[l]))
                ys, = _s5(p, l, zu, s5_t, s5_m, s5_n, s5_a, dt, x0=x0[l])
            xs[p] = _post(p, l, l == DEPTH - 1, x, mods, fo, mo, ys, wglu, wo, n2, wg, wu, wd, norm_f[None])

    nfin, mfin, xfin = (jnp.stack(parts, axis=1) for parts in zip(*finals))
    xfin = xfin.reshape(2, DEPTH, BATCH, S5_GROUPS, 2, S5_STATE)
    new_re, new_im = (jnp.transpose(xfin[:, :, :, :, i], (2, 1, 0, 3, 4)) for i in range(2))
    return (xs[PROMPT], xs[SAMPLE], cfin, nfin[:, :, :, :, 0], mfin[:, :, :, :, 0, 0], new_re, new_im)
```

```python
import collections
import functools
import math

import numpy as np
import jax
import jax.numpy as jnp
from jax import lax
from jax.experimental import pallas as pl
from jax.experimental.pallas import tpu as pltpu

F32 = jnp.float32
BF16 = jnp.bfloat16

D_MODEL = 1024
BATCH = 32
SEQ = 256
DEPTH = 2
DEC_BATCH = 4
DEC_SEQ = 2048
GRID_W = 64
FOURIER_W = 256
FOURIER_DH = 64
MLSTM_W = 384
MLSTM_HEADS = 4
MLSTM_DH = 96
S5_W = 384
S5_GROUP_CH = 16
S5_GROUPS = 24
S5_STATE = 64
N_GATES = 16
D_FF = 2816
EPS = 1e-6

LANES = 128
SUBLANES = 8
VMEM_LIMIT = 56 * 1024 * 1024
POST_VMEM_LIMIT = 60 * 1024 * 1024

HEAD_PAD = LANES
HEADS_W = MLSTM_HEADS * HEAD_PAD
N_AUG = MLSTM_DH
Z_W = S5_W + FOURIER_W + 3 * MLSTM_W
MLSTM_CHUNK = 256
S5_CHUNK = 16
S5_IN = S5_CHUNK * S5_GROUP_CH
S5_ST = 2 * S5_STATE
S5_ST2 = 2 * S5_ST
S5_GB = LANES // S5_GROUP_CH
S5_NB = S5_W // LANES
S5_ROW = S5_GB * S5_IN
TB = 4
TT = 128
TROWS = TB * TT // S5_CHUNK
HPS = 4
N_MODS = 8
NEG = -1e30

Pass = collections.namedtuple("Pass", "nb seq mod_first mod_each")
PROMPT = Pass(BATCH, SEQ, DEC_BATCH, False)
SAMPLE = Pass(DEC_BATCH, DEC_SEQ, 0, True)
T_PASS = BATCH * SEQ
assert T_PASS == DEC_BATCH * DEC_SEQ
R_PASS = T_PASS // S5_CHUNK

_NT = (((1,), (1,)), ((), ()))
_TN = (((0,), (0,)), ((), ()))


def _cparams(sem):
    return pltpu.CompilerParams(dimension_semantics=sem, vmem_limit_bytes=VMEM_LIMIT)


def _full(a):
    return pl.BlockSpec(a.shape, lambda *_: (0,) * a.ndim)


def _layer(a, l):
    return pl.BlockSpec((None,) + a.shape[1:], lambda *_: (l,) + (0,) * (a.ndim - 1))


def _log_sigmoid(x):
    return jnp.minimum(x, 0.0) - jnp.log1p(jnp.exp(-jnp.abs(x)))


TileSpecs = collections.namedtuple("TileSpecs", "n head tail tok rows mod")


def _tile_specs(p, l):
    nk = p.seq // TT
    n = (p.nb // TB) * nk
    head = lambda i: jnp.minimum(i, n - 1)
    tail = lambda i: jnp.maximum(i - 1, 0)
    tok = lambda w, tile: pl.BlockSpec((TB, TT, w), lambda i: (tile(i) // nk, tile(i) % nk, 0))
    rows = lambda tile: pl.BlockSpec((S5_NB, TROWS, S5_ROW), lambda i: (0, tile(i), 0))

    def mod(tile):
        if p.mod_each:
            return pl.BlockSpec((None, TB, 1, 6 * D_MODEL), lambda i: (l, p.mod_first // TB + tile(i) // nk, 0, 0))
        return pl.BlockSpec((None, 1, 1, 6 * D_MODEL), lambda i: (l, p.mod_first, 0, 0))

    return TileSpecs(n, head, tail, tok, rows, mod)


def _slab(b, c):
    tok = slice(b * TT + c * S5_CHUNK, b * TT + (c + 1) * S5_CHUNK)
    chk = slice((c * TB + b) * S5_CHUNK, (c * TB + b + 1) * S5_CHUNK)
    return tok, chk


def _ada_kernel(c_ref, w_ref, b_ref, o_ref):
    a = c_ref[...]
    a = (a * jax.nn.sigmoid(a)).astype(BF16)
    o_ref[0] = jnp.dot(a, w_ref[0].astype(BF16), preferred_element_type=F32) + b_ref[0]


def _ada(cc, w_ada, b_ada):
    tn = 1536
    return pl.pallas_call(
        _ada_kernel,
        grid=(DEPTH, 6 * D_MODEL // tn),
        in_specs=[pl.BlockSpec((N_MODS, D_MODEL), lambda l, j: (0, 0)),
                  pl.BlockSpec((1, D_MODEL, tn), lambda l, j: (l, 0, j)),
                  pl.BlockSpec((1, 1, tn), lambda l, j: (l, 0, j))],
        out_specs=pl.BlockSpec((1, N_MODS, tn), lambda l, j: (l, 0, j)),
        out_shape=jax.ShapeDtypeStruct((DEPTH, N_MODS, 6 * D_MODEL), F32),
        compiler_params=_cparams(("arbitrary", "arbitrary")),
        name="ada_mod",
    )(cc, w_ada, b_ada.reshape(DEPTH, 1, 6 * D_MODEL))


def _mod_norm(x3, mod, nw, first):
    sh = mod[:, :, first * D_MODEL:(first + 1) * D_MODEL]
    sc = mod[:, :, (first + 1) * D_MODEL:(first + 2) * D_MODEL]
    y = x3 * lax.rsqrt(jnp.mean(x3 * x3, axis=-1, keepdims=True) + EPS) * nw
    return (y * (1.0 + sc) + sh).reshape(-1, D_MODEL).astype(BF16)


def _alternate(gens):
    gens = list(gens)
    while gens:
        alive = []
        for g in gens:
            try:
                next(g)
                alive.append(g)
            except StopIteration:
                pass
        gens = alive


def _in_kernel(x_ref, mod_ref, nw_ref, w_ref, wv_ref, wg_ref, bg_ref,
               zq_ref, zk_ref, zo_ref, zf_ref, zu_ref, vt_ref, gt_ref, scr_ref):
    xn = _mod_norm(x_ref[...], mod_ref[...], nw_ref[...], 0)
    zu = jnp.dot(xn, w_ref[:, :S5_W], preferred_element_type=F32)
    for bl in range(S5_NB):
        zb = zu[:, bl * LANES:(bl + 1) * LANES]
        for b in range(TB):
            for c in range(TT // S5_CHUNK):
                tok, chk = _slab(b, c)
                scr_ref[bl, chk, :] = zb[tok]
        by_token = [scr_ref[bl, pl.ds(s, TROWS, stride=S5_CHUNK), :] for s in range(S5_CHUNK)]
        for gl in range(S5_GB):
            lo = gl * S5_GROUP_CH
            zu_ref[bl, :, gl * S5_IN:(gl + 1) * S5_IN] = jnp.concatenate(
                [x[:, lo:lo + S5_GROUP_CH] for x in by_token], axis=1)
        if bl == 0:
            z = jnp.dot(xn, w_ref[:, S5_W:], preferred_element_type=F32)
    zf_ref[...] = z[:, :FOURIER_W].astype(BF16).reshape(TB, TT, FOURIER_W)
    gap = jnp.zeros((TB * TT, HEAD_PAD - MLSTM_DH), BF16)
    for j, ref in enumerate((zq_ref, zk_ref, zo_ref)):
        t = z[:, FOURIER_W + j * MLSTM_W:FOURIER_W + (j + 1) * MLSTM_W].astype(BF16)
        ref[...] = jnp.concatenate(
            [piece for h in range(MLSTM_HEADS) for piece in (t[:, h * MLSTM_DH:(h + 1) * MLSTM_DH], gap)],
            axis=1).reshape(TB, TT, HEADS_W)
    vt = lax.dot_general(wv_ref[...], xn, _NT, preferred_element_type=F32).astype(BF16)
    gt = lax.dot_general(wg_ref[...], xn, _NT, preferred_element_type=F32) + bg_ref[...]
    for b in range(TB):
        toks = slice(b * TT, (b + 1) * TT)
        for h in range(MLSTM_HEADS):
            vt_ref[b, h * HEAD_PAD:h * HEAD_PAD + MLSTM_DH, :] = vt[h * MLSTM_DH:(h + 1) * MLSTM_DH, toks]
            vt_ref[b, h * HEAD_PAD + MLSTM_DH:(h + 1) * HEAD_PAD, :] = jnp.zeros((HEAD_PAD - MLSTM_DH, TT), BF16)
        gt_ref[b] = gt[:, toks]


def _in_proj(p, l, x, mods, nw, w, wv_t, wg_t, bg):
    ts = _tile_specs(p, l)
    nk = p.seq // TT
    tile = lambda i: i
    chan = lambda c_: pl.BlockSpec((TB, c_, TT), lambda i: (i // nk, 0, i % nk))
    outs = [HEADS_W] * 3 + [FOURIER_W]
    return pl.pallas_call(
        _in_kernel,
        grid=(ts.n,),
        in_specs=[ts.tok(D_MODEL, tile), ts.mod(tile),
                  _layer(nw, l), _layer(w, l), _layer(wv_t, l), _layer(wg_t, l), _layer(bg, l)],
        out_specs=[ts.tok(w_, tile) for w_ in outs] + [ts.rows(tile), chan(HEADS_W), chan(N_GATES)],
        out_shape=[jax.ShapeDtypeStruct((p.nb, p.seq, w_), BF16) for w_ in outs]
        + [jax.ShapeDtypeStruct((S5_NB, R_PASS, S5_ROW), F32),
           jax.ShapeDtypeStruct((p.nb, HEADS_W, p.seq), BF16),
           jax.ShapeDtypeStruct((p.nb, N_GATES, p.seq), F32)],
        scratch_shapes=[pltpu.VMEM((S5_NB, TB * TT, LANES), F32)],
        compiler_params=_cparams(("arbitrary",)),
        name="in_proj_%d" % p.seq,
    )(x, mods, nw, w, wv_t, wg_t, bg)


def _dft_consts():
    d = np.arange(FOURIER_DH)
    phi = 2.0 * np.pi * ((d[:, None] * d[None, :]) % FOURIER_DH) / FOURIER_DH
    eye = np.eye(FOURIER_W // FOURIER_DH)
    cd = np.kron(eye, np.cos(phi)) / math.sqrt(FOURIER_DH)
    sd = np.kron(eye, np.sin(phi)) / math.sqrt(FOURIER_DH)
    s = np.arange(SEQ)
    th = 2.0 * np.pi * ((s[:, None] * s[None, :]) % SEQ) / SEQ
    rows = DEC_SEQ // GRID_W
    pos = np.arange(DEC_SEQ)
    r, c = pos // GRID_W, pos % GRID_W
    ph = ((r[:, None] * r[None, :]) * (GRID_W // rows) + c[:, None] * c[None, :]) % GRID_W
    th2 = 2.0 * np.pi * ph / GRID_W
    return (np.concatenate([cd, sd], axis=1),
            np.concatenate([np.cos(th), -np.sin(th)], axis=1) / math.sqrt(SEQ),
            np.concatenate([np.cos(th2), -np.sin(th2)], axis=1) / math.sqrt(DEC_SEQ))


def _fourier_prompt_kernel(nb, zf_ref, cdsd_ref, cs_ref, wf_ref, o_ref):
    t = jnp.dot(zf_ref[...].reshape(nb * SEQ, FOURIER_W), cdsd_ref[...],
                preferred_element_type=F32).astype(BF16)
    fs = []
    for b in range(nb):
        tb = t[b * SEQ:(b + 1) * SEQ]
        st = jnp.concatenate([tb[:, :FOURIER_W], tb[:, FOURIER_W:]], axis=0)
        fs.append(jnp.dot(cs_ref[...], st, preferred_element_type=F32))
    for b in range(nb):
        o_ref[b] = jnp.dot(fs[b].astype(BF16), wf_ref[...], preferred_element_type=F32).astype(BF16)


def _fourier_prompt(l, zf, cdsd, cs, wf):
    nb = 4
    blk = pl.BlockSpec((nb, SEQ, FOURIER_W), lambda i: (i, 0, 0))
    return pl.pallas_call(
        functools.partial(_fourier_prompt_kernel, nb),
        grid=(BATCH // nb,),
        in_specs=[blk, _full(cdsd), _full(cs), _layer(wf, l)],
        out_specs=blk,
        out_shape=jax.ShapeDtypeStruct((BATCH, SEQ, FOURIER_W), BF16),
        compiler_params=_cparams(("arbitrary",)),
        name="fourier_prompt",
    )(zf, cdsd, cs, wf)


def _fourier_sample_kernel(zf_ref, cdsd_ref, ab_ref, wf_ref, o_ref, tt_ref):
    @pl.when(pl.program_id(0) == 0)
    def _():
        for b in range(DEC_BATCH):
            t = jnp.dot(zf_ref[b], cdsd_ref[...], preferred_element_type=F32).astype(BF16)
            tt_ref[b, 0:DEC_SEQ, :] = t[:, :FOURIER_W]
            tt_ref[b, DEC_SEQ:2 * DEC_SEQ, :] = t[:, FOURIER_W:]

    fs = [jnp.dot(ab_ref[...], tt_ref[b], preferred_element_type=F32) for b in range(DEC_BATCH)]
    for b in range(DEC_BATCH):
        o_ref[b] = jnp.dot(fs[b].astype(BF16), wf_ref[...], preferred_element_type=F32).astype(BF16)


def _fourier_sample(l, zf, cdsd, ab, wf):
    tk = 512
    return pl.pallas_call(
        _fourier_sample_kernel,
        grid=(DEC_SEQ // tk,),
        in_specs=[_full(zf), _full(cdsd), pl.BlockSpec((tk, 2 * DEC_SEQ), lambda i: (i, 0)), _layer(wf, l)],
        out_specs=pl.BlockSpec((DEC_BATCH, tk, FOURIER_W), lambda i: (0, i, 0)),
        out_shape=jax.ShapeDtypeStruct((DEC_BATCH, DEC_SEQ, FOURIER_W), BF16),
        scratch_shapes=[pltpu.VMEM((DEC_BATCH, 2 * DEC_SEQ, FOURIER_W), BF16)],
        compiler_params=_cparams(("arbitrary",)),
        name="fourier_sample",
    )(zf, cdsd, ab, wf)


def _split3(x):
    hi = x.astype(BF16).astype(F32)
    mid = (x - hi).astype(BF16).astype(F32)
    lo = (x - hi - mid).astype(BF16).astype(F32)
    return hi, mid, lo


def _gate_kernel(g_ref, o_ref):
    L = MLSTM_CHUNK
    nrow = N_GATES // 2
    row = lax.broadcasted_iota(jnp.int32, (L, L), 0)
    col = lax.broadcasted_iota(jnp.int32, (L, L), 1)
    tri_pre = jnp.where(row <= col, 1.0, 0.0).astype(BF16)
    tri_suf = jnp.where(row >= col, 1.0, 0.0).astype(BF16)
    is_fwd = lax.broadcasted_iota(jnp.int32, (nrow, L), 0) < MLSTM_HEADS
    lane = lax.broadcasted_iota(jnp.int32, (nrow, L), 1)
    chunks = [(bi, slice(c * L, (c + 1) * L)) for bi in range(g_ref.shape[0]) for c in range(g_ref.shape[2] // L)]
    fold = lambda a: a[0:nrow] + a[nrow:2 * nrow] + a[2 * nrow:]
    bs, rs = [], []
    for bi, cols in chunks:
        lf = _log_sigmoid(g_ref[bi, nrow:, cols])
        parts = jnp.concatenate(_split3(lf), axis=0).astype(BF16)
        pre = jnp.dot(parts, tri_pre, preferred_element_type=F32)
        suf = jnp.dot(parts, tri_suf, preferred_element_type=F32)
        bs.append(jnp.where(is_fwd, fold(pre), fold(suf)))
        rs.append(g_ref[bi, 0:nrow, cols] - bs[-1])
    pms, sms = list(rs), list(rs)
    sh = 1
    while sh < L:
        pms = [jnp.maximum(x, jnp.where(lane >= sh, pltpu.roll(x, sh, 1), NEG)) for x in pms]
        sms = [jnp.maximum(x, jnp.where(lane < L - sh, pltpu.roll(x, L - sh, 1), NEG)) for x in sms]
        sh *= 2
    for (bi, cols), b, r, pm, sm in zip(chunks, bs, rs, pms, sms):
        for q, val in enumerate((b, r, jnp.where(is_fwd, pm, sm))):
            for dh in range(nrow):
                o_ref[bi, dh, q:q + 1, cols] = val[dh:dh + 1]


def _gate_prep(p, gt):
    bb = max(1, DEC_SEQ // p.seq)
    nrow = N_GATES // 2
    return pl.pallas_call(
        _gate_kernel,
        grid=(p.nb // bb,),
        in_specs=[pl.BlockSpec((bb, N_GATES, p.seq), lambda i: (i, 0, 0))],
        out_specs=pl.BlockSpec((bb, nrow, 3, p.seq), lambda i: (i, 0, 0, 0)),
        out_shape=jax.ShapeDtypeStruct((p.nb, nrow, 3, p.seq), F32),
        compiler_params=_cparams(("arbitrary",)),
        name="gate_prep_%d" % p.seq,
    )(gt)


def _mlstm_chunk(q, k, vt, pr, ct, m, fwd, mxu_arg, out):
    L = q.shape[0]
    scale = MLSTM_DH ** -0.5
    b, r, cm = pr[0:1], pr[1:2], pr[2:3]
    if mxu_arg:
        ones = jnp.ones((3, L), F32)
        zeros = jnp.zeros((SUBLANES - 6, L), F32)
        lhs = jnp.concatenate(_split3(r) + (ones, zeros), axis=0).astype(BF16)
        rhs = jnp.concatenate((ones,) + _split3(-cm) + (zeros,), axis=0).astype(BF16)
        arg = lax.dot_general(lhs, rhs, _TN, preferred_element_type=F32)
    else:
        r_col = jnp.concatenate([pr, jnp.zeros((SUBLANES - 3, L), F32)], axis=0).T[:, 1:2]
        arg = r_col - cm
    st = lax.dot_general(k, q, _NT, preferred_element_type=F32)
    cq = lax.dot_general(ct.astype(BF16), q, _NT, preferred_element_type=F32)
    last = L - 1 if fwd else 0
    cm_last = cm[:, last:last + 1]
    mx_last = jnp.maximum(m, cm_last)
    vw = (vt.astype(F32) * jnp.exp(r - cm_last)).astype(BF16)
    dct = jnp.dot(vw, k, preferred_element_type=F32)
    yield

    row = lax.broadcasted_iota(jnp.int32, (L, L), 0)
    col = lax.broadcasted_iota(jnp.int32, (L, L), 1)
    e = jnp.where((row <= col) if fwd else (row >= col), jnp.exp(arg), 0.0)
    num = jnp.dot(vt, (st * e).astype(BF16), preferred_element_type=F32)
    ct_new = jnp.exp(m - mx_last) * ct + (jnp.exp(cm_last - mx_last) * scale) * dct
    yield

    mx = jnp.maximum(m, cm)
    num = (jnp.exp(cm - mx) * scale) * num + jnp.exp(m - mx) * cq
    den = num[N_AUG:N_AUG + 1, :]
    h = num * (1.0 / jnp.maximum(jnp.abs(den), jnp.exp(-(b + mx))))
    out += [h, ct_new, b[:, last:last + 1] + mx_last]


def _mlstm_kernel(nc, has_init, want_final, n_prev, *refs):
    refs = list(refs)
    m0_ref, c0_ref = (refs.pop(0), refs.pop(0)) if has_init else (None, None)
    prev_refs = [refs.pop(7) for _ in range(n_prev)]
    q_ref, k_ref, vt_ref, zo_ref, prf_ref, prb_ref, nw_ref, o_ref = refs[:8]
    hbuf_ref = refs[-1]
    L = MLSTM_CHUNK
    bi = pl.program_id(0)
    hg = pl.program_id(1)
    vrow = lax.broadcasted_iota(jnp.int32, (HEAD_PAD, L), 0)

    def chunk(hh, ci, pr_ref, ct, m, dr, out):
        rows = pl.ds(pl.multiple_of(ci * L, L), L)
        lanes = slice(hh * HEAD_PAD, (hh + 1) * HEAD_PAD)
        vt = vt_ref[lanes, rows]
        vt = jnp.where(vrow == N_AUG, jnp.ones_like(vt), vt)
        res = []
        yield from _mlstm_chunk(q_ref[rows, lanes], k_ref[rows, lanes], vt, pr_ref[hh, :, rows], ct, m,
                                dr == 0, nc == 1, res)
        hbuf_ref[hh, dr, ci] = res[0]
        out += res[1:]

    def step(i, carry):
        outs = [[] for _ in range(2 * HPS)]
        gens = []
        for hh in range(HPS):
            cf, mf, cb, mb = carry[4 * hh:4 * hh + 4]
            gens += [chunk(hh, i, prf_ref, cf, mf, 0, outs[2 * hh]),
                     chunk(hh, nc - 1 - i, prb_ref, cb, mb, 1, outs[2 * hh + 1])]
        _alternate(gens)
        return tuple(x for o in outs for x in o)

    carry = []
    for hh in range(HPS):
        for dr in range(2):
            if has_init:
                carry += [c0_ref[0, dr, hh].T, jnp.full((1, 1), m0_ref[bi, dr, hg * HPS + hh], F32)]
            else:
                carry += [jnp.zeros((HEAD_PAD, HEAD_PAD), F32), jnp.zeros((1, 1), F32)]
    carry = step(0, tuple(carry)) if nc == 1 else lax.fori_loop(0, nc, step, tuple(carry))
    if want_final:
        cfin_ref, nfin_ref, mfin_ref = refs[8:11]
        if n_prev:
            for j, ref in enumerate(prev_refs):
                cfin_ref[0, j] = ref[0]
            cfin_ref = cfin_ref.at[:, n_prev]
        for hh in range(HPS):
            for dr in range(2):
                ct, m = carry[4 * hh + 2 * dr], carry[4 * hh + 2 * dr + 1]
                cfin_ref[0, dr, hh] = ct.T[:MLSTM_DH, :MLSTM_DH]
                nfin_ref[0, dr, hh] = ct[N_AUG:N_AUG + 1, :MLSTM_DH]
                mfin_ref[0, dr, hh] = jnp.broadcast_to(m, (SUBLANES, LANES))

    def finish(ci, _):
        rows = pl.ds(pl.multiple_of(ci * L, L), L)
        for hh in range(HPS):
            lanes = slice(hh * HEAD_PAD, (hh + 1) * HEAD_PAD)
            h = jnp.where(vrow < MLSTM_DH, hbuf_ref[hh, 0, ci] + hbuf_ref[hh, 1, ci], 0.0)
            ms = jnp.sum(h * h, axis=0, keepdims=True) * (1.0 / MLSTM_DH)
            hn = (h * lax.rsqrt(ms + EPS) * nw_ref[hh]).T
            o_ref[rows, lanes] = (hn * jax.nn.sigmoid(zo_ref[rows, lanes].astype(F32))).astype(BF16)
        return 0

    if nc == 1:
        finish(0, 0)
    else:
        lax.fori_loop(0, nc, finish, 0)


def _mlstm(p, l, q, k, vt, zo, pr, nw, init=None, c_prev=None):
    want_final = c_prev is not None
    stack = want_final and l == DEPTH - 1
    assert not want_final or len(c_prev) == l
    nc = p.seq // MLSTM_CHUNK
    tok = pl.BlockSpec((None, p.seq, HPS * HEAD_PAD), lambda b, h: (b, 0, h))
    st = lambda r, c: pl.BlockSpec((1, 2, HPS, r, c), lambda b, h: (b, 0, h, 0, 0))
    prs = lambda dr: pl.BlockSpec((None, HPS, 3, p.seq), lambda b, h: (b, dr * (MLSTM_HEADS // HPS) + h, 0, 0))
    in_specs, args = [], []
    if init is not None:
        in_specs += [pl.BlockSpec(memory_space=pltpu.SMEM), st(HEAD_PAD, HEAD_PAD)]
        args += list(init)
    in_specs += [tok, tok, pl.BlockSpec((None, HPS * HEAD_PAD, p.seq), lambda b, h: (b, h, 0)), tok, prs(0), prs(1),
                 pl.BlockSpec((None, HPS, HEAD_PAD, 1), lambda b, h: (l, h, 0, 0))]
    args += [q, k, vt, zo, pr, pr, nw]
    out_specs = [tok]
    out_shape = [jax.ShapeDtypeStruct((p.nb, p.seq, HEADS_W), BF16)]
    if want_final:
        if stack:
            in_specs += [st(MLSTM_DH, MLSTM_DH)] * l
            args += list(c_prev)
            out_specs.append(pl.BlockSpec((1, DEPTH, 2, HPS, MLSTM_DH, MLSTM_DH), lambda b, h: (b, 0, 0, h, 0, 0)))
            out_shape.append(jax.ShapeDtypeStruct((p.nb, DEPTH, 2, MLSTM_HEADS, MLSTM_DH, MLSTM_DH), F32))
        else:
            out_specs.append(st(MLSTM_DH, MLSTM_DH))
            out_shape.append(jax.ShapeDtypeStruct((p.nb, 2, MLSTM_HEADS, MLSTM_DH, MLSTM_DH), F32))
        out_specs += [st(1, MLSTM_DH), st(SUBLANES, LANES)]
        out_shape += [jax.ShapeDtypeStruct((p.nb, 2, MLSTM_HEADS, r, c), F32)
                      for r, c in ((1, MLSTM_DH), (SUBLANES, LANES))]
    return pl.pallas_call(
        functools.partial(_mlstm_kernel, nc, init is not None, want_final, l if stack else 0),
        grid=(p.nb, MLSTM_HEADS // HPS),
        in_specs=in_specs,
        out_specs=out_specs,
        out_shape=out_shape,
        scratch_shapes=[pltpu.VMEM((HPS, 2, nc, HEAD_PAD, MLSTM_CHUNK), F32)],
        compiler_params=_cparams(("arbitrary", "arbitrary")),
        name="mlstm_%d" % p.seq,
    )(*args)


def _cpow(br, bi, e, nbits):
    pr = pi = None
    for bit in range(nbits):
        sel = ((e >> bit) & 1) == 1
        if pr is None:
            pr, pi = jnp.where(sel, br, 1.0), jnp.where(sel, bi, 0.0)
        else:
            pr, pi = jnp.where(sel, pr * br - pi * bi, pr), jnp.where(sel, pr * bi + pi * br, pi)
        if bit + 1 < nbits:
            br, bi = br * br - bi * bi, 2.0 * br * bi
    return pr, pi


def _s5_prep_kernel(lamc_re_ref, lamc_im_ref, lamr_re_ref, lamr_im_ref, lstep_ref,
                    bt_re_ref, bt_im_ref, ct_re_ref, ct_im_ref,
                    t_ref, m_ref, n_ref, a_ref):
    C = S5_CHUNK
    nbits = (C - 1).bit_length()
    assert C == 1 << nbits
    kk = lax.broadcasted_iota(jnp.int32, (S5_STATE, S5_IN), 1) >> 4
    left = lax.broadcasted_iota(jnp.int32, (S5_IN, S5_ST), 1) < S5_STATE
    left16 = lax.broadcasted_iota(jnp.int32, (S5_GROUP_CH, S5_ST), 1) < S5_STATE
    left1 = lax.broadcasted_iota(jnp.int32, (1, S5_ST), 1) < S5_STATE
    lane = lax.broadcasted_iota(jnp.int32, (S5_GROUP_CH, S5_IN), 1)
    sel = jnp.where((lane & (S5_GROUP_CH - 1)) == lax.broadcasted_iota(jnp.int32, (S5_GROUP_CH, S5_IN), 0),
                    1.0, 0.0)
    spread = lambda ref: jnp.dot(ref[0, 0], sel, precision=lax.Precision.HIGHEST, preferred_element_type=F32)
    ct_re = spread(ct_re_ref)
    ct_im = spread(ct_im_ref)
    resp = []
    for d in range(2):
        step = jnp.exp(lstep_ref[0, d, 0])
        lr_c, li_c = lamc_re_ref[0, d, 0] * step, lamc_im_ref[0, d, 0] * step
        lr_r, li_r = lamr_re_ref[0, d, 0], lamr_im_ref[0, d, 0]
        lbc_re, lbc_im = jnp.exp(lr_c) * jnp.cos(li_c), jnp.exp(lr_c) * jnp.sin(li_c)
        mag = jnp.exp(lr_r * step)
        lb_re, lb_im = mag * jnp.cos(li_r * step), mag * jnp.sin(li_r * step)

        pr, pi = _cpow(lbc_re, lbc_im, kk if d == 0 else (C - 1) - kk, nbits)
        pr1, pi1 = pr * lbc_re - pi * lbc_im, pr * lbc_im + pi * lbc_re
        cpr, cpi = ct_re * pr - ct_im * pi, ct_re * pi + ct_im * pr
        cpr1, cpi1 = ct_re * pr1 - ct_im * pi1, ct_re * pi1 + ct_im * pr1

        nr, ni = lb_re - 1.0, lb_im
        den = lr_r * lr_r + li_r * li_r
        kap_re = (nr * lr_r + ni * li_r) / den
        kap_im = (ni * lr_r - nr * li_r) / den
        bb_re = kap_re * bt_re_ref[0, 0] - kap_im * bt_im_ref[0, 0]
        bb_im = kap_re * bt_im_ref[0, 0] + kap_im * bt_re_ref[0, 0]

        resp.append(jnp.dot(jnp.where(left16, bb_re, -bb_im), jnp.concatenate([cpr, cpi], axis=0),
                            precision=lax.Precision.HIGHEST, preferred_element_type=F32))
        m_ref[0, d, 0] = jnp.concatenate([cpr1, -cpi1], axis=0).astype(BF16)

        blocks = [(bb_re, bb_im)]
        for _ in range(C - 1):
            br_, bi_ = blocks[-1]
            blocks.append((br_ * lb_re - bi_ * lb_im, br_ * lb_im + bi_ * lb_re))
        if d == 0:
            blocks.reverse()
        n_re = jnp.concatenate([b_[0] for b_ in blocks], axis=0)
        n_im = jnp.concatenate([b_[1] for b_ in blocks], axis=0)
        n_ref[0, d, 0] = jnp.concatenate([jnp.where(left, n_re, n_im), jnp.where(left, n_im, n_re)],
                                         axis=1).astype(BF16)

        ar, ai = lb_re, lb_im
        for _ in range(nbits):
            ar, ai = ar * ar - ai * ai, 2.0 * ar * ai
        a2 = jnp.where(left1, -ai, ai)
        a_ref[0, d, 0] = jnp.concatenate([jnp.concatenate([ar, ar], axis=1),
                                          jnp.concatenate([a2, -a2], axis=1)], axis=0)

    rf, rb = resp
    for s in range(C):
        nf = S5_GROUP_CH * s
        blk = jnp.where(lane >= nf, pltpu.roll(rf, nf, 1) if nf else rf, 0.0)
        nb = S5_GROUP_CH * (C - 1 - s)
        blk = blk + jnp.where(lane < S5_IN - nb, pltpu.roll(rb, S5_IN - nb, 1) if nb else rb, 0.0)
        t_ref[0, 0, S5_GROUP_CH * s:S5_GROUP_CH * (s + 1), :] = blk.astype(BF16)


def _s5_prep(lam_re, lam_im, log_step, b_re, b_im, c_re, c_im):
    G = S5_GROUPS
    dup = lambda a: jnp.concatenate([a, a], axis=-1)
    lamc = [a.reshape(DEPTH, 2, G, S5_STATE, 1) for a in (lam_re, lam_im)]
    lamr = [dup(a).reshape(DEPTH, 2, G, 1, S5_ST) for a in (lam_re, lam_im)]
    lstep = log_step.reshape(DEPTH, 2, G, 1, 1)
    bt = [dup(jnp.swapaxes(a, 2, 3)) for a in (b_re, b_im)]
    ct = [jnp.swapaxes(a, 2, 3) for a in (c_re, c_im)]
    dspec = lambda r, c: pl.BlockSpec((1, 2, 1, r, c), lambda l, g: (l, 0, g, 0, 0))
    gspec = lambda r, c: pl.BlockSpec((1, 1, r, c), lambda l, g: (l, g, 0, 0))
    t, m, n, a = pl.pallas_call(
        _s5_prep_kernel,
        grid=(DEPTH, G),
        in_specs=[dspec(S5_STATE, 1), dspec(S5_STATE, 1), dspec(1, S5_ST), dspec(1, S5_ST), dspec(1, 1),
                  gspec(S5_GROUP_CH, S5_ST), gspec(S5_GROUP_CH, S5_ST),
                  gspec(S5_STATE, S5_GROUP_CH), gspec(S5_STATE, S5_GROUP_CH)],
        out_specs=[gspec(S5_IN, S5_IN), dspec(S5_ST, S5_IN), dspec(S5_IN, S5_ST2), dspec(2, S5_ST2)],
        out_shape=[jax.ShapeDtypeStruct((DEPTH, G, S5_IN, S5_IN), BF16),
                   jax.ShapeDtypeStruct((DEPTH, 2, G, S5_ST, S5_IN), BF16),
                   jax.ShapeDtypeStruct((DEPTH, 2, G, S5_IN, S5_ST2), BF16),
                   jax.ShapeDtypeStruct((DEPTH, 2, G, 2, S5_ST2), F32)],
        compiler_params=_cparams(("arbitrary", "arbitrary")),
        name="s5_prep",
    )(*lamc, *lamr, lstep, *bt, *ct)
    return t, m, n, jnp.transpose(a, (0, 1, 3, 2, 4)).reshape(DEPTH, 2, 2, G * S5_ST2)


def _s5_kernel(nseg, nchunks, zr_ref, t_ref, m_ref, n_ref, a_ref, x0_ref, d_ref,
               y_ref, xfin_ref, v_ref, xp_ref):
    W = S5_GB * S5_ST2
    for gl in range(S5_GB):
        u = zr_ref[0, :, gl * S5_IN:(gl + 1) * S5_IN].astype(BF16)
        for d in range(2):
            v_ref[d, :, gl * S5_ST2:(gl + 1) * S5_ST2] = jnp.dot(
                u, n_ref[d, gl], preferred_element_type=F32)

    def halves(x, which):
        return [x[:, g * S5_ST2 + h * S5_ST:g * S5_ST2 + (h + 1) * S5_ST]
                for g in range(S5_GB) for h in which]

    assert 2 * TB == SUBLANES
    steps = nchunks // 2
    is_fwd = lax.broadcasted_iota(jnp.int32, (SUBLANES, 1), 0) < TB
    both = lambda f, b: jnp.where(is_fwd, f, b)
    flip = lambda x: pltpu.roll(x, TB, 0)
    a = both(a_ref[0, 0:1, :], a_ref[1, 0:1, :])
    a2 = both(a_ref[0, 1:2, :], a_ref[1, 1:2, :])
    state = lambda x: jnp.concatenate(halves(x, (0,)), axis=1)

    def advance(x, v):
        swapped = jnp.concatenate(halves(x, (1, 0)), axis=1)
        return a * x + a2 * swapped + v

    for seg in range(nseg):
        base = seg * nchunks * TB

        def step(i, x):
            rf = pl.ds(pl.multiple_of(base + i * SUBLANES, SUBLANES), SUBLANES)
            rb = pl.ds(pl.multiple_of(base + (steps - 1 - i) * SUBLANES, SUBLANES), SUBLANES)
            vf, vb = v_ref[0, rf, :], v_ref[1, rb, :]
            e1 = state(x)
            x = advance(x, both(vf, vb))
            e2 = state(x)
            x = advance(x, both(flip(vf), flip(vb)))
            xp_ref[0, rf, :] = both(e1, flip(e2))
            xp_ref[1, rb, :] = both(flip(e2), e1)
            return x

        rows = slice(seg * TB, (seg + 1) * TB)
        x0 = (jnp.zeros((SUBLANES, W), F32) if x0_ref is None
              else jnp.concatenate([x0_ref[0, rows, :], x0_ref[1, rows, :]], axis=0))
        x = state(lax.fori_loop(0, steps, step, x0))
        if xfin_ref is not None:
            xfin_ref[0, rows, :] = x[:TB]
            xfin_ref[1, rows, :] = x[TB:]

    for gl in range(S5_GB):
        cols = slice(gl * S5_IN, (gl + 1) * S5_IN)
        u = zr_ref[0, :, cols]
        y = jnp.dot(u.astype(BF16), t_ref[gl], preferred_element_type=F32)
        for d in range(2):
            y = y + jnp.dot(xp_ref[d, :, gl * S5_ST:(gl + 1) * S5_ST].astype(BF16), m_ref[d, gl],
                            preferred_element_type=F32)
        y_ref[0, :, cols] = jax.nn.gelu(y + d_ref[0, :, cols] * u)


def _s5(p, l, zr, t, m, n, a, dt, x0=None, want_final=False):
    G = S5_GROUPS
    nseg = p.nb // TB
    row = pl.BlockSpec((1, R_PASS, S5_ROW), lambda j: (j, 0, 0))
    dsp = lambda r, c: pl.BlockSpec((None, 2, S5_GB, r, c), lambda j: (l, 0, j, 0, 0))
    lsp = lambda r, w: pl.BlockSpec((2, r, S5_GB * w), lambda j: (0, 0, j))
    in_specs = [row, pl.BlockSpec((None, S5_GB, S5_IN, S5_IN), lambda j: (l, j, 0, 0)),
                dsp(S5_ST, S5_IN), dsp(S5_IN, S5_ST2),
                pl.BlockSpec((None, 2, 2, S5_GB * S5_ST2), lambda j: (l, 0, 0, j)),
                pl.BlockSpec((None, 1, 1, S5_ROW), lambda j: (l, j, 0, 0))]
    args = [zr, t, m, n, a, dt]
    if x0 is not None:
        in_specs.append(lsp(p.nb, S5_ST2))
        args.append(x0)
    out_specs = [row]
    out_shape = [jax.ShapeDtypeStruct((S5_NB, R_PASS, S5_ROW), F32)]
    if want_final:
        out_specs.append(lsp(p.nb, S5_ST))
        out_shape.append(jax.ShapeDtypeStruct((2, p.nb, G * S5_ST), F32))

    def body(zr_ref, t_ref, m_ref, n_ref, a_ref, d_ref, *rest):
        rest = list(rest)
        x0_ref = rest.pop(0) if x0 is not None else None
        y_ref = rest.pop(0)
        xfin_ref = rest.pop(0) if want_final else None
        _s5_kernel(nseg, p.seq // S5_CHUNK, zr_ref, t_ref, m_ref, n_ref, a_ref, x0_ref, d_ref,
                   y_ref, xfin_ref, *rest)

    return pl.pallas_call(
        body,
        grid=(S5_NB,),
        in_specs=in_specs,
        out_specs=out_specs,
        out_shape=out_shape,
        scratch_shapes=[pltpu.VMEM((2, R_PASS, S5_GB * S5_ST2), F32),
                        pltpu.VMEM((2, R_PASS, S5_GB * S5_ST), F32)],
        compiler_params=_cparams(("arbitrary",)),
        name="s5_scan_%d" % p.seq,
    )(*args)


def _mix_stages(x_ref, mod_ref, fo_ref, mo_ref, ys_ref, wglu_ref, wo_ref, nw_ref, scr_ref, out):
    mod = mod_ref[...]
    g1 = mod[:, :, 2 * D_MODEL:3 * D_MODEL]
    flat = lambda ref: ref[...].reshape(TB * TT, ref.shape[-1])
    o_m, o_s = FOURIER_W, FOURIER_W + HEADS_W
    mix = (jnp.dot(flat(fo_ref), wo_ref[:o_m, :], preferred_element_type=F32)
           + jnp.dot(flat(mo_ref), wo_ref[o_m:o_s, :], preferred_element_type=F32))
    yield
    blocks = []
    for bl in range(S5_NB):
        by_group = [ys_ref[bl, :, gl * S5_IN:(gl + 1) * S5_IN] for gl in range(S5_GB)]
        for t in range(S5_CHUNK):
            lo = t * S5_GROUP_CH
            scr_ref[bl, pl.ds(t, TROWS, stride=S5_CHUNK), :] = jnp.concatenate(
                [y[:, lo:lo + S5_GROUP_CH] for y in by_group], axis=1)
        slabs = [scr_ref[bl, _slab(b, c)[1], :] for b in range(TB) for c in range(TT // S5_CHUNK)]
        blocks.append(jnp.concatenate(slabs, axis=0))
        yield
    y = jnp.concatenate(blocks, axis=1).astype(BF16)
    gg = jnp.dot(y, wglu_ref[...], preferred_element_type=F32)
    s_out = (gg[:, :S5_W] * jax.nn.sigmoid(gg[:, S5_W:])).astype(BF16)
    mix = mix + jnp.dot(s_out, wo_ref[o_s:, :], preferred_element_type=F32)
    yield
    x1 = x_ref[...] + g1 * mix.reshape(TB, TT, D_MODEL)
    out += [x1, _mod_norm(x1, mod, nw_ref[...], 3)]


FF_SPLIT = (768, 768, 640, 640)
assert sum(FF_SPLIT) == D_FF


def _ffn_stages(final, xn, x1, mod_ref, wg_ref, wu_ref, wd_ref, nf_ref, out):
    ff = None
    o = 0
    for w in FF_SPLIT:
        cols = slice(o, o + w)
        o += w
        a = jnp.dot(xn, wg_ref[:, cols], preferred_element_type=F32)
        u = jnp.dot(xn, wu_ref[:, cols], preferred_element_type=F32)
        h = (a * jax.nn.sigmoid(a) * u).astype(BF16)
        part = jnp.dot(h, wd_ref[cols, :], preferred_element_type=F32)
        ff = part if ff is None else ff + part
        yield
    g2 = mod_ref[...][:, :, 5 * D_MODEL:6 * D_MODEL]
    x2 = x1 + g2 * ff.reshape(TB, TT, D_MODEL)
    if final:
        x2 = x2 * lax.rsqrt(jnp.mean(x2 * x2, axis=-1, keepdims=True) + EPS) * nf_ref[...]
    out.append(x2)


def _post_kernel(final, n, x_ref, moda_ref, fo_ref, mo_ref, ys_ref, modb_ref, wglu_ref, wo_ref, nw_ref,
                 wg_ref, wu_ref, wd_ref, nf_ref, o_ref, x1_ref, xn_ref, scr_ref):
    i = pl.program_id(0)
    cur = i % 2
    prev = 1 - cur

    def run(do_ffn, do_mix):
        res_a, res_b, gens = [], [], []
        if do_ffn:
            gens.append(_ffn_stages(final, xn_ref[prev], x1_ref[prev], modb_ref, wg_ref, wu_ref, wd_ref,
                                    nf_ref, res_b))
        if do_mix:
            gens.append(_mix_stages(x_ref, moda_ref, fo_ref, mo_ref, ys_ref, wglu_ref, wo_ref, nw_ref,
                                    scr_ref, res_a))
        _alternate(gens)
        if do_ffn:
            o_ref[...] = res_b[0]
        if do_mix:
            x1_ref[cur] = res_a[0]
            xn_ref[cur] = res_a[1]

    pl.when(i == 0)(lambda: run(False, True))
    pl.when(jnp.logical_and(i > 0, i < n))(lambda: run(True, True))
    pl.when(i == n)(lambda: run(True, False))


def _post(p, l, final, x, mods, fo, mo, ys, wglu, wo, nw, wg, wu, wd, nf):
    ts = _tile_specs(p, l)
    once = lambda a: pl.BlockSpec((None,) + a.shape[1:], lambda i: (l,) + (0,) * (a.ndim - 1),
                                  pipeline_mode=pl.Buffered(1))
    return pl.pallas_call(
        functools.partial(_post_kernel, final, ts.n),
        grid=(ts.n + 1,),
        in_specs=[ts.tok(D_MODEL, ts.head), ts.mod(ts.head), ts.tok(FOURIER_W, ts.head),
                  ts.tok(HEADS_W, ts.head), ts.rows(ts.head), ts.mod(ts.tail),
                  once(wglu), once(wo), once(nw), once(wg), once(wu), once(wd),
                  pl.BlockSpec((1, D_MODEL), lambda i: (0, 0))],
        out_specs=ts.tok(D_MODEL, ts.tail),
        out_shape=jax.ShapeDtypeStruct((p.nb, p.seq, D_MODEL), F32),
        scratch_shapes=[pltpu.VMEM((2, TB, TT, D_MODEL), F32), pltpu.VMEM((2, TB * TT, D_MODEL), BF16),
                        pltpu.VMEM((S5_NB, TB * TT, LANES), F32)],
        compiler_params=pltpu.CompilerParams(dimension_semantics=("arbitrary",),
                                             vmem_limit_bytes=POST_VMEM_LIMIT),
        name="post_%d" % p.seq,
    )(x, mods, fo, mo, ys, mods, wglu, wo, nw, wg, wu, wd, nf)


def _pad_heads(a, axis):
    shape = a.shape[:axis] + (MLSTM_HEADS, MLSTM_DH) + a.shape[axis + 1:]
    pad = [(0, 0)] * (a.ndim + 1)
    pad[axis + 1] = (0, HEAD_PAD - MLSTM_DH)
    return jnp.pad(a.reshape(shape), pad).reshape(a.shape[:axis] + (HEADS_W,) + a.shape[axis + 1:])


def _mlstm_state_in(c, n):
    cn = jnp.concatenate([c, n[..., None]], axis=-1)
    return jnp.pad(cn, ((0, 0),) * (c.ndim - 2) + ((0, HEAD_PAD - MLSTM_DH), (0, HEAD_PAD - MLSTM_DH - 1)))


def kernel(x_prompt, x_sample, state_mlstm_C, state_mlstm_n, state_mlstm_m, state_s5_re, state_s5_im,
           c, c_ctx, w_ada, b_ada, norm1_w, norm2_w, w_in, b_gates, w_fourier, mlstm_norm_w,
           s5_lambda_re, s5_lambda_im, s5_log_step, s5_b_re, s5_b_im, s5_c_re, s5_c_im, s5_d,
           w_glu, w_out, w_gate, w_up, w_down, norm_f):
    xs = {PROMPT: x_prompt, SAMPLE: x_sample}
    cc = jnp.concatenate([c, c_ctx[None], jnp.zeros((N_MODS - 1 - DEC_BATCH, D_MODEL), F32)], axis=0)
    mods = _ada(cc, w_ada, b_ada).reshape(DEPTH, N_MODS, 1, 6 * D_MODEL)
    cdsd, cs, ab = (jnp.asarray(a.astype(np.float32)).astype(BF16) for a in _dft_consts())
    s5_t, s5_m, s5_n, s5_a = _s5_prep(s5_lambda_re, s5_lambda_im, s5_log_step, s5_b_re, s5_b_im,
                                      s5_c_re, s5_c_im)

    o_q = FOURIER_W
    o_g = o_q + 3 * MLSTM_W
    o_o = o_g + N_GATES
    o_u = o_o + MLSTM_W
    cols = lambda o, w: w_in[:, :, o:o + w]
    w_cat = jnp.concatenate([cols(o_u, S5_W), cols(0, FOURIER_W), cols(o_q, 2 * MLSTM_W), cols(o_o, MLSTM_W)],
                            axis=2).astype(BF16)
    wv_t = jnp.swapaxes(cols(o_q + 2 * MLSTM_W, MLSTM_W), 1, 2).astype(BF16)
    gate_perm = np.arange(N_GATES).reshape(2, 2, MLSTM_HEADS).transpose(1, 0, 2).reshape(-1)
    wg_t = jnp.swapaxes(w_in[:, :, o_g:o_o], 1, 2)[:, gate_perm].astype(BF16)
    bg = b_gates[:, gate_perm, None]
    wf = w_fourier.astype(BF16)
    nw = _pad_heads(mlstm_norm_w, 1).reshape(DEPTH, MLSTM_HEADS, HEAD_PAD, 1)
    dt = jnp.tile(s5_d[:, :, None, :], (1, 1, S5_CHUNK, 1)).reshape(DEPTH, S5_NB, 1, S5_ROW)
    wo_m = jnp.pad(w_out[:, FOURIER_W:FOURIER_W + MLSTM_W].reshape(DEPTH, MLSTM_HEADS, MLSTM_DH, D_MODEL),
                   ((0, 0), (0, 0), (0, HEAD_PAD - MLSTM_DH), (0, 0))).reshape(DEPTH, HEADS_W, D_MODEL)
    wo = jnp.concatenate([w_out[:, :FOURIER_W], wo_m, w_out[:, FOURIER_W + MLSTM_W:]], axis=1).astype(BF16)
    wglu = w_glu.astype(BF16)
    wg, wu, wd = w_gate.astype(BF16), w_up.astype(BF16), w_down.astype(BF16)
    n1, n2 = norm1_w[:, None, :], norm2_w[:, None, :]

    m0 = jnp.swapaxes(state_mlstm_m, 0, 1)
    c0 = jnp.swapaxes(_mlstm_state_in(state_mlstm_C, state_mlstm_n), 0, 1)
    x0 = jnp.concatenate([state_s5_re, state_s5_im, state_s5_im, state_s5_re], axis=-1)
    x0 = jnp.transpose(x0, (1, 2, 0, 3, 4)).reshape(DEPTH, 2, DEC_BATCH, S5_GROUPS * S5_ST2)

    finals, cfins = [], []
    for l in range(DEPTH):
        for p in (PROMPT, SAMPLE):
            x = xs[p]
            zq, zk, zo, zf, zu, vt, gt = _in_proj(p, l, x, mods, n1, w_cat, wv_t, wg_t, bg)
            pr = _gate_prep(p, gt)
            if p is PROMPT:
                fo = _fourier_prompt(l, zf, cdsd, cs, wf)
                mo, cfin, nfin, mfin = _mlstm(p, l, zq, zk, vt, zo, pr, nw, c_prev=cfins)
                cfins.append(cfin)
                ys, xfin = _s5(p, l, zu, s5_t, s5_m, s5_n, s5_a, dt, want_final=True)
                finals.append([nfin, mfin, xfin])
            else:
                fo = _fourier_sample(l, zf, cdsd, ab, wf)
                mo, = _mlstm(p, l, zq, zk, vt, zo, pr, nw, init=(m0[l], c0[l]))
                ys, = _s5(p, l, zu, s5_t, s5_m, s5_n, s5_a, dt, x0=x0[l])
            xs[p] = _post(p, l, l == DEPTH - 1, x, mods, fo, mo, ys, wglu, wo, n2, wg, wu, wd, norm_f[None])

    nfin, mfin, xfin = (jnp.stack(parts, axis=1) for parts in zip(*finals))
    xfin = xfin.reshape(2, DEPTH, BATCH, S5_GROUPS, 2, S5_STATE)
    new_re, new_im = (jnp.transpose(xfin[:, :, :, :, i], (2, 1, 0, 3, 4)) for i in range(2))
    return (xs[PROMPT], xs[SAMPLE], cfin, nfin[:, :, :, :, 0], mfin[:, :, :, :, 0, 0], new_re, new_im)
```

```python
import collections
import functools
import math

import numpy as np
import jax
import jax.numpy as jnp
from jax import lax
from jax.experimental import pallas as pl
from jax.experimental.pallas import tpu as pltpu

F32 = jnp.float32
BF16 = jnp.bfloat16

D_MODEL = 1024
BATCH = 32
SEQ = 256
DEPTH = 2
DEC_BATCH = 4
DEC_SEQ = 2048
GRID_W = 64
FOURIER_W = 256
FOURIER_DH = 64
MLSTM_W = 384
MLSTM_HEADS = 4
MLSTM_DH = 96
S5_W = 384
S5_GROUP_CH = 16
S5_GROUPS = 24
S5_STATE = 64
N_GATES = 16
D_FF = 2816
EPS = 1e-6

LANES = 128
SUBLANES = 8
MXU_N = 256
VMEM_LIMIT = 56 * 1024 * 1024
POST_VMEM_LIMIT = 60 * 1024 * 1024

HEAD_PAD = LANES
HEADS_W = MLSTM_HEADS * HEAD_PAD
N_AUG = MLSTM_DH
Z_W = S5_W + FOURIER_W + 3 * MLSTM_W
MLSTM_CHUNK = 256
S5_CHUNK = 16
S5_IN = S5_CHUNK * S5_GROUP_CH
S5_ST = 2 * S5_STATE
S5_ST2 = 2 * S5_ST
S5_GB = LANES // S5_GROUP_CH
S5_NB = S5_W // LANES
S5_ROW = S5_GB * S5_IN
TB = 4
TT = 128
TROWS = TB * TT // S5_CHUNK
HPS = 4
N_MODS = 8
NEG = -1e30

Pass = collections.namedtuple("Pass", "nb seq mod_first mod_each")
PROMPT = Pass(BATCH, SEQ, DEC_BATCH, False)
SAMPLE = Pass(DEC_BATCH, DEC_SEQ, 0, True)
T_PASS = BATCH * SEQ
assert T_PASS == DEC_BATCH * DEC_SEQ
R_PASS = T_PASS // S5_CHUNK

_NT = (((1,), (1,)), ((), ()))
_TN = (((0,), (0,)), ((), ()))


def _cparams(sem):
    return pltpu.CompilerParams(dimension_semantics=sem, vmem_limit_bytes=VMEM_LIMIT)


def _full(a):
    return pl.BlockSpec(a.shape, lambda *_: (0,) * a.ndim)


def _layer(a, l):
    return pl.BlockSpec((None,) + a.shape[1:], lambda *_: (l,) + (0,) * (a.ndim - 1))


def _log_sigmoid(x):
    return jnp.minimum(x, 0.0) - jnp.log1p(jnp.exp(-jnp.abs(x)))


TileSpecs = collections.namedtuple("TileSpecs", "n head tail tok rows mod")


def _tile_specs(p, l):
    nk = p.seq // TT
    n = (p.nb // TB) * nk
    head = lambda i: jnp.minimum(i, n - 1)
    tail = lambda i: jnp.maximum(i - 1, 0)
    tok = lambda w, tile: pl.BlockSpec((TB, TT, w), lambda i: (tile(i) // nk, tile(i) % nk, 0))
    rows = lambda tile: pl.BlockSpec((S5_NB, TROWS, S5_ROW), lambda i: (0, tile(i), 0))

    def mod(tile):
        if p.mod_each:
            return pl.BlockSpec((None, TB, 1, 6 * D_MODEL), lambda i: (l, p.mod_first // TB + tile(i) // nk, 0, 0))
        return pl.BlockSpec((None, 1, 1, 6 * D_MODEL), lambda i: (l, p.mod_first, 0, 0))

    return TileSpecs(n, head, tail, tok, rows, mod)


def _slab(b, c):
    tok = slice(b * TT + c * S5_CHUNK, b * TT + (c + 1) * S5_CHUNK)
    chk = slice((c * TB + b) * S5_CHUNK, (c * TB + b + 1) * S5_CHUNK)
    return tok, chk


def _ada_kernel(c_ref, w_ref, b_ref, o_ref):
    a = c_ref[...]
    a = (a * jax.nn.sigmoid(a)).astype(BF16)
    o_ref[0] = jnp.dot(a, w_ref[0].astype(BF16), preferred_element_type=F32) + b_ref[0]


def _ada(cc, w_ada, b_ada):
    tn = 1536
    return pl.pallas_call(
        _ada_kernel,
        grid=(DEPTH, 6 * D_MODEL // tn),
        in_specs=[pl.BlockSpec((N_MODS, D_MODEL), lambda l, j: (0, 0)),
                  pl.BlockSpec((1, D_MODEL, tn), lambda l, j: (l, 0, j)),
                  pl.BlockSpec((1, 1, tn), lambda l, j: (l, 0, j))],
        out_specs=pl.BlockSpec((1, N_MODS, tn), lambda l, j: (l, 0, j)),
        out_shape=jax.ShapeDtypeStruct((DEPTH, N_MODS, 6 * D_MODEL), F32),
        compiler_params=_cparams(("arbitrary", "arbitrary")),
        name="ada_mod",
    )(cc, w_ada, b_ada.reshape(DEPTH, 1, 6 * D_MODEL))


def _mod_norm(x3, mod, nw, first):
    sh = mod[:, :, first * D_MODEL:(first + 1) * D_MODEL]
    sc = mod[:, :, (first + 1) * D_MODEL:(first + 2) * D_MODEL]
    y = x3 * lax.rsqrt(jnp.mean(x3 * x3, axis=-1, keepdims=True) + EPS) * nw
    return (y * (1.0 + sc) + sh).reshape(-1, D_MODEL).astype(BF16)


def _alternate(gens):
    gens = list(gens)
    while gens:
        alive = []
        for g in gens:
            try:
                next(g)
                alive.append(g)
            except StopIteration:
                pass
        gens = alive


def _in_kernel(x_ref, mod_ref, nw_ref, w_ref, wv_ref, wg_ref, bg_ref,
               zq_ref, zk_ref, zo_ref, zf_ref, zu_ref, vt_ref, gt_ref, scr_ref):
    xn = _mod_norm(x_ref[...], mod_ref[...], nw_ref[...], 0)
    w_first = -(-S5_W // MXU_N) * MXU_N
    zu = jnp.dot(xn, w_ref[:, :w_first], preferred_element_type=F32)
    for bl in range(S5_NB):
        zb = zu[:, bl * LANES:(bl + 1) * LANES]
        for b in range(TB):
            for c in range(TT // S5_CHUNK):
                tok, chk = _slab(b, c)
                scr_ref[bl, chk, :] = zb[tok]
        by_token = [scr_ref[bl, pl.ds(s, TROWS, stride=S5_CHUNK), :] for s in range(S5_CHUNK)]
        for gl in range(S5_GB):
            lo = gl * S5_GROUP_CH
            zu_ref[bl, :, gl * S5_IN:(gl + 1) * S5_IN] = jnp.concatenate(
                [x[:, lo:lo + S5_GROUP_CH] for x in by_token], axis=1)
        if bl == 0:
            z = jnp.concatenate([zu[:, S5_W:], jnp.dot(xn, w_ref[:, w_first:], preferred_element_type=F32)],
                                axis=1)
    zf_ref[...] = z[:, :FOURIER_W].astype(BF16).reshape(TB, TT, FOURIER_W)
    gap = jnp.zeros((TB * TT, HEAD_PAD - MLSTM_DH), BF16)
    for j, ref in enumerate((zq_ref, zk_ref, zo_ref)):
        t = z[:, FOURIER_W + j * MLSTM_W:FOURIER_W + (j + 1) * MLSTM_W].astype(BF16)
        ref[...] = jnp.concatenate(
            [piece for h in range(MLSTM_HEADS) for piece in (t[:, h * MLSTM_DH:(h + 1) * MLSTM_DH], gap)],
            axis=1).reshape(TB, TT, HEADS_W)
    vt = lax.dot_general(wv_ref[...], xn, _NT, preferred_element_type=F32).astype(BF16)
    gt = lax.dot_general(wg_ref[...], xn, _NT, preferred_element_type=F32) + bg_ref[...]
    for b in range(TB):
        toks = slice(b * TT, (b + 1) * TT)
        for h in range(MLSTM_HEADS):
            vt_ref[b, h * HEAD_PAD:h * HEAD_PAD + MLSTM_DH, :] = vt[h * MLSTM_DH:(h + 1) * MLSTM_DH, toks]
            vt_ref[b, h * HEAD_PAD + MLSTM_DH:(h + 1) * HEAD_PAD, :] = jnp.zeros((HEAD_PAD - MLSTM_DH, TT), BF16)
        gt_ref[b] = gt[:, toks]


def _in_proj(p, l, x, mods, nw, w, wv_t, wg_t, bg):
    ts = _tile_specs(p, l)
    nk = p.seq // TT
    tile = lambda i: i
    chan = lambda c_: pl.BlockSpec((TB, c_, TT), lambda i: (i // nk, 0, i % nk))
    outs = [HEADS_W] * 3 + [FOURIER_W]
    return pl.pallas_call(
        _in_kernel,
        grid=(ts.n,),
        in_specs=[ts.tok(D_MODEL, tile), ts.mod(tile),
                  _layer(nw, l), _layer(w, l), _layer(wv_t, l), _layer(wg_t, l), _layer(bg, l)],
        out_specs=[ts.tok(w_, tile) for w_ in outs] + [ts.rows(tile), chan(HEADS_W), chan(N_GATES)],
        out_shape=[jax.ShapeDtypeStruct((p.nb, p.seq, w_), BF16) for w_ in outs]
        + [jax.ShapeDtypeStruct((S5_NB, R_PASS, S5_ROW), F32),
           jax.ShapeDtypeStruct((p.nb, HEADS_W, p.seq), BF16),
           jax.ShapeDtypeStruct((p.nb, N_GATES, p.seq), F32)],
        scratch_shapes=[pltpu.VMEM((S5_NB, TB * TT, LANES), F32)],
        compiler_params=_cparams(("arbitrary",)),
        name="in_proj_%d" % p.seq,
    )(x, mods, nw, w, wv_t, wg_t, bg)


def _dft_consts():
    d = np.arange(FOURIER_DH)
    phi = 2.0 * np.pi * ((d[:, None] * d[None, :]) % FOURIER_DH) / FOURIER_DH
    eye = np.eye(FOURIER_W // FOURIER_DH)
    cd = np.kron(eye, np.cos(phi)) / math.sqrt(FOURIER_DH)
    sd = np.kron(eye, np.sin(phi)) / math.sqrt(FOURIER_DH)
    s = np.arange(SEQ)
    th = 2.0 * np.pi * ((s[:, None] * s[None, :]) % SEQ) / SEQ
    rows = DEC_SEQ // GRID_W
    pos = np.arange(DEC_SEQ)
    r, c = pos // GRID_W, pos % GRID_W
    ph = ((r[:, None] * r[None, :]) * (GRID_W // rows) + c[:, None] * c[None, :]) % GRID_W
    th2 = 2.0 * np.pi * ph / GRID_W
    return (np.concatenate([cd, sd], axis=1),
            np.concatenate([np.cos(th), -np.sin(th)], axis=1) / math.sqrt(SEQ),
            np.concatenate([np.cos(th2), -np.sin(th2)], axis=1) / math.sqrt(DEC_SEQ))


def _fourier_prompt_kernel(nb, zf_ref, cdsd_ref, cs_ref, wf_ref, o_ref):
    t = jnp.dot(zf_ref[...].reshape(nb * SEQ, FOURIER_W), cdsd_ref[...],
                preferred_element_type=F32).astype(BF16)
    fs = []
    for b in range(nb):
        tb = t[b * SEQ:(b + 1) * SEQ]
        st = jnp.concatenate([tb[:, :FOURIER_W], tb[:, FOURIER_W:]], axis=0)
        fs.append(jnp.dot(cs_ref[...], st, preferred_element_type=F32))
    for b in range(nb):
        o_ref[b] = jnp.dot(fs[b].astype(BF16), wf_ref[...], preferred_element_type=F32).astype(BF16)


def _fourier_prompt(l, zf, cdsd, cs, wf):
    nb = 4
    blk = pl.BlockSpec((nb, SEQ, FOURIER_W), lambda i: (i, 0, 0))
    return pl.pallas_call(
        functools.partial(_fourier_prompt_kernel, nb),
        grid=(BATCH // nb,),
        in_specs=[blk, _full(cdsd), _full(cs), _layer(wf, l)],
        out_specs=blk,
        out_shape=jax.ShapeDtypeStruct((BATCH, SEQ, FOURIER_W), BF16),
        compiler_params=_cparams(("arbitrary",)),
        name="fourier_prompt",
    )(zf, cdsd, cs, wf)


def _fourier_sample_kernel(zf_ref, cdsd_ref, ab_ref, wf_ref, o_ref, tt_ref):
    @pl.when(pl.program_id(0) == 0)
    def _():
        for b in range(DEC_BATCH):
            t = jnp.dot(zf_ref[b], cdsd_ref[...], preferred_element_type=F32).astype(BF16)
            tt_ref[b, 0:DEC_SEQ, :] = t[:, :FOURIER_W]
            tt_ref[b, DEC_SEQ:2 * DEC_SEQ, :] = t[:, FOURIER_W:]

    fs = [jnp.dot(ab_ref[...], tt_ref[b], preferred_element_type=F32) for b in range(DEC_BATCH)]
    for b in range(DEC_BATCH):
        o_ref[b] = jnp.dot(fs[b].astype(BF16), wf_ref[...], preferred_element_type=F32).astype(BF16)


def _fourier_sample(l, zf, cdsd, ab, wf):
    tk = 512
    return pl.pallas_call(
        _fourier_sample_kernel,
        grid=(DEC_SEQ // tk,),
        in_specs=[_full(zf), _full(cdsd), pl.BlockSpec((tk, 2 * DEC_SEQ), lambda i: (i, 0)), _layer(wf, l)],
        out_specs=pl.BlockSpec((DEC_BATCH, tk, FOURIER_W), lambda i: (0, i, 0)),
        out_shape=jax.ShapeDtypeStruct((DEC_BATCH, DEC_SEQ, FOURIER_W), BF16),
        scratch_shapes=[pltpu.VMEM((DEC_BATCH, 2 * DEC_SEQ, FOURIER_W), BF16)],
        compiler_params=_cparams(("arbitrary",)),
        name="fourier_sample",
    )(zf, cdsd, ab, wf)


def _split3(x):
    hi = x.astype(BF16).astype(F32)
    mid = (x - hi).astype(BF16).astype(F32)
    lo = (x - hi - mid).astype(BF16).astype(F32)
    return hi, mid, lo


def _gate_kernel(g_ref, o_ref):
    L = MLSTM_CHUNK
    nrow = N_GATES // 2
    row = lax.broadcasted_iota(jnp.int32, (L, L), 0)
    col = lax.broadcasted_iota(jnp.int32, (L, L), 1)
    tri_pre = jnp.where(row <= col, 1.0, 0.0).astype(BF16)
    tri_suf = jnp.where(row >= col, 1.0, 0.0).astype(BF16)
    is_fwd = lax.broadcasted_iota(jnp.int32, (nrow, L), 0) < MLSTM_HEADS
    lane = lax.broadcasted_iota(jnp.int32, (nrow, L), 1)
    chunks = [(bi, slice(c * L, (c + 1) * L)) for bi in range(g_ref.shape[0]) for c in range(g_ref.shape[2] // L)]
    fold = lambda a: a[0:nrow] + a[nrow:2 * nrow] + a[2 * nrow:]
    bs, rs = [], []
    for bi, cols in chunks:
        lf = _log_sigmoid(g_ref[bi, nrow:, cols])
        parts = jnp.concatenate(_split3(lf), axis=0).astype(BF16)
        pre = jnp.dot(parts, tri_pre, preferred_element_type=F32)
        suf = jnp.dot(parts, tri_suf, preferred_element_type=F32)
        bs.append(jnp.where(is_fwd, fold(pre), fold(suf)))
        rs.append(g_ref[bi, 0:nrow, cols] - bs[-1])
    pms, sms = list(rs), list(rs)
    sh = 1
    while sh < L:
        pms = [jnp.maximum(x, jnp.where(lane >= sh, pltpu.roll(x, sh, 1), NEG)) for x in pms]
        sms = [jnp.maximum(x, jnp.where(lane < L - sh, pltpu.roll(x, L - sh, 1), NEG)) for x in sms]
        sh *= 2
    for (bi, cols), b, r, pm, sm in zip(chunks, bs, rs, pms, sms):
        for q, val in enumerate((b, r, jnp.where(is_fwd, pm, sm))):
            for dh in range(nrow):
                o_ref[bi, dh, q:q + 1, cols] = val[dh:dh + 1]


def _gate_prep(p, gt):
    bb = max(1, DEC_SEQ // p.seq)
    nrow = N_GATES // 2
    return pl.pallas_call(
        _gate_kernel,
        grid=(p.nb // bb,),
        in_specs=[pl.BlockSpec((bb, N_GATES, p.seq), lambda i: (i, 0, 0))],
        out_specs=pl.BlockSpec((bb, nrow, 3, p.seq), lambda i: (i, 0, 0, 0)),
        out_shape=jax.ShapeDtypeStruct((p.nb, nrow, 3, p.seq), F32),
        compiler_params=_cparams(("arbitrary",)),
        name="gate_prep_%d" % p.seq,
    )(gt)


def _mlstm_chunk(q, k, vt, pr, ct, m, fwd, mxu_arg, out):
    L = q.shape[0]
    scale = MLSTM_DH ** -0.5
    b, r, cm = pr[0:1], pr[1:2], pr[2:3]
    if mxu_arg:
        ones = jnp.ones((3, L), F32)
        zeros = jnp.zeros((SUBLANES - 6, L), F32)
        lhs = jnp.concatenate(_split3(r) + (ones, zeros), axis=0).astype(BF16)
        rhs = jnp.concatenate((ones,) + _split3(-cm) + (zeros,), axis=0).astype(BF16)
        arg = lax.dot_general(lhs, rhs, _TN, preferred_element_type=F32)
    else:
        r_col = jnp.concatenate([pr, jnp.zeros((SUBLANES - 3, L), F32)], axis=0).T[:, 1:2]
        arg = r_col - cm
    st = lax.dot_general(k, q, _NT, preferred_element_type=F32)
    cq = lax.dot_general(ct.astype(BF16), q, _NT, preferred_element_type=F32)
    last = L - 1 if fwd else 0
    cm_last = cm[:, last:last + 1]
    mx_last = jnp.maximum(m, cm_last)
    vw = (vt.astype(F32) * jnp.exp(r - cm_last)).astype(BF16)
    dct = jnp.dot(vw, k, preferred_element_type=F32)
    yield

    row = lax.broadcasted_iota(jnp.int32, (L, L), 0)
    col = lax.broadcasted_iota(jnp.int32, (L, L), 1)
    e = jnp.where((row <= col) if fwd else (row >= col), jnp.exp(arg), 0.0)
    num = jnp.dot(vt, (st * e).astype(BF16), preferred_element_type=F32)
    ct_new = jnp.exp(m - mx_last) * ct + (jnp.exp(cm_last - mx_last) * scale) * dct
    yield

    mx = jnp.maximum(m, cm)
    num = (jnp.exp(cm - mx) * scale) * num + jnp.exp(m - mx) * cq
    den = num[N_AUG:N_AUG + 1, :]
    h = num * (1.0 / jnp.maximum(jnp.abs(den), jnp.exp(-(b + mx))))
    out += [h, ct_new, b[:, last:last + 1] + mx_last]


def _mlstm_kernel(nc, has_init, want_final, n_prev, *refs):
    refs = list(refs)
    m0_ref, c0_ref = (refs.pop(0), refs.pop(0)) if has_init else (None, None)
    prev_refs = [refs.pop(7) for _ in range(n_prev)]
    q_ref, k_ref, vt_ref, zo_ref, prf_ref, prb_ref, nw_ref, o_ref = refs[:8]
    hbuf_ref = refs[-1]
    L = MLSTM_CHUNK
    bi = pl.program_id(0)
    hg = pl.program_id(1)
    vrow = lax.broadcasted_iota(jnp.int32, (HEAD_PAD, L), 0)

    def chunk(hh, ci, pr_ref, ct, m, dr, out):
        rows = pl.ds(pl.multiple_of(ci * L, L), L)
        lanes = slice(hh * HEAD_PAD, (hh + 1) * HEAD_PAD)
        vt = vt_ref[lanes, rows]
        vt = jnp.where(vrow == N_AUG, jnp.ones_like(vt), vt)
        res = []
        yield from _mlstm_chunk(q_ref[rows, lanes], k_ref[rows, lanes], vt, pr_ref[hh, :, rows], ct, m,
                                dr == 0, nc == 1, res)
        hbuf_ref[hh, dr, ci] = res[0]
        out += res[1:]

    def step(i, carry):
        outs = [[] for _ in range(2 * HPS)]
        gens = []
        for hh in range(HPS):
            cf, mf, cb, mb = carry[4 * hh:4 * hh + 4]
            gens += [chunk(hh, i, prf_ref, cf, mf, 0, outs[2 * hh]),
                     chunk(hh, nc - 1 - i, prb_ref, cb, mb, 1, outs[2 * hh + 1])]
        _alternate(gens)
        return tuple(x for o in outs for x in o)

    carry = []
    for hh in range(HPS):
        for dr in range(2):
            if has_init:
                carry += [c0_ref[0, dr, hh].T, jnp.full((1, 1), m0_ref[bi, dr, hg * HPS + hh], F32)]
            else:
                carry += [jnp.zeros((HEAD_PAD, HEAD_PAD), F32), jnp.zeros((1, 1), F32)]
    carry = step(0, tuple(carry)) if nc == 1 else lax.fori_loop(0, nc, step, tuple(carry))
    if want_final:
        cfin_ref, nfin_ref, mfin_ref = refs[8:11]
        if n_prev:
            for j, ref in enumerate(prev_refs):
                cfin_ref[0, j] = ref[0]
            cfin_ref = cfin_ref.at[:, n_prev]
        for hh in range(HPS):
            for dr in range(2):
                ct, m = carry[4 * hh + 2 * dr], carry[4 * hh + 2 * dr + 1]
                cfin_ref[0, dr, hh] = ct.T[:MLSTM_DH, :MLSTM_DH]
                nfin_ref[0, dr, hh] = ct[N_AUG:N_AUG + 1, :MLSTM_DH]
                mfin_ref[0, dr, hh] = jnp.broadcast_to(m, (SUBLANES, LANES))

    def finish(ci, _):
        rows = pl.ds(pl.multiple_of(ci * L, L), L)
        for hh in range(HPS):
            lanes = slice(hh * HEAD_PAD, (hh + 1) * HEAD_PAD)
            h = jnp.where(vrow < MLSTM_DH, hbuf_ref[hh, 0, ci] + hbuf_ref[hh, 1, ci], 0.0)
            ms = jnp.sum(h * h, axis=0, keepdims=True) * (1.0 / MLSTM_DH)
            hn = (h * lax.rsqrt(ms + EPS) * nw_ref[hh]).T
            o_ref[rows, lanes] = (hn * jax.nn.sigmoid(zo_ref[rows, lanes].astype(F32))).astype(BF16)
        return 0

    if nc == 1:
        finish(0, 0)
    else:
        lax.fori_loop(0, nc, finish, 0)


def _mlstm(p, l, q, k, vt, zo, pr, nw, init=None, c_prev=None):
    want_final = c_prev is not None
    stack = want_final and l == DEPTH - 1
    assert not want_final or len(c_prev) == l
    nc = p.seq // MLSTM_CHUNK
    tok = pl.BlockSpec((None, p.seq, HPS * HEAD_PAD), lambda b, h: (b, 0, h))
    st = lambda r, c: pl.BlockSpec((1, 2, HPS, r, c), lambda b, h: (b, 0, h, 0, 0))
    prs = lambda dr: pl.BlockSpec((None, HPS, 3, p.seq), lambda b, h: (b, dr * (MLSTM_HEADS // HPS) + h, 0, 0))
    in_specs, args = [], []
    if init is not None:
        in_specs += [pl.BlockSpec(memory_space=pltpu.SMEM), st(HEAD_PAD, HEAD_PAD)]
        args += list(init)
    in_specs += [tok, tok, pl.BlockSpec((None, HPS * HEAD_PAD, p.seq), lambda b, h: (b, h, 0)), tok, prs(0), prs(1),
                 pl.BlockSpec((None, HPS, HEAD_PAD, 1), lambda b, h: (l, h, 0, 0))]
    args += [q, k, vt, zo, pr, pr, nw]
    out_specs = [tok]
    out_shape = [jax.ShapeDtypeStruct((p.nb, p.seq, HEADS_W), BF16)]
    if want_final:
        if stack:
            in_specs += [st(MLSTM_DH, MLSTM_DH)] * l
            args += list(c_prev)
            out_specs.append(pl.BlockSpec((1, DEPTH, 2, HPS, MLSTM_DH, MLSTM_DH), lambda b, h: (b, 0, 0, h, 0, 0)))
            out_shape.append(jax.ShapeDtypeStruct((p.nb, DEPTH, 2, MLSTM_HEADS, MLSTM_DH, MLSTM_DH), F32))
        else:
            out_specs.append(st(MLSTM_DH, MLSTM_DH))
            out_shape.append(jax.ShapeDtypeStruct((p.nb, 2, MLSTM_HEADS, MLSTM_DH, MLSTM_DH), F32))
        out_specs += [st(1, MLSTM_DH), st(SUBLANES, LANES)]
        out_shape += [jax.ShapeDtypeStruct((p.nb, 2, MLSTM_HEADS, r, c), F32)
                      for r, c in ((1, MLSTM_DH), (SUBLANES, LANES))]
    return pl.pallas_call(
        functools.partial(_mlstm_kernel, nc, init is not None, want_final, l if stack else 0),
        grid=(p.nb, MLSTM_HEADS // HPS),
        in_specs=in_specs,
        out_specs=out_specs,
        out_shape=out_shape,
        scratch_shapes=[pltpu.VMEM((HPS, 2, nc, HEAD_PAD, MLSTM_CHUNK), F32)],
        compiler_params=_cparams(("arbitrary", "arbitrary")),
        name="mlstm_%d" % p.seq,
    )(*args)


def _cpow(br, bi, e, nbits):
    pr = pi = None
    for bit in range(nbits):
        sel = ((e >> bit) & 1) == 1
        if pr is None:
            pr, pi = jnp.where(sel, br, 1.0), jnp.where(sel, bi, 0.0)
        else:
            pr, pi = jnp.where(sel, pr * br - pi * bi, pr), jnp.where(sel, pr * bi + pi * br, pi)
        if bit + 1 < nbits:
            br, bi = br * br - bi * bi, 2.0 * br * bi
    return pr, pi


def _s5_prep_kernel(lamc_re_ref, lamc_im_ref, lamr_re_ref, lamr_im_ref, lstep_ref,
                    bt_re_ref, bt_im_ref, ct_re_ref, ct_im_ref,
                    t_ref, m_ref, n_ref, a_ref):
    C = S5_CHUNK
    nbits = (C - 1).bit_length()
    assert C == 1 << nbits
    kk = lax.broadcasted_iota(jnp.int32, (S5_STATE, S5_IN), 1) >> 4
    left = lax.broadcasted_iota(jnp.int32, (S5_IN, S5_ST), 1) < S5_STATE
    left16 = lax.broadcasted_iota(jnp.int32, (S5_GROUP_CH, S5_ST), 1) < S5_STATE
    left1 = lax.broadcasted_iota(jnp.int32, (1, S5_ST), 1) < S5_STATE
    lane = lax.broadcasted_iota(jnp.int32, (S5_GROUP_CH, S5_IN), 1)
    sel = jnp.where((lane & (S5_GROUP_CH - 1)) == lax.broadcasted_iota(jnp.int32, (S5_GROUP_CH, S5_IN), 0),
                    1.0, 0.0)
    spread = lambda ref: jnp.dot(ref[0, 0], sel, precision=lax.Precision.HIGHEST, preferred_element_type=F32)
    ct_re = spread(ct_re_ref)
    ct_im = spread(ct_im_ref)
    resp = []
    for d in range(2):
        step = jnp.exp(lstep_ref[0, d, 0])
        lr_c, li_c = lamc_re_ref[0, d, 0] * step, lamc_im_ref[0, d, 0] * step
        lr_r, li_r = lamr_re_ref[0, d, 0], lamr_im_ref[0, d, 0]
        lbc_re, lbc_im = jnp.exp(lr_c) * jnp.cos(li_c), jnp.exp(lr_c) * jnp.sin(li_c)
        mag = jnp.exp(lr_r * step)
        lb_re, lb_im = mag * jnp.cos(li_r * step), mag * jnp.sin(li_r * step)

        pr, pi = _cpow(lbc_re, lbc_im, kk if d == 0 else (C - 1) - kk, nbits)
        pr1, pi1 = pr * lbc_re - pi * lbc_im, pr * lbc_im + pi * lbc_re
        cpr, cpi = ct_re * pr - ct_im * pi, ct_re * pi + ct_im * pr
        cpr1, cpi1 = ct_re * pr1 - ct_im * pi1, ct_re * pi1 + ct_im * pr1

        nr, ni = lb_re - 1.0, lb_im
        den = lr_r * lr_r + li_r * li_r
        kap_re = (nr * lr_r + ni * li_r) / den
        kap_im = (ni * lr_r - nr * li_r) / den
        bb_re = kap_re * bt_re_ref[0, 0] - kap_im * bt_im_ref[0, 0]
        bb_im = kap_re * bt_im_ref[0, 0] + kap_im * bt_re_ref[0, 0]

        resp.append(jnp.dot(jnp.where(left16, bb_re, -bb_im), jnp.concatenate([cpr, cpi], axis=0),
                            precision=lax.Precision.HIGHEST, preferred_element_type=F32))
        m_ref[0, d, 0] = jnp.concatenate([cpr1, -cpi1], axis=0).astype(BF16)

        blocks = [(bb_re, bb_im)]
        for _ in range(C - 1):
            br_, bi_ = blocks[-1]
            blocks.append((br_ * lb_re - bi_ * lb_im, br_ * lb_im + bi_ * lb_re))
        if d == 0:
            blocks.reverse()
        n_re = jnp.concatenate([b_[0] for b_ in blocks], axis=0)
        n_im = jnp.concatenate([b_[1] for b_ in blocks], axis=0)
        n_ref[0, d, 0] = jnp.concatenate([jnp.where(left, n_re, n_im), jnp.where(left, n_im, n_re)],
                                         axis=1).astype(BF16)

        ar, ai = lb_re, lb_im
        for _ in range(nbits):
            ar, ai = ar * ar - ai * ai, 2.0 * ar * ai
        a2 = jnp.where(left1, -ai, ai)
        a_ref[0, d, 0] = jnp.concatenate([jnp.concatenate([ar, ar], axis=1),
                                          jnp.concatenate([a2, -a2], axis=1)], axis=0)

    rf, rb = resp
    for s in range(C):
        nf = S5_GROUP_CH * s
        blk = jnp.where(lane >= nf, pltpu.roll(rf, nf, 1) if nf else rf, 0.0)
        nb = S5_GROUP_CH * (C - 1 - s)
        blk = blk + jnp.where(lane < S5_IN - nb, pltpu.roll(rb, S5_IN - nb, 1) if nb else rb, 0.0)
        t_ref[0, 0, S5_GROUP_CH * s:S5_GROUP_CH * (s + 1), :] = blk.astype(BF16)


def _s5_prep(lam_re, lam_im, log_step, b_re, b_im, c_re, c_im):
    G = S5_GROUPS
    dup = lambda a: jnp.concatenate([a, a], axis=-1)
    lamc = [a.reshape(DEPTH, 2, G, S5_STATE, 1) for a in (lam_re, lam_im)]
    lamr = [dup(a).reshape(DEPTH, 2, G, 1, S5_ST) for a in (lam_re, lam_im)]
    lstep = log_step.reshape(DEPTH, 2, G, 1, 1)
    bt = [dup(jnp.swapaxes(a, 2, 3)) for a in (b_re, b_im)]
    ct = [jnp.swapaxes(a, 2, 3) for a in (c_re, c_im)]
    dspec = lambda r, c: pl.BlockSpec((1, 2, 1, r, c), lambda l, g: (l, 0, g, 0, 0))
    gspec = lambda r, c: pl.BlockSpec((1, 1, r, c), lambda l, g: (l, g, 0, 0))
    t, m, n, a = pl.pallas_call(
        _s5_prep_kernel,
        grid=(DEPTH, G),
        in_specs=[dspec(S5_STATE, 1), dspec(S5_STATE, 1), dspec(1, S5_ST), dspec(1, S5_ST), dspec(1, 1),
                  gspec(S5_GROUP_CH, S5_ST), gspec(S5_GROUP_CH, S5_ST),
                  gspec(S5_STATE, S5_GROUP_CH), gspec(S5_STATE, S5_GROUP_CH)],
        out_specs=[gspec(S5_IN, S5_IN), dspec(S5_ST, S5_IN), dspec(S5_IN, S5_ST2), dspec(2, S5_ST2)],
        out_shape=[jax.ShapeDtypeStruct((DEPTH, G, S5_IN, S5_IN), BF16),
                   jax.ShapeDtypeStruct((DEPTH, 2, G, S5_ST, S5_IN), BF16),
                   jax.ShapeDtypeStruct((DEPTH, 2, G, S5_IN, S5_ST2), BF16),
                   jax.ShapeDtypeStruct((DEPTH, 2, G, 2, S5_ST2), F32)],
        compiler_params=_cparams(("arbitrary", "arbitrary")),
        name="s5_prep",
    )(*lamc, *lamr, lstep, *bt, *ct)
    return t, m, n, jnp.transpose(a, (0, 1, 3, 2, 4)).reshape(DEPTH, 2, 2, G * S5_ST2)


def _s5_kernel(nseg, nchunks, zr_ref, t_ref, m_ref, n_ref, a_ref, x0_ref, d_ref,
               y_ref, xfin_ref, v_ref, xp_ref):
    W = S5_GB * S5_ST2
    for gl in range(S5_GB):
        u = zr_ref[0, :, gl * S5_IN:(gl + 1) * S5_IN].astype(BF16)
        for d in range(2):
            v_ref[d, :, gl * S5_ST2:(gl + 1) * S5_ST2] = jnp.dot(
                u, n_ref[d, gl], preferred_element_type=F32)

    def halves(x, which):
        return [x[:, g * S5_ST2 + h * S5_ST:g * S5_ST2 + (h + 1) * S5_ST]
                for g in range(S5_GB) for h in which]

    assert 2 * TB == SUBLANES
    steps = nchunks // 2
    is_fwd = lax.broadcasted_iota(jnp.int32, (SUBLANES, 1), 0) < TB
    both = lambda f, b: jnp.where(is_fwd, f, b)
    flip = lambda x: pltpu.roll(x, TB, 0)
    a = both(a_ref[0, 0:1, :], a_ref[1, 0:1, :])
    a2 = both(a_ref[0, 1:2, :], a_ref[1, 1:2, :])
    state = lambda x: jnp.concatenate(halves(x, (0,)), axis=1)

    def advance(x, v):
        swapped = jnp.concatenate(halves(x, (1, 0)), axis=1)
        return a * x + a2 * swapped + v

    for seg in range(nseg):
        base = seg * nchunks * TB

        def step(i, x):
            rf = pl.ds(pl.multiple_of(base + i * SUBLANES, SUBLANES), SUBLANES)
            rb = pl.ds(pl.multiple_of(base + (steps - 1 - i) * SUBLANES, SUBLANES), SUBLANES)
            vf, vb = v_ref[0, rf, :], v_ref[1, rb, :]
            e1 = state(x)
            x = advance(x, both(vf, vb))
            e2 = state(x)
            x = advance(x, both(flip(vf), flip(vb)))
            xp_ref[0, rf, :] = both(e1, flip(e2))
            xp_ref[1, rb, :] = both(flip(e2), e1)
            return x

        rows = slice(seg * TB, (seg + 1) * TB)
        x0 = (jnp.zeros((SUBLANES, W), F32) if x0_ref is None
              else jnp.concatenate([x0_ref[0, rows, :], x0_ref[1, rows, :]], axis=0))
        x = state(lax.fori_loop(0, steps, step, x0))
        if xfin_ref is not None:
            xfin_ref[0, rows, :] = x[:TB]
            xfin_ref[1, rows, :] = x[TB:]

    for gl in range(S5_GB):
        cols = slice(gl * S5_IN, (gl + 1) * S5_IN)
        u = zr_ref[0, :, cols]
        y = jnp.dot(u.astype(BF16), t_ref[gl], preferred_element_type=F32)
        xin = jnp.concatenate([xp_ref[d, :, gl * S5_ST:(gl + 1) * S5_ST] for d in range(2)], axis=1)
        m_both = jnp.concatenate([m_ref[0, gl], m_ref[1, gl]], axis=0)
        y = y + jnp.dot(xin.astype(BF16), m_both, preferred_element_type=F32)
        y_ref[0, :, cols] = jax.nn.gelu(y + d_ref[0, :, cols] * u)


def _s5(p, l, zr, t, m, n, a, dt, x0=None, want_final=False):
    G = S5_GROUPS
    nseg = p.nb // TB
    row = pl.BlockSpec((1, R_PASS, S5_ROW), lambda j: (j, 0, 0))
    dsp = lambda r, c: pl.BlockSpec((None, 2, S5_GB, r, c), lambda j: (l, 0, j, 0, 0))
    lsp = lambda r, w: pl.BlockSpec((2, r, S5_GB * w), lambda j: (0, 0, j))
    in_specs = [row, pl.BlockSpec((None, S5_GB, S5_IN, S5_IN), lambda j: (l, j, 0, 0)),
                dsp(S5_ST, S5_IN), dsp(S5_IN, S5_ST2),
                pl.BlockSpec((None, 2, 2, S5_GB * S5_ST2), lambda j: (l, 0, 0, j)),
                pl.BlockSpec((None, 1, 1, S5_ROW), lambda j: (l, j, 0, 0))]
    args = [zr, t, m, n, a, dt]
    if x0 is not None:
        in_specs.append(lsp(p.nb, S5_ST2))
        args.append(x0)
    out_specs = [row]
    out_shape = [jax.ShapeDtypeStruct((S5_NB, R_PASS, S5_ROW), F32)]
    if want_final:
        out_specs.append(lsp(p.nb, S5_ST))
        out_shape.append(jax.ShapeDtypeStruct((2, p.nb, G * S5_ST), F32))

    def body(zr_ref, t_ref, m_ref, n_ref, a_ref, d_ref, *rest):
        rest = list(rest)
        x0_ref = rest.pop(0) if x0 is not None else None
        y_ref = rest.pop(0)
        xfin_ref = rest.pop(0) if want_final else None
        _s5_kernel(nseg, p.seq // S5_CHUNK, zr_ref, t_ref, m_ref, n_ref, a_ref, x0_ref, d_ref,
                   y_ref, xfin_ref, *rest)

    return pl.pallas_call(
        body,
        grid=(S5_NB,),
        in_specs=in_specs,
        out_specs=out_specs,
        out_shape=out_shape,
        scratch_shapes=[pltpu.VMEM((2, R_PASS, S5_GB * S5_ST2), F32),
                        pltpu.VMEM((2, R_PASS, S5_GB * S5_ST), F32)],
        compiler_params=_cparams(("arbitrary",)),
        name="s5_scan_%d" % p.seq,
    )(*args)


def _mix_stages(x_ref, mod_ref, fo_ref, mo_ref, ys_ref, wglu_ref, wo_ref, nw_ref, scr_ref, out):
    mod = mod_ref[...]
    g1 = mod[:, :, 2 * D_MODEL:3 * D_MODEL]
    flat = lambda ref: ref[...].reshape(TB * TT, ref.shape[-1])
    mo = flat(mo_ref)
    mo = jnp.concatenate([mo[:, h * HEAD_PAD:h * HEAD_PAD + MLSTM_DH] for h in range(MLSTM_HEADS)], axis=1)
    half = (FOURIER_W + MLSTM_W + S5_W) // 2
    m_split = half - FOURIER_W
    mix = jnp.dot(jnp.concatenate([flat(fo_ref), mo[:, :m_split]], axis=1), wo_ref[:half, :],
                  preferred_element_type=F32)
    yield
    blocks = []
    for bl in range(S5_NB):
        by_group = [ys_ref[bl, :, gl * S5_IN:(gl + 1) * S5_IN] for gl in range(S5_GB)]
        for t in range(S5_CHUNK):
            lo = t * S5_GROUP_CH
            scr_ref[bl, pl.ds(t, TROWS, stride=S5_CHUNK), :] = jnp.concatenate(
                [y[:, lo:lo + S5_GROUP_CH] for y in by_group], axis=1)
        slabs = [scr_ref[bl, _slab(b, c)[1], :] for b in range(TB) for c in range(TT // S5_CHUNK)]
        blocks.append(jnp.concatenate(slabs, axis=0))
        yield
    y = jnp.concatenate(blocks, axis=1).astype(BF16)
    gg = jnp.dot(y, wglu_ref[...], preferred_element_type=F32)
    s_out = (gg[:, :S5_W] * jax.nn.sigmoid(gg[:, S5_W:])).astype(BF16)
    mix = mix + jnp.dot(jnp.concatenate([mo[:, m_split:], s_out], axis=1), wo_ref[half:, :],
                        preferred_element_type=F32)
    yield
    x1 = x_ref[...] + g1 * mix.reshape(TB, TT, D_MODEL)
    out += [x1, _mod_norm(x1, mod, nw_ref[...], 3)]


FF_SPLIT = (1024, 1024, 768)
assert sum(FF_SPLIT) == D_FF and all(w % MXU_N == 0 for w in FF_SPLIT)


def _ffn_stages(final, xn, x1, mod_ref, wg_ref, wu_ref, wd_ref, nf_ref, out):
    ff = None
    o = 0
    for w in FF_SPLIT:
        cols = slice(o, o + w)
        o += w
        a = jnp.dot(xn, wg_ref[:, cols], preferred_element_type=F32)
        u = jnp.dot(xn, wu_ref[:, cols], preferred_element_type=F32)
        h = (a * jax.nn.sigmoid(a) * u).astype(BF16)
        part = jnp.dot(h, wd_ref[cols, :], preferred_element_type=F32)
        ff = part if ff is None else ff + part
        yield
    g2 = mod_ref[...][:, :, 5 * D_MODEL:6 * D_MODEL]
    x2 = x1 + g2 * ff.reshape(TB, TT, D_MODEL)
    if final:
        x2 = x2 * lax.rsqrt(jnp.mean(x2 * x2, axis=-1, keepdims=True) + EPS) * nf_ref[...]
    out.append(x2)


def _post_kernel(final, n, x_ref, moda_ref, fo_ref, mo_ref, ys_ref, modb_ref, wglu_ref, wo_ref, nw_ref,
                 wg_ref, wu_ref, wd_ref, nf_ref, o_ref, x1_ref, xn_ref, scr_ref):
    i = pl.program_id(0)
    cur = i % 2
    prev = 1 - cur

    def run(do_ffn, do_mix):
        res_a, res_b, gens = [], [], []
        if do_ffn:
            gens.append(_ffn_stages(final, xn_ref[prev], x1_ref[prev], modb_ref, wg_ref, wu_ref, wd_ref,
                                    nf_ref, res_b))
        if do_mix:
            gens.append(_mix_stages(x_ref, moda_ref, fo_ref, mo_ref, ys_ref, wglu_ref, wo_ref, nw_ref,
                                    scr_ref, res_a))
        _alternate(gens)
        if do_ffn:
            o_ref[...] = res_b[0]
        if do_mix:
            x1_ref[cur] = res_a[0]
            xn_ref[cur] = res_a[1]

    pl.when(i == 0)(lambda: run(False, True))
    pl.when(jnp.logical_and(i > 0, i < n))(lambda: run(True, True))
    pl.when(i == n)(lambda: run(True, False))


def _post(p, l, final, x, mods, fo, mo, ys, wglu, wo, nw, wg, wu, wd, nf):
    ts = _tile_specs(p, l)
    once = lambda a: pl.BlockSpec((None,) + a.shape[1:], lambda i: (l,) + (0,) * (a.ndim - 1),
                                  pipeline_mode=pl.Buffered(1))
    return pl.pallas_call(
        functools.partial(_post_kernel, final, ts.n),
        grid=(ts.n + 1,),
        in_specs=[ts.tok(D_MODEL, ts.head), ts.mod(ts.head), ts.tok(FOURIER_W, ts.head),
                  ts.tok(HEADS_W, ts.head), ts.rows(ts.head), ts.mod(ts.tail),
                  once(wglu), once(wo), once(nw), once(wg), once(wu), once(wd),
                  pl.BlockSpec((1, D_MODEL), lambda i: (0, 0))],
        out_specs=ts.tok(D_MODEL, ts.tail),
        out_shape=jax.ShapeDtypeStruct((p.nb, p.seq, D_MODEL), F32),
        scratch_shapes=[pltpu.VMEM((2, TB, TT, D_MODEL), F32), pltpu.VMEM((2, TB * TT, D_MODEL), BF16),
                        pltpu.VMEM((S5_NB, TB * TT, LANES), F32)],
        compiler_params=pltpu.CompilerParams(dimension_semantics=("arbitrary",),
                                             vmem_limit_bytes=POST_VMEM_LIMIT),
        name="post_%d" % p.seq,
    )(x, mods, fo, mo, ys, mods, wglu, wo, nw, wg, wu, wd, nf)


def _pad_heads(a, axis):
    shape = a.shape[:axis] + (MLSTM_HEADS, MLSTM_DH) + a.shape[axis + 1:]
    pad = [(0, 0)] * (a.ndim + 1)
    pad[axis + 1] = (0, HEAD_PAD - MLSTM_DH)
    return jnp.pad(a.reshape(shape), pad).reshape(a.shape[:axis] + (HEADS_W,) + a.shape[axis + 1:])


def _mlstm_state_in(c, n):
    cn = jnp.concatenate([c, n[..., None]], axis=-1)
    return jnp.pad(cn, ((0, 0),) * (c.ndim - 2) + ((0, HEAD_PAD - MLSTM_DH), (0, HEAD_PAD - MLSTM_DH - 1)))


def kernel(x_prompt, x_sample, state_mlstm_C, state_mlstm_n, state_mlstm_m, state_s5_re, state_s5_im,
           c, c_ctx, w_ada, b_ada, norm1_w, norm2_w, w_in, b_gates, w_fourier, mlstm_norm_w,
           s5_lambda_re, s5_lambda_im, s5_log_step, s5_b_re, s5_b_im, s5_c_re, s5_c_im, s5_d,
           w_glu, w_out, w_gate, w_up, w_down, norm_f):
    xs = {PROMPT: x_prompt, SAMPLE: x_sample}
    cc = jnp.concatenate([c, c_ctx[None], jnp.zeros((N_MODS - 1 - DEC_BATCH, D_MODEL), F32)], axis=0)
    mods = _ada(cc, w_ada, b_ada).reshape(DEPTH, N_MODS, 1, 6 * D_MODEL)
    cdsd, cs, ab = (jnp.asarray(a.astype(np.float32)).astype(BF16) for a in _dft_consts())
    s5_t, s5_m, s5_n, s5_a = _s5_prep(s5_lambda_re, s5_lambda_im, s5_log_step, s5_b_re, s5_b_im,
                                      s5_c_re, s5_c_im)

    o_q = FOURIER_W
    o_g = o_q + 3 * MLSTM_W
    o_o = o_g + N_GATES
    o_u = o_o + MLSTM_W
    cols = lambda o, w: w_in[:, :, o:o + w]
    w_cat = jnp.concatenate([cols(o_u, S5_W), cols(0, FOURIER_W), cols(o_q, 2 * MLSTM_W), cols(o_o, MLSTM_W)],
                            axis=2).astype(BF16)
    wv_t = jnp.swapaxes(cols(o_q + 2 * MLSTM_W, MLSTM_W), 1, 2).astype(BF16)
    gate_perm = np.arange(N_GATES).reshape(2, 2, MLSTM_HEADS).transpose(1, 0, 2).reshape(-1)
    wg_t = jnp.swapaxes(w_in[:, :, o_g:o_o], 1, 2)[:, gate_perm].astype(BF16)
    bg = b_gates[:, gate_perm, None]
    wf = w_fourier.astype(BF16)
    nw = _pad_heads(mlstm_norm_w, 1).reshape(DEPTH, MLSTM_HEADS, HEAD_PAD, 1)
    dt = jnp.tile(s5_d[:, :, None, :], (1, 1, S5_CHUNK, 1)).reshape(DEPTH, S5_NB, 1, S5_ROW)
    wo = w_out.astype(BF16)
    wglu = w_glu.astype(BF16)
    wg, wu, wd = w_gate.astype(BF16), w_up.astype(BF16), w_down.astype(BF16)
    n1, n2 = norm1_w[:, None, :], norm2_w[:, None, :]

    m0 = jnp.swapaxes(state_mlstm_m, 0, 1)
    c0 = jnp.swapaxes(_mlstm_state_in(state_mlstm_C, state_mlstm_n), 0, 1)
    x0 = jnp.concatenate([state_s5_re, state_s5_im, state_s5_im, state_s5_re], axis=-1)
    x0 = jnp.transpose(x0, (1, 2, 0, 3, 4)).reshape(DEPTH, 2, DEC_BATCH, S5_GROUPS * S5_ST2)

    finals, cfins = [], []
    for l in range(DEPTH):
        for p in (PROMPT, SAMPLE):
            x = xs[p]
            zq, zk, zo, zf, zu, vt, gt = _in_proj(p, l, x, mods, n1, w_cat, wv_t, wg_t, bg)
            pr = _gate_prep(p, gt)
            if p is PROMPT:
                fo = _fourier_prompt(l, zf, cdsd, cs, wf)
                mo, cfin, nfin, mfin = _mlstm(p, l, zq, zk, vt, zo, pr, nw, c_prev=cfins)
                cfins.append(cfin)
                ys, xfin = _s5(p, l, zu, s5_t, s5_m, s5_n, s5_a, dt, want_final=True)
                finals.append([nfin, mfin, xfin])
            else:
                fo = _fourier_sample(l, zf, cdsd, ab, wf)
                mo, = _mlstm(p, l, zq, zk, vt, zo, pr, nw, init=(m0[l], c0[l]))
                ys, = _s5(p, l, zu, s5_t, s5_m, s5_n, s5_a, dt, x0=x0[l])
            xs[p] = _post(p, l, l == DEPTH - 1, x, mods, fo, mo, ys, wglu, wo, n2, wg, wu, wd, norm_f[None])

    nfin, mfin, xfin = (jnp.stack(parts, axis=1) for parts in zip(*finals))
    xfin = xfin.reshape(2, DEPTH, BATCH, S5_GROUPS, 2, S5_STATE)
    new_re, new_im = (jnp.transpose(xfin[:, :, :, :, i], (2, 1, 0, 3, 4)) for i in range(2))
    return (xs[PROMPT], xs[SAMPLE], cfin, nfin[:, :, :, :, 0], mfin[:, :, :, :, 0, 0], new_re, new_im)
```

```python
import collections
import functools
import math

import numpy as np
import jax
import jax.numpy as jnp
from jax import lax
from jax.experimental import pallas as pl
from jax.experimental.pallas import tpu as pltpu

F32 = jnp.float32
BF16 = jnp.bfloat16

D_MODEL = 1024
BATCH = 32
SEQ = 256
DEPTH = 2
DEC_BATCH = 4
DEC_SEQ = 2048
GRID_W = 64
FOURIER_W = 256
FOURIER_DH = 64
MLSTM_W = 384
MLSTM_HEADS = 4
MLSTM_DH = 96
S5_W = 384
S5_GROUP_CH = 16
S5_GROUPS = 24
S5_STATE = 64
N_GATES = 16
D_FF = 2816
EPS = 1e-6

LANES = 128
SUBLANES = 8
MXU_N = 256
VMEM_LIMIT = 56 * 1024 * 1024
POST_VMEM_LIMIT = 60 * 1024 * 1024

HEAD_PAD = LANES
HEADS_W = MLSTM_HEADS * HEAD_PAD
N_AUG = MLSTM_DH
Z_W = S5_W + FOURIER_W + 3 * MLSTM_W
MLSTM_CHUNK = 256
S5_CHUNK = 16
S5_IN = S5_CHUNK * S5_GROUP_CH
S5_ST = 2 * S5_STATE
S5_ST2 = 2 * S5_ST
S5_GB = LANES // S5_GROUP_CH
S5_NB = S5_W // LANES
S5_ROW = S5_GB * S5_IN
TB = 4
TT = 128
TROWS = TB * TT // S5_CHUNK
HPS = 4
N_MODS = 8
NEG = -1e30

Pass = collections.namedtuple("Pass", "nb seq mod_first mod_each")
PROMPT = Pass(BATCH, SEQ, DEC_BATCH, False)
SAMPLE = Pass(DEC_BATCH, DEC_SEQ, 0, True)
T_PASS = BATCH * SEQ
assert T_PASS == DEC_BATCH * DEC_SEQ
R_PASS = T_PASS // S5_CHUNK

_NT = (((1,), (1,)), ((), ()))
_TN = (((0,), (0,)), ((), ()))


def _cparams(sem):
    return pltpu.CompilerParams(dimension_semantics=sem, vmem_limit_bytes=VMEM_LIMIT)


def _full(a):
    return pl.BlockSpec(a.shape, lambda *_: (0,) * a.ndim)


def _layer(a, l):
    return pl.BlockSpec((None,) + a.shape[1:], lambda *_: (l,) + (0,) * (a.ndim - 1))


def _log_sigmoid(x):
    return jnp.minimum(x, 0.0) - jnp.log1p(jnp.exp(-jnp.abs(x)))


TileSpecs = collections.namedtuple("TileSpecs", "n head tail tok rows mod")


def _tile_specs(p, l):
    nk = p.seq // TT
    n = (p.nb // TB) * nk
    head = lambda i: jnp.minimum(i, n - 1)
    tail = lambda i: jnp.maximum(i - 1, 0)
    tok = lambda w, tile: pl.BlockSpec((TB, TT, w), lambda i: (tile(i) // nk, tile(i) % nk, 0))
    rows = lambda tile: pl.BlockSpec((S5_NB, TROWS, S5_ROW), lambda i: (0, tile(i), 0))

    def mod(tile):
        if p.mod_each:
            return pl.BlockSpec((None, TB, 1, 6 * D_MODEL), lambda i: (l, p.mod_first // TB + tile(i) // nk, 0, 0))
        return pl.BlockSpec((None, 1, 1, 6 * D_MODEL), lambda i: (l, p.mod_first, 0, 0))

    return TileSpecs(n, head, tail, tok, rows, mod)


def _slab(b, c):
    tok = slice(b * TT + c * S5_CHUNK, b * TT + (c + 1) * S5_CHUNK)
    chk = slice((c * TB + b) * S5_CHUNK, (c * TB + b + 1) * S5_CHUNK)
    return tok, chk


def _ada_kernel(c_ref, w_ref, b_ref, o_ref):
    a = c_ref[...]
    a = (a * jax.nn.sigmoid(a)).astype(BF16)
    o_ref[0] = jnp.dot(a, w_ref[0].astype(BF16), preferred_element_type=F32) + b_ref[0]


def _ada(cc, w_ada, b_ada):
    tn = 1536
    return pl.pallas_call(
        _ada_kernel,
        grid=(DEPTH, 6 * D_MODEL // tn),
        in_specs=[pl.BlockSpec((N_MODS, D_MODEL), lambda l, j: (0, 0)),
                  pl.BlockSpec((1, D_MODEL, tn), lambda l, j: (l, 0, j)),
                  pl.BlockSpec((1, 1, tn), lambda l, j: (l, 0, j))],
        out_specs=pl.BlockSpec((1, N_MODS, tn), lambda l, j: (l, 0, j)),
        out_shape=jax.ShapeDtypeStruct((DEPTH, N_MODS, 6 * D_MODEL), F32),
        compiler_params=_cparams(("arbitrary", "arbitrary")),
        name="ada_mod",
    )(cc, w_ada, b_ada.reshape(DEPTH, 1, 6 * D_MODEL))


def _mod_norm(x3, mod, nw, first):
    sh = mod[:, :, first * D_MODEL:(first + 1) * D_MODEL]
    sc = mod[:, :, (first + 1) * D_MODEL:(first + 2) * D_MODEL]
    y = x3 * lax.rsqrt(jnp.mean(x3 * x3, axis=-1, keepdims=True) + EPS) * nw
    return (y * (1.0 + sc) + sh).reshape(-1, D_MODEL).astype(BF16)


def _alternate(gens):
    gens = list(gens)
    while gens:
        alive = []
        for g in gens:
            try:
                next(g)
                alive.append(g)
            except StopIteration:
                pass
        gens = alive


def _in_kernel(x_ref, mod_ref, nw_ref, w_ref, wv_ref, wg_ref, bg_ref,
               zq_ref, zk_ref, zo_ref, zf_ref, zu_ref, vt_ref, gt_ref, scr_ref):
    xn = _mod_norm(x_ref[...], mod_ref[...], nw_ref[...], 0)
    w_first = -(-S5_W // MXU_N) * MXU_N
    zu = jnp.dot(xn, w_ref[:, :w_first], preferred_element_type=F32)
    for bl in range(S5_NB):
        zb = zu[:, bl * LANES:(bl + 1) * LANES]
        for b in range(TB):
            for c in range(TT // S5_CHUNK):
                tok, chk = _slab(b, c)
                scr_ref[bl, chk, :] = zb[tok]
        by_token = [scr_ref[bl, pl.ds(s, TROWS, stride=S5_CHUNK), :] for s in range(S5_CHUNK)]
        for gl in range(S5_GB):
            lo = gl * S5_GROUP_CH
            zu_ref[bl, :, gl * S5_IN:(gl + 1) * S5_IN] = jnp.concatenate(
                [x[:, lo:lo + S5_GROUP_CH] for x in by_token], axis=1)
        if bl == 0:
            z = jnp.concatenate([zu[:, S5_W:], jnp.dot(xn, w_ref[:, w_first:], preferred_element_type=F32)],
                                axis=1)
    zf_ref[...] = z[:, :FOURIER_W].astype(BF16).reshape(TB, TT, FOURIER_W)
    gap = jnp.zeros((TB * TT, HEAD_PAD - MLSTM_DH), BF16)
    for j, ref in enumerate((zq_ref, zk_ref, zo_ref)):
        t = z[:, FOURIER_W + j * MLSTM_W:FOURIER_W + (j + 1) * MLSTM_W].astype(BF16)
        ref[...] = jnp.concatenate(
            [piece for h in range(MLSTM_HEADS) for piece in (t[:, h * MLSTM_DH:(h + 1) * MLSTM_DH], gap)],
            axis=1).reshape(TB, TT, HEADS_W)
    vt = lax.dot_general(wv_ref[...], xn, _NT, preferred_element_type=F32).astype(BF16)
    gt = lax.dot_general(wg_ref[...], xn, _NT, preferred_element_type=F32) + bg_ref[...]
    for b in range(TB):
        toks = slice(b * TT, (b + 1) * TT)
        for h in range(MLSTM_HEADS):
            vt_ref[b, h * HEAD_PAD:h * HEAD_PAD + MLSTM_DH, :] = vt[h * MLSTM_DH:(h + 1) * MLSTM_DH, toks]
            vt_ref[b, h * HEAD_PAD + MLSTM_DH:(h + 1) * HEAD_PAD, :] = jnp.zeros((HEAD_PAD - MLSTM_DH, TT), BF16)
        gt_ref[b] = gt[:, toks]


def _in_proj(p, l, x, mods, nw, w, wv_t, wg_t, bg):
    ts = _tile_specs(p, l)
    nk = p.seq // TT
    tile = lambda i: i
    chan = lambda c_: pl.BlockSpec((TB, c_, TT), lambda i: (i // nk, 0, i % nk))
    outs = [HEADS_W] * 3 + [FOURIER_W]
    return pl.pallas_call(
        _in_kernel,
        grid=(ts.n,),
        in_specs=[ts.tok(D_MODEL, tile), ts.mod(tile),
                  _layer(nw, l), _layer(w, l), _layer(wv_t, l), _layer(wg_t, l), _layer(bg, l)],
        out_specs=[ts.tok(w_, tile) for w_ in outs] + [ts.rows(tile), chan(HEADS_W), chan(N_GATES)],
        out_shape=[jax.ShapeDtypeStruct((p.nb, p.seq, w_), BF16) for w_ in outs]
        + [jax.ShapeDtypeStruct((S5_NB, R_PASS, S5_ROW), F32),
           jax.ShapeDtypeStruct((p.nb, HEADS_W, p.seq), BF16),
           jax.ShapeDtypeStruct((p.nb, N_GATES, p.seq), F32)],
        scratch_shapes=[pltpu.VMEM((S5_NB, TB * TT, LANES), F32)],
        compiler_params=_cparams(("arbitrary",)),
        name="in_proj_%d" % p.seq,
    )(x, mods, nw, w, wv_t, wg_t, bg)


def _dft_consts():
    d = np.arange(FOURIER_DH)
    phi = 2.0 * np.pi * ((d[:, None] * d[None, :]) % FOURIER_DH) / FOURIER_DH
    eye = np.eye(FOURIER_W // FOURIER_DH)
    cd = np.kron(eye, np.cos(phi)) / math.sqrt(FOURIER_DH)
    sd = np.kron(eye, np.sin(phi)) / math.sqrt(FOURIER_DH)
    s = np.arange(SEQ)
    th = 2.0 * np.pi * ((s[:, None] * s[None, :]) % SEQ) / SEQ
    rows = DEC_SEQ // GRID_W
    pos = np.arange(DEC_SEQ)
    r, c = pos // GRID_W, pos % GRID_W
    ph = ((r[:, None] * r[None, :]) * (GRID_W // rows) + c[:, None] * c[None, :]) % GRID_W
    th2 = 2.0 * np.pi * ph / GRID_W
    return (np.concatenate([cd, sd], axis=1),
            np.concatenate([np.cos(th), -np.sin(th)], axis=1) / math.sqrt(SEQ),
            np.concatenate([np.cos(th2), -np.sin(th2)], axis=1) / math.sqrt(DEC_SEQ))


def _fourier_prompt_kernel(nb, zf_ref, cdsd_ref, cs_ref, wf_ref, o_ref):
    t = jnp.dot(zf_ref[...].reshape(nb * SEQ, FOURIER_W), cdsd_ref[...],
                preferred_element_type=F32).astype(BF16)
    fs = []
    for b in range(nb):
        tb = t[b * SEQ:(b + 1) * SEQ]
        st = jnp.concatenate([tb[:, :FOURIER_W], tb[:, FOURIER_W:]], axis=0)
        fs.append(jnp.dot(cs_ref[...], st, preferred_element_type=F32))
    for b in range(nb):
        o_ref[b] = jnp.dot(fs[b].astype(BF16), wf_ref[...], preferred_element_type=F32).astype(BF16)


def _fourier_prompt(l, zf, cdsd, cs, wf):
    nb = 4
    blk = pl.BlockSpec((nb, SEQ, FOURIER_W), lambda i: (i, 0, 0))
    return pl.pallas_call(
        functools.partial(_fourier_prompt_kernel, nb),
        grid=(BATCH // nb,),
        in_specs=[blk, _full(cdsd), _full(cs), _layer(wf, l)],
        out_specs=blk,
        out_shape=jax.ShapeDtypeStruct((BATCH, SEQ, FOURIER_W), BF16),
        compiler_params=_cparams(("arbitrary",)),
        name="fourier_prompt",
    )(zf, cdsd, cs, wf)


def _fourier_sample_kernel(zf_ref, cdsd_ref, ab_ref, wf_ref, o_ref, tt_ref):
    @pl.when(pl.program_id(0) == 0)
    def _():
        for b in range(DEC_BATCH):
            t = jnp.dot(zf_ref[b], cdsd_ref[...], preferred_element_type=F32).astype(BF16)
            tt_ref[b, 0:DEC_SEQ, :] = t[:, :FOURIER_W]
            tt_ref[b, DEC_SEQ:2 * DEC_SEQ, :] = t[:, FOURIER_W:]

    fs = [jnp.dot(ab_ref[...], tt_ref[b], preferred_element_type=F32) for b in range(DEC_BATCH)]
    for b in range(DEC_BATCH):
        o_ref[b] = jnp.dot(fs[b].astype(BF16), wf_ref[...], preferred_element_type=F32).astype(BF16)


def _fourier_sample(l, zf, cdsd, ab, wf):
    tk = 512
    return pl.pallas_call(
        _fourier_sample_kernel,
        grid=(DEC_SEQ // tk,),
        in_specs=[_full(zf), _full(cdsd), pl.BlockSpec((tk, 2 * DEC_SEQ), lambda i: (i, 0)), _layer(wf, l)],
        out_specs=pl.BlockSpec((DEC_BATCH, tk, FOURIER_W), lambda i: (0, i, 0)),
        out_shape=jax.ShapeDtypeStruct((DEC_BATCH, DEC_SEQ, FOURIER_W), BF16),
        scratch_shapes=[pltpu.VMEM((DEC_BATCH, 2 * DEC_SEQ, FOURIER_W), BF16)],
        compiler_params=_cparams(("arbitrary",)),
        name="fourier_sample",
    )(zf, cdsd, ab, wf)


def _split3(x):
    hi = x.astype(BF16).astype(F32)
    mid = (x - hi).astype(BF16).astype(F32)
    lo = (x - hi - mid).astype(BF16).astype(F32)
    return hi, mid, lo


def _gate_kernel(g_ref, o_ref):
    L = MLSTM_CHUNK
    nrow = N_GATES // 2
    row = lax.broadcasted_iota(jnp.int32, (L, L), 0)
    col = lax.broadcasted_iota(jnp.int32, (L, L), 1)
    tri_pre = jnp.where(row <= col, 1.0, 0.0).astype(BF16)
    tri_suf = jnp.where(row >= col, 1.0, 0.0).astype(BF16)
    is_fwd = lax.broadcasted_iota(jnp.int32, (nrow, L), 0) < MLSTM_HEADS
    lane = lax.broadcasted_iota(jnp.int32, (nrow, L), 1)
    chunks = [(bi, slice(c * L, (c + 1) * L)) for bi in range(g_ref.shape[0]) for c in range(g_ref.shape[2] // L)]
    fold = lambda a: a[0:nrow] + a[nrow:2 * nrow] + a[2 * nrow:]
    bs, rs = [], []
    for bi, cols in chunks:
        lf = _log_sigmoid(g_ref[bi, nrow:, cols])
        parts = jnp.concatenate(_split3(lf), axis=0).astype(BF16)
        pre = jnp.dot(parts, tri_pre, preferred_element_type=F32)
        suf = jnp.dot(parts, tri_suf, preferred_element_type=F32)
        bs.append(jnp.where(is_fwd, fold(pre), fold(suf)))
        rs.append(g_ref[bi, 0:nrow, cols] - bs[-1])
    pms, sms = list(rs), list(rs)
    sh = 1
    while sh < L:
        pms = [jnp.maximum(x, jnp.where(lane >= sh, pltpu.roll(x, sh, 1), NEG)) for x in pms]
        sms = [jnp.maximum(x, jnp.where(lane < L - sh, pltpu.roll(x, L - sh, 1), NEG)) for x in sms]
        sh *= 2
    for (bi, cols), b, r, pm, sm in zip(chunks, bs, rs, pms, sms):
        for q, val in enumerate((b, r, jnp.where(is_fwd, pm, sm))):
            for dh in range(nrow):
                o_ref[bi, dh, q:q + 1, cols] = val[dh:dh + 1]


def _gate_prep(p, gt):
    bb = max(1, DEC_SEQ // p.seq)
    nrow = N_GATES // 2
    return pl.pallas_call(
        _gate_kernel,
        grid=(p.nb // bb,),
        in_specs=[pl.BlockSpec((bb, N_GATES, p.seq), lambda i: (i, 0, 0))],
        out_specs=pl.BlockSpec((bb, nrow, 3, p.seq), lambda i: (i, 0, 0, 0)),
        out_shape=jax.ShapeDtypeStruct((p.nb, nrow, 3, p.seq), F32),
        compiler_params=_cparams(("arbitrary",)),
        name="gate_prep_%d" % p.seq,
    )(gt)


def _mlstm_chunk(q, k, vt, pr, ct, m, fwd, mxu_arg, out):
    L = q.shape[0]
    scale = MLSTM_DH ** -0.5
    b, r, cm = pr[0:1], pr[1:2], pr[2:3]
    if mxu_arg:
        ones = jnp.ones((3, L), F32)
        zeros = jnp.zeros((SUBLANES - 6, L), F32)
        lhs = jnp.concatenate(_split3(r) + (ones, zeros), axis=0).astype(BF16)
        rhs = jnp.concatenate((ones,) + _split3(-cm) + (zeros,), axis=0).astype(BF16)
        arg = lax.dot_general(lhs, rhs, _TN, preferred_element_type=F32)
    else:
        r_col = jnp.concatenate([pr, jnp.zeros((SUBLANES - 3, L), F32)], axis=0).T[:, 1:2]
        arg = r_col - cm
    st = lax.dot_general(k, q, _NT, preferred_element_type=F32)
    cq = lax.dot_general(ct.astype(BF16), q, _NT, preferred_element_type=F32)
    last = L - 1 if fwd else 0
    cm_last = cm[:, last:last + 1]
    mx_last = jnp.maximum(m, cm_last)
    vw = (vt.astype(F32) * jnp.exp(r - cm_last)).astype(BF16)
    dct = jnp.dot(vw, k, preferred_element_type=F32)
    yield

    row = lax.broadcasted_iota(jnp.int32, (L, L), 0)
    col = lax.broadcasted_iota(jnp.int32, (L, L), 1)
    e = jnp.where((row <= col) if fwd else (row >= col), jnp.exp(arg), 0.0)
    num = jnp.dot(vt, (st * e).astype(BF16), preferred_element_type=F32)
    ct_new = jnp.exp(m - mx_last) * ct + (jnp.exp(cm_last - mx_last) * scale) * dct
    yield

    mx = jnp.maximum(m, cm)
    num = (jnp.exp(cm - mx) * scale) * num + jnp.exp(m - mx) * cq
    den = num[N_AUG:N_AUG + 1, :]
    h = num * (1.0 / jnp.maximum(jnp.abs(den), jnp.exp(-(b + mx))))
    out += [h, ct_new, b[:, last:last + 1] + mx_last]


def _mlstm_kernel(nc, sps, has_init, want_final, n_prev, *refs):
    refs = list(refs)
    m0_ref, c0_ref = (refs.pop(0), refs.pop(0)) if has_init else (None, None)
    prev_refs = [refs.pop(7) for _ in range(n_prev)]
    nw_ref, hbuf_ref = refs[6], refs[-1]
    for s in range(sps):
        seq_refs = [r.at[s] for r in refs[:6] + refs[7:-1]]
        _mlstm_sequence(nc, pl.program_id(0) * sps + s, m0_ref, None if c0_ref is None else c0_ref.at[s],
                        [r.at[s] for r in prev_refs], nw_ref, hbuf_ref, want_final, *seq_refs)


def _mlstm_sequence(nc, bi, m0_ref, c0_ref, prev_refs, nw_ref, hbuf_ref, want_final,
                    q_ref, k_ref, vt_ref, zo_ref, prf_ref, prb_ref, o_ref, *fin_refs):
    has_init = c0_ref is not None
    n_prev = len(prev_refs)
    L = MLSTM_CHUNK
    hg = pl.program_id(1)
    vrow = lax.broadcasted_iota(jnp.int32, (HEAD_PAD, L), 0)

    def chunk(hh, ci, pr_ref, ct, m, dr, out):
        rows = pl.ds(pl.multiple_of(ci * L, L), L)
        lanes = slice(hh * HEAD_PAD, (hh + 1) * HEAD_PAD)
        vt = vt_ref[lanes, rows]
        vt = jnp.where(vrow == N_AUG, jnp.ones_like(vt), vt)
        res = []
        yield from _mlstm_chunk(q_ref[rows, lanes], k_ref[rows, lanes], vt, pr_ref[hh, :, rows], ct, m,
                                dr == 0, nc == 1, res)
        hbuf_ref[hh, dr, ci] = res[0]
        out += res[1:]

    def step(i, carry):
        outs = [[] for _ in range(2 * HPS)]
        gens = []
        for hh in range(HPS):
            cf, mf, cb, mb = carry[4 * hh:4 * hh + 4]
            gens += [chunk(hh, i, prf_ref, cf, mf, 0, outs[2 * hh]),
                     chunk(hh, nc - 1 - i, prb_ref, cb, mb, 1, outs[2 * hh + 1])]
        _alternate(gens)
        return tuple(x for o in outs for x in o)

    carry = []
    for hh in range(HPS):
        for dr in range(2):
            if has_init:
                carry += [c0_ref[dr, hh].T, jnp.full((1, 1), m0_ref[bi, dr, hg * HPS + hh], F32)]
            else:
                carry += [jnp.zeros((HEAD_PAD, HEAD_PAD), F32), jnp.zeros((1, 1), F32)]
    carry = step(0, tuple(carry)) if nc == 1 else lax.fori_loop(0, nc, step, tuple(carry))
    if want_final:
        cfin_ref, nfin_ref, mfin_ref = fin_refs
        if n_prev:
            for j, ref in enumerate(prev_refs):
                cfin_ref[j] = ref[...]
            cfin_ref = cfin_ref.at[n_prev]
        for hh in range(HPS):
            for dr in range(2):
                ct, m = carry[4 * hh + 2 * dr], carry[4 * hh + 2 * dr + 1]
                cfin_ref[dr, hh] = ct.T[:MLSTM_DH, :MLSTM_DH]
                nfin_ref[dr, hh] = ct[N_AUG:N_AUG + 1, :MLSTM_DH]
                mfin_ref[dr, hh] = jnp.broadcast_to(m, (SUBLANES, LANES))

    def finish(ci, _):
        rows = pl.ds(pl.multiple_of(ci * L, L), L)
        for hh in range(HPS):
            lanes = slice(hh * HEAD_PAD, (hh + 1) * HEAD_PAD)
            h = jnp.where(vrow < MLSTM_DH, hbuf_ref[hh, 0, ci] + hbuf_ref[hh, 1, ci], 0.0)
            ms = jnp.sum(h * h, axis=0, keepdims=True) * (1.0 / MLSTM_DH)
            hn = (h * lax.rsqrt(ms + EPS) * nw_ref[hh]).T
            o_ref[rows, lanes] = (hn * jax.nn.sigmoid(zo_ref[rows, lanes].astype(F32))).astype(BF16)
        return 0

    if nc == 1:
        finish(0, 0)
    else:
        lax.fori_loop(0, nc, finish, 0)


def _mlstm(p, l, q, k, vt, zo, pr, nw, init=None, c_prev=None):
    want_final = c_prev is not None
    stack = want_final and l == DEPTH - 1
    assert not want_final or len(c_prev) == l
    nc = p.seq // MLSTM_CHUNK
    sps = max(1, 2 * MLSTM_CHUNK // p.seq)
    tok = pl.BlockSpec((sps, p.seq, HPS * HEAD_PAD), lambda b, h: (b, 0, h))
    st = lambda r, c: pl.BlockSpec((sps, 2, HPS, r, c), lambda b, h: (b, 0, h, 0, 0))
    prs = lambda dr: pl.BlockSpec((sps, HPS, 3, p.seq), lambda b, h: (b, dr * (MLSTM_HEADS // HPS) + h, 0, 0))
    in_specs, args = [], []
    if init is not None:
        in_specs += [pl.BlockSpec(memory_space=pltpu.SMEM), st(HEAD_PAD, HEAD_PAD)]
        args += list(init)
    in_specs += [tok, tok, pl.BlockSpec((sps, HPS * HEAD_PAD, p.seq), lambda b, h: (b, h, 0)), tok, prs(0), prs(1),
                 pl.BlockSpec((None, HPS, HEAD_PAD, 1), lambda b, h: (l, h, 0, 0))]
    args += [q, k, vt, zo, pr, pr, nw]
    out_specs = [tok]
    out_shape = [jax.ShapeDtypeStruct((p.nb, p.seq, HEADS_W), BF16)]
    if want_final:
        if stack:
            in_specs += [st(MLSTM_DH, MLSTM_DH)] * l
            args += list(c_prev)
            out_specs.append(pl.BlockSpec((sps, DEPTH, 2, HPS, MLSTM_DH, MLSTM_DH), lambda b, h: (b, 0, 0, h, 0, 0)))
            out_shape.append(jax.ShapeDtypeStruct((p.nb, DEPTH, 2, MLSTM_HEADS, MLSTM_DH, MLSTM_DH), F32))
        else:
            out_specs.append(st(MLSTM_DH, MLSTM_DH))
            out_shape.append(jax.ShapeDtypeStruct((p.nb, 2, MLSTM_HEADS, MLSTM_DH, MLSTM_DH), F32))
        out_specs += [st(1, MLSTM_DH), st(SUBLANES, LANES)]
        out_shape += [jax.ShapeDtypeStruct((p.nb, 2, MLSTM_HEADS, r, c), F32)
                      for r, c in ((1, MLSTM_DH), (SUBLANES, LANES))]
    return pl.pallas_call(
        functools.partial(_mlstm_kernel, nc, sps, init is not None, want_final, l if stack else 0),
        grid=(p.nb // sps, MLSTM_HEADS // HPS),
        in_specs=in_specs,
        out_specs=out_specs,
        out_shape=out_shape,
        scratch_shapes=[pltpu.VMEM((HPS, 2, nc, HEAD_PAD, MLSTM_CHUNK), F32)],
        compiler_params=_cparams(("arbitrary", "arbitrary")),
        name="mlstm_%d" % p.seq,
    )(*args)


def _cpow(br, bi, e, nbits):
    pr = pi = None
    for bit in range(nbits):
        sel = ((e >> bit) & 1) == 1
        if pr is None:
            pr, pi = jnp.where(sel, br, 1.0), jnp.where(sel, bi, 0.0)
        else:
            pr, pi = jnp.where(sel, pr * br - pi * bi, pr), jnp.where(sel, pr * bi + pi * br, pi)
        if bit + 1 < nbits:
            br, bi = br * br - bi * bi, 2.0 * br * bi
    return pr, pi


def _s5_prep_kernel(lamc_re_ref, lamc_im_ref, lamr_re_ref, lamr_im_ref, lstep_ref,
                    bt_re_ref, bt_im_ref, ct_re_ref, ct_im_ref,
                    t_ref, m_ref, n_ref, a_ref):
    C = S5_CHUNK
    nbits = (C - 1).bit_length()
    assert C == 1 << nbits
    kk = lax.broadcasted_iota(jnp.int32, (S5_STATE, S5_IN), 1) >> 4
    left = lax.broadcasted_iota(jnp.int32, (S5_IN, S5_ST), 1) < S5_STATE
    left16 = lax.broadcasted_iota(jnp.int32, (S5_GROUP_CH, S5_ST), 1) < S5_STATE
    left1 = lax.broadcasted_iota(jnp.int32, (1, S5_ST), 1) < S5_STATE
    lane = lax.broadcasted_iota(jnp.int32, (S5_GROUP_CH, S5_IN), 1)
    sel = jnp.where((lane & (S5_GROUP_CH - 1)) == lax.broadcasted_iota(jnp.int32, (S5_GROUP_CH, S5_IN), 0),
                    1.0, 0.0)
    spread = lambda ref: jnp.dot(ref[0, 0], sel, precision=lax.Precision.HIGHEST, preferred_element_type=F32)
    ct_re = spread(ct_re_ref)
    ct_im = spread(ct_im_ref)
    resp = []
    for d in range(2):
        step = jnp.exp(lstep_ref[0, d, 0])
        lr_c, li_c = lamc_re_ref[0, d, 0] * step, lamc_im_ref[0, d, 0] * step
        lr_r, li_r = lamr_re_ref[0, d, 0], lamr_im_ref[0, d, 0]
        lbc_re, lbc_im = jnp.exp(lr_c) * jnp.cos(li_c), jnp.exp(lr_c) * jnp.sin(li_c)
        mag = jnp.exp(lr_r * step)
        lb_re, lb_im = mag * jnp.cos(li_r * step), mag * jnp.sin(li_r * step)

        pr, pi = _cpow(lbc_re, lbc_im, kk if d == 0 else (C - 1) - kk, nbits)
        pr1, pi1 = pr * lbc_re - pi * lbc_im, pr * lbc_im + pi * lbc_re
        cpr, cpi = ct_re * pr - ct_im * pi, ct_re * pi + ct_im * pr
        cpr1, cpi1 = ct_re * pr1 - ct_im * pi1, ct_re * pi1 + ct_im * pr1

        nr, ni = lb_re - 1.0, lb_im
        den = lr_r * lr_r + li_r * li_r
        kap_re = (nr * lr_r + ni * li_r) / den
        kap_im = (ni * lr_r - nr * li_r) / den
        bb_re = kap_re * bt_re_ref[0, 0] - kap_im * bt_im_ref[0, 0]
        bb_im = kap_re * bt_im_ref[0, 0] + kap_im * bt_re_ref[0, 0]

        resp.append(jnp.dot(jnp.where(left16, bb_re, -bb_im), jnp.concatenate([cpr, cpi], axis=0),
                            precision=lax.Precision.HIGHEST, preferred_element_type=F32))
        m_ref[0, d, 0] = jnp.concatenate([cpr1, -cpi1], axis=0).astype(BF16)

        blocks = [(bb_re, bb_im)]
        for _ in range(C - 1):
            br_, bi_ = blocks[-1]
            blocks.append((br_ * lb_re - bi_ * lb_im, br_ * lb_im + bi_ * lb_re))
        if d == 0:
            blocks.reverse()
        n_re = jnp.concatenate([b_[0] for b_ in blocks], axis=0)
        n_im = jnp.concatenate([b_[1] for b_ in blocks], axis=0)
        n_ref[0, d, 0] = jnp.concatenate([jnp.where(left, n_re, n_im), jnp.where(left, n_im, n_re)],
                                         axis=1).astype(BF16)

        ar, ai = lb_re, lb_im
        for _ in range(nbits):
            ar, ai = ar * ar - ai * ai, 2.0 * ar * ai
        a2 = jnp.where(left1, -ai, ai)
        a_ref[0, d, 0] = jnp.concatenate([jnp.concatenate([ar, ar], axis=1),
                                          jnp.concatenate([a2, -a2], axis=1)], axis=0)

    rf, rb = resp
    for s in range(C):
        nf = S5_GROUP_CH * s
        blk = jnp.where(lane >= nf, pltpu.roll(rf, nf, 1) if nf else rf, 0.0)
        nb = S5_GROUP_CH * (C - 1 - s)
        blk = blk + jnp.where(lane < S5_IN - nb, pltpu.roll(rb, S5_IN - nb, 1) if nb else rb, 0.0)
        t_ref[0, 0, S5_GROUP_CH * s:S5_GROUP_CH * (s + 1), :] = blk.astype(BF16)


def _s5_prep(lam_re, lam_im, log_step, b_re, b_im, c_re, c_im):
    G = S5_GROUPS
    dup = lambda a: jnp.concatenate([a, a], axis=-1)
    lamc = [a.reshape(DEPTH, 2, G, S5_STATE, 1) for a in (lam_re, lam_im)]
    lamr = [dup(a).reshape(DEPTH, 2, G, 1, S5_ST) for a in (lam_re, lam_im)]
    lstep = log_step.reshape(DEPTH, 2, G, 1, 1)
    bt = [dup(jnp.swapaxes(a, 2, 3)) for a in (b_re, b_im)]
    ct = [jnp.swapaxes(a, 2, 3) for a in (c_re, c_im)]
    dspec = lambda r, c: pl.BlockSpec((1, 2, 1, r, c), lambda l, g: (l, 0, g, 0, 0))
    gspec = lambda r, c: pl.BlockSpec((1, 1, r, c), lambda l, g: (l, g, 0, 0))
    t, m, n, a = pl.pallas_call(
        _s5_prep_kernel,
        grid=(DEPTH, G),
        in_specs=[dspec(S5_STATE, 1), dspec(S5_STATE, 1), dspec(1, S5_ST), dspec(1, S5_ST), dspec(1, 1),
                  gspec(S5_GROUP_CH, S5_ST), gspec(S5_GROUP_CH, S5_ST),
                  gspec(S5_STATE, S5_GROUP_CH), gspec(S5_STATE, S5_GROUP_CH)],
        out_specs=[gspec(S5_IN, S5_IN), dspec(S5_ST, S5_IN), dspec(S5_IN, S5_ST2), dspec(2, S5_ST2)],
        out_shape=[jax.ShapeDtypeStruct((DEPTH, G, S5_IN, S5_IN), BF16),
                   jax.ShapeDtypeStruct((DEPTH, 2, G, S5_ST, S5_IN), BF16),
                   jax.ShapeDtypeStruct((DEPTH, 2, G, S5_IN, S5_ST2), BF16),
                   jax.ShapeDtypeStruct((DEPTH, 2, G, 2, S5_ST2), F32)],
        compiler_params=_cparams(("arbitrary", "arbitrary")),
        name="s5_prep",
    )(*lamc, *lamr, lstep, *bt, *ct)
    return t, m, n, jnp.transpose(a, (0, 1, 3, 2, 4)).reshape(DEPTH, 2, 2, G * S5_ST2)


def _s5_kernel(nseg, nchunks, zr_ref, t_ref, m_ref, n_ref, a_ref, x0_ref, d_ref,
               y_ref, xfin_ref, v_ref, xp_ref):
    W = S5_GB * S5_ST2
    for gl in range(S5_GB):
        u = zr_ref[0, :, gl * S5_IN:(gl + 1) * S5_IN].astype(BF16)
        for d in range(2):
            v_ref[d, :, gl * S5_ST2:(gl + 1) * S5_ST2] = jnp.dot(
                u, n_ref[d, gl], preferred_element_type=F32)

    def halves(x, which):
        return [x[:, g * S5_ST2 + h * S5_ST:g * S5_ST2 + (h + 1) * S5_ST]
                for g in range(S5_GB) for h in which]

    assert 2 * TB == SUBLANES
    steps = nchunks // 2
    is_fwd = lax.broadcasted_iota(jnp.int32, (SUBLANES, 1), 0) < TB
    both = lambda f, b: jnp.where(is_fwd, f, b)
    flip = lambda x: pltpu.roll(x, TB, 0)
    a = both(a_ref[0, 0:1, :], a_ref[1, 0:1, :])
    a2 = both(a_ref[0, 1:2, :], a_ref[1, 1:2, :])
    state = lambda x: jnp.concatenate(halves(x, (0,)), axis=1)

    def advance(x, v):
        swapped = jnp.concatenate(halves(x, (1, 0)), axis=1)
        return a * x + a2 * swapped + v

    for seg in range(nseg):
        base = seg * nchunks * TB

        def step(i, x):
            rf = pl.ds(pl.multiple_of(base + i * SUBLANES, SUBLANES), SUBLANES)
            rb = pl.ds(pl.multiple_of(base + (steps - 1 - i) * SUBLANES, SUBLANES), SUBLANES)
            vf, vb = v_ref[0, rf, :], v_ref[1, rb, :]
            e1 = state(x)
            x = advance(x, both(vf, vb))
            e2 = state(x)
            x = advance(x, both(flip(vf), flip(vb)))
            xp_ref[0, rf, :] = both(e1, flip(e2))
            xp_ref[1, rb, :] = both(flip(e2), e1)
            return x

        rows = slice(seg * TB, (seg + 1) * TB)
        x0 = (jnp.zeros((SUBLANES, W), F32) if x0_ref is None
              else jnp.concatenate([x0_ref[0, rows, :], x0_ref[1, rows, :]], axis=0))
        x = state(lax.fori_loop(0, steps, step, x0))
        if xfin_ref is not None:
            xfin_ref[0, rows, :] = x[:TB]
            xfin_ref[1, rows, :] = x[TB:]

    for gl in range(S5_GB):
        cols = slice(gl * S5_IN, (gl + 1) * S5_IN)
        u = zr_ref[0, :, cols]
        y = jnp.dot(u.astype(BF16), t_ref[gl], preferred_element_type=F32)
        xin = jnp.concatenate([xp_ref[d, :, gl * S5_ST:(gl + 1) * S5_ST] for d in range(2)], axis=1)
        m_both = jnp.concatenate([m_ref[0, gl], m_ref[1, gl]], axis=0)
        y = y + jnp.dot(xin.astype(BF16), m_both, preferred_element_type=F32)
        y_ref[0, :, cols] = jax.nn.gelu(y + d_ref[0, :, cols] * u)


def _s5(p, l, zr, t, m, n, a, dt, x0=None, want_final=False):
    G = S5_GROUPS
    nseg = p.nb // TB
    row = pl.BlockSpec((1, R_PASS, S5_ROW), lambda j: (j, 0, 0))
    dsp = lambda r, c: pl.BlockSpec((None, 2, S5_GB, r, c), lambda j: (l, 0, j, 0, 0))
    lsp = lambda r, w: pl.BlockSpec((2, r, S5_GB * w), lambda j: (0, 0, j))
    in_specs = [row, pl.BlockSpec((None, S5_GB, S5_IN, S5_IN), lambda j: (l, j, 0, 0)),
                dsp(S5_ST, S5_IN), dsp(S5_IN, S5_ST2),
                pl.BlockSpec((None, 2, 2, S5_GB * S5_ST2), lambda j: (l, 0, 0, j)),
                pl.BlockSpec((None, 1, 1, S5_ROW), lambda j: (l, j, 0, 0))]
    args = [zr, t, m, n, a, dt]
    if x0 is not None:
        in_specs.append(lsp(p.nb, S5_ST2))
        args.append(x0)
    out_specs = [row]
    out_shape = [jax.ShapeDtypeStruct((S5_NB, R_PASS, S5_ROW), F32)]
    if want_final:
        out_specs.append(lsp(p.nb, S5_ST))
        out_shape.append(jax.ShapeDtypeStruct((2, p.nb, G * S5_ST), F32))

    def body(zr_ref, t_ref, m_ref, n_ref, a_ref, d_ref, *rest):
        rest = list(rest)
        x0_ref = rest.pop(0) if x0 is not None else None
        y_ref = rest.pop(0)
        xfin_ref = rest.pop(0) if want_final else None
        _s5_kernel(nseg, p.seq // S5_CHUNK, zr_ref, t_ref, m_ref, n_ref, a_ref, x0_ref, d_ref,
                   y_ref, xfin_ref, *rest)

    return pl.pallas_call(
        body,
        grid=(S5_NB,),
        in_specs=in_specs,
        out_specs=out_specs,
        out_shape=out_shape,
        scratch_shapes=[pltpu.VMEM((2, R_PASS, S5_GB * S5_ST2), F32),
                        pltpu.VMEM((2, R_PASS, S5_GB * S5_ST), F32)],
        compiler_params=_cparams(("arbitrary",)),
        name="s5_scan_%d" % p.seq,
    )(*args)


def _mix_stages(x_ref, mod_ref, fo_ref, mo_ref, ys_ref, wglu_ref, wo_ref, nw_ref, scr_ref, out):
    mod = mod_ref[...]
    g1 = mod[:, :, 2 * D_MODEL:3 * D_MODEL]
    flat = lambda ref: ref[...].reshape(TB * TT, ref.shape[-1])
    mo = flat(mo_ref)
    mo = jnp.concatenate([mo[:, h * HEAD_PAD:h * HEAD_PAD + MLSTM_DH] for h in range(MLSTM_HEADS)], axis=1)
    half = (FOURIER_W + MLSTM_W + S5_W) // 2
    m_split = half - FOURIER_W
    mix = jnp.dot(jnp.concatenate([flat(fo_ref), mo[:, :m_split]], axis=1), wo_ref[:half, :],
                  preferred_element_type=F32)
    yield
    blocks = []
    for bl in range(S5_NB):
        by_group = [ys_ref[bl, :, gl * S5_IN:(gl + 1) * S5_IN] for gl in range(S5_GB)]
        for t in range(S5_CHUNK):
            lo = t * S5_GROUP_CH
            scr_ref[bl, pl.ds(t, TROWS, stride=S5_CHUNK), :] = jnp.concatenate(
                [y[:, lo:lo + S5_GROUP_CH] for y in by_group], axis=1)
        slabs = [scr_ref[bl, _slab(b, c)[1], :] for b in range(TB) for c in range(TT // S5_CHUNK)]
        blocks.append(jnp.concatenate(slabs, axis=0))
        yield
    y = jnp.concatenate(blocks, axis=1).astype(BF16)
    gg = jnp.dot(y, wglu_ref[...], preferred_element_type=F32)
    s_out = (gg[:, :S5_W] * jax.nn.sigmoid(gg[:, S5_W:])).astype(BF16)
    mix = mix + jnp.dot(jnp.concatenate([mo[:, m_split:], s_out], axis=1), wo_ref[half:, :],
                        preferred_element_type=F32)
    yield
    x1 = x_ref[...] + g1 * mix.reshape(TB, TT, D_MODEL)
    out += [x1, _mod_norm(x1, mod, nw_ref[...], 3)]


FF_SPLIT = (1024, 1024, 768)
assert sum(FF_SPLIT) == D_FF and all(w % MXU_N == 0 for w in FF_SPLIT)


def _ffn_stages(final, xn, x1, mod_ref, wg_ref, wu_ref, wd_ref, nf_ref, out):
    ff = None
    o = 0
    for w in FF_SPLIT:
        cols = slice(o, o + w)
        o += w
        a = jnp.dot(xn, wg_ref[:, cols], preferred_element_type=F32)
        u = jnp.dot(xn, wu_ref[:, cols], preferred_element_type=F32)
        h = (a * jax.nn.sigmoid(a) * u).astype(BF16)
        part = jnp.dot(h, wd_ref[cols, :], preferred_element_type=F32)
        ff = part if ff is None else ff + part
        yield
    g2 = mod_ref[...][:, :, 5 * D_MODEL:6 * D_MODEL]
    x2 = x1 + g2 * ff.reshape(TB, TT, D_MODEL)
    if final:
        x2 = x2 * lax.rsqrt(jnp.mean(x2 * x2, axis=-1, keepdims=True) + EPS) * nf_ref[...]
    out.append(x2)


def _post_kernel(final, n, x_ref, moda_ref, fo_ref, mo_ref, ys_ref, modb_ref, wglu_ref, wo_ref, nw_ref,
                 wg_ref, wu_ref, wd_ref, nf_ref, o_ref, x1_ref, xn_ref, scr_ref):
    i = pl.program_id(0)
    cur = i % 2
    prev = 1 - cur

    def run(do_ffn, do_mix):
        res_a, res_b, gens = [], [], []
        if do_ffn:
            gens.append(_ffn_stages(final, xn_ref[prev], x1_ref[prev], modb_ref, wg_ref, wu_ref, wd_ref,
                                    nf_ref, res_b))
        if do_mix:
            gens.append(_mix_stages(x_ref, moda_ref, fo_ref, mo_ref, ys_ref, wglu_ref, wo_ref, nw_ref,
                                    scr_ref, res_a))
        _alternate(gens)
        if do_ffn:
            o_ref[...] = res_b[0]
        if do_mix:
            x1_ref[cur] = res_a[0]
            xn_ref[cur] = res_a[1]

    pl.when(i == 0)(lambda: run(False, True))
    pl.when(jnp.logical_and(i > 0, i < n))(lambda: run(True, True))
    pl.when(i == n)(lambda: run(True, False))


def _post(p, l, final, x, mods, fo, mo, ys, wglu, wo, nw, wg, wu, wd, nf):
    ts = _tile_specs(p, l)
    once = lambda a: pl.BlockSpec((None,) + a.shape[1:], lambda i: (l,) + (0,) * (a.ndim - 1),
                                  pipeline_mode=pl.Buffered(1))
    return pl.pallas_call(
        functools.partial(_post_kernel, final, ts.n),
        grid=(ts.n + 1,),
        in_specs=[ts.tok(D_MODEL, ts.head), ts.mod(ts.head), ts.tok(FOURIER_W, ts.head),
                  ts.tok(HEADS_W, ts.head), ts.rows(ts.head), ts.mod(ts.tail),
                  once(wglu), once(wo), once(nw), once(wg), once(wu), once(wd),
                  pl.BlockSpec((1, D_MODEL), lambda i: (0, 0))],
        out_specs=ts.tok(D_MODEL, ts.tail),
        out_shape=jax.ShapeDtypeStruct((p.nb, p.seq, D_MODEL), F32),
        scratch_shapes=[pltpu.VMEM((2, TB, TT, D_MODEL), F32), pltpu.VMEM((2, TB * TT, D_MODEL), BF16),
                        pltpu.VMEM((S5_NB, TB * TT, LANES), F32)],
        compiler_params=pltpu.CompilerParams(dimension_semantics=("arbitrary",),
                                             vmem_limit_bytes=POST_VMEM_LIMIT),
        name="post_%d" % p.seq,
    )(x, mods, fo, mo, ys, mods, wglu, wo, nw, wg, wu, wd, nf)


def _pad_heads(a, axis):
    shape = a.shape[:axis] + (MLSTM_HEADS, MLSTM_DH) + a.shape[axis + 1:]
    pad = [(0, 0)] * (a.ndim + 1)
    pad[axis + 1] = (0, HEAD_PAD - MLSTM_DH)
    return jnp.pad(a.reshape(shape), pad).reshape(a.shape[:axis] + (HEADS_W,) + a.shape[axis + 1:])


def _mlstm_state_in(c, n):
    cn = jnp.concatenate([c, n[..., None]], axis=-1)
    return jnp.pad(cn, ((0, 0),) * (c.ndim - 2) + ((0, HEAD_PAD - MLSTM_DH), (0, HEAD_PAD - MLSTM_DH - 1)))


def kernel(x_prompt, x_sample, state_mlstm_C, state_mlstm_n, state_mlstm_m, state_s5_re, state_s5_im,
           c, c_ctx, w_ada, b_ada, norm1_w, norm2_w, w_in, b_gates, w_fourier, mlstm_norm_w,
           s5_lambda_re, s5_lambda_im, s5_log_step, s5_b_re, s5_b_im, s5_c_re, s5_c_im, s5_d,
           w_glu, w_out, w_gate, w_up, w_down, norm_f):
    xs = {PROMPT: x_prompt, SAMPLE: x_sample}
    cc = jnp.concatenate([c, c_ctx[None], jnp.zeros((N_MODS - 1 - DEC_BATCH, D_MODEL), F32)], axis=0)
    mods = _ada(cc, w_ada, b_ada).reshape(DEPTH, N_MODS, 1, 6 * D_MODEL)
    cdsd, cs, ab = (jnp.asarray(a.astype(np.float32)).astype(BF16) for a in _dft_consts())
    s5_t, s5_m, s5_n, s5_a = _s5_prep(s5_lambda_re, s5_lambda_im, s5_log_step, s5_b_re, s5_b_im,
                                      s5_c_re, s5_c_im)

    o_q = FOURIER_W
    o_g = o_q + 3 * MLSTM_W
    o_o = o_g + N_GATES
    o_u = o_o + MLSTM_W
    cols = lambda o, w: w_in[:, :, o:o + w]
    w_cat = jnp.concatenate([cols(o_u, S5_W), cols(0, FOURIER_W), cols(o_q, 2 * MLSTM_W), cols(o_o, MLSTM_W)],
                            axis=2).astype(BF16)
    wv_t = jnp.swapaxes(cols(o_q + 2 * MLSTM_W, MLSTM_W), 1, 2).astype(BF16)
    gate_perm = np.arange(N_GATES).reshape(2, 2, MLSTM_HEADS).transpose(1, 0, 2).reshape(-1)
    wg_t = jnp.swapaxes(w_in[:, :, o_g:o_o], 1, 2)[:, gate_perm].astype(BF16)
    bg = b_gates[:, gate_perm, None]
    wf = w_fourier.astype(BF16)
    nw = _pad_heads(mlstm_norm_w, 1).reshape(DEPTH, MLSTM_HEADS, HEAD_PAD, 1)
    dt = jnp.tile(s5_d[:, :, None, :], (1, 1, S5_CHUNK, 1)).reshape(DEPTH, S5_NB, 1, S5_ROW)
    wo = w_out.astype(BF16)
    wglu = w_glu.astype(BF16)
    wg, wu, wd = w_gate.astype(BF16), w_up.astype(BF16), w_down.astype(BF16)
    n1, n2 = norm1_w[:, None, :], norm2_w[:, None, :]

    m0 = jnp.swapaxes(state_mlstm_m, 0, 1)
    c0 = jnp.swapaxes(_mlstm_state_in(state_mlstm_C, state_mlstm_n), 0, 1)
    x0 = jnp.concatenate([state_s5_re, state_s5_im, state_s5_im, state_s5_re], axis=-1)
    x0 = jnp.transpose(x0, (1, 2, 0, 3, 4)).reshape(DEPTH, 2, DEC_BATCH, S5_GROUPS * S5_ST2)

    finals, cfins = [], []
    for l in range(DEPTH):
        for p in (PROMPT, SAMPLE):
            x = xs[p]
            zq, zk, zo, zf, zu, vt, gt = _in_proj(p, l, x, mods, n1, w_cat, wv_t, wg_t, bg)
            pr = _gate_prep(p, gt)
            if p is PROMPT:
                fo = _fourier_prompt(l, zf, cdsd, cs, wf)
                mo, cfin, nfin, mfin = _mlstm(p, l, zq, zk, vt, zo, pr, nw, c_prev=cfins)
                cfins.append(cfin)
                ys, xfin = _s5(p, l, zu, s5_t, s5_m, s5_n, s5_a, dt, want_final=True)
                finals.append([nfin, mfin, xfin])
            else:
                fo = _fourier_sample(l, zf, cdsd, ab, wf)
                mo, = _mlstm(p, l, zq, zk, vt, zo, pr, nw, init=(m0[l], c0[l]))
                ys, = _s5(p, l, zu, s5_t, s5_m, s5_n, s5_a, dt, x0=x0[l])
            xs[p] = _post(p, l, l == DEPTH - 1, x, mods, fo, mo, ys, wglu, wo, n2, wg, wu, wd, norm_f[None])

    nfin, mfin, xfin = (jnp.stack(parts, axis=1) for parts in zip(*finals))
    xfin = xfin.reshape(2, DEPTH, BATCH, S5_GROUPS, 2, S5_STATE)
    new_re, new_im = (jnp.transpose(xfin[:, :, :, :, i], (2, 1, 0, 3, 4)) for i in range(2))
    return (xs[PROMPT], xs[SAMPLE], cfin, nfin[:, :, :, :, 0], mfin[:, :, :, :, 0, 0], new_re, new_im)
```

```python
import collections
import functools
import math

import numpy as np
import jax
import jax.numpy as jnp
from jax import lax
from jax.experimental import pallas as pl
from jax.experimental.pallas import tpu as pltpu

F32 = jnp.float32
BF16 = jnp.bfloat16

D_MODEL = 1024
BATCH = 32
SEQ = 256
DEPTH = 2
DEC_BATCH = 4
DEC_SEQ = 2048
GRID_W = 64
FOURIER_W = 256
FOURIER_DH = 64
MLSTM_W = 384
MLSTM_HEADS = 4
MLSTM_DH = 96
S5_W = 384
S5_GROUP_CH = 16
S5_GROUPS = 24
S5_STATE = 64
N_GATES = 16
D_FF = 2816
EPS = 1e-6

LANES = 128
SUBLANES = 8
MXU_N = 256
VMEM_LIMIT = 56 * 1024 * 1024
POST_VMEM_LIMIT = 60 * 1024 * 1024

HEAD_PAD = LANES
HEADS_W = MLSTM_HEADS * HEAD_PAD
N_AUG = MLSTM_DH
Z_W = S5_W + FOURIER_W + 3 * MLSTM_W
MLSTM_CHUNK = 256
S5_CHUNK = 16
S5_IN = S5_CHUNK * S5_GROUP_CH
S5_ST = 2 * S5_STATE
S5_ST2 = 2 * S5_ST
S5_GB = LANES // S5_GROUP_CH
S5_NB = S5_W // LANES
S5_ROW = S5_GB * S5_IN
TB = 4
TT = 128
TROWS = TB * TT // S5_CHUNK
HPS = 4
N_MODS = 8
NEG = -1e30

Pass = collections.namedtuple("Pass", "nb seq mod_first mod_each")
PROMPT = Pass(BATCH, SEQ, DEC_BATCH, False)
SAMPLE = Pass(DEC_BATCH, DEC_SEQ, 0, True)
T_PASS = BATCH * SEQ
assert T_PASS == DEC_BATCH * DEC_SEQ
R_PASS = T_PASS // S5_CHUNK

_NT = (((1,), (1,)), ((), ()))
_TN = (((0,), (0,)), ((), ()))


def _cparams(sem):
    return pltpu.CompilerParams(dimension_semantics=sem, vmem_limit_bytes=VMEM_LIMIT)


def _full(a):
    return pl.BlockSpec(a.shape, lambda *_: (0,) * a.ndim)


def _layer(a, l):
    return pl.BlockSpec((None,) + a.shape[1:], lambda *_: (l,) + (0,) * (a.ndim - 1))


def _log_sigmoid(x):
    return jnp.minimum(x, 0.0) - jnp.log1p(jnp.exp(-jnp.abs(x)))


TileSpecs = collections.namedtuple("TileSpecs", "n head tail tok rows mod")


def _tile_specs(p, l):
    nk = p.seq // TT
    n = (p.nb // TB) * nk
    head = lambda i: jnp.minimum(i, n - 1)
    tail = lambda i: jnp.maximum(i - 1, 0)
    tok = lambda w, tile: pl.BlockSpec((TB, TT, w), lambda i: (tile(i) // nk, tile(i) % nk, 0))
    rows = lambda tile: pl.BlockSpec((S5_NB, TROWS, S5_ROW), lambda i: (0, tile(i), 0))

    def mod(tile):
        if p.mod_each:
            return pl.BlockSpec((None, TB, 1, 6 * D_MODEL), lambda i: (l, p.mod_first // TB + tile(i) // nk, 0, 0))
        return pl.BlockSpec((None, 1, 1, 6 * D_MODEL), lambda i: (l, p.mod_first, 0, 0))

    return TileSpecs(n, head, tail, tok, rows, mod)


def _slab(b, c):
    tok = slice(b * TT + c * S5_CHUNK, b * TT + (c + 1) * S5_CHUNK)
    chk = slice((c * TB + b) * S5_CHUNK, (c * TB + b + 1) * S5_CHUNK)
    return tok, chk


def _ada_kernel(c_ref, w_ref, b_ref, o_ref):
    a = c_ref[...]
    a = (a * jax.nn.sigmoid(a)).astype(BF16)
    o_ref[0] = jnp.dot(a, w_ref[0].astype(BF16), preferred_element_type=F32) + b_ref[0]


def _ada(cc, w_ada, b_ada):
    tn = 3072
    return pl.pallas_call(
        _ada_kernel,
        grid=(DEPTH, 6 * D_MODEL // tn),
        in_specs=[pl.BlockSpec((N_MODS, D_MODEL), lambda l, j: (0, 0)),
                  pl.BlockSpec((1, D_MODEL, tn), lambda l, j: (l, 0, j)),
                  pl.BlockSpec((1, 1, tn), lambda l, j: (l, 0, j))],
        out_specs=pl.BlockSpec((1, N_MODS, tn), lambda l, j: (l, 0, j)),
        out_shape=jax.ShapeDtypeStruct((DEPTH, N_MODS, 6 * D_MODEL), F32),
        compiler_params=_cparams(("arbitrary", "arbitrary")),
        name="ada_mod",
    )(cc, w_ada, b_ada.reshape(DEPTH, 1, 6 * D_MODEL))


def _mod_norm(x3, mod, nw, first):
    sh = mod[:, :, first * D_MODEL:(first + 1) * D_MODEL]
    sc = mod[:, :, (first + 1) * D_MODEL:(first + 2) * D_MODEL]
    y = x3 * lax.rsqrt(jnp.mean(x3 * x3, axis=-1, keepdims=True) + EPS) * nw
    return (y * (1.0 + sc) + sh).reshape(-1, D_MODEL).astype(BF16)


def _alternate(gens):
    gens = list(gens)
    while gens:
        alive = []
        for g in gens:
            try:
                next(g)
                alive.append(g)
            except StopIteration:
                pass
        gens = alive


def _in_kernel(x_ref, mod_ref, nw_ref, w_ref, wv_ref, wg_ref, bg_ref,
               zq_ref, zk_ref, zo_ref, zf_ref, zu_ref, vt_ref, gt_ref, scr_ref):
    xn = _mod_norm(x_ref[...], mod_ref[...], nw_ref[...], 0)
    w_first = -(-S5_W // MXU_N) * MXU_N
    zu = jnp.dot(xn, w_ref[:, :w_first], preferred_element_type=F32)
    for bl in range(S5_NB):
        zb = zu[:, bl * LANES:(bl + 1) * LANES]
        for b in range(TB):
            for c in range(TT // S5_CHUNK):
                tok, chk = _slab(b, c)
                scr_ref[bl, chk, :] = zb[tok]
        by_token = [scr_ref[bl, pl.ds(s, TROWS, stride=S5_CHUNK), :] for s in range(S5_CHUNK)]
        for gl in range(S5_GB):
            lo = gl * S5_GROUP_CH
            zu_ref[bl, :, gl * S5_IN:(gl + 1) * S5_IN] = jnp.concatenate(
                [x[:, lo:lo + S5_GROUP_CH] for x in by_token], axis=1)
        if bl == 0:
            z = jnp.concatenate([zu[:, S5_W:], jnp.dot(xn, w_ref[:, w_first:], preferred_element_type=F32)],
                                axis=1)
    zf_ref[...] = z[:, :FOURIER_W].astype(BF16).reshape(TB, TT, FOURIER_W)
    gap = jnp.zeros((TB * TT, HEAD_PAD - MLSTM_DH), BF16)
    for j, ref in enumerate((zq_ref, zk_ref, zo_ref)):
        t = z[:, FOURIER_W + j * MLSTM_W:FOURIER_W + (j + 1) * MLSTM_W].astype(BF16)
        ref[...] = jnp.concatenate(
            [piece for h in range(MLSTM_HEADS) for piece in (t[:, h * MLSTM_DH:(h + 1) * MLSTM_DH], gap)],
            axis=1).reshape(TB, TT, HEADS_W)
    vt = lax.dot_general(wv_ref[...], xn, _NT, preferred_element_type=F32).astype(BF16)
    gt = lax.dot_general(wg_ref[...], xn, _NT, preferred_element_type=F32) + bg_ref[...]
    for b in range(TB):
        toks = slice(b * TT, (b + 1) * TT)
        for h in range(MLSTM_HEADS):
            vt_ref[b, h * HEAD_PAD:h * HEAD_PAD + MLSTM_DH, :] = vt[h * MLSTM_DH:(h + 1) * MLSTM_DH, toks]
            vt_ref[b, h * HEAD_PAD + MLSTM_DH:(h + 1) * HEAD_PAD, :] = jnp.zeros((HEAD_PAD - MLSTM_DH, TT), BF16)
        gt_ref[b] = gt[:, toks]


def _in_proj(p, l, x, mods, nw, w, wv_t, wg_t, bg):
    ts = _tile_specs(p, l)
    nk = p.seq // TT
    tile = lambda i: i
    chan = lambda c_: pl.BlockSpec((TB, c_, TT), lambda i: (i // nk, 0, i % nk))
    outs = [HEADS_W] * 3 + [FOURIER_W]
    return pl.pallas_call(
        _in_kernel,
        grid=(ts.n,),
        in_specs=[ts.tok(D_MODEL, tile), ts.mod(tile),
                  _layer(nw, l), _layer(w, l), _layer(wv_t, l), _layer(wg_t, l), _layer(bg, l)],
        out_specs=[ts.tok(w_, tile) for w_ in outs] + [ts.rows(tile), chan(HEADS_W), chan(N_GATES)],
        out_shape=[jax.ShapeDtypeStruct((p.nb, p.seq, w_), BF16) for w_ in outs]
        + [jax.ShapeDtypeStruct((S5_NB, R_PASS, S5_ROW), F32),
           jax.ShapeDtypeStruct((p.nb, HEADS_W, p.seq), BF16),
           jax.ShapeDtypeStruct((p.nb, N_GATES, p.seq), F32)],
        scratch_shapes=[pltpu.VMEM((S5_NB, TB * TT, LANES), F32)],
        compiler_params=_cparams(("arbitrary",)),
        name="in_proj_%d" % p.seq,
    )(x, mods, nw, w, wv_t, wg_t, bg)


def _dft_consts():
    d = np.arange(FOURIER_DH)
    phi = 2.0 * np.pi * ((d[:, None] * d[None, :]) % FOURIER_DH) / FOURIER_DH
    eye = np.eye(FOURIER_W // FOURIER_DH)
    cd = np.kron(eye, np.cos(phi)) / math.sqrt(FOURIER_DH)
    sd = np.kron(eye, np.sin(phi)) / math.sqrt(FOURIER_DH)
    s = np.arange(SEQ)
    th = 2.0 * np.pi * ((s[:, None] * s[None, :]) % SEQ) / SEQ
    rows = DEC_SEQ // GRID_W
    pos = np.arange(DEC_SEQ)
    r, c = pos // GRID_W, pos % GRID_W
    ph = ((r[:, None] * r[None, :]) * (GRID_W // rows) + c[:, None] * c[None, :]) % GRID_W
    th2 = 2.0 * np.pi * ph / GRID_W
    return (np.concatenate([cd, sd], axis=1),
            np.concatenate([np.cos(th), -np.sin(th)], axis=1) / math.sqrt(SEQ),
            np.concatenate([np.cos(th2), -np.sin(th2)], axis=1) / math.sqrt(DEC_SEQ))


def _fourier_prompt_kernel(nb, zf_ref, cdsd_ref, cs_ref, wf_ref, o_ref):
    t = jnp.dot(zf_ref[...].reshape(nb * SEQ, FOURIER_W), cdsd_ref[...],
                preferred_element_type=F32).astype(BF16)
    fs = []
    for b in range(nb):
        tb = t[b * SEQ:(b + 1) * SEQ]
        st = jnp.concatenate([tb[:, :FOURIER_W], tb[:, FOURIER_W:]], axis=0)
        fs.append(jnp.dot(cs_ref[...], st, preferred_element_type=F32))
    for b in range(nb):
        o_ref[b] = jnp.dot(fs[b].astype(BF16), wf_ref[...], preferred_element_type=F32).astype(BF16)


def _fourier_prompt(l, zf, cdsd, cs, wf):
    nb = 8
    blk = pl.BlockSpec((nb, SEQ, FOURIER_W), lambda i: (i, 0, 0))
    return pl.pallas_call(
        functools.partial(_fourier_prompt_kernel, nb),
        grid=(BATCH // nb,),
        in_specs=[blk, _full(cdsd), _full(cs), _layer(wf, l)],
        out_specs=blk,
        out_shape=jax.ShapeDtypeStruct((BATCH, SEQ, FOURIER_W), BF16),
        compiler_params=_cparams(("arbitrary",)),
        name="fourier_prompt",
    )(zf, cdsd, cs, wf)


def _fourier_sample_kernel(zf_ref, cdsd_ref, ab_ref, wf_ref, o_ref, tt_ref):
    @pl.when(pl.program_id(0) == 0)
    def _():
        for b in range(DEC_BATCH):
            t = jnp.dot(zf_ref[b], cdsd_ref[...], preferred_element_type=F32).astype(BF16)
            tt_ref[b, 0:DEC_SEQ, :] = t[:, :FOURIER_W]
            tt_ref[b, DEC_SEQ:2 * DEC_SEQ, :] = t[:, FOURIER_W:]

    fs = [jnp.dot(ab_ref[...], tt_ref[b], preferred_element_type=F32) for b in range(DEC_BATCH)]
    for b in range(DEC_BATCH):
        o_ref[b] = jnp.dot(fs[b].astype(BF16), wf_ref[...], preferred_element_type=F32).astype(BF16)


def _fourier_sample(l, zf, cdsd, ab, wf):
    tk = 512
    return pl.pallas_call(
        _fourier_sample_kernel,
        grid=(DEC_SEQ // tk,),
        in_specs=[_full(zf), _full(cdsd), pl.BlockSpec((tk, 2 * DEC_SEQ), lambda i: (i, 0)), _layer(wf, l)],
        out_specs=pl.BlockSpec((DEC_BATCH, tk, FOURIER_W), lambda i: (0, i, 0)),
        out_shape=jax.ShapeDtypeStruct((DEC_BATCH, DEC_SEQ, FOURIER_W), BF16),
        scratch_shapes=[pltpu.VMEM((DEC_BATCH, 2 * DEC_SEQ, FOURIER_W), BF16)],
        compiler_params=_cparams(("arbitrary",)),
        name="fourier_sample",
    )(zf, cdsd, ab, wf)


def _split3(x):
    hi = x.astype(BF16).astype(F32)
    mid = (x - hi).astype(BF16).astype(F32)
    lo = (x - hi - mid).astype(BF16).astype(F32)
    return hi, mid, lo


def _gate_kernel(g_ref, o_ref):
    L = MLSTM_CHUNK
    nrow = N_GATES // 2
    row = lax.broadcasted_iota(jnp.int32, (L, L), 0)
    col = lax.broadcasted_iota(jnp.int32, (L, L), 1)
    tri_pre = jnp.where(row <= col, 1.0, 0.0).astype(BF16)
    tri_suf = jnp.where(row >= col, 1.0, 0.0).astype(BF16)
    is_fwd = lax.broadcasted_iota(jnp.int32, (nrow, L), 0) < MLSTM_HEADS
    lane = lax.broadcasted_iota(jnp.int32, (nrow, L), 1)
    chunks = [(bi, slice(c * L, (c + 1) * L)) for bi in range(g_ref.shape[0]) for c in range(g_ref.shape[2] // L)]
    fold = lambda a: a[0:nrow] + a[nrow:2 * nrow] + a[2 * nrow:]
    bs, rs = [], []
    for bi, cols in chunks:
        lf = _log_sigmoid(g_ref[bi, nrow:, cols])
        parts = jnp.concatenate(_split3(lf), axis=0).astype(BF16)
        pre = jnp.dot(parts, tri_pre, preferred_element_type=F32)
        suf = jnp.dot(parts, tri_suf, preferred_element_type=F32)
        bs.append(jnp.where(is_fwd, fold(pre), fold(suf)))
        rs.append(g_ref[bi, 0:nrow, cols] - bs[-1])
    pms, sms = list(rs), list(rs)
    sh = 1
    while sh < L:
        pms = [jnp.maximum(x, jnp.where(lane >= sh, pltpu.roll(x, sh, 1), NEG)) for x in pms]
        sms = [jnp.maximum(x, jnp.where(lane < L - sh, pltpu.roll(x, L - sh, 1), NEG)) for x in sms]
        sh *= 2
    for (bi, cols), b, r, pm, sm in zip(chunks, bs, rs, pms, sms):
        for q, val in enumerate((b, r, jnp.where(is_fwd, pm, sm))):
            for dh in range(nrow):
                o_ref[bi, dh, q:q + 1, cols] = val[dh:dh + 1]


def _gate_prep(p, gt):
    bb = max(1, DEC_SEQ // p.seq)
    nrow = N_GATES // 2
    return pl.pallas_call(
        _gate_kernel,
        grid=(p.nb // bb,),
        in_specs=[pl.BlockSpec((bb, N_GATES, p.seq), lambda i: (i, 0, 0))],
        out_specs=pl.BlockSpec((bb, nrow, 3, p.seq), lambda i: (i, 0, 0, 0)),
        out_shape=jax.ShapeDtypeStruct((p.nb, nrow, 3, p.seq), F32),
        compiler_params=_cparams(("arbitrary",)),
        name="gate_prep_%d" % p.seq,
    )(gt)


def _mlstm_chunk(q, k, vt, pr, ct, m, fwd, mxu_arg, out):
    L = q.shape[0]
    scale = MLSTM_DH ** -0.5
    b, r, cm = pr[0:1], pr[1:2], pr[2:3]
    if mxu_arg:
        ones = jnp.ones((3, L), F32)
        zeros = jnp.zeros((SUBLANES - 6, L), F32)
        lhs = jnp.concatenate(_split3(r) + (ones, zeros), axis=0).astype(BF16)
        rhs = jnp.concatenate((ones,) + _split3(-cm) + (zeros,), axis=0).astype(BF16)
        arg = lax.dot_general(lhs, rhs, _TN, preferred_element_type=F32)
    else:
        r_col = jnp.concatenate([pr, jnp.zeros((SUBLANES - 3, L), F32)], axis=0).T[:, 1:2]
        arg = r_col - cm
    st = lax.dot_general(k, q, _NT, preferred_element_type=F32)
    cq = lax.dot_general(ct.astype(BF16), q, _NT, preferred_element_type=F32)
    last = L - 1 if fwd else 0
    cm_last = cm[:, last:last + 1]
    mx_last = jnp.maximum(m, cm_last)
    vw = (vt.astype(F32) * jnp.exp(r - cm_last)).astype(BF16)
    dct = jnp.dot(vw, k, preferred_element_type=F32)
    yield

    row = lax.broadcasted_iota(jnp.int32, (L, L), 0)
    col = lax.broadcasted_iota(jnp.int32, (L, L), 1)
    e = jnp.where((row <= col) if fwd else (row >= col), jnp.exp(arg), 0.0)
    num = jnp.dot(vt, (st * e).astype(BF16), preferred_element_type=F32)
    ct_new = jnp.exp(m - mx_last) * ct + (jnp.exp(cm_last - mx_last) * scale) * dct
    yield

    mx = jnp.maximum(m, cm)
    num = (jnp.exp(cm - mx) * scale) * num + jnp.exp(m - mx) * cq
    den = num[N_AUG:N_AUG + 1, :]
    h = num * (1.0 / jnp.maximum(jnp.abs(den), jnp.exp(-(b + mx))))
    out += [h, ct_new, b[:, last:last + 1] + mx_last]


def _mlstm_kernel(nc, sps, has_init, want_final, n_prev, *refs):
    refs = list(refs)
    m0_ref, c0_ref = (refs.pop(0), refs.pop(0)) if has_init else (None, None)
    prev_refs = [refs.pop(7) for _ in range(n_prev)]
    nw_ref, hbuf_ref = refs[6], refs[-1]
    for s in range(sps):
        seq_refs = [r.at[s] for r in refs[:6] + refs[7:-1]]
        _mlstm_sequence(nc, pl.program_id(0) * sps + s, m0_ref, None if c0_ref is None else c0_ref.at[s],
                        [r.at[s] for r in prev_refs], nw_ref, hbuf_ref, want_final, *seq_refs)


def _mlstm_sequence(nc, bi, m0_ref, c0_ref, prev_refs, nw_ref, hbuf_ref, want_final,
                    q_ref, k_ref, vt_ref, zo_ref, prf_ref, prb_ref, o_ref, *fin_refs):
    has_init = c0_ref is not None
    n_prev = len(prev_refs)
    L = MLSTM_CHUNK
    hg = pl.program_id(1)
    vrow = lax.broadcasted_iota(jnp.int32, (HEAD_PAD, L), 0)

    def chunk(hh, ci, pr_ref, ct, m, dr, out):
        rows = pl.ds(pl.multiple_of(ci * L, L), L)
        lanes = slice(hh * HEAD_PAD, (hh + 1) * HEAD_PAD)
        vt = vt_ref[lanes, rows]
        vt = jnp.where(vrow == N_AUG, jnp.ones_like(vt), vt)
        res = []
        yield from _mlstm_chunk(q_ref[rows, lanes], k_ref[rows, lanes], vt, pr_ref[hh, :, rows], ct, m,
                                dr == 0, nc == 1, res)
        hbuf_ref[hh, dr, ci] = res[0]
        out += res[1:]

    def step(i, carry):
        outs = [[] for _ in range(2 * HPS)]
        gens = []
        for hh in range(HPS):
            cf, mf, cb, mb = carry[4 * hh:4 * hh + 4]
            gens += [chunk(hh, i, prf_ref, cf, mf, 0, outs[2 * hh]),
                     chunk(hh, nc - 1 - i, prb_ref, cb, mb, 1, outs[2 * hh + 1])]
        _alternate(gens)
        return tuple(x for o in outs for x in o)

    carry = []
    for hh in range(HPS):
        for dr in range(2):
            if has_init:
                carry += [c0_ref[dr, hh].T, jnp.full((1, 1), m0_ref[bi, dr, hg * HPS + hh], F32)]
            else:
                carry += [jnp.zeros((HEAD_PAD, HEAD_PAD), F32), jnp.zeros((1, 1), F32)]
    carry = step(0, tuple(carry)) if nc == 1 else lax.fori_loop(0, nc, step, tuple(carry))
    if want_final:
        cfin_ref, nfin_ref, mfin_ref = fin_refs
        if n_prev:
            for j, ref in enumerate(prev_refs):
                cfin_ref[j] = ref[...]
            cfin_ref = cfin_ref.at[n_prev]
        for hh in range(HPS):
            for dr in range(2):
                ct, m = carry[4 * hh + 2 * dr], carry[4 * hh + 2 * dr + 1]
                cfin_ref[dr, hh] = ct.T[:MLSTM_DH, :MLSTM_DH]
                nfin_ref[dr, hh] = ct[N_AUG:N_AUG + 1, :MLSTM_DH]
                mfin_ref[dr, hh] = jnp.broadcast_to(m, (SUBLANES, LANES))

    def finish(ci, _):
        rows = pl.ds(pl.multiple_of(ci * L, L), L)
        for hh in range(HPS):
            lanes = slice(hh * HEAD_PAD, (hh + 1) * HEAD_PAD)
            h = jnp.where(vrow < MLSTM_DH, hbuf_ref[hh, 0, ci] + hbuf_ref[hh, 1, ci], 0.0)
            ms = jnp.sum(h * h, axis=0, keepdims=True) * (1.0 / MLSTM_DH)
            hn = (h * lax.rsqrt(ms + EPS) * nw_ref[hh]).T
            o_ref[rows, lanes] = (hn * jax.nn.sigmoid(zo_ref[rows, lanes].astype(F32))).astype(BF16)
        return 0

    if nc == 1:
        finish(0, 0)
    else:
        lax.fori_loop(0, nc, finish, 0)


def _mlstm(p, l, q, k, vt, zo, pr, nw, init=None, c_prev=None):
    want_final = c_prev is not None
    stack = want_final and l == DEPTH - 1
    assert not want_final or len(c_prev) == l
    nc = p.seq // MLSTM_CHUNK
    sps = max(1, 4 * MLSTM_CHUNK // p.seq)
    tok = pl.BlockSpec((sps, p.seq, HPS * HEAD_PAD), lambda b, h: (b, 0, h))
    st = lambda r, c: pl.BlockSpec((sps, 2, HPS, r, c), lambda b, h: (b, 0, h, 0, 0))
    prs = lambda dr: pl.BlockSpec((sps, HPS, 3, p.seq), lambda b, h: (b, dr * (MLSTM_HEADS // HPS) + h, 0, 0))
    in_specs, args = [], []
    if init is not None:
        in_specs += [pl.BlockSpec(memory_space=pltpu.SMEM), st(HEAD_PAD, HEAD_PAD)]
        args += list(init)
    in_specs += [tok, tok, pl.BlockSpec((sps, HPS * HEAD_PAD, p.seq), lambda b, h: (b, h, 0)), tok, prs(0), prs(1),
                 pl.BlockSpec((None, HPS, HEAD_PAD, 1), lambda b, h: (l, h, 0, 0))]
    args += [q, k, vt, zo, pr, pr, nw]
    out_specs = [tok]
    out_shape = [jax.ShapeDtypeStruct((p.nb, p.seq, HEADS_W), BF16)]
    if want_final:
        if stack:
            in_specs += [st(MLSTM_DH, MLSTM_DH)] * l
            args += list(c_prev)
            out_specs.append(pl.BlockSpec((sps, DEPTH, 2, HPS, MLSTM_DH, MLSTM_DH), lambda b, h: (b, 0, 0, h, 0, 0)))
            out_shape.append(jax.ShapeDtypeStruct((p.nb, DEPTH, 2, MLSTM_HEADS, MLSTM_DH, MLSTM_DH), F32))
        else:
            out_specs.append(st(MLSTM_DH, MLSTM_DH))
            out_shape.append(jax.ShapeDtypeStruct((p.nb, 2, MLSTM_HEADS, MLSTM_DH, MLSTM_DH), F32))
        out_specs += [st(1, MLSTM_DH), st(SUBLANES, LANES)]
        out_shape += [jax.ShapeDtypeStruct((p.nb, 2, MLSTM_HEADS, r, c), F32)
                      for r, c in ((1, MLSTM_DH), (SUBLANES, LANES))]
    return pl.pallas_call(
        functools.partial(_mlstm_kernel, nc, sps, init is not None, want_final, l if stack else 0),
        grid=(p.nb // sps, MLSTM_HEADS // HPS),
        in_specs=in_specs,
        out_specs=out_specs,
        out_shape=out_shape,
        scratch_shapes=[pltpu.VMEM((HPS, 2, nc, HEAD_PAD, MLSTM_CHUNK), F32)],
        compiler_params=_cparams(("arbitrary", "arbitrary")),
        name="mlstm_%d" % p.seq,
    )(*args)


def _cpow(br, bi, e, nbits):
    pr = pi = None
    for bit in range(nbits):
        sel = ((e >> bit) & 1) == 1
        if pr is None:
            pr, pi = jnp.where(sel, br, 1.0), jnp.where(sel, bi, 0.0)
        else:
            pr, pi = jnp.where(sel, pr * br - pi * bi, pr), jnp.where(sel, pr * bi + pi * br, pi)
        if bit + 1 < nbits:
            br, bi = br * br - bi * bi, 2.0 * br * bi
    return pr, pi


def _s5_prep_kernel(lamc_re_ref, lamc_im_ref, lamr_re_ref, lamr_im_ref, lstep_ref,
                    bt_re_ref, bt_im_ref, ct_re_ref, ct_im_ref,
                    t_ref, m_ref, n_ref, a_ref):
    C = S5_CHUNK
    nbits = (C - 1).bit_length()
    assert C == 1 << nbits
    kk = lax.broadcasted_iota(jnp.int32, (S5_STATE, S5_IN), 1) >> 4
    left = lax.broadcasted_iota(jnp.int32, (S5_IN, S5_ST), 1) < S5_STATE
    left16 = lax.broadcasted_iota(jnp.int32, (S5_GROUP_CH, S5_ST), 1) < S5_STATE
    left1 = lax.broadcasted_iota(jnp.int32, (1, S5_ST), 1) < S5_STATE
    lane = lax.broadcasted_iota(jnp.int32, (S5_GROUP_CH, S5_IN), 1)
    sel = jnp.where((lane & (S5_GROUP_CH - 1)) == lax.broadcasted_iota(jnp.int32, (S5_GROUP_CH, S5_IN), 0),
                    1.0, 0.0)
    spread = lambda ref: jnp.dot(ref[0, 0], sel, precision=lax.Precision.HIGHEST, preferred_element_type=F32)
    ct_re = spread(ct_re_ref)
    ct_im = spread(ct_im_ref)
    resp = []
    for d in range(2):
        step = jnp.exp(lstep_ref[0, d, 0])
        lr_c, li_c = lamc_re_ref[0, d, 0] * step, lamc_im_ref[0, d, 0] * step
        lr_r, li_r = lamr_re_ref[0, d, 0], lamr_im_ref[0, d, 0]
        lbc_re, lbc_im = jnp.exp(lr_c) * jnp.cos(li_c), jnp.exp(lr_c) * jnp.sin(li_c)
        mag = jnp.exp(lr_r * step)
        lb_re, lb_im = mag * jnp.cos(li_r * step), mag * jnp.sin(li_r * step)

        pr, pi = _cpow(lbc_re, lbc_im, kk if d == 0 else (C - 1) - kk, nbits)
        pr1, pi1 = pr * lbc_re - pi * lbc_im, pr * lbc_im + pi * lbc_re
        cpr, cpi = ct_re * pr - ct_im * pi, ct_re * pi + ct_im * pr
        cpr1, cpi1 = ct_re * pr1 - ct_im * pi1, ct_re * pi1 + ct_im * pr1

        nr, ni = lb_re - 1.0, lb_im
        den = lr_r * lr_r + li_r * li_r
        kap_re = (nr * lr_r + ni * li_r) / den
        kap_im = (ni * lr_r - nr * li_r) / den
        bb_re = kap_re * bt_re_ref[0, 0] - kap_im * bt_im_ref[0, 0]
        bb_im = kap_re * bt_im_ref[0, 0] + kap_im * bt_re_ref[0, 0]

        resp.append(jnp.dot(jnp.where(left16, bb_re, -bb_im), jnp.concatenate([cpr, cpi], axis=0),
                            precision=lax.Precision.HIGHEST, preferred_element_type=F32))
        m_ref[0, d, 0] = jnp.concatenate([cpr1, -cpi1], axis=0).astype(BF16)

        blocks = [(bb_re, bb_im)]
        for _ in range(C - 1):
            br_, bi_ = blocks[-1]
            blocks.append((br_ * lb_re - bi_ * lb_im, br_ * lb_im + bi_ * lb_re))
        if d == 0:
            blocks.reverse()
        n_re = jnp.concatenate([b_[0] for b_ in blocks], axis=0)
        n_im = jnp.concatenate([b_[1] for b_ in blocks], axis=0)
        n_ref[0, d, 0] = jnp.concatenate([jnp.where(left, n_re, n_im), jnp.where(left, n_im, n_re)],
                                         axis=1).astype(BF16)

        ar, ai = lb_re, lb_im
        for _ in range(nbits):
            ar, ai = ar * ar - ai * ai, 2.0 * ar * ai
        a2 = jnp.where(left1, -ai, ai)
        a_ref[0, d, 0] = jnp.concatenate([jnp.concatenate([ar, ar], axis=1),
                                          jnp.concatenate([a2, -a2], axis=1)], axis=0)

    rf, rb = resp
    for s in range(C):
        nf = S5_GROUP_CH * s
        blk = jnp.where(lane >= nf, pltpu.roll(rf, nf, 1) if nf else rf, 0.0)
        nb = S5_GROUP_CH * (C - 1 - s)
        blk = blk + jnp.where(lane < S5_IN - nb, pltpu.roll(rb, S5_IN - nb, 1) if nb else rb, 0.0)
        t_ref[0, 0, S5_GROUP_CH * s:S5_GROUP_CH * (s + 1), :] = blk.astype(BF16)


def _s5_prep(lam_re, lam_im, log_step, b_re, b_im, c_re, c_im):
    G = S5_GROUPS
    dup = lambda a: jnp.concatenate([a, a], axis=-1)
    lamc = [a.reshape(DEPTH, 2, G, S5_STATE, 1) for a in (lam_re, lam_im)]
    lamr = [dup(a).reshape(DEPTH, 2, G, 1, S5_ST) for a in (lam_re, lam_im)]
    lstep = log_step.reshape(DEPTH, 2, G, 1, 1)
    bt = [dup(jnp.swapaxes(a, 2, 3)) for a in (b_re, b_im)]
    ct = [jnp.swapaxes(a, 2, 3) for a in (c_re, c_im)]
    dspec = lambda r, c: pl.BlockSpec((1, 2, 1, r, c), lambda l, g: (l, 0, g, 0, 0))
    gspec = lambda r, c: pl.BlockSpec((1, 1, r, c), lambda l, g: (l, g, 0, 0))
    t, m, n, a = pl.pallas_call(
        _s5_prep_kernel,
        grid=(DEPTH, G),
        in_specs=[dspec(S5_STATE, 1), dspec(S5_STATE, 1), dspec(1, S5_ST), dspec(1, S5_ST), dspec(1, 1),
                  gspec(S5_GROUP_CH, S5_ST), gspec(S5_GROUP_CH, S5_ST),
                  gspec(S5_STATE, S5_GROUP_CH), gspec(S5_STATE, S5_GROUP_CH)],
        out_specs=[gspec(S5_IN, S5_IN), dspec(S5_ST, S5_IN), dspec(S5_IN, S5_ST2), dspec(2, S5_ST2)],
        out_shape=[jax.ShapeDtypeStruct((DEPTH, G, S5_IN, S5_IN), BF16),
                   jax.ShapeDtypeStruct((DEPTH, 2, G, S5_ST, S5_IN), BF16),
                   jax.ShapeDtypeStruct((DEPTH, 2, G, S5_IN, S5_ST2), BF16),
                   jax.ShapeDtypeStruct((DEPTH, 2, G, 2, S5_ST2), F32)],
        compiler_params=_cparams(("arbitrary", "arbitrary")),
        name="s5_prep",
    )(*lamc, *lamr, lstep, *bt, *ct)
    return t, m, n, jnp.transpose(a, (0, 1, 3, 2, 4)).reshape(DEPTH, 2, 2, G * S5_ST2)


def _s5_kernel(nseg, nchunks, zr_ref, t_ref, m_ref, n_ref, a_ref, x0_ref, d_ref,
               y_ref, xfin_ref, v_ref, xp_ref):
    W = S5_GB * S5_ST2
    for gl in range(S5_GB):
        u = zr_ref[0, :, gl * S5_IN:(gl + 1) * S5_IN].astype(BF16)
        for d in range(2):
            v_ref[d, :, gl * S5_ST2:(gl + 1) * S5_ST2] = jnp.dot(
                u, n_ref[d, gl], preferred_element_type=F32)

    def halves(x, which):
        return [x[:, g * S5_ST2 + h * S5_ST:g * S5_ST2 + (h + 1) * S5_ST]
                for g in range(S5_GB) for h in which]

    assert 2 * TB == SUBLANES
    steps = nchunks // 2
    is_fwd = lax.broadcasted_iota(jnp.int32, (SUBLANES, 1), 0) < TB
    both = lambda f, b: jnp.where(is_fwd, f, b)
    flip = lambda x: pltpu.roll(x, TB, 0)
    a = both(a_ref[0, 0:1, :], a_ref[1, 0:1, :])
    a2 = both(a_ref[0, 1:2, :], a_ref[1, 1:2, :])
    state = lambda x: jnp.concatenate(halves(x, (0,)), axis=1)

    def advance(x, v):
        swapped = jnp.concatenate(halves(x, (1, 0)), axis=1)
        return a * x + a2 * swapped + v

    for seg in range(nseg):
        base = seg * nchunks * TB

        def step(i, x):
            rf = pl.ds(pl.multiple_of(base + i * SUBLANES, SUBLANES), SUBLANES)
            rb = pl.ds(pl.multiple_of(base + (steps - 1 - i) * SUBLANES, SUBLANES), SUBLANES)
            vf, vb = v_ref[0, rf, :], v_ref[1, rb, :]
            e1 = state(x)
            x = advance(x, both(vf, vb))
            e2 = state(x)
            x = advance(x, both(flip(vf), flip(vb)))
            xp_ref[0, rf, :] = both(e1, flip(e2))
            xp_ref[1, rb, :] = both(flip(e2), e1)
            return x

        rows = slice(seg * TB, (seg + 1) * TB)
        x0 = (jnp.zeros((SUBLANES, W), F32) if x0_ref is None
              else jnp.concatenate([x0_ref[0, rows, :], x0_ref[1, rows, :]], axis=0))
        x = state(lax.fori_loop(0, steps, step, x0))
        if xfin_ref is not None:
            xfin_ref[0, rows, :] = x[:TB]
            xfin_ref[1, rows, :] = x[TB:]

    for gl in range(S5_GB):
        cols = slice(gl * S5_IN, (gl + 1) * S5_IN)
        u = zr_ref[0, :, cols]
        y = jnp.dot(u.astype(BF16), t_ref[gl], preferred_element_type=F32)
        xin = jnp.concatenate([xp_ref[d, :, gl * S5_ST:(gl + 1) * S5_ST] for d in range(2)], axis=1)
        m_both = jnp.concatenate([m_ref[0, gl], m_ref[1, gl]], axis=0)
        y = y + jnp.dot(xin.astype(BF16), m_both, preferred_element_type=F32)
        y_ref[0, :, cols] = jax.nn.gelu(y + d_ref[0, :, cols] * u)


def _s5(p, l, zr, t, m, n, a, dt, x0=None, want_final=False):
    G = S5_GROUPS
    nseg = p.nb // TB
    row = pl.BlockSpec((1, R_PASS, S5_ROW), lambda j: (j, 0, 0))
    dsp = lambda r, c: pl.BlockSpec((None, 2, S5_GB, r, c), lambda j: (l, 0, j, 0, 0))
    lsp = lambda r, w: pl.BlockSpec((2, r, S5_GB * w), lambda j: (0, 0, j))
    in_specs = [row, pl.BlockSpec((None, S5_GB, S5_IN, S5_IN), lambda j: (l, j, 0, 0)),
                dsp(S5_ST, S5_IN), dsp(S5_IN, S5_ST2),
                pl.BlockSpec((None, 2, 2, S5_GB * S5_ST2), lambda j: (l, 0, 0, j)),
                pl.BlockSpec((None, 1, 1, S5_ROW), lambda j: (l, j, 0, 0))]
    args = [zr, t, m, n, a, dt]
    if x0 is not None:
        in_specs.append(lsp(p.nb, S5_ST2))
        args.append(x0)
    out_specs = [row]
    out_shape = [jax.ShapeDtypeStruct((S5_NB, R_PASS, S5_ROW), F32)]
    if want_final:
        out_specs.append(lsp(p.nb, S5_ST))
        out_shape.append(jax.ShapeDtypeStruct((2, p.nb, G * S5_ST), F32))

    def body(zr_ref, t_ref, m_ref, n_ref, a_ref, d_ref, *rest):
        rest = list(rest)
        x0_ref = rest.pop(0) if x0 is not None else None
        y_ref = rest.pop(0)
        xfin_ref = rest.pop(0) if want_final else None
        _s5_kernel(nseg, p.seq // S5_CHUNK, zr_ref, t_ref, m_ref, n_ref, a_ref, x0_ref, d_ref,
                   y_ref, xfin_ref, *rest)

    return pl.pallas_call(
        body,
        grid=(S5_NB,),
        in_specs=in_specs,
        out_specs=out_specs,
        out_shape=out_shape,
        scratch_shapes=[pltpu.VMEM((2, R_PASS, S5_GB * S5_ST2), F32),
                        pltpu.VMEM((2, R_PASS, S5_GB * S5_ST), F32)],
        compiler_params=_cparams(("arbitrary",)),
        name="s5_scan_%d" % p.seq,
    )(*args)


def _mix_stages(x_ref, mod_ref, fo_ref, mo_ref, ys_ref, wglu_ref, wo_ref, nw_ref, scr_ref, out):
    mod = mod_ref[...]
    g1 = mod[:, :, 2 * D_MODEL:3 * D_MODEL]
    flat = lambda ref: ref[...].reshape(TB * TT, ref.shape[-1])
    mo = flat(mo_ref)
    mo = jnp.concatenate([mo[:, h * HEAD_PAD:h * HEAD_PAD + MLSTM_DH] for h in range(MLSTM_HEADS)], axis=1)
    half = (FOURIER_W + MLSTM_W + S5_W) // 2
    m_split = half - FOURIER_W
    mix = jnp.dot(jnp.concatenate([flat(fo_ref), mo[:, :m_split]], axis=1), wo_ref[:half, :],
                  preferred_element_type=F32)
    yield
    blocks = []
    for bl in range(S5_NB):
        by_group = [ys_ref[bl, :, gl * S5_IN:(gl + 1) * S5_IN] for gl in range(S5_GB)]
        for t in range(S5_CHUNK):
            lo = t * S5_GROUP_CH
            scr_ref[bl, pl.ds(t, TROWS, stride=S5_CHUNK), :] = jnp.concatenate(
                [y[:, lo:lo + S5_GROUP_CH] for y in by_group], axis=1)
        slabs = [scr_ref[bl, _slab(b, c)[1], :] for b in range(TB) for c in range(TT // S5_CHUNK)]
        blocks.append(jnp.concatenate(slabs, axis=0))
        yield
    y = jnp.concatenate(blocks, axis=1).astype(BF16)
    gg = jnp.dot(y, wglu_ref[...], preferred_element_type=F32)
    s_out = (gg[:, :S5_W] * jax.nn.sigmoid(gg[:, S5_W:])).astype(BF16)
    mix = mix + jnp.dot(jnp.concatenate([mo[:, m_split:], s_out], axis=1), wo_ref[half:, :],
                        preferred_element_type=F32)
    yield
    x1 = x_ref[...] + g1 * mix.reshape(TB, TT, D_MODEL)
    out += [x1, _mod_norm(x1, mod, nw_ref[...], 3)]


FF_SPLIT = (1024, 1024, 768)
assert sum(FF_SPLIT) == D_FF and all(w % MXU_N == 0 for w in FF_SPLIT)


def _ffn_stages(final, xn, x1, mod_ref, wg_ref, wu_ref, wd_ref, nf_ref, out):
    ff = None
    o = 0
    for w in FF_SPLIT:
        cols = slice(o, o + w)
        o += w
        a = jnp.dot(xn, wg_ref[:, cols], preferred_element_type=F32)
        u = jnp.dot(xn, wu_ref[:, cols], preferred_element_type=F32)
        h = (a * jax.nn.sigmoid(a) * u).astype(BF16)
        part = jnp.dot(h, wd_ref[cols, :], preferred_element_type=F32)
        ff = part if ff is None else ff + part
        yield
    g2 = mod_ref[...][:, :, 5 * D_MODEL:6 * D_MODEL]
    x2 = x1 + g2 * ff.reshape(TB, TT, D_MODEL)
    if final:
        x2 = x2 * lax.rsqrt(jnp.mean(x2 * x2, axis=-1, keepdims=True) + EPS) * nf_ref[...]
    out.append(x2)


def _post_kernel(final, n, x_ref, moda_ref, fo_ref, mo_ref, ys_ref, modb_ref, wglu_ref, wo_ref, nw_ref,
                 wg_ref, wu_ref, wd_ref, nf_ref, o_ref, x1_ref, xn_ref, scr_ref):
    i = pl.program_id(0)
    cur = i % 2
    prev = 1 - cur

    def run(do_ffn, do_mix):
        res_a, res_b, gens = [], [], []
        if do_ffn:
            gens.append(_ffn_stages(final, xn_ref[prev], x1_ref[prev], modb_ref, wg_ref, wu_ref, wd_ref,
                                    nf_ref, res_b))
        if do_mix:
            gens.append(_mix_stages(x_ref, moda_ref, fo_ref, mo_ref, ys_ref, wglu_ref, wo_ref, nw_ref,
                                    scr_ref, res_a))
        _alternate(gens)
        if do_ffn:
            o_ref[...] = res_b[0]
        if do_mix:
            x1_ref[cur] = res_a[0]
            xn_ref[cur] = res_a[1]

    pl.when(i == 0)(lambda: run(False, True))
    pl.when(jnp.logical_and(i > 0, i < n))(lambda: run(True, True))
    pl.when(i == n)(lambda: run(True, False))


def _post(p, l, final, x, mods, fo, mo, ys, wglu, wo, nw, wg, wu, wd, nf):
    ts = _tile_specs(p, l)
    once = lambda a: pl.BlockSpec((None,) + a.shape[1:], lambda i: (l,) + (0,) * (a.ndim - 1),
                                  pipeline_mode=pl.Buffered(1))
    return pl.pallas_call(
        functools.partial(_post_kernel, final, ts.n),
        grid=(ts.n + 1,),
        in_specs=[ts.tok(D_MODEL, ts.head), ts.mod(ts.head), ts.tok(FOURIER_W, ts.head),
                  ts.tok(HEADS_W, ts.head), ts.rows(ts.head), ts.mod(ts.tail),
                  once(wglu), once(wo), once(nw), once(wg), once(wu), once(wd),
                  pl.BlockSpec((1, D_MODEL), lambda i: (0, 0))],
        out_specs=ts.tok(D_MODEL, ts.tail),
        out_shape=jax.ShapeDtypeStruct((p.nb, p.seq, D_MODEL), F32),
        scratch_shapes=[pltpu.VMEM((2, TB, TT, D_MODEL), F32), pltpu.VMEM((2, TB * TT, D_MODEL), BF16),
                        pltpu.VMEM((S5_NB, TB * TT, LANES), F32)],
        compiler_params=pltpu.CompilerParams(dimension_semantics=("arbitrary",),
                                             vmem_limit_bytes=POST_VMEM_LIMIT),
        name="post_%d" % p.seq,
    )(x, mods, fo, mo, ys, mods, wglu, wo, nw, wg, wu, wd, nf)


def _pad_heads(a, axis):
    shape = a.shape[:axis] + (MLSTM_HEADS, MLSTM_DH) + a.shape[axis + 1:]
    pad = [(0, 0)] * (a.ndim + 1)
    pad[axis + 1] = (0, HEAD_PAD - MLSTM_DH)
    return jnp.pad(a.reshape(shape), pad).reshape(a.shape[:axis] + (HEADS_W,) + a.shape[axis + 1:])


def _mlstm_state_in(c, n):
    cn = jnp.concatenate([c, n[..., None]], axis=-1)
    return jnp.pad(cn, ((0, 0),) * (c.ndim - 2) + ((0, HEAD_PAD - MLSTM_DH), (0, HEAD_PAD - MLSTM_DH - 1)))


def kernel(x_prompt, x_sample, state_mlstm_C, state_mlstm_n, state_mlstm_m, state_s5_re, state_s5_im,
           c, c_ctx, w_ada, b_ada, norm1_w, norm2_w, w_in, b_gates, w_fourier, mlstm_norm_w,
           s5_lambda_re, s5_lambda_im, s5_log_step, s5_b_re, s5_b_im, s5_c_re, s5_c_im, s5_d,
           w_glu, w_out, w_gate, w_up, w_down, norm_f):
    xs = {PROMPT: x_prompt, SAMPLE: x_sample}
    cc = jnp.concatenate([c, c_ctx[None], jnp.zeros((N_MODS - 1 - DEC_BATCH, D_MODEL), F32)], axis=0)
    mods = _ada(cc, w_ada, b_ada).reshape(DEPTH, N_MODS, 1, 6 * D_MODEL)
    cdsd, cs, ab = (jnp.asarray(a.astype(np.float32)).astype(BF16) for a in _dft_consts())
    s5_t, s5_m, s5_n, s5_a = _s5_prep(s5_lambda_re, s5_lambda_im, s5_log_step, s5_b_re, s5_b_im,
                                      s5_c_re, s5_c_im)

    o_q = FOURIER_W
    o_g = o_q + 3 * MLSTM_W
    o_o = o_g + N_GATES
    o_u = o_o + MLSTM_W
    cols = lambda o, w: w_in[:, :, o:o + w]
    w_cat = jnp.concatenate([cols(o_u, S5_W), cols(0, FOURIER_W), cols(o_q, 2 * MLSTM_W), cols(o_o, MLSTM_W)],
                            axis=2).astype(BF16)
    wv_t = jnp.swapaxes(cols(o_q + 2 * MLSTM_W, MLSTM_W), 1, 2).astype(BF16)
    gate_perm = np.arange(N_GATES).reshape(2, 2, MLSTM_HEADS).transpose(1, 0, 2).reshape(-1)
    wg_t = jnp.swapaxes(w_in[:, :, o_g:o_o], 1, 2)[:, gate_perm].astype(BF16)
    bg = b_gates[:, gate_perm, None]
    wf = w_fourier.astype(BF16)
    nw = _pad_heads(mlstm_norm_w, 1).reshape(DEPTH, MLSTM_HEADS, HEAD_PAD, 1)
    dt = jnp.tile(s5_d[:, :, None, :], (1, 1, S5_CHUNK, 1)).reshape(DEPTH, S5_NB, 1, S5_ROW)
    wo = w_out.astype(BF16)
    wglu = w_glu.astype(BF16)
    wg, wu, wd = w_gate.astype(BF16), w_up.astype(BF16), w_down.astype(BF16)
    n1, n2 = norm1_w[:, None, :], norm2_w[:, None, :]

    m0 = jnp.swapaxes(state_mlstm_m, 0, 1)
    c0 = jnp.swapaxes(_mlstm_state_in(state_mlstm_C, state_mlstm_n), 0, 1)
    x0 = jnp.concatenate([state_s5_re, state_s5_im, state_s5_im, state_s5_re], axis=-1)
    x0 = jnp.transpose(x0, (1, 2, 0, 3, 4)).reshape(DEPTH, 2, DEC_BATCH, S5_GROUPS * S5_ST2)

    finals, cfins = [], []
    for l in range(DEPTH):
        for p in (PROMPT, SAMPLE):
            x = xs[p]
            zq, zk, zo, zf, zu, vt, gt = _in_proj(p, l, x, mods, n1, w_cat, wv_t, wg_t, bg)
            pr = _gate_prep(p, gt)
            if p is PROMPT:
                fo = _fourier_prompt(l, zf, cdsd, cs, wf)
                mo, cfin, nfin, mfin = _mlstm(p, l, zq, zk, vt, zo, pr, nw, c_prev=cfins)
                cfins.append(cfin)
                ys, xfin = _s5(p, l, zu, s5_t, s5_m, s5_n, s5_a, dt, want_final=True)
                finals.append([nfin, mfin, xfin])
            else:
                fo = _fourier_sample(l, zf, cdsd, ab, wf)
                mo, = _mlstm(p, l, zq, zk, vt, zo, pr, nw, init=(m0[l], c0[l]))
                ys, = _s5(p, l, zu, s5_t, s5_m, s5_n, s5_a, dt, x0=x0[l])
            xs[p] = _post(p, l, l == DEPTH - 1, x, mods, fo, mo, ys, wglu, wo, n2, wg, wu, wd, norm_f[None])

    nfin, mfin, xfin = (jnp.stack(parts, axis=1) for parts in zip(*finals))
    xfin = xfin.reshape(2, DEPTH, BATCH, S5_GROUPS, 2, S5_STATE)
    new_re, new_im = (jnp.transpose(xfin[:, :, :, :, i], (2, 1, 0, 3, 4)) for i in range(2))
    return (xs[PROMPT], xs[SAMPLE], cfin, nfin[:, :, :, :, 0], mfin[:, :, :, :, 0, 0], new_re, new_im)
```

```python
import collections
import functools
import math

import numpy as np
import jax
import jax.numpy as jnp
from jax import lax
from jax.experimental import pallas as pl
from jax.experimental.pallas import tpu as pltpu

F32 = jnp.float32
BF16 = jnp.bfloat16

D_MODEL = 1024
BATCH = 32
SEQ = 256
DEPTH = 2
DEC_BATCH = 4
DEC_SEQ = 2048
GRID_W = 64
FOURIER_W = 256
FOURIER_DH = 64
MLSTM_W = 384
MLSTM_HEADS = 4
MLSTM_DH = 96
S5_W = 384
S5_GROUP_CH = 16
S5_GROUPS = 24
S5_STATE = 64
N_GATES = 16
D_FF = 2816
EPS = 1e-6

LANES = 128
SUBLANES = 8
MXU_N = 256
VMEM_LIMIT = 56 * 1024 * 1024
POST_VMEM_LIMIT = 60 * 1024 * 1024

HEAD_PAD = LANES
HEADS_W = MLSTM_HEADS * HEAD_PAD
N_AUG = MLSTM_DH
Z_W = S5_W + FOURIER_W + 3 * MLSTM_W
MLSTM_CHUNK = 256
S5_CHUNK = 16
S5_IN = S5_CHUNK * S5_GROUP_CH
S5_ST = 2 * S5_STATE
S5_ST2 = 2 * S5_ST
S5_GB = LANES // S5_GROUP_CH
S5_NB = S5_W // LANES
S5_ROW = S5_GB * S5_IN
TB = 4
TT = 128
TROWS = TB * TT // S5_CHUNK
HPS = 4
N_MODS = 8
NEG = -1e30

Pass = collections.namedtuple("Pass", "nb seq mod_first mod_each")
PROMPT = Pass(BATCH, SEQ, DEC_BATCH, False)
SAMPLE = Pass(DEC_BATCH, DEC_SEQ, 0, True)
T_PASS = BATCH * SEQ
assert T_PASS == DEC_BATCH * DEC_SEQ
R_PASS = T_PASS // S5_CHUNK

_NT = (((1,), (1,)), ((), ()))
_TN = (((0,), (0,)), ((), ()))


def _cparams(sem):
    return pltpu.CompilerParams(dimension_semantics=sem, vmem_limit_bytes=VMEM_LIMIT)


def _full(a):
    return pl.BlockSpec(a.shape, lambda *_: (0,) * a.ndim)


def _layer(a, l):
    return pl.BlockSpec((None,) + a.shape[1:], lambda *_: (l,) + (0,) * (a.ndim - 1))


def _log_sigmoid(x):
    return jnp.minimum(x, 0.0) - jnp.log1p(jnp.exp(-jnp.abs(x)))


TileSpecs = collections.namedtuple("TileSpecs", "n head tail tok rows mod")


def _tile_specs(p, l):
    nk = p.seq // TT
    n = (p.nb // TB) * nk
    head = lambda i: jnp.minimum(i, n - 1)
    tail = lambda i: jnp.maximum(i - 1, 0)
    tok = lambda w, tile: pl.BlockSpec((TB, TT, w), lambda i: (tile(i) // nk, tile(i) % nk, 0))
    rows = lambda tile: pl.BlockSpec((S5_NB, TROWS, S5_ROW), lambda i: (0, tile(i), 0))

    def mod(tile):
        if p.mod_each:
            return pl.BlockSpec((None, TB, 1, 6 * D_MODEL), lambda i: (l, p.mod_first // TB + tile(i) // nk, 0, 0))
        return pl.BlockSpec((None, 1, 1, 6 * D_MODEL), lambda i: (l, p.mod_first, 0, 0))

    return TileSpecs(n, head, tail, tok, rows, mod)


def _slab(b, c):
    tok = slice(b * TT + c * S5_CHUNK, b * TT + (c + 1) * S5_CHUNK)
    chk = slice((c * TB + b) * S5_CHUNK, (c * TB + b + 1) * S5_CHUNK)
    return tok, chk


def _ada_kernel(c_ref, w_ref, b_ref, o_ref):
    a = c_ref[...]
    a = (a * jax.nn.sigmoid(a)).astype(BF16)
    o_ref[0] = jnp.dot(a, w_ref[0].astype(BF16), preferred_element_type=F32) + b_ref[0]


def _ada(cc, w_ada, b_ada):
    tn = 3072
    return pl.pallas_call(
        _ada_kernel,
        grid=(DEPTH, 6 * D_MODEL // tn),
        in_specs=[pl.BlockSpec((N_MODS, D_MODEL), lambda l, j: (0, 0)),
                  pl.BlockSpec((1, D_MODEL, tn), lambda l, j: (l, 0, j)),
                  pl.BlockSpec((1, 1, tn), lambda l, j: (l, 0, j))],
        out_specs=pl.BlockSpec((1, N_MODS, tn), lambda l, j: (l, 0, j)),
        out_shape=jax.ShapeDtypeStruct((DEPTH, N_MODS, 6 * D_MODEL), F32),
        compiler_params=_cparams(("arbitrary", "arbitrary")),
        name="ada_mod",
    )(cc, w_ada, b_ada.reshape(DEPTH, 1, 6 * D_MODEL))


def _mod_norm(x3, mod, nw, first):
    sh = mod[:, :, first * D_MODEL:(first + 1) * D_MODEL]
    sc = mod[:, :, (first + 1) * D_MODEL:(first + 2) * D_MODEL]
    y = x3 * lax.rsqrt(jnp.mean(x3 * x3, axis=-1, keepdims=True) + EPS) * nw
    return (y * (1.0 + sc) + sh).reshape(-1, D_MODEL).astype(BF16)


def _alternate(gens):
    gens = list(gens)
    while gens:
        alive = []
        for g in gens:
            try:
                next(g)
                alive.append(g)
            except StopIteration:
                pass
        gens = alive


def _in_kernel(x_ref, mod_ref, nw_ref, w_ref, wv_ref, wg_ref, bg_ref,
               zq_ref, zk_ref, zo_ref, zf_ref, zu_ref, vt_ref, gt_ref, scr_ref):
    xn = _mod_norm(x_ref[...], mod_ref[...], nw_ref[...], 0)
    w_first = -(-S5_W // MXU_N) * MXU_N
    zu = jnp.dot(xn, w_ref[:, :w_first], preferred_element_type=F32)
    for bl in range(S5_NB):
        zb = zu[:, bl * LANES:(bl + 1) * LANES]
        for b in range(TB):
            for c in range(TT // S5_CHUNK):
                tok, chk = _slab(b, c)
                scr_ref[bl, chk, :] = zb[tok]
        by_token = [scr_ref[bl, pl.ds(s, TROWS, stride=S5_CHUNK), :] for s in range(S5_CHUNK)]
        for gl in range(S5_GB):
            lo = gl * S5_GROUP_CH
            zu_ref[bl, :, gl * S5_IN:(gl + 1) * S5_IN] = jnp.concatenate(
                [x[:, lo:lo + S5_GROUP_CH] for x in by_token], axis=1)
        if bl == 0:
            z = jnp.concatenate([zu[:, S5_W:], jnp.dot(xn, w_ref[:, w_first:], preferred_element_type=F32)],
                                axis=1)
    zf_ref[...] = z[:, :FOURIER_W].astype(BF16).reshape(TB, TT, FOURIER_W)
    gap = jnp.zeros((TB * TT, HEAD_PAD - MLSTM_DH), BF16)
    for j, ref in enumerate((zq_ref, zk_ref, zo_ref)):
        t = z[:, FOURIER_W + j * MLSTM_W:FOURIER_W + (j + 1) * MLSTM_W].astype(BF16)
        ref[...] = jnp.concatenate(
            [piece for h in range(MLSTM_HEADS) for piece in (t[:, h * MLSTM_DH:(h + 1) * MLSTM_DH], gap)],
            axis=1).reshape(TB, TT, HEADS_W)
    vt = lax.dot_general(wv_ref[...], xn, _NT, preferred_element_type=F32).astype(BF16)
    gt = lax.dot_general(wg_ref[...], xn, _NT, preferred_element_type=F32) + bg_ref[...]
    for b in range(TB):
        toks = slice(b * TT, (b + 1) * TT)
        for h in range(MLSTM_HEADS):
            vt_ref[b, h * HEAD_PAD:h * HEAD_PAD + MLSTM_DH, :] = vt[h * MLSTM_DH:(h + 1) * MLSTM_DH, toks]
            vt_ref[b, h * HEAD_PAD + MLSTM_DH:(h + 1) * HEAD_PAD, :] = jnp.zeros((HEAD_PAD - MLSTM_DH, TT), BF16)
        gt_ref[b] = gt[:, toks]


def _in_proj(p, l, x, mods, nw, w, wv_t, wg_t, bg):
    ts = _tile_specs(p, l)
    nk = p.seq // TT
    tile = lambda i: i
    chan = lambda c_: pl.BlockSpec((TB, c_, TT), lambda i: (i // nk, 0, i % nk))
    outs = [HEADS_W] * 3 + [FOURIER_W]
    return pl.pallas_call(
        _in_kernel,
        grid=(ts.n,),
        in_specs=[ts.tok(D_MODEL, tile), ts.mod(tile),
                  _layer(nw, l), _layer(w, l), _layer(wv_t, l), _layer(wg_t, l), _layer(bg, l)],
        out_specs=[ts.tok(w_, tile) for w_ in outs] + [ts.rows(tile), chan(HEADS_W), chan(N_GATES)],
        out_shape=[jax.ShapeDtypeStruct((p.nb, p.seq, w_), BF16) for w_ in outs]
        + [jax.ShapeDtypeStruct((S5_NB, R_PASS, S5_ROW), F32),
           jax.ShapeDtypeStruct((p.nb, HEADS_W, p.seq), BF16),
           jax.ShapeDtypeStruct((p.nb, N_GATES, p.seq), F32)],
        scratch_shapes=[pltpu.VMEM((S5_NB, TB * TT, LANES), F32)],
        compiler_params=_cparams(("arbitrary",)),
        name="in_proj_%d" % p.seq,
    )(x, mods, nw, w, wv_t, wg_t, bg)


def _dft_consts():
    d = np.arange(FOURIER_DH)
    phi = 2.0 * np.pi * ((d[:, None] * d[None, :]) % FOURIER_DH) / FOURIER_DH
    eye = np.eye(FOURIER_W // FOURIER_DH)
    cd = np.kron(eye, np.cos(phi)) / math.sqrt(FOURIER_DH)
    sd = np.kron(eye, np.sin(phi)) / math.sqrt(FOURIER_DH)
    s = np.arange(SEQ)
    th = 2.0 * np.pi * ((s[:, None] * s[None, :]) % SEQ) / SEQ
    rows = DEC_SEQ // GRID_W
    pos = np.arange(DEC_SEQ)
    r, c = pos // GRID_W, pos % GRID_W
    ph = ((r[:, None] * r[None, :]) * (GRID_W // rows) + c[:, None] * c[None, :]) % GRID_W
    th2 = 2.0 * np.pi * ph / GRID_W
    return (np.concatenate([cd, sd], axis=1),
            np.concatenate([np.cos(th), -np.sin(th)], axis=1) / math.sqrt(SEQ),
            np.concatenate([np.cos(th2), -np.sin(th2)], axis=1) / math.sqrt(DEC_SEQ))


def _fourier_prompt_kernel(nb, zf_ref, cdsd_ref, cs_ref, wf_ref, o_ref):
    t = jnp.dot(zf_ref[...].reshape(nb * SEQ, FOURIER_W), cdsd_ref[...],
                preferred_element_type=F32).astype(BF16)
    fs = []
    for b in range(nb):
        tb = t[b * SEQ:(b + 1) * SEQ]
        st = jnp.concatenate([tb[:, :FOURIER_W], tb[:, FOURIER_W:]], axis=0)
        fs.append(jnp.dot(cs_ref[...], st, preferred_element_type=F32))
    for b in range(nb):
        o_ref[b] = jnp.dot(fs[b].astype(BF16), wf_ref[...], preferred_element_type=F32).astype(BF16)


def _fourier_prompt(l, zf, cdsd, cs, wf):
    nb = 8
    blk = pl.BlockSpec((nb, SEQ, FOURIER_W), lambda i: (i, 0, 0))
    return pl.pallas_call(
        functools.partial(_fourier_prompt_kernel, nb),
        grid=(BATCH // nb,),
        in_specs=[blk, _full(cdsd), _full(cs), _layer(wf, l)],
        out_specs=blk,
        out_shape=jax.ShapeDtypeStruct((BATCH, SEQ, FOURIER_W), BF16),
        compiler_params=_cparams(("arbitrary",)),
        name="fourier_prompt",
    )(zf, cdsd, cs, wf)


def _fourier_sample_kernel(zf_ref, cdsd_ref, ab_ref, wf_ref, o_ref, tt_ref):
    @pl.when(pl.program_id(0) == 0)
    def _():
        for b in range(DEC_BATCH):
            t = jnp.dot(zf_ref[b], cdsd_ref[...], preferred_element_type=F32).astype(BF16)
            tt_ref[b, 0:DEC_SEQ, :] = t[:, :FOURIER_W]
            tt_ref[b, DEC_SEQ:2 * DEC_SEQ, :] = t[:, FOURIER_W:]

    fs = [jnp.dot(ab_ref[...], tt_ref[b], preferred_element_type=F32) for b in range(DEC_BATCH)]
    for b in range(DEC_BATCH):
        o_ref[b] = jnp.dot(fs[b].astype(BF16), wf_ref[...], preferred_element_type=F32).astype(BF16)


def _fourier_sample(l, zf, cdsd, ab, wf):
    tk = 512
    return pl.pallas_call(
        _fourier_sample_kernel,
        grid=(DEC_SEQ // tk,),
        in_specs=[_full(zf), _full(cdsd), pl.BlockSpec((tk, 2 * DEC_SEQ), lambda i: (i, 0)), _layer(wf, l)],
        out_specs=pl.BlockSpec((DEC_BATCH, tk, FOURIER_W), lambda i: (0, i, 0)),
        out_shape=jax.ShapeDtypeStruct((DEC_BATCH, DEC_SEQ, FOURIER_W), BF16),
        scratch_shapes=[pltpu.VMEM((DEC_BATCH, 2 * DEC_SEQ, FOURIER_W), BF16)],
        compiler_params=_cparams(("arbitrary",)),
        name="fourier_sample",
    )(zf, cdsd, ab, wf)


def _split3(x):
    hi = x.astype(BF16).astype(F32)
    mid = (x - hi).astype(BF16).astype(F32)
    lo = (x - hi - mid).astype(BF16).astype(F32)
    return hi, mid, lo


def _gate_kernel(g_ref, o_ref):
    L = MLSTM_CHUNK
    nrow = N_GATES // 2
    row = lax.broadcasted_iota(jnp.int32, (L, L), 0)
    col = lax.broadcasted_iota(jnp.int32, (L, L), 1)
    tri_pre = jnp.where(row <= col, 1.0, 0.0).astype(BF16)
    tri_suf = jnp.where(row >= col, 1.0, 0.0).astype(BF16)
    is_fwd = lax.broadcasted_iota(jnp.int32, (nrow, L), 0) < MLSTM_HEADS
    lane = lax.broadcasted_iota(jnp.int32, (nrow, L), 1)
    chunks = [(bi, slice(c * L, (c + 1) * L)) for bi in range(g_ref.shape[0]) for c in range(g_ref.shape[2] // L)]
    fold = lambda a: a[0:nrow] + a[nrow:2 * nrow] + a[2 * nrow:]
    bs, rs = [], []
    for bi, cols in chunks:
        lf = _log_sigmoid(g_ref[bi, nrow:, cols])
        parts = jnp.concatenate(_split3(lf), axis=0).astype(BF16)
        pre = jnp.dot(parts, tri_pre, preferred_element_type=F32)
        suf = jnp.dot(parts, tri_suf, preferred_element_type=F32)
        bs.append(jnp.where(is_fwd, fold(pre), fold(suf)))
        rs.append(g_ref[bi, 0:nrow, cols] - bs[-1])
    pms, sms = list(rs), list(rs)
    sh = 1
    while sh < L:
        pms = [jnp.maximum(x, jnp.where(lane >= sh, pltpu.roll(x, sh, 1), NEG)) for x in pms]
        sms = [jnp.maximum(x, jnp.where(lane < L - sh, pltpu.roll(x, L - sh, 1), NEG)) for x in sms]
        sh *= 2
    for (bi, cols), b, r, pm, sm in zip(chunks, bs, rs, pms, sms):
        for q, val in enumerate((b, r, jnp.where(is_fwd, pm, sm))):
            for dh in range(nrow):
                o_ref[bi, dh, q:q + 1, cols] = val[dh:dh + 1]


def _gate_prep(p, gt):
    bb = max(1, DEC_SEQ // p.seq)
    nrow = N_GATES // 2
    return pl.pallas_call(
        _gate_kernel,
        grid=(p.nb // bb,),
        in_specs=[pl.BlockSpec((bb, N_GATES, p.seq), lambda i: (i, 0, 0))],
        out_specs=pl.BlockSpec((bb, nrow, 3, p.seq), lambda i: (i, 0, 0, 0)),
        out_shape=jax.ShapeDtypeStruct((p.nb, nrow, 3, p.seq), F32),
        compiler_params=_cparams(("arbitrary",)),
        name="gate_prep_%d" % p.seq,
    )(gt)


def _mlstm_chunk(q, k, vt, pr, ct, m, fwd, mxu_arg, out):
    L = q.shape[0]
    scale = MLSTM_DH ** -0.5
    b, r, cm = pr[0:1], pr[1:2], pr[2:3]
    if mxu_arg:
        ones = jnp.ones((3, L), F32)
        zeros = jnp.zeros((SUBLANES - 6, L), F32)
        lhs = jnp.concatenate(_split3(r) + (ones, zeros), axis=0).astype(BF16)
        rhs = jnp.concatenate((ones,) + _split3(-cm) + (zeros,), axis=0).astype(BF16)
        arg = lax.dot_general(lhs, rhs, _TN, preferred_element_type=F32)
    else:
        r_col = jnp.concatenate([pr, jnp.zeros((SUBLANES - 3, L), F32)], axis=0).T[:, 1:2]
        arg = r_col - cm
    st = lax.dot_general(k, q, _NT, preferred_element_type=F32)
    cq = lax.dot_general(ct.astype(BF16), q, _NT, preferred_element_type=F32)
    last = L - 1 if fwd else 0
    cm_last = cm[:, last:last + 1]
    mx_last = jnp.maximum(m, cm_last)
    vw = (vt.astype(F32) * jnp.exp(r - cm_last)).astype(BF16)
    dct = jnp.dot(vw, k, preferred_element_type=F32)
    yield

    row = lax.broadcasted_iota(jnp.int32, (L, L), 0)
    col = lax.broadcasted_iota(jnp.int32, (L, L), 1)
    e = jnp.where((row <= col) if fwd else (row >= col), jnp.exp(arg), 0.0)
    num = jnp.dot(vt, (st * e).astype(BF16), preferred_element_type=F32)
    ct_new = jnp.exp(m - mx_last) * ct + (jnp.exp(cm_last - mx_last) * scale) * dct
    yield

    mx = jnp.maximum(m, cm)
    num = (jnp.exp(cm - mx) * scale) * num + jnp.exp(m - mx) * cq
    den = num[N_AUG:N_AUG + 1, :]
    h = num * (1.0 / jnp.maximum(jnp.abs(den), jnp.exp(-(b + mx))))
    out += [h, ct_new, b[:, last:last + 1] + mx_last]


def _mlstm_kernel(nc, sps, has_init, want_final, n_prev, *refs):
    refs = list(refs)
    m0_ref, c0_ref = (refs.pop(0), refs.pop(0)) if has_init else (None, None)
    prev_refs = [refs.pop(7) for _ in range(n_prev)]
    nw_ref, hbuf_ref = refs[6], refs[-1]
    for s in range(sps):
        seq_refs = [r.at[s] for r in refs[:6] + refs[7:-1]]
        _mlstm_sequence(nc, pl.program_id(0) * sps + s, m0_ref, None if c0_ref is None else c0_ref.at[s],
                        [r.at[s] for r in prev_refs], nw_ref, hbuf_ref, want_final, *seq_refs)


def _mlstm_sequence(nc, bi, m0_ref, c0_ref, prev_refs, nw_ref, hbuf_ref, want_final,
                    q_ref, k_ref, vt_ref, zo_ref, prf_ref, prb_ref, o_ref, *fin_refs):
    has_init = c0_ref is not None
    n_prev = len(prev_refs)
    L = MLSTM_CHUNK
    hg = pl.program_id(1)
    vrow = lax.broadcasted_iota(jnp.int32, (HEAD_PAD, L), 0)

    def chunk(hh, ci, pr_ref, ct, m, dr, out):
        rows = pl.ds(pl.multiple_of(ci * L, L), L)
        lanes = slice(hh * HEAD_PAD, (hh + 1) * HEAD_PAD)
        vt = vt_ref[lanes, rows]
        vt = jnp.where(vrow == N_AUG, jnp.ones_like(vt), vt)
        res = []
        yield from _mlstm_chunk(q_ref[rows, lanes], k_ref[rows, lanes], vt, pr_ref[hh, :, rows], ct, m,
                                dr == 0, nc == 1, res)
        hbuf_ref[hh, dr, ci] = res[0]
        out += res[1:]

    def step(i, carry):
        outs = [[] for _ in range(2 * HPS)]
        gens = []
        for hh in range(HPS):
            cf, mf, cb, mb = carry[4 * hh:4 * hh + 4]
            gens += [chunk(hh, i, prf_ref, cf, mf, 0, outs[2 * hh]),
                     chunk(hh, nc - 1 - i, prb_ref, cb, mb, 1, outs[2 * hh + 1])]
        _alternate(gens)
        return tuple(x for o in outs for x in o)

    carry = []
    for hh in range(HPS):
        for dr in range(2):
            if has_init:
                carry += [c0_ref[dr, hh].T, jnp.full((1, 1), m0_ref[bi, dr, hg * HPS + hh], F32)]
            else:
                carry += [jnp.zeros((HEAD_PAD, HEAD_PAD), F32), jnp.zeros((1, 1), F32)]
    carry = step(0, tuple(carry)) if nc == 1 else lax.fori_loop(0, nc, step, tuple(carry))
    if want_final:
        cfin_ref, nfin_ref, mfin_ref = fin_refs
        if n_prev:
            for j, ref in enumerate(prev_refs):
                cfin_ref[j] = ref[...]
            cfin_ref = cfin_ref.at[n_prev]
        for hh in range(HPS):
            for dr in range(2):
                ct, m = carry[4 * hh + 2 * dr], carry[4 * hh + 2 * dr + 1]
                cfin_ref[dr, hh] = ct.T[:MLSTM_DH, :MLSTM_DH]
                nfin_ref[dr, hh] = ct[N_AUG:N_AUG + 1, :MLSTM_DH]
                mfin_ref[dr, hh] = jnp.broadcast_to(m, (SUBLANES, LANES))

    def finish(ci, _):
        rows = pl.ds(pl.multiple_of(ci * L, L), L)
        for hh in range(HPS):
            lanes = slice(hh * HEAD_PAD, (hh + 1) * HEAD_PAD)
            h = jnp.where(vrow < MLSTM_DH, hbuf_ref[hh, 0, ci] + hbuf_ref[hh, 1, ci], 0.0)
            ms = jnp.sum(h * h, axis=0, keepdims=True) * (1.0 / MLSTM_DH)
            hn = (h * lax.rsqrt(ms + EPS) * nw_ref[hh]).T
            o_ref[rows, lanes] = (hn * jax.nn.sigmoid(zo_ref[rows, lanes].astype(F32))).astype(BF16)
        return 0

    if nc == 1:
        finish(0, 0)
    else:
        lax.fori_loop(0, nc, finish, 0)


def _mlstm(p, l, q, k, vt, zo, pr, nw, init=None, c_prev=None):
    want_final = c_prev is not None
    stack = want_final and l == DEPTH - 1
    assert not want_final or len(c_prev) == l
    nc = p.seq // MLSTM_CHUNK
    sps = max(1, 4 * MLSTM_CHUNK // p.seq)
    tok = pl.BlockSpec((sps, p.seq, HPS * HEAD_PAD), lambda b, h: (b, 0, h))
    st = lambda r, c: pl.BlockSpec((sps, 2, HPS, r, c), lambda b, h: (b, 0, h, 0, 0))
    prs = lambda dr: pl.BlockSpec((sps, HPS, 3, p.seq), lambda b, h: (b, dr * (MLSTM_HEADS // HPS) + h, 0, 0))
    in_specs, args = [], []
    if init is not None:
        in_specs += [pl.BlockSpec(memory_space=pltpu.SMEM), st(HEAD_PAD, HEAD_PAD)]
        args += list(init)
    in_specs += [tok, tok, pl.BlockSpec((sps, HPS * HEAD_PAD, p.seq), lambda b, h: (b, h, 0)), tok, prs(0), prs(1),
                 pl.BlockSpec((None, HPS, HEAD_PAD, 1), lambda b, h: (l, h, 0, 0))]
    args += [q, k, vt, zo, pr, pr, nw]
    out_specs = [tok]
    out_shape = [jax.ShapeDtypeStruct((p.nb, p.seq, HEADS_W), BF16)]
    if want_final:
        if stack:
            in_specs += [st(MLSTM_DH, MLSTM_DH)] * l
            args += list(c_prev)
            out_specs.append(pl.BlockSpec((sps, DEPTH, 2, HPS, MLSTM_DH, MLSTM_DH), lambda b, h: (b, 0, 0, h, 0, 0)))
            out_shape.append(jax.ShapeDtypeStruct((p.nb, DEPTH, 2, MLSTM_HEADS, MLSTM_DH, MLSTM_DH), F32))
        else:
            out_specs.append(st(MLSTM_DH, MLSTM_DH))
            out_shape.append(jax.ShapeDtypeStruct((p.nb, 2, MLSTM_HEADS, MLSTM_DH, MLSTM_DH), F32))
        out_specs += [st(1, MLSTM_DH), st(SUBLANES, LANES)]
        out_shape += [jax.ShapeDtypeStruct((p.nb, 2, MLSTM_HEADS, r, c), F32)
                      for r, c in ((1, MLSTM_DH), (SUBLANES, LANES))]
    return pl.pallas_call(
        functools.partial(_mlstm_kernel, nc, sps, init is not None, want_final, l if stack else 0),
        grid=(p.nb // sps, MLSTM_HEADS // HPS),
        in_specs=in_specs,
        out_specs=out_specs,
        out_shape=out_shape,
        scratch_shapes=[pltpu.VMEM((HPS, 2, nc, HEAD_PAD, MLSTM_CHUNK), F32)],
        compiler_params=_cparams(("arbitrary", "arbitrary")),
        name="mlstm_%d" % p.seq,
    )(*args)


def _cpow(br, bi, e, nbits):
    pr = pi = None
    for bit in range(nbits):
        sel = ((e >> bit) & 1) == 1
        if pr is None:
            pr, pi = jnp.where(sel, br, 1.0), jnp.where(sel, bi, 0.0)
        else:
            pr, pi = jnp.where(sel, pr * br - pi * bi, pr), jnp.where(sel, pr * bi + pi * br, pi)
        if bit + 1 < nbits:
            br, bi = br * br - bi * bi, 2.0 * br * bi
    return pr, pi


def _s5_prep_kernel(lamr_re_ref, lamr_im_ref, lstep_ref,
                    bt_re_ref, bt_im_ref, ct_re_ref, ct_im_ref,
                    t_ref, m_ref, n_ref, a_ref):
    C = S5_CHUNK
    nbits = (C - 1).bit_length()
    assert C == 1 << nbits
    kk = lax.broadcasted_iota(jnp.int32, (S5_STATE, S5_IN), 1) >> 4
    left = lax.broadcasted_iota(jnp.int32, (S5_IN, S5_ST), 1) < S5_STATE
    left16 = lax.broadcasted_iota(jnp.int32, (S5_GROUP_CH, S5_ST), 1) < S5_STATE
    left1 = lax.broadcasted_iota(jnp.int32, (1, S5_ST), 1) < S5_STATE
    lane = lax.broadcasted_iota(jnp.int32, (S5_GROUP_CH, S5_IN), 1)
    sel = jnp.where((lane & (S5_GROUP_CH - 1)) == lax.broadcasted_iota(jnp.int32, (S5_GROUP_CH, S5_IN), 0),
                    1.0, 0.0)
    spread = lambda ref: jnp.dot(ref[0, 0], sel, precision=lax.Precision.HIGHEST, preferred_element_type=F32)
    ct_re = spread(ct_re_ref)
    ct_im = spread(ct_im_ref)
    resp = []
    for d in range(2):
        step = jnp.exp(lstep_ref[0, d, 0])
        lr_r, li_r = lamr_re_ref[0, d, 0], lamr_im_ref[0, d, 0]
        mag = jnp.exp(lr_r * step)
        lb_re, lb_im = mag * jnp.cos(li_r * step), mag * jnp.sin(li_r * step)
        cols = jnp.concatenate([lb_re, lb_im, jnp.zeros((SUBLANES - 2, S5_ST), F32)], axis=0).T
        lbc_re, lbc_im = cols[:S5_STATE, 0:1], cols[:S5_STATE, 1:2]

        pr, pi = _cpow(lbc_re, lbc_im, kk if d == 0 else (C - 1) - kk, nbits)
        pr1, pi1 = pr * lbc_re - pi * lbc_im, pr * lbc_im + pi * lbc_re
        cpr, cpi = ct_re * pr - ct_im * pi, ct_re * pi + ct_im * pr
        cpr1, cpi1 = ct_re * pr1 - ct_im * pi1, ct_re * pi1 + ct_im * pr1

        nr, ni = lb_re - 1.0, lb_im
        den = lr_r * lr_r + li_r * li_r
        kap_re = (nr * lr_r + ni * li_r) / den
        kap_im = (ni * lr_r - nr * li_r) / den
        bb_re = kap_re * bt_re_ref[0, 0] - kap_im * bt_im_ref[0, 0]
        bb_im = kap_re * bt_im_ref[0, 0] + kap_im * bt_re_ref[0, 0]

        resp.append(jnp.dot(jnp.where(left16, bb_re, -bb_im), jnp.concatenate([cpr, cpi], axis=0),
                            precision=lax.Precision.HIGHEST, preferred_element_type=F32))
        m_ref[0, d, 0] = jnp.concatenate([cpr1, -cpi1], axis=0).astype(BF16)

        blocks = [(bb_re, bb_im)]
        for _ in range(C - 1):
            br_, bi_ = blocks[-1]
            blocks.append((br_ * lb_re - bi_ * lb_im, br_ * lb_im + bi_ * lb_re))
        if d == 0:
            blocks.reverse()
        n_re = jnp.concatenate([b_[0] for b_ in blocks], axis=0)
        n_im = jnp.concatenate([b_[1] for b_ in blocks], axis=0)
        n_ref[0, d, 0] = jnp.concatenate([jnp.where(left, n_re, n_im), jnp.where(left, n_im, n_re)],
                                         axis=1).astype(BF16)

        ar, ai = lb_re, lb_im
        for _ in range(nbits):
            ar, ai = ar * ar - ai * ai, 2.0 * ar * ai
        a2 = jnp.where(left1, -ai, ai)
        a_ref[0, d, 0] = jnp.concatenate([jnp.concatenate([ar, ar], axis=1),
                                          jnp.concatenate([a2, -a2], axis=1)], axis=0)

    rf, rb = resp
    for s in range(C):
        nf = S5_GROUP_CH * s
        blk = jnp.where(lane >= nf, pltpu.roll(rf, nf, 1) if nf else rf, 0.0)
        nb = S5_GROUP_CH * (C - 1 - s)
        blk = blk + jnp.where(lane < S5_IN - nb, pltpu.roll(rb, S5_IN - nb, 1) if nb else rb, 0.0)
        t_ref[0, 0, S5_GROUP_CH * s:S5_GROUP_CH * (s + 1), :] = blk.astype(BF16)


def _s5_prep(lam_re, lam_im, log_step, b_re, b_im, c_re, c_im):
    G = S5_GROUPS
    dup = lambda a: jnp.concatenate([a, a], axis=-1)
    lamr = [dup(a).reshape(DEPTH, 2, G, 1, S5_ST) for a in (lam_re, lam_im)]
    lstep = log_step.reshape(DEPTH, 2, G, 1, 1)
    bt = [dup(jnp.swapaxes(a, 2, 3)) for a in (b_re, b_im)]
    ct = [jnp.swapaxes(a, 2, 3) for a in (c_re, c_im)]
    dspec = lambda r, c: pl.BlockSpec((1, 2, 1, r, c), lambda l, g: (l, 0, g, 0, 0))
    gspec = lambda r, c: pl.BlockSpec((1, 1, r, c), lambda l, g: (l, g, 0, 0))
    t, m, n, a = pl.pallas_call(
        _s5_prep_kernel,
        grid=(DEPTH, G),
        in_specs=[dspec(1, S5_ST), dspec(1, S5_ST), dspec(1, 1),
                  gspec(S5_GROUP_CH, S5_ST), gspec(S5_GROUP_CH, S5_ST),
                  gspec(S5_STATE, S5_GROUP_CH), gspec(S5_STATE, S5_GROUP_CH)],
        out_specs=[gspec(S5_IN, S5_IN), dspec(S5_ST, S5_IN), dspec(S5_IN, S5_ST2), dspec(2, S5_ST2)],
        out_shape=[jax.ShapeDtypeStruct((DEPTH, G, S5_IN, S5_IN), BF16),
                   jax.ShapeDtypeStruct((DEPTH, 2, G, S5_ST, S5_IN), BF16),
                   jax.ShapeDtypeStruct((DEPTH, 2, G, S5_IN, S5_ST2), BF16),
                   jax.ShapeDtypeStruct((DEPTH, 2, G, 2, S5_ST2), F32)],
        compiler_params=_cparams(("arbitrary", "arbitrary")),
        name="s5_prep",
    )(*lamr, lstep, *bt, *ct)
    return t, m, n, jnp.transpose(a, (0, 1, 3, 2, 4)).reshape(DEPTH, 2, 2, G * S5_ST2)


def _s5_kernel(nseg, nchunks, zr_ref, t_ref, m_ref, n_ref, a_ref, x0_ref, d_ref,
               y_ref, xfin_ref, v_ref, xp_ref):
    W = S5_GB * S5_ST2
    for gl in range(S5_GB):
        u = zr_ref[0, :, gl * S5_IN:(gl + 1) * S5_IN].astype(BF16)
        for d in range(2):
            v_ref[d, :, gl * S5_ST2:(gl + 1) * S5_ST2] = jnp.dot(
                u, n_ref[d, gl], preferred_element_type=F32)

    def halves(x, which):
        return [x[:, g * S5_ST2 + h * S5_ST:g * S5_ST2 + (h + 1) * S5_ST]
                for g in range(S5_GB) for h in which]

    assert 2 * TB == SUBLANES
    steps = nchunks // 2
    is_fwd = lax.broadcasted_iota(jnp.int32, (SUBLANES, 1), 0) < TB
    both = lambda f, b: jnp.where(is_fwd, f, b)
    flip = lambda x: pltpu.roll(x, TB, 0)
    a = both(a_ref[0, 0:1, :], a_ref[1, 0:1, :])
    a2 = both(a_ref[0, 1:2, :], a_ref[1, 1:2, :])
    state = lambda x: jnp.concatenate(halves(x, (0,)), axis=1)

    def advance(x, v):
        swapped = jnp.concatenate(halves(x, (1, 0)), axis=1)
        return a * x + a2 * swapped + v

    for seg in range(nseg):
        base = seg * nchunks * TB

        def step(i, x):
            rf = pl.ds(pl.multiple_of(base + i * SUBLANES, SUBLANES), SUBLANES)
            rb = pl.ds(pl.multiple_of(base + (steps - 1 - i) * SUBLANES, SUBLANES), SUBLANES)
            vf, vb = v_ref[0, rf, :], v_ref[1, rb, :]
            e1 = state(x)
            x = advance(x, both(vf, vb))
            e2 = state(x)
            x = advance(x, both(flip(vf), flip(vb)))
            xp_ref[0, rf, :] = both(e1, flip(e2))
            xp_ref[1, rb, :] = both(flip(e2), e1)
            return x

        rows = slice(seg * TB, (seg + 1) * TB)
        x0 = (jnp.zeros((SUBLANES, W), F32) if x0_ref is None
              else jnp.concatenate([x0_ref[0, rows, :], x0_ref[1, rows, :]], axis=0))
        x = state(lax.fori_loop(0, steps, step, x0))
        if xfin_ref is not None:
            xfin_ref[0, rows, :] = x[:TB]
            xfin_ref[1, rows, :] = x[TB:]

    for gl in range(S5_GB):
        cols = slice(gl * S5_IN, (gl + 1) * S5_IN)
        u = zr_ref[0, :, cols]
        y = jnp.dot(u.astype(BF16), t_ref[gl], preferred_element_type=F32)
        xin = jnp.concatenate([xp_ref[d, :, gl * S5_ST:(gl + 1) * S5_ST] for d in range(2)], axis=1)
        m_both = jnp.concatenate([m_ref[0, gl], m_ref[1, gl]], axis=0)
        y = y + jnp.dot(xin.astype(BF16), m_both, preferred_element_type=F32)
        y_ref[0, :, cols] = jax.nn.gelu(y + d_ref[0, :, cols] * u)


def _s5(p, l, zr, t, m, n, a, dt, x0=None, want_final=False):
    G = S5_GROUPS
    nseg = p.nb // TB
    row = pl.BlockSpec((1, R_PASS, S5_ROW), lambda j: (j, 0, 0))
    dsp = lambda r, c: pl.BlockSpec((None, 2, S5_GB, r, c), lambda j: (l, 0, j, 0, 0))
    lsp = lambda r, w: pl.BlockSpec((2, r, S5_GB * w), lambda j: (0, 0, j))
    in_specs = [row, pl.BlockSpec((None, S5_GB, S5_IN, S5_IN), lambda j: (l, j, 0, 0)),
                dsp(S5_ST, S5_IN), dsp(S5_IN, S5_ST2),
                pl.BlockSpec((None, 2, 2, S5_GB * S5_ST2), lambda j: (l, 0, 0, j)),
                pl.BlockSpec((None, 1, 1, S5_ROW), lambda j: (l, j, 0, 0))]
    args = [zr, t, m, n, a, dt]
    if x0 is not None:
        in_specs.append(lsp(p.nb, S5_ST2))
        args.append(x0)
    out_specs = [row]
    out_shape = [jax.ShapeDtypeStruct((S5_NB, R_PASS, S5_ROW), F32)]
    if want_final:
        out_specs.append(lsp(p.nb, S5_ST))
        out_shape.append(jax.ShapeDtypeStruct((2, p.nb, G * S5_ST), F32))

    def body(zr_ref, t_ref, m_ref, n_ref, a_ref, d_ref, *rest):
        rest = list(rest)
        x0_ref = rest.pop(0) if x0 is not None else None
        y_ref = rest.pop(0)
        xfin_ref = rest.pop(0) if want_final else None
        _s5_kernel(nseg, p.seq // S5_CHUNK, zr_ref, t_ref, m_ref, n_ref, a_ref, x0_ref, d_ref,
                   y_ref, xfin_ref, *rest)

    return pl.pallas_call(
        body,
        grid=(S5_NB,),
        in_specs=in_specs,
        out_specs=out_specs,
        out_shape=out_shape,
        scratch_shapes=[pltpu.VMEM((2, R_PASS, S5_GB * S5_ST2), F32),
                        pltpu.VMEM((2, R_PASS, S5_GB * S5_ST), F32)],
        compiler_params=_cparams(("arbitrary",)),
        name="s5_scan_%d" % p.seq,
    )(*args)


def _mix_stages(x_ref, mod_ref, fo_ref, mo_ref, ys_ref, wglu_ref, wo_ref, nw_ref, scr_ref, out):
    mod = mod_ref[...]
    g1 = mod[:, :, 2 * D_MODEL:3 * D_MODEL]
    flat = lambda ref: ref[...].reshape(TB * TT, ref.shape[-1])
    mo = flat(mo_ref)
    mo = jnp.concatenate([mo[:, h * HEAD_PAD:h * HEAD_PAD + MLSTM_DH] for h in range(MLSTM_HEADS)], axis=1)
    half = (FOURIER_W + MLSTM_W + S5_W) // 2
    m_split = half - FOURIER_W
    mix = jnp.dot(jnp.concatenate([flat(fo_ref), mo[:, :m_split]], axis=1), wo_ref[:half, :],
                  preferred_element_type=F32)
    yield
    blocks = []
    for bl in range(S5_NB):
        by_group = [ys_ref[bl, :, gl * S5_IN:(gl + 1) * S5_IN] for gl in range(S5_GB)]
        for t in range(S5_CHUNK):
            lo = t * S5_GROUP_CH
            scr_ref[bl, pl.ds(t, TROWS, stride=S5_CHUNK), :] = jnp.concatenate(
                [y[:, lo:lo + S5_GROUP_CH] for y in by_group], axis=1)
        slabs = [scr_ref[bl, _slab(b, c)[1], :] for b in range(TB) for c in range(TT // S5_CHUNK)]
        blocks.append(jnp.concatenate(slabs, axis=0))
        yield
    y = jnp.concatenate(blocks, axis=1).astype(BF16)
    gg = jnp.dot(y, wglu_ref[...], preferred_element_type=F32)
    s_out = (gg[:, :S5_W] * jax.nn.sigmoid(gg[:, S5_W:])).astype(BF16)
    mix = mix + jnp.dot(jnp.concatenate([mo[:, m_split:], s_out], axis=1), wo_ref[half:, :],
                        preferred_element_type=F32)
    yield
    x1 = x_ref[...] + g1 * mix.reshape(TB, TT, D_MODEL)
    out += [x1, _mod_norm(x1, mod, nw_ref[...], 3)]


FF_SPLIT = (1024, 1024, 768)
assert sum(FF_SPLIT) == D_FF and all(w % MXU_N == 0 for w in FF_SPLIT)


def _ffn_stages(final, xn, x1, mod_ref, wg_ref, wu_ref, wd_ref, nf_ref, out):
    ff = None
    o = 0
    for w in FF_SPLIT:
        cols = slice(o, o + w)
        o += w
        a = jnp.dot(xn, wg_ref[:, cols], preferred_element_type=F32)
        u = jnp.dot(xn, wu_ref[:, cols], preferred_element_type=F32)
        h = (a * jax.nn.sigmoid(a) * u).astype(BF16)
        part = jnp.dot(h, wd_ref[cols, :], preferred_element_type=F32)
        ff = part if ff is None else ff + part
        yield
    g2 = mod_ref[...][:, :, 5 * D_MODEL:6 * D_MODEL]
    x2 = x1 + g2 * ff.reshape(TB, TT, D_MODEL)
    if final:
        x2 = x2 * lax.rsqrt(jnp.mean(x2 * x2, axis=-1, keepdims=True) + EPS) * nf_ref[...]
    out.append(x2)


def _post_kernel(final, n, x_ref, moda_ref, fo_ref, mo_ref, ys_ref, modb_ref, wglu_ref, wo_ref, nw_ref,
                 wg_ref, wu_ref, wd_ref, nf_ref, o_ref, x1_ref, xn_ref, scr_ref):
    i = pl.program_id(0)
    cur = i % 2
    prev = 1 - cur

    def run(do_ffn, do_mix):
        res_a, res_b, gens = [], [], []
        if do_ffn:
            gens.append(_ffn_stages(final, xn_ref[prev], x1_ref[prev], modb_ref, wg_ref, wu_ref, wd_ref,
                                    nf_ref, res_b))
        if do_mix:
            gens.append(_mix_stages(x_ref, moda_ref, fo_ref, mo_ref, ys_ref, wglu_ref, wo_ref, nw_ref,
                                    scr_ref, res_a))
        _alternate(gens)
        if do_ffn:
            o_ref[...] = res_b[0]
        if do_mix:
            x1_ref[cur] = res_a[0]
            xn_ref[cur] = res_a[1]

    pl.when(i == 0)(lambda: run(False, True))
    pl.when(jnp.logical_and(i > 0, i < n))(lambda: run(True, True))
    pl.when(i == n)(lambda: run(True, False))


def _post(p, l, final, x, mods, fo, mo, ys, wglu, wo, nw, wg, wu, wd, nf):
    ts = _tile_specs(p, l)
    once = lambda a: pl.BlockSpec((None,) + a.shape[1:], lambda i: (l,) + (0,) * (a.ndim - 1),
                                  pipeline_mode=pl.Buffered(1))
    return pl.pallas_call(
        functools.partial(_post_kernel, final, ts.n),
        grid=(ts.n + 1,),
        in_specs=[ts.tok(D_MODEL, ts.head), ts.mod(ts.head), ts.tok(FOURIER_W, ts.head),
                  ts.tok(HEADS_W, ts.head), ts.rows(ts.head), ts.mod(ts.tail),
                  once(wglu), once(wo), once(nw), once(wg), once(wu), once(wd),
                  pl.BlockSpec((1, D_MODEL), lambda i: (0, 0))],
        out_specs=ts.tok(D_MODEL, ts.tail),
        out_shape=jax.ShapeDtypeStruct((p.nb, p.seq, D_MODEL), F32),
        scratch_shapes=[pltpu.VMEM((2, TB, TT, D_MODEL), F32), pltpu.VMEM((2, TB * TT, D_MODEL), BF16),
                        pltpu.VMEM((S5_NB, TB * TT, LANES), F32)],
        compiler_params=pltpu.CompilerParams(dimension_semantics=("arbitrary",),
                                             vmem_limit_bytes=POST_VMEM_LIMIT),
        name="post_%d" % p.seq,
    )(x, mods, fo, mo, ys, mods, wglu, wo, nw, wg, wu, wd, nf)


def _pad_heads(a, axis):
    shape = a.shape[:axis] + (MLSTM_HEADS, MLSTM_DH) + a.shape[axis + 1:]
    pad = [(0, 0)] * (a.ndim + 1)
    pad[axis + 1] = (0, HEAD_PAD - MLSTM_DH)
    return jnp.pad(a.reshape(shape), pad).reshape(a.shape[:axis] + (HEADS_W,) + a.shape[axis + 1:])


def _mlstm_state_in(c, n):
    cn = jnp.concatenate([c, n[..., None]], axis=-1)
    return jnp.pad(cn, ((0, 0),) * (c.ndim - 2) + ((0, HEAD_PAD - MLSTM_DH), (0, HEAD_PAD - MLSTM_DH - 1)))


def kernel(x_prompt, x_sample, state_mlstm_C, state_mlstm_n, state_mlstm_m, state_s5_re, state_s5_im,
           c, c_ctx, w_ada, b_ada, norm1_w, norm2_w, w_in, b_gates, w_fourier, mlstm_norm_w,
           s5_lambda_re, s5_lambda_im, s5_log_step, s5_b_re, s5_b_im, s5_c_re, s5_c_im, s5_d,
           w_glu, w_out, w_gate, w_up, w_down, norm_f):
    xs = {PROMPT: x_prompt, SAMPLE: x_sample}
    cc = jnp.concatenate([c, c_ctx[None], jnp.zeros((N_MODS - 1 - DEC_BATCH, D_MODEL), F32)], axis=0)
    mods = _ada(cc, w_ada, b_ada).reshape(DEPTH, N_MODS, 1, 6 * D_MODEL)
    cdsd, cs, ab = (jnp.asarray(a.astype(np.float32)).astype(BF16) for a in _dft_consts())
    s5_t, s5_m, s5_n, s5_a = _s5_prep(s5_lambda_re, s5_lambda_im, s5_log_step, s5_b_re, s5_b_im,
                                      s5_c_re, s5_c_im)

    o_q = FOURIER_W
    o_g = o_q + 3 * MLSTM_W
    o_o = o_g + N_GATES
    o_u = o_o + MLSTM_W
    cols = lambda o, w: w_in[:, :, o:o + w]
    w_cat = jnp.concatenate([cols(o_u, S5_W), cols(0, FOURIER_W), cols(o_q, 2 * MLSTM_W), cols(o_o, MLSTM_W)],
                            axis=2).astype(BF16)
    wv_t = jnp.swapaxes(cols(o_q + 2 * MLSTM_W, MLSTM_W), 1, 2).astype(BF16)
    gate_perm = np.arange(N_GATES).reshape(2, 2, MLSTM_HEADS).transpose(1, 0, 2).reshape(-1)
    wg_t = jnp.swapaxes(w_in[:, :, o_g:o_o], 1, 2)[:, gate_perm].astype(BF16)
    bg = b_gates[:, gate_perm, None]
    wf = w_fourier.astype(BF16)
    nw = _pad_heads(mlstm_norm_w, 1).reshape(DEPTH, MLSTM_HEADS, HEAD_PAD, 1)
    dt = jnp.tile(s5_d[:, :, None, :], (1, 1, S5_CHUNK, 1)).reshape(DEPTH, S5_NB, 1, S5_ROW)
    wo = w_out.astype(BF16)
    wglu = w_glu.astype(BF16)
    wg, wu, wd = w_gate.astype(BF16), w_up.astype(BF16), w_down.astype(BF16)
    n1, n2 = norm1_w[:, None, :], norm2_w[:, None, :]

    m0 = jnp.swapaxes(state_mlstm_m, 0, 1)
    c0 = jnp.swapaxes(_mlstm_state_in(state_mlstm_C, state_mlstm_n), 0, 1)
    x0 = jnp.concatenate([state_s5_re, state_s5_im, state_s5_im, state_s5_re], axis=-1)
    x0 = jnp.transpose(x0, (1, 2, 0, 3, 4)).reshape(DEPTH, 2, DEC_BATCH, S5_GROUPS * S5_ST2)

    finals, cfins = [], []
    for l in range(DEPTH):
        for p in (PROMPT, SAMPLE):
            x = xs[p]
            zq, zk, zo, zf, zu, vt, gt = _in_proj(p, l, x, mods, n1, w_cat, wv_t, wg_t, bg)
            pr = _gate_prep(p, gt)
            if p is PROMPT:
                fo = _fourier_prompt(l, zf, cdsd, cs, wf)
                mo, cfin, nfin, mfin = _mlstm(p, l, zq, zk, vt, zo, pr, nw, c_prev=cfins)
                cfins.append(cfin)
                ys, xfin = _s5(p, l, zu, s5_t, s5_m, s5_n, s5_a, dt, want_final=True)
                finals.append([nfin, mfin, xfin])
            else:
                fo = _fourier_sample(l, zf, cdsd, ab, wf)
                mo, = _mlstm(p, l, zq, zk, vt, zo, pr, nw, init=(m0[l], c0[l]))
                ys, = _s5(p, l, zu, s5_t, s5_m, s5_n, s5_a, dt, x0=x0[l])
            xs[p] = _post(p, l, l == DEPTH - 1, x, mods, fo, mo, ys, wglu, wo, n2, wg, wu, wd, norm_f[None])

    nfin, mfin, xfin = (jnp.stack(parts, axis=1) for parts in zip(*finals))
    xfin = xfin.reshape(2, DEPTH, BATCH, S5_GROUPS, 2, S5_STATE)
    new_re, new_im = (jnp.transpose(xfin[:, :, :, :, i], (2, 1, 0, 3, 4)) for i in range(2))
    return (xs[PROMPT], xs[SAMPLE], cfin, nfin[:, :, :, :, 0], mfin[:, :, :, :, 0, 0], new_re, new_im)
```
